```python
import jax, jax.numpy as jnp
from jax import lax
import numpy as np

D_MODEL = 1024
BATCH = 8
SEQ = 4096
DEPTH = 1

GRID_W = 64
MLSTM_HEADS = 4
MLSTM_WIDTH = D_MODEL
MLSTM_HEAD_DIM = MLSTM_WIDTH // MLSTM_HEADS
MLSTM_CHUNK = 64
CONV_WIDTH = 3
NA_HEADS = 8
NA_WIDTH = D_MODEL // 2
NA_HEAD_DIM = NA_WIDTH // NA_HEADS
NA_ROWS = 8
NA_COLS = 16
Q_BLOCK_COLS = 16
K_BLOCK_COLS = 32
N_BRANCHES = 2
EPS = 1e-6
NEG = -1e30
IN_SPLITS = (MLSTM_WIDTH, MLSTM_WIDTH, MLSTM_WIDTH, MLSTM_WIDTH, MLSTM_WIDTH, 4 * MLSTM_HEADS, NA_WIDTH, NA_WIDTH, NA_WIDTH, NA_WIDTH, N_BRANCHES * D_MODEL)
IN_TOTAL = 5 * MLSTM_WIDTH + 4 * MLSTM_HEADS + 4 * NA_WIDTH + N_BRANCHES * D_MODEL

kernel_name = 'hybrid_mlstm_natten_block'


def rms_norm(x, gain):
    xf = x.astype(jnp.float32)
    y = xf * lax.rsqrt(jnp.mean(xf * xf, axis=-1, keepdims=True) + EPS)
    return (y * gain.astype(jnp.float32)).astype(x.dtype)


def centred_dwconv(u, w, b):
    s = u.shape[1]
    pad = CONV_WIDTH // 2
    up = jnp.pad(u, ((0, 0), (pad, pad), (0, 0)))
    return sum(up[:, j:j + s] * w[j] for j in range(CONV_WIDTH)) + b


def mlstm_direction(q, k, v, i_pre, log_f):
    bsz, nh, s, d = q.shape
    nc = s // MLSTM_CHUNK
    L = MLSTM_CHUNK

    def chunks(a):
        return jnp.moveaxis(a.reshape(a.shape[:2] + (nc, L) + a.shape[3:]), 2, 0)

    qc, kc, vc, ic = chunks(q), chunks(k), chunks(v), chunks(i_pre)
    bc = jnp.cumsum(chunks(log_f), axis=-1)
    lower = jnp.tril(jnp.ones((L, L), dtype=bool))

    def step(carry, inp):
        C, n, m = carry
        q_, k_, v_, i_, b_ = inp
        g_ = b_[..., -1]
        log_d = jnp.where(lower, b_[..., :, None] - b_[..., None, :] + i_[..., None, :], NEG)
        log_inter = b_ + m[..., None]
        m_t = jnp.maximum(log_inter, jnp.max(log_d, axis=-1))
        d_mat = jnp.exp(log_d - m_t[..., None])
        w_inter = jnp.exp(log_inter - m_t)
        sc = jnp.einsum('bhtd,bhsd->bhts', q_, k_) * d_mat
        num = jnp.einsum('bhts,bhsd->bhtd', sc, v_) + w_inter[..., None] * jnp.einsum('bhtd,bhde->bhte', q_, C)
        den = jnp.sum(sc, axis=-1) + w_inter * jnp.einsum('bhtd,bhd->bht', q_, n)
        h = num / jnp.maximum(jnp.abs(den), jnp.exp(-m_t))[..., None]
        log_w = g_[..., None] - b_ + i_
        m_new = jnp.maximum(g_ + m, jnp.max(log_w, axis=-1))
        w = jnp.exp(log_w - m_new[..., None])
        decay = jnp.exp(g_ + m - m_new)
        C = decay[..., None, None] * C + jnp.einsum('bhs,bhsd,bhse->bhde', w, k_, v_)
        n = decay[..., None] * n + jnp.einsum('bhs,bhsd->bhd', w, k_)
        return (C, n, m_new), h

    init = (jnp.zeros((bsz, nh, d, d), jnp.float32), jnp.zeros((bsz, nh, d), jnp.float32), jnp.zeros((bsz, nh), jnp.float32))
    _, hs = lax.scan(step, init, (qc, kc, vc, ic, bc))
    return jnp.moveaxis(hs, 0, 2).reshape(bsz, nh, s, d)


def bidirectional_mlstm(q, k, v, gates, b_igate, b_fgate):
    bsz, s, _ = q.shape

    def heads(a):
        return a.astype(jnp.float32).reshape(bsz, s, MLSTM_HEADS, MLSTM_HEAD_DIM).transpose(0, 2, 1, 3)

    qh, kh, vh = heads(q), heads(k) * (MLSTM_HEAD_DIM ** -0.5), heads(v)
    g = gates.astype(jnp.float32).reshape(bsz, s, 2, 2, MLSTM_HEADS)
    i_pre = jnp.transpose(g[:, :, :, 0] + b_igate.astype(jnp.float32), (2, 0, 3, 1))
    log_f = jnp.transpose(jax.nn.log_sigmoid(g[:, :, :, 1] + b_fgate.astype(jnp.float32)), (2, 0, 3, 1))
    h_fwd = mlstm_direction(qh, kh, vh, i_pre[0], log_f[0])
    flip = lambda a: jnp.flip(a, axis=2)
    h_bwd = flip(mlstm_direction(flip(qh), flip(kh), flip(vh), flip(i_pre[1]), flip(log_f[1])))
    h = h_fwd + h_bwd
    return h.transpose(0, 2, 1, 3).reshape(bsz, s, MLSTM_WIDTH)


def neighbourhood_attention(q, k, v, rpb):
    bsz, s, _ = q.shape
    rows = s // GRID_W
    kh = min(NA_ROWS, rows)

    def grid(a):
        return a.astype(jnp.float32).reshape(bsz, rows, GRID_W, NA_HEADS, NA_HEAD_DIM).transpose(0, 3, 1, 2, 4)

    qg, kg, vg = grid(q), grid(k), grid(v)
    n_cb = GRID_W // Q_BLOCK_COLS
    q_cols = np.arange(GRID_W).reshape(n_cb, Q_BLOCK_COLS)
    win_start = np.clip(q_cols - NA_COLS // 2, 0, GRID_W - NA_COLS)
    kb_start = np.clip(np.arange(n_cb) * Q_BLOCK_COLS - NA_COLS // 2, 0, GRID_W - K_BLOCK_COLS)
    key_cols = kb_start[:, None] + np.arange(K_BLOCK_COLS)[None, :]
    kc3 = key_cols[:, None, :]
    col_mask = jnp.asarray((kc3 >= win_start[:, :, None]) & (kc3 < win_start[:, :, None] + NA_COLS))
    dc_idx = np.clip(kc3 - q_cols[:, :, None] + NA_COLS - 1, 0, 2 * NA_COLS - 2)
    rpb_col = rpb.astype(jnp.float32)[:, :, dc_idx]
    scale = NA_HEAD_DIM ** -0.5

    def row_fn(r):
        rs = jnp.clip(r - kh // 2, 0, rows - kh)
        k_rows = lax.dynamic_slice_in_dim(kg, rs, kh, axis=2)
        v_rows = lax.dynamic_slice_in_dim(vg, rs, kh, axis=2)
        k_blk = k_rows[:, :, :, key_cols, :]
        v_blk = v_rows[:, :, :, key_cols, :].transpose(0, 1, 3, 2, 4, 5).reshape(bsz, NA_HEADS, n_cb, kh * K_BLOCK_COLS, NA_HEAD_DIM)
        q_blk = lax.dynamic_index_in_dim(qg, r, axis=2, keepdims=False).reshape(bsz, NA_HEADS, n_cb, Q_BLOCK_COLS, NA_HEAD_DIM)
        sc = jnp.einsum('bhcid,bhacjd->bhciaj', q_blk, k_blk) * scale
        dr_idx = rs + jnp.arange(kh) - r + NA_ROWS - 1
        bias = rpb_col[:, dr_idx].transpose(0, 2, 3, 1, 4)
        sc = jnp.where(col_mask[:, :, None, :], sc + bias[None], NEG)
        p = jax.nn.softmax(sc.reshape(bsz, NA_HEADS, n_cb, Q_BLOCK_COLS, kh * K_BLOCK_COLS), axis=-1)
        o = jnp.einsum('bhcin,bhcnd->bhcid', p, v_blk)
        return o.reshape(bsz, NA_HEADS, GRID_W, NA_HEAD_DIM)

    out = lax.map(row_fn, jnp.arange(rows))
    return out.transpose(1, 0, 3, 2, 4).reshape(bsz, s, NA_WIDTH)


def hybrid_mixer(h, w_in, conv_w, conv_b, b_igate, b_fgate, mlstm_norm_gain, rpb, w_proj_a, w_proj_b, b_merge):
    bsz, s, _ = h.shape
    proj = h @ w_in
    offsets = [int(o) for o in np.cumsum(IN_SPLITS)[:-1]]
    q_a, k_a, v_a, o_a, z_a, gates, q_b, k_b, v_b, z_b, g_merge = jnp.split(proj, offsets, axis=-1)
    qk = jax.nn.silu(centred_dwconv(jnp.concatenate([q_a, k_a], axis=-1), conv_w, conv_b))
    q_a, k_a = jnp.split(qk, 2, axis=-1)
    h_a = jax.nn.sigmoid(o_a.astype(jnp.float32)) * bidirectional_mlstm(q_a, k_a, v_a, gates, b_igate, b_fgate)
    h_a = rms_norm(h_a.reshape(bsz, s, MLSTM_HEADS, MLSTM_HEAD_DIM), mlstm_norm_gain.reshape(MLSTM_HEADS, MLSTM_HEAD_DIM)).reshape(bsz, s, MLSTM_WIDTH)
    y_a = h_a.astype(h.dtype) * jax.nn.silu(z_a)
    y_b = neighbourhood_attention(q_b, k_b, v_b, rpb).astype(h.dtype) * jax.nn.silu(z_b)
    gm = jax.nn.sigmoid(g_merge.reshape(bsz, s, N_BRANCHES, D_MODEL) + b_merge)
    return gm[:, :, 0] * (y_a @ w_proj_a) + gm[:, :, 1] * (y_b @ w_proj_b)


def setup_inputs(seed: int = 0) -> dict:
    key = jax.random.key(seed)
    ks = jax.random.split(key, 18)

    def nrm(k, shape, scale):
        return jax.random.normal(k, shape, jnp.float32) * scale

    return {
        'x': nrm(ks[0], (BATCH, SEQ, D_MODEL), 1.0),
        'c': nrm(ks[1], (BATCH, D_MODEL), 1.0),
        'w_ada': nrm(ks[2], (DEPTH, D_MODEL, 3 * D_MODEL), 0.5 * D_MODEL ** -0.5),
        'b_ada': nrm(ks[3], (DEPTH, 3 * D_MODEL), 0.01),
        'norm_gain': 1.0 + nrm(ks[4], (DEPTH, D_MODEL), 0.02),
        'w_in': nrm(ks[5], (DEPTH, D_MODEL, IN_TOTAL), D_MODEL ** -0.5),
        'conv_w': nrm(ks[6], (DEPTH, CONV_WIDTH, 2 * MLSTM_WIDTH), CONV_WIDTH ** -0.5),
        'conv_b': nrm(ks[7], (DEPTH, 2 * MLSTM_WIDTH), 0.01),
        'b_igate': nrm(ks[8], (DEPTH, 2, MLSTM_HEADS), 0.1),
        'b_fgate': jax.random.uniform(ks[9], (DEPTH, 2, MLSTM_HEADS), jnp.float32, 3.0, 6.0),
        'mlstm_norm_gain': 1.0 + nrm(ks[10], (DEPTH, MLSTM_WIDTH), 0.02),
        'rpb': nrm(ks[11], (DEPTH, NA_HEADS, 2 * NA_ROWS - 1, 2 * NA_COLS - 1), 0.1),
        'w_proj_a': nrm(ks[12], (DEPTH, MLSTM_WIDTH, D_MODEL), MLSTM_WIDTH ** -0.5),
        'w_proj_b': nrm(ks[13], (DEPTH, NA_WIDTH, D_MODEL), NA_WIDTH ** -0.5),
        'b_merge': nrm(ks[14], (DEPTH, N_BRANCHES, D_MODEL), 0.1),
        'w_out': nrm(ks[15], (DEPTH, D_MODEL, D_MODEL), D_MODEL ** -0.5),
        'final_gain': 1.0 + nrm(ks[16], (D_MODEL,), 0.02),
    }


def reference(x, c, w_ada, b_ada, norm_gain, w_in, conv_w, conv_b, b_igate, b_fgate, mlstm_norm_gain, rpb, w_proj_a, w_proj_b, b_merge, w_out, final_gain):
    cond = jax.nn.silu(c)
    for layer in range(DEPTH):
        mod = cond @ w_ada[layer] + b_ada[layer]
        shift, scale, gate = jnp.split(mod, 3, axis=-1)
        h = rms_norm(x, norm_gain[layer]) * (1.0 + scale[:, None, :]) + shift[:, None, :]
        merged = hybrid_mixer(h, w_in[layer], conv_w[layer], conv_b[layer], b_igate[layer], b_fgate[layer], mlstm_norm_gain[layer], rpb[layer], w_proj_a[layer], w_proj_b[layer], b_merge[layer])
        x = x + gate[:, None, :] * (merged @ w_out[layer])
    return rms_norm(x, final_gain)
```

```python
import functools

import numpy as np
import jax
import jax.numpy as jnp
from jax import lax
from jax.experimental import pallas as pl
from jax.experimental.pallas import tpu as pltpu

F32 = jnp.float32
BF16 = jnp.bfloat16

D_MODEL = 1024
GRID_W = 64
MLSTM_HEADS = 4
MLSTM_HEAD_DIM = 256
NA_HEADS = 8
NA_WIDTH = 512
NA_HEAD_DIM = 64
NA_ROWS = 8
NA_COLS = 16
EPS = 1e-6
NEG = -1e30

VMEM_LIMIT_BYTES = 56 * 1024 * 1024
LANES = 128

CHUNK = 256

N_MAIN = 9 * D_MODEL
GATE_LANES = 128

Q_ROWS = 8
SLAB_ROWS = 16
Q_BLOCKS = ((0, 24), (24, 16), (40, 24))
K_STARTS = (0, 16, 32)
K_WIDTH = 32


def _sigmoid(x):
    return 1.0 / (1.0 + jnp.exp(-x))


def _silu(x):
    return x * _sigmoid(x)


def _adaln_kernel(c_ref, w_ref, b_ref, o_ref):
    cond = _silu(c_ref[...]).astype(BF16)
    o_ref[...] = jnp.dot(cond, w_ref[...].astype(BF16), preferred_element_type=F32) + b_ref[...]


def _adaln_mod(c, w_ada, b_ada):
    bsz, d = c.shape
    n = w_ada.shape[1]
    tn = 512
    return pl.pallas_call(
        _adaln_kernel,
        out_shape=jax.ShapeDtypeStruct((bsz, n), F32),
        grid=(n // tn,),
        in_specs=[
            pl.BlockSpec((bsz, d), lambda j: (0, 0)),
            pl.BlockSpec((d, tn), lambda j: (0, j)),
            pl.BlockSpec((1, tn), lambda j: (0, j)),
        ],
        out_specs=pl.BlockSpec((bsz, tn), lambda j: (0, j)),
        compiler_params=pltpu.CompilerParams(dimension_semantics=("arbitrary",)),
        name="adaln_mod",
    )(c, w_ada, b_ada.reshape(1, n))


def _in_proj_kernel(x_ref, mod_ref, gain_ref, w_ref, wg_ref, proj_ref, gates_ref, h_ref, *, rows):
    d = x_ref.shape[1]

    @pl.when(pl.program_id(1) == 0)
    def _():
        shift = mod_ref[0, :, 0:d]
        scale = mod_ref[0, :, d:2 * d]
        gain = gain_ref[...]
        for r in range(0, x_ref.shape[0], rows):
            xf = x_ref[r:r + rows, :]
            y = xf * lax.rsqrt(jnp.mean(xf * xf, axis=-1, keepdims=True) + EPS) * gain
            h_ref[r:r + rows, :] = (y * (1.0 + scale) + shift).astype(BF16)
        gates_ref[...] = jnp.dot(h_ref[...], wg_ref[...], preferred_element_type=F32)

    proj_ref[...] = jnp.dot(h_ref[...], w_ref[...], preferred_element_type=F32).astype(BF16)


def _in_proj(x2, mod3, norm_gain, w_main, w_gate, seq, tm):
    n_tok, d = x2.shape
    tn = D_MODEL
    return pl.pallas_call(
        functools.partial(_in_proj_kernel, rows=256),
        out_shape=(
            jax.ShapeDtypeStruct((n_tok, N_MAIN), BF16),
            jax.ShapeDtypeStruct((n_tok, GATE_LANES), F32),
        ),
        grid=(n_tok // tm, N_MAIN // tn),
        in_specs=[
            pl.BlockSpec((tm, d), lambda m, n: (m, 0)),
            pl.BlockSpec((1, 1, 3 * d), lambda m, n: ((m * tm) // seq, 0, 0)),
            pl.BlockSpec((1, d), lambda m, n: (0, 0)),
            pl.BlockSpec((d, tn), lambda m, n: (0, n)),
            pl.BlockSpec((d, GATE_LANES), lambda m, n: (0, 0)),
        ],
        out_specs=(
            pl.BlockSpec((tm, tn), lambda m, n: (m, n)),
            pl.BlockSpec((tm, GATE_LANES), lambda m, n: (m, 0)),
        ),
        scratch_shapes=[pltpu.VMEM((tm, d), BF16)],
        compiler_params=pltpu.CompilerParams(
            dimension_semantics=("arbitrary", "arbitrary"), vmem_limit_bytes=VMEM_LIMIT_BYTES),
        name="in_proj",
    )(x2, mod3, norm_gain.reshape(1, d), w_main, w_gate)


def _split3(x):
    hi = x.astype(BF16)
    r1 = x - hi.astype(F32)
    mid = r1.astype(BF16)
    lo = (r1 - mid.astype(F32)).astype(BF16)
    return hi, mid, lo


def _gate_prep_kernel(g_ref, bias_ref, col_ref, row_ref):
    v = g_ref[0] + bias_ref[...]
    length = v.shape[0]
    lane = lax.broadcasted_iota(jnp.int32, v.shape, 1)
    is_f = (lane % 2) == 1
    is_bwd = ((lane // 2) % 2) == 1
    log_f = jnp.where(is_f, jnp.minimum(v, 0.0) - jnp.log1p(jnp.exp(-jnp.abs(v))), 0.0)
    t = lax.broadcasted_iota(jnp.int32, (length, length), 0)
    s = lax.broadcasted_iota(jnp.int32, (length, length), 1)
    lower = (s <= t).astype(BF16)
    upper = (s >= t).astype(BF16)
    cum_f = jnp.zeros_like(v)
    cum_b = jnp.zeros_like(v)
    for piece in _split3(log_f):
        cum_f = cum_f + jnp.dot(lower, piece, preferred_element_type=F32)
        cum_b = cum_b + jnp.dot(upper, piece, preferred_element_type=F32)
    out = jnp.where(is_f, jnp.where(is_bwd, cum_b, cum_f), v)
    col_ref[0] = out
    row_ref[0, 0] = out.T[0:row_ref.shape[2], :]


def _gate_prep(gates3, bias_row):
    bsz, seq, _ = gates3.shape
    nc = seq // CHUNK
    rows = 8 * MLSTM_HEADS
    return pl.pallas_call(
        _gate_prep_kernel,
        out_shape=(
            jax.ShapeDtypeStruct((bsz, seq, GATE_LANES), F32),
            jax.ShapeDtypeStruct((bsz, nc, rows, CHUNK), F32),
        ),
        grid=(bsz, nc),
        in_specs=[
            pl.BlockSpec((1, CHUNK, GATE_LANES), lambda b, c: (b, c, 0)),
            pl.BlockSpec((1, GATE_LANES), lambda b, c: (0, 0)),
        ],
        out_specs=(
            pl.BlockSpec((1, CHUNK, GATE_LANES), lambda b, c: (b, c, 0)),
            pl.BlockSpec((1, 1, rows, CHUNK), lambda b, c: (b, c, 0, 0)),
        ),
        compiler_params=pltpu.CompilerParams(dimension_semantics=("arbitrary", "arbitrary")),
        name="gate_prep",
    )(gates3, bias_row)


def _mlstm_chunk(qc, kc, vv, i_row, b_row, i_col, b_col, g, c_ref, n_ref, m_ref, causal_mask):
    c_state = c_ref[...]
    n_state = n_ref[...]
    m_state = m_ref[...]
    log_d = jnp.where(causal_mask, b_col - b_row + i_row, NEG)
    log_inter = b_col + m_state
    m_t = jnp.maximum(log_inter, jnp.max(log_d, axis=-1, keepdims=True))
    d_mat = jnp.exp(log_d - m_t)
    w_inter = jnp.exp(log_inter - m_t)
    scores = lax.dot_general(qc, kc, (((1,), (1,)), ((), ())), preferred_element_type=F32) * d_mat
    num = jnp.dot(scores.astype(BF16), vv, preferred_element_type=F32)
    num = num + w_inter * jnp.dot(qc, c_state.astype(BF16), preferred_element_type=F32)
    q_n = jnp.sum(qc.astype(F32) * n_state, axis=-1, keepdims=True)
    den = jnp.sum(scores, axis=-1, keepdims=True) + w_inter * q_n
    h = num * (1.0 / jnp.maximum(jnp.abs(den), jnp.exp(-m_t)))

    log_w = g - b_col + i_col
    m_new = jnp.maximum(g + m_state, jnp.max(log_w, axis=0, keepdims=True))
    w = jnp.exp(log_w - m_new)
    decay = jnp.exp(g + m_state - m_new)
    kw = kc.astype(F32) * w
    c_ref[...] = decay * c_state + lax.dot_general(
        kw.astype(BF16), vv, (((0,), (0,)), ((), ())), preferred_element_type=F32)
    n_ref[...] = decay * n_state + jnp.sum(kw, axis=0, keepdims=True)
    m_ref[...] = m_new
    return h


def _mlstm_kernel(q_ref, k_ref, v_ref, o_ref, z_ref, gcol_ref, grow_ref, cwq_ref, cwk_ref, cbq_ref, cbk_ref,
                  gain_ref, y_ref, stage_ref, qc_ref, kc_ref, h_ref, gc_ref, cf_ref, nf_ref, mf_ref,
                  cb_ref, nb_ref, mb_ref):
    seq, d = q_ref.shape[1], q_ref.shape[2]
    nc = seq // CHUNK
    head = pl.program_id(1)

    stage_ref[0:8, :] = jnp.zeros((8, d), F32)
    stage_ref[8 + seq:16 + seq, :] = jnp.zeros((8, d), F32)

    def conv_silu(src_ref, w_ref, b_ref, dst_ref, out_scale):
        def fill(c, carry):
            r = pl.multiple_of(c * CHUNK, CHUNK)
            stage_ref[pl.ds(8 + r, CHUNK), :] = src_ref[0, pl.ds(r, CHUNK), :].astype(F32)
            return carry

        lax.fori_loop(0, nc, fill, 0)
        w0, w1, w2 = w_ref[0:1, :], w_ref[1:2, :], w_ref[2:3, :]
        bias = b_ref[...]

        def conv(c, carry):
            r = pl.multiple_of(c * CHUNK, CHUNK)
            win = stage_ref[pl.ds(r, CHUNK + 16), :]
            y = win[7:7 + CHUNK] * w0 + win[8:8 + CHUNK] * w1 + win[9:9 + CHUNK] * w2 + bias
            dst_ref[pl.ds(r, CHUNK), :] = (_silu(y) * out_scale).astype(BF16)
            return carry

        lax.fori_loop(0, nc, conv, 0)

    conv_silu(q_ref, cwq_ref, cbq_ref, qc_ref, 1.0)
    conv_silu(k_ref, cwk_ref, cbk_ref, kc_ref, MLSTM_HEAD_DIM ** -0.5)

    gc_ref[...] = pltpu.roll(gcol_ref[0], (GATE_LANES - 8 * head) % GATE_LANES, 1)

    for ref in (cf_ref, nf_ref, mf_ref, cb_ref, nb_ref, mb_ref):
        ref[...] = jnp.zeros(ref.shape, F32)

    t_idx = lax.broadcasted_iota(jnp.int32, (CHUNK, CHUNK), 0)
    s_idx = lax.broadcasted_iota(jnp.int32, (CHUNK, CHUNK), 1)
    mask_f = s_idx <= t_idx
    mask_b = s_idx >= t_idx

    def direction(chunk, backward):
        r = pl.multiple_of(chunk * CHUNK, CHUNK)
        rows = grow_ref[0, chunk]
        cols = gc_ref[pl.ds(r, CHUNK), :]
        qc = qc_ref[pl.ds(r, CHUNK), :]
        kc = kc_ref[pl.ds(r, CHUNK), :]
        vv = v_ref[0, pl.ds(r, CHUNK), :]
        if backward:
            i_row, b_row = rows[2:3, :], rows[3:4, :]
            i_col, b_col = cols[:, 2:3], cols[:, 3:4]
            g = b_col[0:1, :]
            return _mlstm_chunk(qc, kc, vv, i_row, b_row, i_col, b_col, g, cb_ref, nb_ref, mb_ref, mask_b), r
        i_row, b_row = rows[0:1, :], rows[1:2, :]
        i_col, b_col = cols[:, 0:1], cols[:, 1:2]
        g = b_col[CHUNK - 1:CHUNK, :]
        return _mlstm_chunk(qc, kc, vv, i_row, b_row, i_col, b_col, g, cf_ref, nf_ref, mf_ref, mask_f), r

    def first_half(j, carry):
        hf, rf = direction(j, False)
        h_ref[pl.ds(rf, CHUNK), :] = hf
        hb, rb = direction(nc - 1 - j, True)
        h_ref[pl.ds(rb, CHUNK), :] = hb
        return carry

    gain = gain_ref[...]

    def finish(h_dir, r):
        h_sum = (h_ref[pl.ds(r, CHUNK), :] + h_dir) * _sigmoid(o_ref[0, pl.ds(r, CHUNK), :].astype(F32))
        normed = h_sum * lax.rsqrt(jnp.mean(h_sum * h_sum, axis=-1, keepdims=True) + EPS) * gain
        y_ref[0, pl.ds(r, CHUNK), :] = (normed * _silu(z_ref[0, pl.ds(r, CHUNK), :].astype(F32))).astype(BF16)

    def second_half(j, carry):
        finish(*direction(j, False))
        finish(*direction(nc - 1 - j, True))
        return carry

    lax.fori_loop(0, nc // 2, first_half, 0)
    lax.fori_loop(nc // 2, nc, second_half, 0)


def _mlstm(proj3, gcol, grow, conv_w, conv_b, head_gain):
    bsz, seq, _ = proj3.shape
    d = MLSTM_HEAD_DIM
    nh = MLSTM_HEADS
    nc = seq // CHUNK

    def col_block(offset):
        return pl.BlockSpec((1, seq, d), lambda b, h: (b, 0, offset + h))

    return pl.pallas_call(
        _mlstm_kernel,
        out_shape=jax.ShapeDtypeStruct((bsz, seq, nh * d), BF16),
        grid=(bsz, nh),
        in_specs=[
            col_block(0), col_block(nh), col_block(2 * nh), col_block(3 * nh), col_block(4 * nh),
            pl.BlockSpec((1, seq, GATE_LANES), lambda b, h: (b, 0, 0)),
            pl.BlockSpec((1, nc, 8, CHUNK), lambda b, h: (b, 0, h, 0)),
            pl.BlockSpec((3, d), lambda b, h: (0, h)),
            pl.BlockSpec((3, d), lambda b, h: (0, nh + h)),
            pl.BlockSpec((1, d), lambda b, h: (0, h)),
            pl.BlockSpec((1, d), lambda b, h: (0, nh + h)),
            pl.BlockSpec((1, d), lambda b, h: (0, h)),
        ],
        out_specs=pl.BlockSpec((1, seq, d), lambda b, h: (b, 0, h)),
        scratch_shapes=[
            pltpu.VMEM((seq + 16, d), F32),
            pltpu.VMEM((seq, d), BF16),
            pltpu.VMEM((seq, d), BF16),
            pltpu.VMEM((seq, d), F32),
            pltpu.VMEM((seq, GATE_LANES), F32),
            pltpu.VMEM((d, d), F32), pltpu.VMEM((1, d), F32), pltpu.VMEM((1, 1), F32),
            pltpu.VMEM((d, d), F32), pltpu.VMEM((1, d), F32), pltpu.VMEM((1, 1), F32),
        ],
        compiler_params=pltpu.CompilerParams(
            dimension_semantics=("arbitrary", "arbitrary"), vmem_limit_bytes=VMEM_LIMIT_BYTES),
        name="mlstm",
    )(proj3, proj3, proj3, proj3, proj3, gcol, grow, conv_w, conv_w,
      conv_b.reshape(1, -1), conv_b.reshape(1, -1), head_gain.reshape(1, -1))


def _natten_bias_tables(rpb, rows):
    q_i, q_col, q_kb = [], [], []
    for (c0, w), kb in zip(Q_BLOCKS, K_STARTS):
        for i in range(Q_ROWS):
            for j in range(w):
                q_i.append(i)
                q_col.append(c0 + j)
                q_kb.append(kb)
    q_i, q_col, q_kb = (np.asarray(a)[:, None] for a in (q_i, q_col, q_kb))
    key = np.arange(SLAB_ROWS * K_WIDTH)[None, :]
    k_a, k_c = key // K_WIDTH, key % K_WIDTH
    k_col = q_kb + k_c
    win = np.clip(q_col - NA_COLS // 2, 0, GRID_W - NA_COLS)
    col_ok = (k_col >= win) & (k_col < win + NA_COLS)
    dc = np.clip(k_col - q_col + NA_COLS - 1, 0, 2 * NA_COLS - 2)
    idx, valid = [], []
    for slab_off, rs_rel in ((0, np.maximum(q_i - NA_ROWS // 2, 0)),
                             (-(NA_ROWS // 2), q_i - NA_ROWS // 2),
                             (Q_ROWS - SLAB_ROWS, np.minimum(q_i - NA_ROWS // 2, 0))):
        k_row = slab_off + k_a
        ok = (k_row >= rs_rel) & (k_row < rs_rel + NA_ROWS) & col_ok
        dr = np.clip(k_row - q_i + NA_ROWS - 1, 0, 2 * NA_ROWS - 2)
        idx.append(dr * (2 * NA_COLS - 1) + dc)
        valid.append(ok)
    idx = jnp.asarray(np.stack(idx).astype(np.int32))
    valid = jnp.asarray(np.stack(valid))
    flat = rpb.astype(F32).reshape(NA_HEADS, -1)
    table = jnp.take(flat, idx, axis=1)
    return jnp.where(valid[None], table, NEG).transpose(1, 0, 2, 3)


def _natten_kernel(q_ref, k0_ref, k1_ref, k2_ref, k3_ref, v0_ref, v1_ref, v2_ref, v3_ref, z_ref, tab_ref,
                   y_ref, qf_ref, of_ref):
    k_refs = (k0_ref, k1_ref, k2_ref, k3_ref)
    v_refs = (v0_ref, v1_ref, v2_ref, v3_ref)
    piece_rows = SLAB_ROWS // 4
    qf_ref[...] = q_ref[0, 0].astype(F32) * (NA_HEAD_DIM ** -0.5)
    row_off = 0
    for (c0, w), kb in zip(Q_BLOCKS, K_STARTS):
        nq = Q_ROWS * w
        lane = lax.broadcasted_iota(jnp.int32, (nq, LANES), 1)
        first_head = lane < NA_HEAD_DIM
        for pair in range(NA_HEADS // 2):
            lanes = pl.ds(pair * LANES, LANES)
            qp = jnp.concatenate([qf_ref[pl.ds(i * GRID_W + c0, w), lanes] for i in range(Q_ROWS)], axis=0)
            qs = jnp.concatenate([jnp.where(first_head, qp, 0.0), jnp.where(first_head, 0.0, qp)],
                                 axis=0).astype(BF16)
            kk = jnp.concatenate(
                [k_refs[a // piece_rows][0, 0, pl.ds((a % piece_rows) * GRID_W + kb, K_WIDTH), lanes]
                 for a in range(SLAB_ROWS)], axis=0)
            vv = jnp.concatenate(
                [v_refs[a // piece_rows][0, 0, pl.ds((a % piece_rows) * GRID_W + kb, K_WIDTH), lanes]
                 for a in range(SLAB_ROWS)], axis=0)
            s = lax.dot_general(qs, kk, (((1,), (1,)), ((), ())), preferred_element_type=F32)
            bias = jnp.concatenate([tab_ref[0, 2 * pair, pl.ds(row_off, nq), :],
                                    tab_ref[0, 2 * pair + 1, pl.ds(row_off, nq), :]], axis=0)
            s = s + bias
            p = jnp.exp(s - jnp.max(s, axis=-1, keepdims=True))
            o2 = jnp.dot(p.astype(BF16), vv, preferred_element_type=F32)
            o2 = o2 * (1.0 / jnp.sum(p, axis=-1, keepdims=True))
            o = jnp.where(first_head, o2[0:nq], o2[nq:2 * nq])
            for i in range(Q_ROWS):
                of_ref[pl.ds(i * GRID_W + c0, w), lanes] = o[i * w:(i + 1) * w]
        row_off += nq
    y_ref[0, 0] = (of_ref[...] * _silu(z_ref[0, 0].astype(F32))).astype(BF16)


def _natten(proj3, tables):
    bsz, seq, n_main = proj3.shape
    q_tok = Q_ROWS * GRID_W
    p_tok = (SLAB_ROWS // 4) * GRID_W
    groups = seq // q_tok
    pieces = seq // p_tok
    w = NA_WIDTH
    qb, kb, vb, zb = (5 * D_MODEL) // w, (5 * D_MODEL) // w + 1, (5 * D_MODEL) // w + 2, (5 * D_MODEL) // w + 3
    proj_q = proj3.reshape(bsz, groups, q_tok, n_main)
    proj_p = proj3.reshape(bsz, pieces, p_tok, n_main)

    def slab_start(g):
        return jnp.clip(2 * g - 1, 0, pieces - 4)

    def piece_spec(col, a):
        return pl.BlockSpec((1, 1, p_tok, w), lambda g, b: (b, slab_start(g) + a, 0, col))

    def variant(g):
        return (g > 0).astype(jnp.int32) + (g == groups - 1).astype(jnp.int32)

    nq_total = Q_ROWS * GRID_W
    return pl.pallas_call(
        _natten_kernel,
        out_shape=jax.ShapeDtypeStruct((bsz, groups, q_tok, w), BF16),
        grid=(groups, bsz),
        in_specs=[pl.BlockSpec((1, 1, q_tok, w), lambda g, b: (b, g, 0, qb))]
        + [piece_spec(kb, a) for a in range(4)] + [piece_spec(vb, a) for a in range(4)]
        + [pl.BlockSpec((1, 1, q_tok, w), lambda g, b: (b, g, 0, zb)),
           pl.BlockSpec((1, NA_HEADS, nq_total, SLAB_ROWS * K_WIDTH), lambda g, b: (variant(g), 0, 0, 0))],
        out_specs=pl.BlockSpec((1, 1, q_tok, w), lambda g, b: (b, g, 0, 0)),
        scratch_shapes=[pltpu.VMEM((q_tok, w), F32), pltpu.VMEM((q_tok, w), F32)],
        compiler_params=pltpu.CompilerParams(
            dimension_semantics=("arbitrary", "arbitrary"), vmem_limit_bytes=VMEM_LIMIT_BYTES),
        name="natten",
    )(proj_q, *([proj_p] * 8), proj_q, tables).reshape(bsz, seq, w)


def _merge_out_kernel(ya_ref, yb_ref, g0_ref, g1_ref, x_ref, mod_ref, bm_ref, wa_ref, wb_ref, wo_ref, fg_ref,
                      out_ref):
    d = x_ref.shape[1]
    pa = jnp.dot(ya_ref[...], wa_ref[...], preferred_element_type=F32)
    pb = jnp.dot(yb_ref[...], wb_ref[...], preferred_element_type=F32)
    merged = (_sigmoid(g0_ref[...].astype(F32) + bm_ref[0:1, :]) * pa
              + _sigmoid(g1_ref[...].astype(F32) + bm_ref[1:2, :]) * pb)
    mixed = jnp.dot(merged.astype(BF16), wo_ref[...], preferred_element_type=F32)
    xo = x_ref[...] + mod_ref[0, :, 2 * d:3 * d] * mixed
    out_ref[...] = xo * lax.rsqrt(jnp.mean(xo * xo, axis=-1, keepdims=True) + EPS) * fg_ref[...]


def _merge_out(ya2, yb2, proj2, x2, mod3, b_merge, wa, wb, wo, final_gain, seq, tm):
    n_tok, d = x2.shape
    g0 = (N_MAIN - 2 * d) // d

    def full(shape):
        return pl.BlockSpec(shape, lambda m: (0,) * len(shape))

    return pl.pallas_call(
        _merge_out_kernel,
        out_shape=jax.ShapeDtypeStruct((n_tok, d), F32),
        grid=(n_tok // tm,),
        in_specs=[
            pl.BlockSpec((tm, d), lambda m: (m, 0)),
            pl.BlockSpec((tm, NA_WIDTH), lambda m: (m, 0)),
            pl.BlockSpec((tm, d), lambda m: (m, g0)),
            pl.BlockSpec((tm, d), lambda m: (m, g0 + 1)),
            pl.BlockSpec((tm, d), lambda m: (m, 0)),
            pl.BlockSpec((1, 1, 3 * d), lambda m: ((m * tm) // seq, 0, 0)),
            full((2, d)), full((d, d)), full((NA_WIDTH, d)), full((d, d)), full((1, d)),
        ],
        out_specs=pl.BlockSpec((tm, d), lambda m: (m, 0)),
        compiler_params=pltpu.CompilerParams(
            dimension_semantics=("arbitrary",), vmem_limit_bytes=VMEM_LIMIT_BYTES),
        name="merge_out",
    )(ya2, yb2, proj2, proj2, x2, mod3, b_merge, wa, wb, wo, final_gain.reshape(1, d))


def _gate_layout(n_cols):
    src = np.arange(n_cols)
    direction, gate, head = src // (2 * MLSTM_HEADS), (src // MLSTM_HEADS) % 2, src % MLSTM_HEADS
    return 8 * head + 2 * direction + gate


def _layer(x, mod, norm_gain, w_in, conv_w, conv_b, b_igate, b_fgate, mlstm_norm_gain, rpb, w_proj_a, w_proj_b,
           b_merge, w_out, final_gain):
    bsz, seq, d = x.shape
    n_tok = bsz * seq
    wa = MLSTM_HEADS * MLSTM_HEAD_DIM
    n_gates = 4 * MLSTM_HEADS
    g_lo = 5 * wa
    w_main = jnp.concatenate([w_in[:, :g_lo], w_in[:, g_lo + n_gates:]], axis=1).astype(BF16)
    lanes = _gate_layout(n_gates)
    w_gate = jnp.zeros((d, GATE_LANES), F32).at[:, lanes].set(w_in[:, g_lo:g_lo + n_gates]).astype(BF16)
    gate_bias = jnp.zeros((1, GATE_LANES), F32).at[0, lanes].set(
        jnp.stack([b_igate, b_fgate], axis=1).reshape(-1))

    x2 = x.reshape(n_tok, d)
    mod3 = mod.reshape(bsz, 1, 3 * d)
    proj2, gates2 = _in_proj(x2, mod3, norm_gain, w_main, w_gate, seq, tm=min(2048, seq))
    proj3 = proj2.reshape(bsz, seq, N_MAIN)
    gcol, grow = _gate_prep(gates2.reshape(bsz, seq, GATE_LANES), gate_bias)
    y_a = _mlstm(proj3, gcol, grow, conv_w, conv_b, mlstm_norm_gain)
    y_b = _natten(proj3, _natten_bias_tables(rpb, seq // GRID_W))
    out2 = _merge_out(y_a.reshape(n_tok, wa), y_b.reshape(n_tok, NA_WIDTH), proj2, x2, mod3, b_merge,
                      w_proj_a.astype(BF16), w_proj_b.astype(BF16), w_out.astype(BF16), final_gain, seq, tm=512)
    return out2.reshape(bsz, seq, d)


def kernel(x, c, w_ada, b_ada, norm_gain, w_in, conv_w, conv_b, b_igate, b_fgate, mlstm_norm_gain, rpb, w_proj_a,
           w_proj_b, b_merge, w_out, final_gain):
    depth = w_ada.shape[0]
    assert depth == 1, "the fused final norm assumes a single layer"
    mod = _adaln_mod(c, w_ada[0], b_ada[0])
    return _layer(x, mod, norm_gain[0], w_in[0], conv_w[0], conv_b[0], b_igate[0], b_fgate[0], mlstm_norm_gain[0],
                  rpb[0], w_proj_a[0], w_proj_b[0], b_merge[0], w_out[0], final_gain)
```

```python
import functools

import numpy as np
import jax
import jax.numpy as jnp
from jax import lax
from jax.experimental import pallas as pl
from jax.experimental.pallas import tpu as pltpu

F32 = jnp.float32
BF16 = jnp.bfloat16

D_MODEL = 1024
GRID_W = 64
MLSTM_HEADS = 4
MLSTM_HEAD_DIM = 256
NA_HEADS = 8
NA_WIDTH = 512
NA_HEAD_DIM = 64
NA_ROWS = 8
NA_COLS = 16
EPS = 1e-6
NEG = -1e30
LOG2E = 1.4426950408889634

VMEM_LIMIT_BYTES = 56 * 1024 * 1024
LANES = 128

CHUNK = 256

N_MAIN = 9 * D_MODEL
GATE_LANES = 128

Q_ROWS = 8
SLAB_ROWS = 16
Q_BLOCKS = ((0, 24), (24, 16), (40, 24))
Q_BLOCK_MAX = 24
K_STARTS = (0, 16, 32)
K_WIDTH = 32
N_KEYS = SLAB_ROWS * K_WIDTH


def _sigmoid(x):
    return 1.0 / (1.0 + jnp.exp(-x))


def _silu(x):
    return x * _sigmoid(x)


def _adaln_kernel(c_ref, w_ref, b_ref, o_ref):
    cond = _silu(c_ref[...]).astype(BF16)
    o_ref[...] = jnp.dot(cond, w_ref[...].astype(BF16), preferred_element_type=F32) + b_ref[...]


def _adaln_mod(c, w_ada, b_ada):
    bsz, d = c.shape
    n = w_ada.shape[1]
    tn = 512
    return pl.pallas_call(
        _adaln_kernel,
        out_shape=jax.ShapeDtypeStruct((bsz, n), F32),
        grid=(n // tn,),
        in_specs=[
            pl.BlockSpec((bsz, d), lambda j: (0, 0)),
            pl.BlockSpec((d, tn), lambda j: (0, j)),
            pl.BlockSpec((1, tn), lambda j: (0, j)),
        ],
        out_specs=pl.BlockSpec((bsz, tn), lambda j: (0, j)),
        compiler_params=pltpu.CompilerParams(dimension_semantics=("arbitrary",)),
        name="adaln_mod",
    )(c, w_ada, b_ada.reshape(1, n))


def _in_proj_kernel(x_ref, mod_ref, gain_ref, w_ref, wg_ref, proj_ref, gates_ref, h_ref, *, rows):
    d = x_ref.shape[1]

    @pl.when(pl.program_id(1) == 0)
    def _():
        shift = mod_ref[0, :, 0:d]
        scale = mod_ref[0, :, d:2 * d]
        gain = gain_ref[...]
        for r in range(0, x_ref.shape[0], rows):
            xf = x_ref[r:r + rows, :]
            y = xf * lax.rsqrt(jnp.mean(xf * xf, axis=-1, keepdims=True) + EPS) * gain
            h_ref[r:r + rows, :] = (y * (1.0 + scale) + shift).astype(BF16)
        gates_ref[...] = jnp.dot(h_ref[...], wg_ref[...], preferred_element_type=F32)

    proj_ref[...] = jnp.dot(h_ref[...], w_ref[...], preferred_element_type=F32).astype(BF16)


def _in_proj(x2, mod3, norm_gain, w_main, w_gate, seq, tm):
    n_tok, d = x2.shape
    tn = D_MODEL
    return pl.pallas_call(
        functools.partial(_in_proj_kernel, rows=256),
        out_shape=(
            jax.ShapeDtypeStruct((n_tok, N_MAIN), BF16),
            jax.ShapeDtypeStruct((n_tok, GATE_LANES), F32),
        ),
        grid=(n_tok // tm, N_MAIN // tn),
        in_specs=[
            pl.BlockSpec((tm, d), lambda m, n: (m, 0)),
            pl.BlockSpec((1, 1, 3 * d), lambda m, n: ((m * tm) // seq, 0, 0)),
            pl.BlockSpec((1, d), lambda m, n: (0, 0)),
            pl.BlockSpec((d, tn), lambda m, n: (0, n)),
            pl.BlockSpec((d, GATE_LANES), lambda m, n: (0, 0)),
        ],
        out_specs=(
            pl.BlockSpec((tm, tn), lambda m, n: (m, n)),
            pl.BlockSpec((tm, GATE_LANES), lambda m, n: (m, 0)),
        ),
        scratch_shapes=[pltpu.VMEM((tm, d), BF16)],
        compiler_params=pltpu.CompilerParams(
            dimension_semantics=("arbitrary", "arbitrary"), vmem_limit_bytes=VMEM_LIMIT_BYTES),
        name="in_proj",
    )(x2, mod3, norm_gain.reshape(1, d), w_main, w_gate)


def _split3(x):
    hi = x.astype(BF16)
    r1 = x - hi.astype(F32)
    mid = r1.astype(BF16)
    lo = (r1 - mid.astype(F32)).astype(BF16)
    return hi, mid, lo


def _gate_prep_kernel(g_ref, bias_ref, col_ref, row_ref):
    v = g_ref[0] + bias_ref[...]
    length = v.shape[0]
    lane = lax.broadcasted_iota(jnp.int32, v.shape, 1)
    tok = lax.broadcasted_iota(jnp.int32, v.shape, 0)
    is_f = (lane % 2) == 1
    is_bwd = ((lane // 2) % 2) == 1
    log_f = jnp.where(is_f, (jnp.minimum(v, 0.0) - jnp.log1p(jnp.exp(-jnp.abs(v)))) * LOG2E, 0.0)
    t = lax.broadcasted_iota(jnp.int32, (length, length), 0)
    s = lax.broadcasted_iota(jnp.int32, (length, length), 1)
    lower = (s <= t).astype(BF16)
    upper = (s >= t).astype(BF16)
    cum_f = jnp.zeros_like(v)
    cum_b = jnp.zeros_like(v)
    for piece in _split3(log_f):
        cum_f = cum_f + jnp.dot(lower, piece, preferred_element_type=F32)
        cum_b = cum_b + jnp.dot(upper, piece, preferred_element_type=F32)
    b = jnp.where(is_bwd, cum_b, cum_f)
    r = v * LOG2E - pltpu.roll(b, GATE_LANES - 1, 1)
    pre, suf = r, r
    step = 1
    while step < length:
        pre = jnp.maximum(pre, jnp.where(tok >= step, pltpu.roll(pre, step, 0), NEG))
        suf = jnp.maximum(suf, jnp.where(tok < length - step, pltpu.roll(suf, length - step, 0), NEG))
        step *= 2
    cm = jnp.where(is_bwd, suf, pre)
    col_ref[0] = jnp.where(is_f, b, cm)
    row_ref[0, 0] = r.T[0:row_ref.shape[2], :]


def _gate_prep(gates3, bias_row):
    bsz, seq, _ = gates3.shape
    nc = seq // CHUNK
    rows = 8 * MLSTM_HEADS
    return pl.pallas_call(
        _gate_prep_kernel,
        out_shape=(
            jax.ShapeDtypeStruct((bsz, seq, GATE_LANES), F32),
            jax.ShapeDtypeStruct((bsz, nc, rows, CHUNK), F32),
        ),
        grid=(bsz, nc),
        in_specs=[
            pl.BlockSpec((1, CHUNK, GATE_LANES), lambda b, c: (b, c, 0)),
            pl.BlockSpec((1, GATE_LANES), lambda b, c: (0, 0)),
        ],
        out_specs=(
            pl.BlockSpec((1, CHUNK, GATE_LANES), lambda b, c: (b, c, 0)),
            pl.BlockSpec((1, 1, rows, CHUNK), lambda b, c: (b, c, 0, 0)),
        ),
        compiler_params=pltpu.CompilerParams(dimension_semantics=("arbitrary", "arbitrary")),
        name="gate_prep",
    )(gates3, bias_row)


def _mlstm_chunk(qc, kt, vv, r_row, cm_rep, b_rep, cm_end, g, cn_ref, m_ref, causal_mask):
    d = vv.shape[1]
    cn = cn_ref[...]
    m_state = m_ref[...]
    m_col = jnp.maximum(m_state, cm_rep)
    w_inter = jnp.exp2(m_state - m_col)
    m_wide = jnp.concatenate([m_col, m_col], axis=1)
    d_mat = jnp.where(causal_mask, jnp.exp2(r_row - m_wide), 0.0)
    scores = jnp.dot(qc, kt, preferred_element_type=F32) * d_mat
    inter = jnp.dot(qc, cn.astype(BF16), preferred_element_type=F32)
    w_wide = jnp.concatenate([w_inter, w_inter], axis=1)
    num = jnp.dot(scores.astype(BF16), vv, preferred_element_type=F32) + w_wide * inter[:, 0:d]
    den = jnp.sum(scores, axis=-1, keepdims=True) + w_inter * inter[:, d:]
    inv = 1.0 / jnp.maximum(jnp.abs(den), jnp.exp2(-(b_rep + m_col)))
    h = num * jnp.concatenate([inv, inv], axis=1)

    m_end = jnp.maximum(m_state, cm_end)
    w_row = jnp.exp2(r_row - jnp.concatenate([m_end, m_end], axis=1))
    decay = jnp.exp2(m_state - m_end)
    kw = kt * w_row.astype(BF16)
    v_aug = jnp.concatenate([vv, jnp.ones((vv.shape[0], LANES), BF16)], axis=1)
    decay_wide = jnp.concatenate([decay, decay, decay], axis=1)
    cn_ref[...] = decay_wide * cn + jnp.dot(kw, v_aug, preferred_element_type=F32)
    m_ref[...] = g + m_end
    return h


def _mlstm_kernel(q_ref, k_ref, v_ref, o_ref, z_ref, gcol_ref, grow_ref, cwq_ref, cwk_ref, cbq_ref, cbk_ref,
                  gain_ref, y_ref, stage_ref, qc_ref, kt_ref, h_ref, gc_ref, cnf_ref, mf_ref, cnb_ref, mb_ref):
    seq, d = q_ref.shape[1], q_ref.shape[2]
    nc = seq // CHUNK
    head = pl.program_id(1)

    stage_ref[0:8, :] = jnp.zeros((8, d), F32)
    stage_ref[8 + seq:16 + seq, :] = jnp.zeros((8, d), F32)

    def conv_silu(src_ref, w_ref, b_ref, store):
        def fill(c, carry):
            r = pl.multiple_of(c * CHUNK, CHUNK)
            stage_ref[pl.ds(8 + r, CHUNK), :] = src_ref[0, pl.ds(r, CHUNK), :].astype(F32)
            return carry

        lax.fori_loop(0, nc, fill, 0)
        w0, w1, w2 = w_ref[0:1, :], w_ref[1:2, :], w_ref[2:3, :]
        bias = b_ref[...]

        def conv(c, carry):
            r = pl.multiple_of(c * CHUNK, CHUNK)
            win = stage_ref[pl.ds(r, CHUNK + 16), :]
            y = win[7:7 + CHUNK] * w0 + win[8:8 + CHUNK] * w1 + win[9:9 + CHUNK] * w2 + bias
            store(c, r, _silu(y))
            return carry

        lax.fori_loop(0, nc, conv, 0)

    def store_q(c, r, y):
        qc_ref[pl.ds(r, CHUNK), :] = y.astype(BF16)

    def store_kt(c, r, y):
        kt_ref[c] = (y * (MLSTM_HEAD_DIM ** -0.5)).T.astype(BF16)

    conv_silu(q_ref, cwq_ref, cbq_ref, store_q)
    conv_silu(k_ref, cwk_ref, cbk_ref, store_kt)

    gc_ref[...] = pltpu.roll(gcol_ref[0], (GATE_LANES - 8 * head) % GATE_LANES, 1)

    for ref in (cnf_ref, mf_ref, cnb_ref, mb_ref):
        ref[...] = jnp.zeros(ref.shape, F32)

    t_idx = lax.broadcasted_iota(jnp.int32, (CHUNK, CHUNK), 0)
    s_idx = lax.broadcasted_iota(jnp.int32, (CHUNK, CHUNK), 1)
    mask_f = s_idx <= t_idx
    mask_b = s_idx >= t_idx

    def replicate(col):
        return jnp.broadcast_to(col, (col.shape[0], LANES))

    def direction(chunk, backward):
        r = pl.multiple_of(chunk * CHUNK, CHUNK)
        rows = grow_ref[0, chunk]
        cols = gc_ref[pl.ds(r, CHUNK), :]
        qc = qc_ref[pl.ds(r, CHUNK), :]
        kt = kt_ref[chunk]
        vv = v_ref[0, pl.ds(r, CHUNK), :]
        if backward:
            cm_rep, b_rep = replicate(cols[:, 2:3]), replicate(cols[:, 3:4])
            end = 0
            state, mask, r_row = (cnb_ref, mb_ref), mask_b, rows[2:3, :]
        else:
            cm_rep, b_rep = replicate(cols[:, 0:1]), replicate(cols[:, 1:2])
            end = CHUNK - 1
            state, mask, r_row = (cnf_ref, mf_ref), mask_f, rows[0:1, :]
        cm_end, g = cm_rep[end:end + 1, :], b_rep[end:end + 1, :]
        return _mlstm_chunk(qc, kt, vv, r_row, cm_rep, b_rep, cm_end, g, *state, mask), r

    def first_half(j, carry):
        hf, rf = direction(j, False)
        h_ref[pl.ds(rf, CHUNK), :] = hf
        hb, rb = direction(nc - 1 - j, True)
        h_ref[pl.ds(rb, CHUNK), :] = hb
        return carry

    gain = gain_ref[...]

    def finish(h_dir, r):
        h_sum = (h_ref[pl.ds(r, CHUNK), :] + h_dir) * _sigmoid(o_ref[0, pl.ds(r, CHUNK), :].astype(F32))
        normed = h_sum * lax.rsqrt(jnp.mean(h_sum * h_sum, axis=-1, keepdims=True) + EPS) * gain
        y_ref[0, pl.ds(r, CHUNK), :] = (normed * _silu(z_ref[0, pl.ds(r, CHUNK), :].astype(F32))).astype(BF16)

    def second_half(j, carry):
        finish(*direction(j, False))
        finish(*direction(nc - 1 - j, True))
        return carry

    lax.fori_loop(0, nc // 2, first_half, 0)
    lax.fori_loop(nc // 2, nc, second_half, 0)


def _mlstm(proj3, gcol, grow, conv_w, conv_b, head_gain):
    bsz, seq, _ = proj3.shape
    d = MLSTM_HEAD_DIM
    nh = MLSTM_HEADS
    nc = seq // CHUNK

    def col_block(offset):
        return pl.BlockSpec((1, seq, d), lambda b, h: (b, 0, offset + h))

    return pl.pallas_call(
        _mlstm_kernel,
        out_shape=jax.ShapeDtypeStruct((bsz, seq, nh * d), BF16),
        grid=(bsz, nh),
        in_specs=[
            col_block(0), col_block(nh), col_block(2 * nh), col_block(3 * nh), col_block(4 * nh),
            pl.BlockSpec((1, seq, GATE_LANES), lambda b, h: (b, 0, 0)),
            pl.BlockSpec((1, nc, 8, CHUNK), lambda b, h: (b, 0, h, 0)),
            pl.BlockSpec((3, d), lambda b, h: (0, h)),
            pl.BlockSpec((3, d), lambda b, h: (0, nh + h)),
            pl.BlockSpec((1, d), lambda b, h: (0, h)),
            pl.BlockSpec((1, d), lambda b, h: (0, nh + h)),
            pl.BlockSpec((1, d), lambda b, h: (0, h)),
        ],
        out_specs=pl.BlockSpec((1, seq, d), lambda b, h: (b, 0, h)),
        scratch_shapes=[
            pltpu.VMEM((seq + 16, d), F32),
            pltpu.VMEM((seq, d), BF16),
            pltpu.VMEM((nc, d, CHUNK), BF16),
            pltpu.VMEM((seq, d), F32),
            pltpu.VMEM((seq, GATE_LANES), F32),
            pltpu.VMEM((d, d + LANES), F32), pltpu.VMEM((1, LANES), F32),
            pltpu.VMEM((d, d + LANES), F32), pltpu.VMEM((1, LANES), F32),
        ],
        compiler_params=pltpu.CompilerParams(
            dimension_semantics=("arbitrary", "arbitrary"), vmem_limit_bytes=VMEM_LIMIT_BYTES),
        name="mlstm",
    )(proj3, proj3, proj3, proj3, proj3, gcol, grow, conv_w, conv_w,
      conv_b.reshape(1, -1), conv_b.reshape(1, -1), head_gain.reshape(1, -1))


def _natten_bias_rows(rpb):
    n_dc = 2 * NA_COLS - 1
    n_dr = 2 * NA_ROWS - 1
    left = NA_COLS
    period = 64
    padded = jnp.pad(rpb.astype(F32), ((0, 0), (0, 0), (left, period - left - n_dc)), constant_values=NEG)
    out = []
    for (c0, w), kb in zip(Q_BLOCKS, K_STARTS):
        base = kb - c0 + NA_COLS - 1 + left
        toe = jnp.tile(padded, (1, 1, w))[..., :(period - 1) * w].reshape(NA_HEADS, n_dr, w, period - 1)
        toe = toe[..., base:base + K_WIDTH]
        q_col = c0 + np.arange(w)[:, None]
        k_col = kb + np.arange(K_WIDTH)[None, :]
        win = np.clip(q_col - NA_COLS // 2, 0, GRID_W - NA_COLS)
        col_ok = jnp.asarray((k_col >= win) & (k_col < win + NA_COLS))
        toe = jnp.where(col_ok[None, None], toe, NEG).transpose(0, 2, 1, 3).reshape(NA_HEADS, w, n_dr * K_WIDTH)
        out.append(jnp.pad(toe, ((0, 0), (0, Q_BLOCK_MAX - w), (0, N_KEYS - n_dr * K_WIDTH)), constant_values=NEG))
    return jnp.stack(out, axis=1)


_GROUP_KINDS = (
    (0, lambda i: max(i - NA_ROWS // 2, 0)),
    (-(NA_ROWS // 2), lambda i: i),
    (Q_ROWS - SLAB_ROWS, lambda i: min(i + NA_ROWS // 2, NA_ROWS)),
)


def _build_bias(e_ref, tab_ref, slab_off, first_valid):
    for h in range(NA_HEADS):
        row_off = 0
        for blk, (_, w) in enumerate(Q_BLOCKS):
            e = e_ref[h, blk, 0:w, :]
            lane = lax.broadcasted_iota(jnp.int32, (w, N_KEYS), 1)
            for i in range(Q_ROWS):
                shift = slab_off - i + NA_ROWS - 1
                amount = (-shift * K_WIDTH) % N_KEYS
                moved = e if amount == 0 else pltpu.roll(e, amount, 1)
                lo = first_valid(i) * K_WIDTH
                ok = (lane >= lo) & (lane < lo + NA_ROWS * K_WIDTH)
                tab_ref[h, row_off + i * w:row_off + (i + 1) * w, :] = jnp.where(ok, moved, NEG)
            row_off += Q_ROWS * w


def _natten_kernel(q_ref, k0_ref, k1_ref, k2_ref, k3_ref, v0_ref, v1_ref, v2_ref, v3_ref, z_ref, e_ref,
                   y_ref, qf_ref, of_ref, tab_ref):
    k_refs = (k0_ref, k1_ref, k2_ref, k3_ref)
    v_refs = (v0_ref, v1_ref, v2_ref, v3_ref)
    piece_rows = SLAB_ROWS // 4
    group = pl.program_id(0)
    n_groups = pl.num_programs(0)

    first_batch = pl.program_id(1) == 0
    conditions = (group == 0, jnp.logical_and(group > 0, group < n_groups - 1), group == n_groups - 1)
    for cond, (slab_off, first_valid) in zip(conditions, _GROUP_KINDS):
        @pl.when(jnp.logical_and(first_batch, cond))
        def _(slab_off=slab_off, first_valid=first_valid):
            _build_bias(e_ref, tab_ref, slab_off, first_valid)

    qf_ref[...] = q_ref[0, 0].astype(F32) * (NA_HEAD_DIM ** -0.5)
    row_off = 0
    for (c0, w), kb in zip(Q_BLOCKS, K_STARTS):
        nq = Q_ROWS * w
        lane = lax.broadcasted_iota(jnp.int32, (nq, LANES), 1)
        first_head = lane < NA_HEAD_DIM
        for pair in range(NA_HEADS // 2):
            lanes = pl.ds(pair * LANES, LANES)
            qp = jnp.concatenate([qf_ref[pl.ds(i * GRID_W + c0, w), lanes] for i in range(Q_ROWS)], axis=0)
            qs = jnp.concatenate([jnp.where(first_head, qp, 0.0), jnp.where(first_head, 0.0, qp)],
                                 axis=0).astype(BF16)
            kk = jnp.concatenate(
                [k_refs[a // piece_rows][0, 0, pl.ds((a % piece_rows) * GRID_W + kb, K_WIDTH), lanes]
                 for a in range(SLAB_ROWS)], axis=0)
            vv = jnp.concatenate(
                [v_refs[a // piece_rows][0, 0, pl.ds((a % piece_rows) * GRID_W + kb, K_WIDTH), lanes]
                 for a in range(SLAB_ROWS)], axis=0)
            s = lax.dot_general(qs, kk, (((1,), (1,)), ((), ())), preferred_element_type=F32)
            bias = jnp.concatenate([tab_ref[2 * pair, pl.ds(row_off, nq), :],
                                    tab_ref[2 * pair + 1, pl.ds(row_off, nq), :]], axis=0)
            s = s + bias
            p = jnp.exp(s - jnp.max(s, axis=-1, keepdims=True))
            o2 = jnp.dot(p.astype(BF16), vv, preferred_element_type=F32)
            o2 = o2 * (1.0 / jnp.sum(p, axis=-1, keepdims=True))
            o = jnp.where(first_head, o2[0:nq], o2[nq:2 * nq])
            for i in range(Q_ROWS):
                of_ref[pl.ds(i * GRID_W + c0, w), lanes] = o[i * w:(i + 1) * w]
        row_off += nq
    y_ref[0, 0] = (of_ref[...] * _silu(z_ref[0, 0].astype(F32))).astype(BF16)


def _natten(proj3, bias_rows):
    bsz, seq, n_main = proj3.shape
    q_tok = Q_ROWS * GRID_W
    p_tok = (SLAB_ROWS // 4) * GRID_W
    groups = seq // q_tok
    pieces = seq // p_tok
    assert groups >= 2, "needs at least two 8-row query groups"
    w = NA_WIDTH
    qb, kb, vb, zb = (5 * D_MODEL) // w, (5 * D_MODEL) // w + 1, (5 * D_MODEL) // w + 2, (5 * D_MODEL) // w + 3
    proj_q = proj3.reshape(bsz, groups, q_tok, n_main)
    proj_p = proj3.reshape(bsz, pieces, p_tok, n_main)

    def slab_start(g):
        return jnp.clip(2 * g - 1, 0, pieces - 4)

    def piece_spec(col, a):
        return pl.BlockSpec((1, 1, p_tok, w), lambda g, b: (b, slab_start(g) + a, 0, col))

    return pl.pallas_call(
        _natten_kernel,
        out_shape=jax.ShapeDtypeStruct((bsz, groups, q_tok, w), BF16),
        grid=(groups, bsz),
        in_specs=[pl.BlockSpec((1, 1, q_tok, w), lambda g, b: (b, g, 0, qb))]
        + [piece_spec(kb, a) for a in range(4)] + [piece_spec(vb, a) for a in range(4)]
        + [pl.BlockSpec((1, 1, q_tok, w), lambda g, b: (b, g, 0, zb)),
           pl.BlockSpec(bias_rows.shape, lambda g, b: (0, 0, 0, 0))],
        out_specs=pl.BlockSpec((1, 1, q_tok, w), lambda g, b: (b, g, 0, 0)),
        scratch_shapes=[pltpu.VMEM((q_tok, w), F32), pltpu.VMEM((q_tok, w), F32),
                        pltpu.VMEM((NA_HEADS, q_tok, N_KEYS), F32)],
        compiler_params=pltpu.CompilerParams(
            dimension_semantics=("arbitrary", "arbitrary"), vmem_limit_bytes=VMEM_LIMIT_BYTES),
        name="natten",
    )(proj_q, *([proj_p] * 8), proj_q, bias_rows).reshape(bsz, seq, w)


def _merge_out_kernel(ya_ref, yb_ref, g0_ref, g1_ref, x_ref, mod_ref, bm_ref, wa_ref, wb_ref, wo_ref, fg_ref,
                      out_ref):
    d = x_ref.shape[1]
    pa = jnp.dot(ya_ref[...], wa_ref[...], preferred_element_type=F32)
    pb = jnp.dot(yb_ref[...], wb_ref[...], preferred_element_type=F32)
    merged = (_sigmoid(g0_ref[...].astype(F32) + bm_ref[0:1, :]) * pa
              + _sigmoid(g1_ref[...].astype(F32) + bm_ref[1:2, :]) * pb)
    mixed = jnp.dot(merged.astype(BF16), wo_ref[...], preferred_element_type=F32)
    xo = x_ref[...] + mod_ref[0, :, 2 * d:3 * d] * mixed
    out_ref[...] = xo * lax.rsqrt(jnp.mean(xo * xo, axis=-1, keepdims=True) + EPS) * fg_ref[...]


def _merge_out(ya2, yb2, proj2, x2, mod3, b_merge, wa, wb, wo, final_gain, seq, tm):
    n_tok, d = x2.shape
    g0 = (N_MAIN - 2 * d) // d

    def full(shape):
        return pl.BlockSpec(shape, lambda m: (0,) * len(shape))

    return pl.pallas_call(
        _merge_out_kernel,
        out_shape=jax.ShapeDtypeStruct((n_tok, d), F32),
        grid=(n_tok // tm,),
        in_specs=[
            pl.BlockSpec((tm, d), lambda m: (m, 0)),
            pl.BlockSpec((tm, NA_WIDTH), lambda m: (m, 0)),
            pl.BlockSpec((tm, d), lambda m: (m, g0)),
            pl.BlockSpec((tm, d), lambda m: (m, g0 + 1)),
            pl.BlockSpec((tm, d), lambda m: (m, 0)),
            pl.BlockSpec((1, 1, 3 * d), lambda m: ((m * tm) // seq, 0, 0)),
            full((2, d)), full((d, d)), full((NA_WIDTH, d)), full((d, d)), full((1, d)),
        ],
        out_specs=pl.BlockSpec((tm, d), lambda m: (m, 0)),
        compiler_params=pltpu.CompilerParams(
            dimension_semantics=("arbitrary",), vmem_limit_bytes=VMEM_LIMIT_BYTES),
        name="merge_out",
    )(ya2, yb2, proj2, proj2, x2, mod3, b_merge, wa, wb, wo, final_gain.reshape(1, d))


def _gate_layout(n_cols):
    src = np.arange(n_cols)
    direction, gate, head = src // (2 * MLSTM_HEADS), (src // MLSTM_HEADS) % 2, src % MLSTM_HEADS
    return 8 * head + 2 * direction + gate


def _layer(x, mod, norm_gain, w_in, conv_w, conv_b, b_igate, b_fgate, mlstm_norm_gain, rpb, w_proj_a, w_proj_b,
           b_merge, w_out, final_gain):
    bsz, seq, d = x.shape
    n_tok = bsz * seq
    wa = MLSTM_HEADS * MLSTM_HEAD_DIM
    n_gates = 4 * MLSTM_HEADS
    g_lo = 5 * wa
    w_main = jnp.concatenate([w_in[:, :g_lo], w_in[:, g_lo + n_gates:]], axis=1).astype(BF16)
    lanes = _gate_layout(n_gates)
    w_gate = jnp.zeros((d, GATE_LANES), F32).at[:, lanes].set(w_in[:, g_lo:g_lo + n_gates]).astype(BF16)
    gate_bias = jnp.zeros((1, GATE_LANES), F32).at[0, lanes].set(
        jnp.stack([b_igate, b_fgate], axis=1).reshape(-1))

    x2 = x.reshape(n_tok, d)
    mod3 = mod.reshape(bsz, 1, 3 * d)
    proj2, gates2 = _in_proj(x2, mod3, norm_gain, w_main, w_gate, seq, tm=min(2048, seq))
    proj3 = proj2.reshape(bsz, seq, N_MAIN)
    gcol, grow = _gate_prep(gates2.reshape(bsz, seq, GATE_LANES), gate_bias)
    y_a = _mlstm(proj3, gcol, grow, conv_w, conv_b, mlstm_norm_gain)
    y_b = _natten(proj3, _natten_bias_rows(rpb))
    out2 = _merge_out(y_a.reshape(n_tok, wa), y_b.reshape(n_tok, NA_WIDTH), proj2, x2, mod3, b_merge,
                      w_proj_a.astype(BF16), w_proj_b.astype(BF16), w_out.astype(BF16), final_gain, seq, tm=512)
    return out2.reshape(bsz, seq, d)


def kernel(x, c, w_ada, b_ada, norm_gain, w_in, conv_w, conv_b, b_igate, b_fgate, mlstm_norm_gain, rpb, w_proj_a,
           w_proj_b, b_merge, w_out, final_gain):
    depth = w_ada.shape[0]
    assert depth == 1, "the fused final norm assumes a single layer"
    mod = _adaln_mod(c, w_ada[0], b_ada[0])
    return _layer(x, mod, norm_gain[0], w_in[0], conv_w[0], conv_b[0], b_igate[0], b_fgate[0], mlstm_norm_gain[0],
                  rpb[0], w_proj_a[0], w_proj_b[0], b_merge[0], w_out[0], final_gain)
```

```python
import functools

import numpy as np
import jax
import jax.numpy as jnp
from jax import lax
from jax.experimental import pallas as pl
from jax.experimental.pallas import tpu as pltpu

F32 = jnp.float32
BF16 = jnp.bfloat16

D_MODEL = 1024
GRID_W = 64
MLSTM_HEADS = 4
MLSTM_HEAD_DIM = 256
NA_HEADS = 8
NA_WIDTH = 512
NA_HEAD_DIM = 64
NA_ROWS = 8
NA_COLS = 16
EPS = 1e-6
NEG = -1e30
LOG2E = 1.4426950408889634

VMEM_LIMIT_BYTES = 56 * 1024 * 1024
LANES = 128

CHUNK = 256

N_MAIN = 9 * D_MODEL
GATE_LANES = 128

Q_ROWS = 8
SLAB_ROWS = 16
Q_BLOCKS = ((0, 24), (24, 16), (40, 24))
Q_BLOCK_MAX = 24
K_STARTS = (0, 16, 32)
K_WIDTH = 32
N_KEYS = SLAB_ROWS * K_WIDTH


def _sigmoid(x):
    return 1.0 / (1.0 + jnp.exp2(x * -LOG2E))


def _silu(x):
    return x * _sigmoid(x)


def _adaln_kernel(c_ref, w_ref, b_ref, o_ref):
    cond = _silu(c_ref[...]).astype(BF16)
    o_ref[...] = jnp.dot(cond, w_ref[...].astype(BF16), preferred_element_type=F32) + b_ref[...]


def _adaln_mod(c, w_ada, b_ada):
    bsz, d = c.shape
    n = w_ada.shape[1]
    tn = 512
    return pl.pallas_call(
        _adaln_kernel,
        out_shape=jax.ShapeDtypeStruct((bsz, n), F32),
        grid=(n // tn,),
        in_specs=[
            pl.BlockSpec((bsz, d), lambda j: (0, 0)),
            pl.BlockSpec((d, tn), lambda j: (0, j)),
            pl.BlockSpec((1, tn), lambda j: (0, j)),
        ],
        out_specs=pl.BlockSpec((bsz, tn), lambda j: (0, j)),
        compiler_params=pltpu.CompilerParams(dimension_semantics=("arbitrary",)),
        name="adaln_mod",
    )(c, w_ada, b_ada.reshape(1, n))


def _in_proj_kernel(x_ref, mod_ref, gain_ref, w_ref, wg_ref, proj_ref, gates_ref, h_ref, *, rows):
    d = x_ref.shape[1]

    @pl.when(pl.program_id(1) == 0)
    def _():
        shift = mod_ref[0, :, 0:d]
        scale = mod_ref[0, :, d:2 * d]
        gain = gain_ref[...]
        for r in range(0, x_ref.shape[0], rows):
            xf = x_ref[r:r + rows, :]
            y = xf * lax.rsqrt(jnp.mean(xf * xf, axis=-1, keepdims=True) + EPS) * gain
            h_ref[r:r + rows, :] = (y * (1.0 + scale) + shift).astype(BF16)
        gates_ref[...] = jnp.dot(h_ref[...], wg_ref[...], preferred_element_type=F32)

    proj_ref[...] = jnp.dot(h_ref[...], w_ref[...], preferred_element_type=F32).astype(BF16)


def _in_proj(x2, mod3, norm_gain, w_main, w_gate, seq, tm):
    n_tok, d = x2.shape
    tn = D_MODEL
    return pl.pallas_call(
        functools.partial(_in_proj_kernel, rows=256),
        out_shape=(
            jax.ShapeDtypeStruct((n_tok, N_MAIN), BF16),
            jax.ShapeDtypeStruct((n_tok, GATE_LANES), F32),
        ),
        grid=(n_tok // tm, N_MAIN // tn),
        in_specs=[
            pl.BlockSpec((tm, d), lambda m, n: (m, 0)),
            pl.BlockSpec((1, 1, 3 * d), lambda m, n: ((m * tm) // seq, 0, 0)),
            pl.BlockSpec((1, d), lambda m, n: (0, 0)),
            pl.BlockSpec((d, tn), lambda m, n: (0, n)),
            pl.BlockSpec((d, GATE_LANES), lambda m, n: (0, 0)),
        ],
        out_specs=(
            pl.BlockSpec((tm, tn), lambda m, n: (m, n)),
            pl.BlockSpec((tm, GATE_LANES), lambda m, n: (m, 0)),
        ),
        scratch_shapes=[pltpu.VMEM((tm, d), BF16)],
        compiler_params=pltpu.CompilerParams(
            dimension_semantics=("arbitrary", "arbitrary"), vmem_limit_bytes=VMEM_LIMIT_BYTES),
        name="in_proj",
    )(x2, mod3, norm_gain.reshape(1, d), w_main, w_gate)


def _gate_prep_kernel(g_ref, bias_ref, col_ref, row_ref):
    n_rows = row_ref.shape[2]
    nc = row_ref.shape[1]
    v = (g_ref[0] + bias_ref[...]).T[0:n_rows, :]
    seq = v.shape[1]
    row = lax.broadcasted_iota(jnp.int32, v.shape, 0)
    pos = lax.broadcasted_iota(jnp.int32, v.shape, 1) % CHUNK
    is_f = (row % 2) == 1
    is_bwd = ((row // 2) % 2) == 1

    def scan(x, combine, identity):
        pre, suf = x, x
        step = 1
        while step < CHUNK:
            pre = combine(pre, jnp.where(pos >= step, pltpu.roll(pre, step, 1), identity))
            suf = combine(suf, jnp.where(pos < CHUNK - step, pltpu.roll(suf, seq - step, 1), identity))
            step *= 2
        return jnp.where(is_bwd, suf, pre)

    log_f = jnp.where(is_f, (jnp.minimum(v, 0.0) - jnp.log1p(jnp.exp(-jnp.abs(v)))) * LOG2E, 0.0)
    b = scan(log_f, jnp.add, 0.0)
    r = v * LOG2E - pltpu.roll(b, n_rows - 1, 0)
    cm = scan(r, jnp.maximum, NEG)
    cols = jnp.where(is_f, b, cm)
    col_ref[0] = jnp.concatenate([cols, jnp.zeros((GATE_LANES - n_rows, seq), F32)], axis=0).T
    for c in range(nc):
        row_ref[0, c] = r[:, c * CHUNK:(c + 1) * CHUNK]


def _gate_prep(gates3, bias_row):
    bsz, seq, _ = gates3.shape
    nc = seq // CHUNK
    rows = 8 * MLSTM_HEADS
    return pl.pallas_call(
        _gate_prep_kernel,
        out_shape=(
            jax.ShapeDtypeStruct((bsz, seq, GATE_LANES), F32),
            jax.ShapeDtypeStruct((bsz, nc, rows, CHUNK), F32),
        ),
        grid=(bsz,),
        in_specs=[
            pl.BlockSpec((1, seq, GATE_LANES), lambda b: (b, 0, 0)),
            pl.BlockSpec((1, GATE_LANES), lambda b: (0, 0)),
        ],
        out_specs=(
            pl.BlockSpec((1, seq, GATE_LANES), lambda b: (b, 0, 0)),
            pl.BlockSpec((1, nc, rows, CHUNK), lambda b: (b, 0, 0, 0)),
        ),
        compiler_params=pltpu.CompilerParams(dimension_semantics=("arbitrary",)),
        name="gate_prep",
    )(gates3, bias_row)


def _mlstm_chunk(qc, kt, vv, r_row, cm_rep, b_rep, cm_end, g, cn_ref, m_ref, causal_mask):
    d = vv.shape[1]
    cn = cn_ref[...]
    m_state = m_ref[...]
    m_col = jnp.maximum(m_state, cm_rep)
    w_inter = jnp.exp2(m_state - m_col)
    m_wide = jnp.concatenate([m_col, m_col], axis=1)
    d_mat = jnp.where(causal_mask, jnp.exp2(r_row - m_wide), 0.0)
    scores = jnp.dot(qc, kt, preferred_element_type=F32) * d_mat
    inter = jnp.dot(qc, cn.astype(BF16), preferred_element_type=F32)
    w_wide = jnp.concatenate([w_inter, w_inter], axis=1)
    num = jnp.dot(scores.astype(BF16), vv, preferred_element_type=F32) + w_wide * inter[:, 0:d]
    den = jnp.sum(scores, axis=-1, keepdims=True) + w_inter * inter[:, d:]
    inv = 1.0 / jnp.maximum(jnp.abs(den), jnp.exp2(-(b_rep + m_col)))
    h = num * jnp.concatenate([inv, inv], axis=1)

    m_end = jnp.maximum(m_state, cm_end)
    w_row = jnp.exp2(r_row - jnp.concatenate([m_end, m_end], axis=1))
    decay = jnp.exp2(m_state - m_end)
    kw = kt * w_row.astype(BF16)
    v_aug = jnp.concatenate([vv, jnp.ones((vv.shape[0], LANES), BF16)], axis=1)
    decay_wide = jnp.concatenate([decay, decay, decay], axis=1)
    cn_ref[...] = decay_wide * cn + jnp.dot(kw, v_aug, preferred_element_type=F32)
    m_ref[...] = g + m_end
    return h


def _mlstm_kernel(q_ref, k_ref, v_ref, o_ref, z_ref, gcol_ref, grow_ref, cwq_ref, cwk_ref, cbq_ref, cbk_ref,
                  gain_ref, y_ref, qc_ref, kt_ref, h_ref, gc_ref, cnf_ref, mf_ref, cnb_ref, mb_ref):
    seq, d = q_ref.shape[1], q_ref.shape[2]
    nc = seq // CHUNK
    head = pl.program_id(1)

    t_idx = lax.broadcasted_iota(jnp.int32, (CHUNK, CHUNK), 0)
    s_idx = lax.broadcasted_iota(jnp.int32, (CHUNK, CHUNK), 1)
    taps = jnp.concatenate([(s_idx == t_idx - 1).astype(BF16), (s_idx == t_idx).astype(BF16),
                            (s_idx == t_idx + 1).astype(BF16)], axis=0)
    edge = lax.broadcasted_iota(jnp.int32, (8, d), 0)

    def conv_silu(src_ref, w_ref, b_ref, out_scale, store):
        w0, w1, w2 = 0.5 * w_ref[0:1, :], 0.5 * w_ref[1:2, :], 0.5 * w_ref[2:3, :]
        bias = 0.5 * b_ref[...]

        def conv(c, carry):
            r = pl.multiple_of(c * CHUNK, CHUNK)
            p = jnp.dot(taps, src_ref[0, pl.ds(r, CHUNK), :], preferred_element_type=F32)
            t = p[0:CHUNK] * w0 + p[CHUNK:2 * CHUNK] * w1 + p[2 * CHUNK:3 * CHUNK] * w2 + bias
            above = src_ref[0, pl.ds(pl.multiple_of(jnp.maximum(r - 16, 0), 16), 16), :][15:16, :].astype(F32)
            below = src_ref[0, pl.ds(pl.multiple_of(jnp.minimum(r + CHUNK, seq - 16), 16), 16), :][0:1, :].astype(F32)
            above = above * jnp.where(c > 0, w0, 0.0)
            below = below * jnp.where(c < nc - 1, w2, 0.0)
            t = jnp.concatenate([t[0:8] + jnp.where(edge == 0, above, 0.0), t[8:CHUNK - 8],
                                 t[CHUNK - 8:CHUNK] + jnp.where(edge == 7, below, 0.0)], axis=0)
            store(c, r, (t * out_scale) * (1.0 + jnp.tanh(t)))
            return carry

        lax.fori_loop(0, nc, conv, 0, unroll=4)

    def store_q(c, r, y):
        qc_ref[pl.ds(r, CHUNK), :] = y.astype(BF16)

    def store_kt(c, r, y):
        kt_ref[c] = y.T.astype(BF16)

    conv_silu(q_ref, cwq_ref, cbq_ref, 1.0, store_q)
    conv_silu(k_ref, cwk_ref, cbk_ref, MLSTM_HEAD_DIM ** -0.5, store_kt)

    gc_ref[...] = pltpu.roll(gcol_ref[0], (GATE_LANES - 8 * head) % GATE_LANES, 1)

    for ref in (cnf_ref, mf_ref, cnb_ref, mb_ref):
        ref[...] = jnp.zeros(ref.shape, F32)

    mask_f = s_idx <= t_idx
    mask_b = s_idx >= t_idx

    def replicate(col):
        return jnp.broadcast_to(col, (col.shape[0], LANES))

    def direction(chunk, backward):
        r = pl.multiple_of(chunk * CHUNK, CHUNK)
        rows = grow_ref[0, chunk]
        cols = gc_ref[pl.ds(r, CHUNK), :]
        qc = qc_ref[pl.ds(r, CHUNK), :]
        kt = kt_ref[chunk]
        vv = v_ref[0, pl.ds(r, CHUNK), :]
        if backward:
            cm_rep, b_rep = replicate(cols[:, 2:3]), replicate(cols[:, 3:4])
            end = 0
            state, mask, r_row = (cnb_ref, mb_ref), mask_b, rows[2:3, :]
        else:
            cm_rep, b_rep = replicate(cols[:, 0:1]), replicate(cols[:, 1:2])
            end = CHUNK - 1
            state, mask, r_row = (cnf_ref, mf_ref), mask_f, rows[0:1, :]
        cm_end, g = cm_rep[end:end + 1, :], b_rep[end:end + 1, :]
        return _mlstm_chunk(qc, kt, vv, r_row, cm_rep, b_rep, cm_end, g, *state, mask), r

    def first_half(j, carry):
        hf, rf = direction(j, False)
        h_ref[pl.ds(rf, CHUNK), :] = hf
        hb, rb = direction(nc - 1 - j, True)
        h_ref[pl.ds(rb, CHUNK), :] = hb
        return carry

    gain = gain_ref[...]

    def finish(h_dir, r):
        h_sum = (h_ref[pl.ds(r, CHUNK), :] + h_dir) * _sigmoid(o_ref[0, pl.ds(r, CHUNK), :].astype(F32))
        normed = h_sum * lax.rsqrt(jnp.mean(h_sum * h_sum, axis=-1, keepdims=True) + EPS) * gain
        y_ref[0, pl.ds(r, CHUNK), :] = (normed * _silu(z_ref[0, pl.ds(r, CHUNK), :].astype(F32))).astype(BF16)

    def second_half(j, carry):
        finish(*direction(j, False))
        finish(*direction(nc - 1 - j, True))
        return carry

    lax.fori_loop(0, nc // 2, first_half, 0, unroll=2)
    lax.fori_loop(nc // 2, nc, second_half, 0)


def _mlstm(proj3, gcol, grow, conv_w, conv_b, head_gain):
    bsz, seq, _ = proj3.shape
    d = MLSTM_HEAD_DIM
    nh = MLSTM_HEADS
    nc = seq // CHUNK

    def col_block(offset):
        return pl.BlockSpec((1, seq, d), lambda b, h: (b, 0, offset + h))

    return pl.pallas_call(
        _mlstm_kernel,
        out_shape=jax.ShapeDtypeStruct((bsz, seq, nh * d), BF16),
        grid=(bsz, nh),
        in_specs=[
            col_block(0), col_block(nh), col_block(2 * nh), col_block(3 * nh), col_block(4 * nh),
            pl.BlockSpec((1, seq, GATE_LANES), lambda b, h: (b, 0, 0)),
            pl.BlockSpec((1, nc, 8, CHUNK), lambda b, h: (b, 0, h, 0)),
            pl.BlockSpec((3, d), lambda b, h: (0, h)),
            pl.BlockSpec((3, d), lambda b, h: (0, nh + h)),
            pl.BlockSpec((1, d), lambda b, h: (0, h)),
            pl.BlockSpec((1, d), lambda b, h: (0, nh + h)),
            pl.BlockSpec((1, d), lambda b, h: (0, h)),
        ],
        out_specs=pl.BlockSpec((1, seq, d), lambda b, h: (b, 0, h)),
        scratch_shapes=[
            pltpu.VMEM((seq, d), BF16),
            pltpu.VMEM((nc, d, CHUNK), BF16),
            pltpu.VMEM((seq, d), F32),
            pltpu.VMEM((seq, GATE_LANES), F32),
            pltpu.VMEM((d, d + LANES), F32), pltpu.VMEM((1, LANES), F32),
            pltpu.VMEM((d, d + LANES), F32), pltpu.VMEM((1, LANES), F32),
        ],
        compiler_params=pltpu.CompilerParams(
            dimension_semantics=("arbitrary", "arbitrary"), vmem_limit_bytes=VMEM_LIMIT_BYTES),
        name="mlstm",
    )(proj3, proj3, proj3, proj3, proj3, gcol, grow, conv_w, conv_w,
      conv_b.reshape(1, -1), conv_b.reshape(1, -1), head_gain.reshape(1, -1))


def _natten_bias_rows(rpb):
    n_dc = 2 * NA_COLS - 1
    n_dr = 2 * NA_ROWS - 1
    left = NA_COLS
    period = 64
    padded = jnp.pad(rpb.astype(F32), ((0, 0), (0, 0), (left, period - left - n_dc)), constant_values=NEG)
    out = []
    for (c0, w), kb in zip(Q_BLOCKS, K_STARTS):
        base = kb - c0 + NA_COLS - 1 + left
        toe = jnp.tile(padded, (1, 1, w))[..., :(period - 1) * w].reshape(NA_HEADS, n_dr, w, period - 1)
        toe = toe[..., base:base + K_WIDTH]
        q_col = c0 + np.arange(w)[:, None]
        k_col = kb + np.arange(K_WIDTH)[None, :]
        win = np.clip(q_col - NA_COLS // 2, 0, GRID_W - NA_COLS)
        col_ok = jnp.asarray((k_col >= win) & (k_col < win + NA_COLS))
        toe = jnp.where(col_ok[None, None], toe, NEG).transpose(0, 2, 1, 3).reshape(NA_HEADS, w, n_dr * K_WIDTH)
        out.append(jnp.pad(toe, ((0, 0), (0, Q_BLOCK_MAX - w), (0, N_KEYS - n_dr * K_WIDTH)), constant_values=NEG))
    return jnp.stack(out, axis=1)


_GROUP_KINDS = (
    (0, lambda i: max(i - NA_ROWS // 2, 0)),
    (-(NA_ROWS // 2), lambda i: i),
    (Q_ROWS - SLAB_ROWS, lambda i: min(i + NA_ROWS // 2, NA_ROWS)),
)


def _build_bias(e_ref, tab_ref, slab_off, first_valid):
    for h in range(NA_HEADS):
        row_off = 0
        for blk, (_, w) in enumerate(Q_BLOCKS):
            e = e_ref[h, blk, 0:w, :]
            lane = lax.broadcasted_iota(jnp.int32, (w, N_KEYS), 1)
            for i in range(Q_ROWS):
                shift = slab_off - i + NA_ROWS - 1
                amount = (-shift * K_WIDTH) % N_KEYS
                moved = e if amount == 0 else pltpu.roll(e, amount, 1)
                lo = first_valid(i) * K_WIDTH
                ok = (lane >= lo) & (lane < lo + NA_ROWS * K_WIDTH)
                tab_ref[h, row_off + i * w:row_off + (i + 1) * w, :] = jnp.where(ok, moved, NEG)
            row_off += Q_ROWS * w


def _natten_kernel(q_ref, k0_ref, k1_ref, k2_ref, k3_ref, v0_ref, v1_ref, v2_ref, v3_ref, z_ref, e_ref,
                   y_ref, qf_ref, of_ref, tab_ref):
    k_refs = (k0_ref, k1_ref, k2_ref, k3_ref)
    v_refs = (v0_ref, v1_ref, v2_ref, v3_ref)
    piece_rows = SLAB_ROWS // 4
    group = pl.program_id(0)
    n_groups = pl.num_programs(0)

    first_batch = pl.program_id(1) == 0
    conditions = (group == 0, jnp.logical_and(group > 0, group < n_groups - 1), group == n_groups - 1)
    for cond, (slab_off, first_valid) in zip(conditions, _GROUP_KINDS):
        @pl.when(jnp.logical_and(first_batch, cond))
        def _(slab_off=slab_off, first_valid=first_valid):
            _build_bias(e_ref, tab_ref, slab_off, first_valid)

    qf_ref[...] = q_ref[0, 0].astype(F32) * (NA_HEAD_DIM ** -0.5)
    row_off = 0
    for (c0, w), kb in zip(Q_BLOCKS, K_STARTS):
        nq = Q_ROWS * w
        lane = lax.broadcasted_iota(jnp.int32, (nq, LANES), 1)
        first_head = lane < NA_HEAD_DIM
        for pair in range(NA_HEADS // 2):
            lanes = pl.ds(pair * LANES, LANES)
            qp = jnp.concatenate([qf_ref[pl.ds(i * GRID_W + c0, w), lanes] for i in range(Q_ROWS)], axis=0)
            qs = jnp.concatenate([jnp.where(first_head, qp, 0.0), jnp.where(first_head, 0.0, qp)],
                                 axis=0).astype(BF16)
            kk = jnp.concatenate(
                [k_refs[a // piece_rows][0, 0, pl.ds((a % piece_rows) * GRID_W + kb, K_WIDTH), lanes]
                 for a in range(SLAB_ROWS)], axis=0)
            vv = jnp.concatenate(
                [v_refs[a // piece_rows][0, 0, pl.ds((a % piece_rows) * GRID_W + kb, K_WIDTH), lanes]
                 for a in range(SLAB_ROWS)], axis=0)
            s = lax.dot_general(qs, kk, (((1,), (1,)), ((), ())), preferred_element_type=F32)
            bias = jnp.concatenate([tab_ref[2 * pair, pl.ds(row_off, nq), :],
                                    tab_ref[2 * pair + 1, pl.ds(row_off, nq), :]], axis=0)
            s = s + bias
            p = jnp.exp(s - jnp.max(s, axis=-1, keepdims=True))
            o2 = jnp.dot(p.astype(BF16), vv, preferred_element_type=F32)
            o2 = o2 * (1.0 / jnp.sum(p, axis=-1, keepdims=True))
            o = jnp.where(first_head, o2[0:nq], o2[nq:2 * nq])
            for i in range(Q_ROWS):
                of_ref[pl.ds(i * GRID_W + c0, w), lanes] = o[i * w:(i + 1) * w]
        row_off += nq
    y_ref[0, 0] = (of_ref[...] * _silu(z_ref[0, 0].astype(F32))).astype(BF16)


def _natten(proj3, bias_rows):
    bsz, seq, n_main = proj3.shape
    q_tok = Q_ROWS * GRID_W
    p_tok = (SLAB_ROWS // 4) * GRID_W
    groups = seq // q_tok
    pieces = seq // p_tok
    assert groups >= 2, "needs at least two 8-row query groups"
    w = NA_WIDTH
    qb, kb, vb, zb = (5 * D_MODEL) // w, (5 * D_MODEL) // w + 1, (5 * D_MODEL) // w + 2, (5 * D_MODEL) // w + 3
    proj_q = proj3.reshape(bsz, groups, q_tok, n_main)
    proj_p = proj3.reshape(bsz, pieces, p_tok, n_main)

    def slab_start(g):
        return jnp.clip(2 * g - 1, 0, pieces - 4)

    def piece_spec(col, a):
        return pl.BlockSpec((1, 1, p_tok, w), lambda g, b: (b, slab_start(g) + a, 0, col))

    return pl.pallas_call(
        _natten_kernel,
        out_shape=jax.ShapeDtypeStruct((bsz, groups, q_tok, w), BF16),
        grid=(groups, bsz),
        in_specs=[pl.BlockSpec((1, 1, q_tok, w), lambda g, b: (b, g, 0, qb))]
        + [piece_spec(kb, a) for a in range(4)] + [piece_spec(vb, a) for a in range(4)]
        + [pl.BlockSpec((1, 1, q_tok, w), lambda g, b: (b, g, 0, zb)),
           pl.BlockSpec(bias_rows.shape, lambda g, b: (0, 0, 0, 0))],
        out_specs=pl.BlockSpec((1, 1, q_tok, w), lambda g, b: (b, g, 0, 0)),
        scratch_shapes=[pltpu.VMEM((q_tok, w), F32), pltpu.VMEM((q_tok, w), F32),
                        pltpu.VMEM((NA_HEADS, q_tok, N_KEYS), F32)],
        compiler_params=pltpu.CompilerParams(
            dimension_semantics=("arbitrary", "arbitrary"), vmem_limit_bytes=VMEM_LIMIT_BYTES),
        name="natten",
    )(proj_q, *([proj_p] * 8), proj_q, bias_rows).reshape(bsz, seq, w)


def _merge_out_kernel(ya_ref, yb_ref, g0_ref, g1_ref, x_ref, mod_ref, bm_ref, wa_ref, wb_ref, wo_ref, fg_ref,
                      out_ref):
    d = x_ref.shape[1]
    pa = jnp.dot(ya_ref[...], wa_ref[...], preferred_element_type=F32)
    pb = jnp.dot(yb_ref[...], wb_ref[...], preferred_element_type=F32)
    merged = (_sigmoid(g0_ref[...].astype(F32) + bm_ref[0:1, :]) * pa
              + _sigmoid(g1_ref[...].astype(F32) + bm_ref[1:2, :]) * pb)
    mixed = jnp.dot(merged.astype(BF16), wo_ref[...], preferred_element_type=F32)
    xo = x_ref[...] + mod_ref[0, :, 2 * d:3 * d] * mixed
    out_ref[...] = xo * lax.rsqrt(jnp.mean(xo * xo, axis=-1, keepdims=True) + EPS) * fg_ref[...]


def _merge_out(ya2, yb2, proj2, x2, mod3, b_merge, wa, wb, wo, final_gain, seq, tm):
    n_tok, d = x2.shape
    g0 = (N_MAIN - 2 * d) // d

    def full(shape):
        return pl.BlockSpec(shape, lambda m: (0,) * len(shape))

    return pl.pallas_call(
        _merge_out_kernel,
        out_shape=jax.ShapeDtypeStruct((n_tok, d), F32),
        grid=(n_tok // tm,),
        in_specs=[
            pl.BlockSpec((tm, d), lambda m: (m, 0)),
            pl.BlockSpec((tm, NA_WIDTH), lambda m: (m, 0)),
            pl.BlockSpec((tm, d), lambda m: (m, g0)),
            pl.BlockSpec((tm, d), lambda m: (m, g0 + 1)),
            pl.BlockSpec((tm, d), lambda m: (m, 0)),
            pl.BlockSpec((1, 1, 3 * d), lambda m: ((m * tm) // seq, 0, 0)),
            full((2, d)), full((d, d)), full((NA_WIDTH, d)), full((d, d)), full((1, d)),
        ],
        out_specs=pl.BlockSpec((tm, d), lambda m: (m, 0)),
        compiler_params=pltpu.CompilerParams(
            dimension_semantics=("arbitrary",), vmem_limit_bytes=VMEM_LIMIT_BYTES),
        name="merge_out",
    )(ya2, yb2, proj2, proj2, x2, mod3, b_merge, wa, wb, wo, final_gain.reshape(1, d))


def _gate_layout(n_cols):
    src = np.arange(n_cols)
    direction, gate, head = src // (2 * MLSTM_HEADS), (src // MLSTM_HEADS) % 2, src % MLSTM_HEADS
    return 8 * head + 2 * direction + gate


def _layer(x, mod, norm_gain, w_in, conv_w, conv_b, b_igate, b_fgate, mlstm_norm_gain, rpb, w_proj_a, w_proj_b,
           b_merge, w_out, final_gain):
    bsz, seq, d = x.shape
    n_tok = bsz * seq
    wa = MLSTM_HEADS * MLSTM_HEAD_DIM
    n_gates = 4 * MLSTM_HEADS
    g_lo = 5 * wa
    w_main = jnp.concatenate([w_in[:, :g_lo], w_in[:, g_lo + n_gates:]], axis=1).astype(BF16)
    lanes = _gate_layout(n_gates)
    w_gate = jnp.zeros((d, GATE_LANES), F32).at[:, lanes].set(w_in[:, g_lo:g_lo + n_gates]).astype(BF16)
    gate_bias = jnp.zeros((1, GATE_LANES), F32).at[0, lanes].set(
        jnp.stack([b_igate, b_fgate], axis=1).reshape(-1))

    x2 = x.reshape(n_tok, d)
    mod3 = mod.reshape(bsz, 1, 3 * d)
    proj2, gates2 = _in_proj(x2, mod3, norm_gain, w_main, w_gate, seq, tm=min(2048, seq))
    proj3 = proj2.reshape(bsz, seq, N_MAIN)
    gcol, grow = _gate_prep(gates2.reshape(bsz, seq, GATE_LANES), gate_bias)
    y_a = _mlstm(proj3, gcol, grow, conv_w, conv_b, mlstm_norm_gain)
    y_b = _natten(proj3, _natten_bias_rows(rpb))
    out2 = _merge_out(y_a.reshape(n_tok, wa), y_b.reshape(n_tok, NA_WIDTH), proj2, x2, mod3, b_merge,
                      w_proj_a.astype(BF16), w_proj_b.astype(BF16), w_out.astype(BF16), final_gain, seq, tm=512)
    return out2.reshape(bsz, seq, d)


def kernel(x, c, w_ada, b_ada, norm_gain, w_in, conv_w, conv_b, b_igate, b_fgate, mlstm_norm_gain, rpb, w_proj_a,
           w_proj_b, b_merge, w_out, final_gain):
    depth = w_ada.shape[0]
    assert depth == 1, "the fused final norm assumes a single layer"
    mod = _adaln_mod(c, w_ada[0], b_ada[0])
    return _layer(x, mod, norm_gain[0], w_in[0], conv_w[0], conv_b[0], b_igate[0], b_fgate[0], mlstm_norm_gain[0],
                  rpb[0], w_proj_a[0], w_proj_b[0], b_merge[0], w_out[0], final_gain)
```

```python
import functools

import numpy as np
import jax
import jax.numpy as jnp
from jax import lax
from jax.experimental import pallas as pl
from jax.experimental.pallas import tpu as pltpu

F32 = jnp.float32
BF16 = jnp.bfloat16

D_MODEL = 1024
GRID_W = 64
MLSTM_HEADS = 4
MLSTM_HEAD_DIM = 256
NA_HEADS = 8
NA_WIDTH = 512
NA_HEAD_DIM = 64
NA_ROWS = 8
NA_COLS = 16
EPS = 1e-6
NEG = -1e30
LOG2E = 1.4426950408889634

VMEM_LIMIT_BYTES = 56 * 1024 * 1024
LANES = 128

CHUNK = 256

N_MAIN = 9 * D_MODEL
GATE_LANES = 128

Q_ROWS = 8
SLAB_ROWS = 16
Q_BLOCKS = ((0, 24), (24, 16), (40, 24))
Q_BLOCK_MAX = 24
K_STARTS = (0, 16, 32)
K_WIDTH = 32
N_KEYS = SLAB_ROWS * K_WIDTH


def _sigmoid(x):
    return 1.0 / (1.0 + jnp.exp2(x * -LOG2E))


def _silu(x):
    return x * _sigmoid(x)


def _adaln_kernel(c_ref, w_ref, b_ref, o_ref):
    cond = _silu(c_ref[...]).astype(BF16)
    o_ref[...] = jnp.dot(cond, w_ref[...].astype(BF16), preferred_element_type=F32) + b_ref[...]


def _adaln_mod(c, w_ada, b_ada):
    bsz, d = c.shape
    n = w_ada.shape[1]
    tn = 512
    return pl.pallas_call(
        _adaln_kernel,
        out_shape=jax.ShapeDtypeStruct((bsz, n), F32),
        grid=(n // tn,),
        in_specs=[
            pl.BlockSpec((bsz, d), lambda j: (0, 0)),
            pl.BlockSpec((d, tn), lambda j: (0, j)),
            pl.BlockSpec((1, tn), lambda j: (0, j)),
        ],
        out_specs=pl.BlockSpec((bsz, tn), lambda j: (0, j)),
        compiler_params=pltpu.CompilerParams(dimension_semantics=("arbitrary",)),
        name="adaln_mod",
    )(c, w_ada, b_ada.reshape(1, n))


def _in_proj_kernel(x_ref, mod_ref, gain_ref, w_ref, wg_ref, proj_ref, gates_ref, h_ref, *, rows):
    d = x_ref.shape[1]

    @pl.when(pl.program_id(1) == 0)
    def _():
        shift = mod_ref[0, :, 0:d]
        scale = mod_ref[0, :, d:2 * d]
        gain = gain_ref[...]
        for r in range(0, x_ref.shape[0], rows):
            xf = x_ref[r:r + rows, :]
            y = xf * lax.rsqrt(jnp.mean(xf * xf, axis=-1, keepdims=True) + EPS) * gain
            h_ref[r:r + rows, :] = (y * (1.0 + scale) + shift).astype(BF16)
        gates_ref[...] = jnp.dot(h_ref[...], wg_ref[...], preferred_element_type=F32)

    proj_ref[...] = jnp.dot(h_ref[...], w_ref[...], preferred_element_type=F32).astype(BF16)


def _in_proj(x2, mod3, norm_gain, w_main, w_gate, seq, tm):
    n_tok, d = x2.shape
    tn = D_MODEL
    return pl.pallas_call(
        functools.partial(_in_proj_kernel, rows=256),
        out_shape=(
            jax.ShapeDtypeStruct((n_tok, N_MAIN), BF16),
            jax.ShapeDtypeStruct((n_tok, GATE_LANES), F32),
        ),
        grid=(n_tok // tm, N_MAIN // tn),
        in_specs=[
            pl.BlockSpec((tm, d), lambda m, n: (m, 0)),
            pl.BlockSpec((1, 1, 3 * d), lambda m, n: ((m * tm) // seq, 0, 0)),
            pl.BlockSpec((1, d), lambda m, n: (0, 0)),
            pl.BlockSpec((d, tn), lambda m, n: (0, n)),
            pl.BlockSpec((d, GATE_LANES), lambda m, n: (0, 0)),
        ],
        out_specs=(
            pl.BlockSpec((tm, tn), lambda m, n: (m, n)),
            pl.BlockSpec((tm, GATE_LANES), lambda m, n: (m, 0)),
        ),
        scratch_shapes=[pltpu.VMEM((tm, d), BF16)],
        compiler_params=pltpu.CompilerParams(
            dimension_semantics=("arbitrary", "arbitrary"), vmem_limit_bytes=VMEM_LIMIT_BYTES),
        name="in_proj",
    )(x2, mod3, norm_gain.reshape(1, d), w_main, w_gate)


def _gate_prep_kernel(g_ref, bias_ref, col_ref, row_ref):
    n_rows = row_ref.shape[2]
    nc = row_ref.shape[1]
    v = (g_ref[0] + bias_ref[...]).T[0:n_rows, :]
    seq = v.shape[1]
    row = lax.broadcasted_iota(jnp.int32, v.shape, 0)
    pos = lax.broadcasted_iota(jnp.int32, v.shape, 1) % CHUNK
    is_f = (row % 2) == 1
    is_bwd = ((row // 2) % 2) == 1

    def scan(x, combine, identity):
        pre, suf = x, x
        step = 1
        while step < CHUNK:
            pre = combine(pre, jnp.where(pos >= step, pltpu.roll(pre, step, 1), identity))
            suf = combine(suf, jnp.where(pos < CHUNK - step, pltpu.roll(suf, seq - step, 1), identity))
            step *= 2
        return jnp.where(is_bwd, suf, pre)

    log_f = jnp.where(is_f, (jnp.minimum(v, 0.0) - jnp.log1p(jnp.exp(-jnp.abs(v)))) * LOG2E, 0.0)
    b = scan(log_f, jnp.add, 0.0)
    r = v * LOG2E - pltpu.roll(b, n_rows - 1, 0)
    cm = scan(r, jnp.maximum, NEG)
    cols = jnp.where(is_f, b, cm)
    col_ref[0] = jnp.concatenate([cols, jnp.zeros((GATE_LANES - n_rows, seq), F32)], axis=0).T
    for c in range(nc):
        row_ref[0, c] = r[:, c * CHUNK:(c + 1) * CHUNK]


def _gate_prep(gates3, bias_row):
    bsz, seq, _ = gates3.shape
    nc = seq // CHUNK
    rows = 8 * MLSTM_HEADS
    return pl.pallas_call(
        _gate_prep_kernel,
        out_shape=(
            jax.ShapeDtypeStruct((bsz, seq, GATE_LANES), F32),
            jax.ShapeDtypeStruct((bsz, nc, rows, CHUNK), F32),
        ),
        grid=(bsz,),
        in_specs=[
            pl.BlockSpec((1, seq, GATE_LANES), lambda b: (b, 0, 0)),
            pl.BlockSpec((1, GATE_LANES), lambda b: (0, 0)),
        ],
        out_specs=(
            pl.BlockSpec((1, seq, GATE_LANES), lambda b: (b, 0, 0)),
            pl.BlockSpec((1, nc, rows, CHUNK), lambda b: (b, 0, 0, 0)),
        ),
        compiler_params=pltpu.CompilerParams(dimension_semantics=("arbitrary",)),
        name="gate_prep",
    )(gates3, bias_row)


def _mlstm_chunk(qc, kt, vv, r_row, cm_rep, b_rep, cm_end, g, cn_ref, m_ref, causal_mask):
    d = vv.shape[1]
    cn = cn_ref[...]
    m_state = m_ref[...]
    m_col = jnp.maximum(m_state, cm_rep)
    w_inter = jnp.exp2(m_state - m_col)
    m_wide = jnp.concatenate([m_col, m_col], axis=1)
    d_mat = jnp.where(causal_mask, jnp.exp2(r_row - m_wide), 0.0)
    scores = jnp.dot(qc, kt, preferred_element_type=F32) * d_mat
    inter = jnp.dot(qc, cn.astype(BF16), preferred_element_type=F32)
    w_wide = jnp.concatenate([w_inter, w_inter], axis=1)
    num = jnp.dot(scores.astype(BF16), vv, preferred_element_type=F32) + w_wide * inter[:, 0:d]
    den = jnp.sum(scores, axis=-1, keepdims=True) + w_inter * inter[:, d:]
    inv = 1.0 / jnp.maximum(jnp.abs(den), jnp.exp2(-(b_rep + m_col)))
    h = num * jnp.concatenate([inv, inv], axis=1)

    m_end = jnp.maximum(m_state, cm_end)
    w_row = jnp.exp2(r_row - jnp.concatenate([m_end, m_end], axis=1))
    decay = jnp.exp2(m_state - m_end)
    kw = kt * w_row.astype(BF16)
    v_aug = jnp.concatenate([vv, jnp.ones((vv.shape[0], LANES), BF16)], axis=1)
    decay_wide = jnp.concatenate([decay, decay, decay], axis=1)
    cn_ref[...] = decay_wide * cn + jnp.dot(kw, v_aug, preferred_element_type=F32)
    m_ref[...] = g + m_end
    return h


def _mlstm_kernel(q_ref, k_ref, v_ref, o_ref, z_ref, gcol_ref, grow_ref, cwq_ref, cwk_ref, cbq_ref, cbk_ref,
                  gain_ref, y_ref, qc_ref, kt_ref, h_ref, gc_ref, cnf_ref, mf_ref, cnb_ref, mb_ref):
    seq, d = q_ref.shape[1], q_ref.shape[2]
    nc = seq // CHUNK
    head = pl.program_id(1)

    t_idx = lax.broadcasted_iota(jnp.int32, (CHUNK, CHUNK), 0)
    s_idx = lax.broadcasted_iota(jnp.int32, (CHUNK, CHUNK), 1)
    taps = jnp.concatenate([(s_idx == t_idx - 1).astype(BF16), (s_idx == t_idx + 1).astype(BF16)],
                           axis=0)
    edge = lax.broadcasted_iota(jnp.int32, (8, d), 0)

    def conv_silu(src_ref, w_ref, b_ref, out_scale, store):
        w0, w1, w2 = 0.5 * w_ref[0:1, :], 0.5 * w_ref[1:2, :], 0.5 * w_ref[2:3, :]
        bias = 0.5 * b_ref[...]

        def conv(c, carry):
            r = pl.multiple_of(c * CHUNK, CHUNK)
            src = src_ref[0, pl.ds(r, CHUNK), :]
            p = jnp.dot(taps, src, preferred_element_type=F32)
            t = p[0:CHUNK] * w0 + src.astype(F32) * w1 + p[CHUNK:2 * CHUNK] * w2 + bias
            above = src_ref[0, pl.ds(pl.multiple_of(jnp.maximum(r - 16, 0), 16), 16), :][15:16, :].astype(F32)
            below = src_ref[0, pl.ds(pl.multiple_of(jnp.minimum(r + CHUNK, seq - 16), 16), 16), :][0:1, :].astype(F32)
            above = above * jnp.where(c > 0, w0, 0.0)
            below = below * jnp.where(c < nc - 1, w2, 0.0)
            t = jnp.concatenate([t[0:8] + jnp.where(edge == 0, above, 0.0), t[8:CHUNK - 8],
                                 t[CHUNK - 8:CHUNK] + jnp.where(edge == 7, below, 0.0)], axis=0)
            store(c, r, (t * out_scale) * (1.0 + jnp.tanh(t)))
            return carry

        lax.fori_loop(0, nc, conv, 0, unroll=4)

    def store_q(c, r, y):
        qc_ref[pl.ds(r, CHUNK), :] = y.astype(BF16)

    def store_kt(c, r, y):
        kt_ref[c] = y.T.astype(BF16)

    conv_silu(q_ref, cwq_ref, cbq_ref, 1.0, store_q)
    conv_silu(k_ref, cwk_ref, cbk_ref, MLSTM_HEAD_DIM ** -0.5, store_kt)

    gc_ref[...] = pltpu.roll(gcol_ref[0], (GATE_LANES - 8 * head) % GATE_LANES, 1)

    for ref in (cnf_ref, mf_ref, cnb_ref, mb_ref):
        ref[...] = jnp.zeros(ref.shape, F32)

    mask_f = s_idx <= t_idx
    mask_b = s_idx >= t_idx

    def replicate(col):
        return jnp.broadcast_to(col, (col.shape[0], LANES))

    def direction(chunk, backward):
        r = pl.multiple_of(chunk * CHUNK, CHUNK)
        rows = grow_ref[0, chunk]
        cols = gc_ref[pl.ds(r, CHUNK), :]
        qc = qc_ref[pl.ds(r, CHUNK), :]
        kt = kt_ref[chunk]
        vv = v_ref[0, pl.ds(r, CHUNK), :]
        if backward:
            cm_rep, b_rep = replicate(cols[:, 2:3]), replicate(cols[:, 3:4])
            end = 0
            state, mask, r_row = (cnb_ref, mb_ref), mask_b, rows[2:3, :]
        else:
            cm_rep, b_rep = replicate(cols[:, 0:1]), replicate(cols[:, 1:2])
            end = CHUNK - 1
            state, mask, r_row = (cnf_ref, mf_ref), mask_f, rows[0:1, :]
        cm_end, g = cm_rep[end:end + 1, :], b_rep[end:end + 1, :]
        return _mlstm_chunk(qc, kt, vv, r_row, cm_rep, b_rep, cm_end, g, *state, mask), r

    def first_half(j, carry):
        hf, rf = direction(j, False)
        h_ref[pl.ds(rf, CHUNK), :] = hf
        hb, rb = direction(nc - 1 - j, True)
        h_ref[pl.ds(rb, CHUNK), :] = hb
        return carry

    gain = gain_ref[...]

    def finish(h_dir, r):
        h_sum = (h_ref[pl.ds(r, CHUNK), :] + h_dir) * _sigmoid(o_ref[0, pl.ds(r, CHUNK), :].astype(F32))
        normed = h_sum * lax.rsqrt(jnp.mean(h_sum * h_sum, axis=-1, keepdims=True) + EPS) * gain
        y_ref[0, pl.ds(r, CHUNK), :] = (normed * _silu(z_ref[0, pl.ds(r, CHUNK), :].astype(F32))).astype(BF16)

    def second_half(j, carry):
        finish(*direction(j, False))
        finish(*direction(nc - 1 - j, True))
        return carry

    lax.fori_loop(0, nc // 2, first_half, 0, unroll=2)
    lax.fori_loop(nc // 2, nc, second_half, 0)


def _mlstm(proj3, gcol, grow, conv_w, conv_b, head_gain):
    bsz, seq, _ = proj3.shape
    d = MLSTM_HEAD_DIM
    nh = MLSTM_HEADS
    nc = seq // CHUNK

    def col_block(offset):
        return pl.BlockSpec((1, seq, d), lambda b, h: (b, 0, offset + h))

    return pl.pallas_call(
        _mlstm_kernel,
        out_shape=jax.ShapeDtypeStruct((bsz, seq, nh * d), BF16),
        grid=(bsz, nh),
        in_specs=[
            col_block(0), col_block(nh), col_block(2 * nh), col_block(3 * nh), col_block(4 * nh),
            pl.BlockSpec((1, seq, GATE_LANES), lambda b, h: (b, 0, 0)),
            pl.BlockSpec((1, nc, 8, CHUNK), lambda b, h: (b, 0, h, 0)),
            pl.BlockSpec((3, d), lambda b, h: (0, h)),
            pl.BlockSpec((3, d), lambda b, h: (0, nh + h)),
            pl.BlockSpec((1, d), lambda b, h: (0, h)),
            pl.BlockSpec((1, d), lambda b, h: (0, nh + h)),
            pl.BlockSpec((1, d), lambda b, h: (0, h)),
        ],
        out_specs=pl.BlockSpec((1, seq, d), lambda b, h: (b, 0, h)),
        scratch_shapes=[
            pltpu.VMEM((seq, d), BF16),
            pltpu.VMEM((nc, d, CHUNK), BF16),
            pltpu.VMEM((seq, d), F32),
            pltpu.VMEM((seq, GATE_LANES), F32),
            pltpu.VMEM((d, d + LANES), F32), pltpu.VMEM((1, LANES), F32),
            pltpu.VMEM((d, d + LANES), F32), pltpu.VMEM((1, LANES), F32),
        ],
        compiler_params=pltpu.CompilerParams(
            dimension_semantics=("arbitrary", "arbitrary"), vmem_limit_bytes=VMEM_LIMIT_BYTES),
        name="mlstm",
    )(proj3, proj3, proj3, proj3, proj3, gcol, grow, conv_w, conv_w,
      conv_b.reshape(1, -1), conv_b.reshape(1, -1), head_gain.reshape(1, -1))


def _natten_bias_rows(rpb):
    n_dc = 2 * NA_COLS - 1
    n_dr = 2 * NA_ROWS - 1
    left = NA_COLS
    period = 64
    padded = jnp.pad(rpb.astype(F32), ((0, 0), (0, 0), (left, period - left - n_dc)), constant_values=NEG)
    out = []
    for (c0, w), kb in zip(Q_BLOCKS, K_STARTS):
        base = kb - c0 + NA_COLS - 1 + left
        toe = jnp.tile(padded, (1, 1, w))[..., :(period - 1) * w].reshape(NA_HEADS, n_dr, w, period - 1)
        toe = toe[..., base:base + K_WIDTH]
        q_col = c0 + np.arange(w)[:, None]
        k_col = kb + np.arange(K_WIDTH)[None, :]
        win = np.clip(q_col - NA_COLS // 2, 0, GRID_W - NA_COLS)
        col_ok = jnp.asarray((k_col >= win) & (k_col < win + NA_COLS))
        toe = jnp.where(col_ok[None, None], toe, NEG).transpose(0, 2, 1, 3).reshape(NA_HEADS, w, n_dr * K_WIDTH)
        out.append(jnp.pad(toe, ((0, 0), (0, Q_BLOCK_MAX - w), (0, N_KEYS - n_dr * K_WIDTH)), constant_values=NEG))
    return jnp.stack(out, axis=1)


_GROUP_KINDS = (
    (0, lambda i: max(i - NA_ROWS // 2, 0)),
    (-(NA_ROWS // 2), lambda i: i),
    (Q_ROWS - SLAB_ROWS, lambda i: min(i + NA_ROWS // 2, NA_ROWS)),
)


def _build_bias(e_ref, tab_ref, slab_off, first_valid):
    for h in range(NA_HEADS):
        row_off = 0
        for blk, (_, w) in enumerate(Q_BLOCKS):
            e = e_ref[h, blk, 0:w, :]
            lane = lax.broadcasted_iota(jnp.int32, (w, N_KEYS), 1)
            for i in range(Q_ROWS):
                shift = slab_off - i + NA_ROWS - 1
                amount = (-shift * K_WIDTH) % N_KEYS
                moved = e if amount == 0 else pltpu.roll(e, amount, 1)
                lo = first_valid(i) * K_WIDTH
                ok = (lane >= lo) & (lane < lo + NA_ROWS * K_WIDTH)
                tab_ref[h, row_off + i * w:row_off + (i + 1) * w, :] = jnp.where(ok, moved, NEG)
            row_off += Q_ROWS * w


def _natten_kernel(q_ref, k0_ref, k1_ref, k2_ref, k3_ref, v0_ref, v1_ref, v2_ref, v3_ref, z_ref, e_ref,
                   y_ref, qf_ref, of_ref, tab_ref):
    k_refs = (k0_ref, k1_ref, k2_ref, k3_ref)
    v_refs = (v0_ref, v1_ref, v2_ref, v3_ref)
    piece_rows = SLAB_ROWS // 4
    group = pl.program_id(0)
    n_groups = pl.num_programs(0)

    first_batch = pl.program_id(1) == 0
    conditions = (group == 0, jnp.logical_and(group > 0, group < n_groups - 1), group == n_groups - 1)
    for cond, (slab_off, first_valid) in zip(conditions, _GROUP_KINDS):
        @pl.when(jnp.logical_and(first_batch, cond))
        def _(slab_off=slab_off, first_valid=first_valid):
            _build_bias(e_ref, tab_ref, slab_off, first_valid)

    qf_ref[...] = q_ref[0, 0].astype(F32) * (NA_HEAD_DIM ** -0.5)
    row_off = 0
    for (c0, w), kb in zip(Q_BLOCKS, K_STARTS):
        nq = Q_ROWS * w
        lane = lax.broadcasted_iota(jnp.int32, (nq, LANES), 1)
        first_head = lane < NA_HEAD_DIM
        for pair in range(NA_HEADS // 2):
            lanes = pl.ds(pair * LANES, LANES)
            qp = jnp.concatenate([qf_ref[pl.ds(i * GRID_W + c0, w), lanes] for i in range(Q_ROWS)], axis=0)
            qs = jnp.concatenate([jnp.where(first_head, qp, 0.0), jnp.where(first_head, 0.0, qp)],
                                 axis=0).astype(BF16)
            kk = jnp.concatenate(
                [k_refs[a // piece_rows][0, 0, pl.ds((a % piece_rows) * GRID_W + kb, K_WIDTH), lanes]
                 for a in range(SLAB_ROWS)], axis=0)
            vv = jnp.concatenate(
                [v_refs[a // piece_rows][0, 0, pl.ds((a % piece_rows) * GRID_W + kb, K_WIDTH), lanes]
                 for a in range(SLAB_ROWS)], axis=0)
            s = lax.dot_general(qs, kk, (((1,), (1,)), ((), ())), preferred_element_type=F32)
            bias = jnp.concatenate([tab_ref[2 * pair, pl.ds(row_off, nq), :],
                                    tab_ref[2 * pair + 1, pl.ds(row_off, nq), :]], axis=0)
            s = s + bias
            p = jnp.exp(s - jnp.max(s, axis=-1, keepdims=True))
            o2 = jnp.dot(p.astype(BF16), vv, preferred_element_type=F32)
            o2 = o2 * (1.0 / jnp.sum(p, axis=-1, keepdims=True))
            o = jnp.where(first_head, o2[0:nq], o2[nq:2 * nq])
            for i in range(Q_ROWS):
                of_ref[pl.ds(i * GRID_W + c0, w), lanes] = o[i * w:(i + 1) * w]
        row_off += nq
    y_ref[0, 0] = (of_ref[...] * _silu(z_ref[0, 0].astype(F32))).astype(BF16)


def _natten(proj3, bias_rows):
    bsz, seq, n_main = proj3.shape
    q_tok = Q_ROWS * GRID_W
    p_tok = (SLAB_ROWS // 4) * GRID_W
    groups = seq // q_tok
    pieces = seq // p_tok
    assert groups >= 2, "needs at least two 8-row query groups"
    w = NA_WIDTH
    qb, kb, vb, zb = (5 * D_MODEL) // w, (5 * D_MODEL) // w + 1, (5 * D_MODEL) // w + 2, (5 * D_MODEL) // w + 3
    proj_q = proj3.reshape(bsz, groups, q_tok, n_main)
    proj_p = proj3.reshape(bsz, pieces, p_tok, n_main)

    def slab_start(g):
        return jnp.clip(2 * g - 1, 0, pieces - 4)

    def piece_spec(col, a):
        return pl.BlockSpec((1, 1, p_tok, w), lambda g, b: (b, slab_start(g) + a, 0, col))

    return pl.pallas_call(
        _natten_kernel,
        out_shape=jax.ShapeDtypeStruct((bsz, groups, q_tok, w), BF16),
        grid=(groups, bsz),
        in_specs=[pl.BlockSpec((1, 1, q_tok, w), lambda g, b: (b, g, 0, qb))]
        + [piece_spec(kb, a) for a in range(4)] + [piece_spec(vb, a) for a in range(4)]
        + [pl.BlockSpec((1, 1, q_tok, w), lambda g, b: (b, g, 0, zb)),
           pl.BlockSpec(bias_rows.shape, lambda g, b: (0, 0, 0, 0))],
        out_specs=pl.BlockSpec((1, 1, q_tok, w), lambda g, b: (b, g, 0, 0)),
        scratch_shapes=[pltpu.VMEM((q_tok, w), F32), pltpu.VMEM((q_tok, w), F32),
                        pltpu.VMEM((NA_HEADS, q_tok, N_KEYS), F32)],
        compiler_params=pltpu.CompilerParams(
            dimension_semantics=("arbitrary", "arbitrary"), vmem_limit_bytes=VMEM_LIMIT_BYTES),
        name="natten",
    )(proj_q, *([proj_p] * 8), proj_q, bias_rows).reshape(bsz, seq, w)


def _merge_out_kernel(ya_ref, yb_ref, g0_ref, g1_ref, x_ref, mod_ref, bm_ref, wa_ref, wb_ref, wo_ref, fg_ref,
                      out_ref):
    d = x_ref.shape[1]
    pa = jnp.dot(ya_ref[...], wa_ref[...], preferred_element_type=F32)
    pb = jnp.dot(yb_ref[...], wb_ref[...], preferred_element_type=F32)
    merged = (_sigmoid(g0_ref[...].astype(F32) + bm_ref[0:1, :]) * pa
              + _sigmoid(g1_ref[...].astype(F32) + bm_ref[1:2, :]) * pb)
    mixed = jnp.dot(merged.astype(BF16), wo_ref[...], preferred_element_type=F32)
    xo = x_ref[...] + mod_ref[0, :, 2 * d:3 * d] * mixed
    out_ref[...] = xo * lax.rsqrt(jnp.mean(xo * xo, axis=-1, keepdims=True) + EPS) * fg_ref[...]


def _merge_out(ya2, yb2, proj2, x2, mod3, b_merge, wa, wb, wo, final_gain, seq, tm):
    n_tok, d = x2.shape
    g0 = (N_MAIN - 2 * d) // d

    def full(shape):
        return pl.BlockSpec(shape, lambda m: (0,) * len(shape))

    return pl.pallas_call(
        _merge_out_kernel,
        out_shape=jax.ShapeDtypeStruct((n_tok, d), F32),
        grid=(n_tok // tm,),
        in_specs=[
            pl.BlockSpec((tm, d), lambda m: (m, 0)),
            pl.BlockSpec((tm, NA_WIDTH), lambda m: (m, 0)),
            pl.BlockSpec((tm, d), lambda m: (m, g0)),
            pl.BlockSpec((tm, d), lambda m: (m, g0 + 1)),
            pl.BlockSpec((tm, d), lambda m: (m, 0)),
            pl.BlockSpec((1, 1, 3 * d), lambda m: ((m * tm) // seq, 0, 0)),
            full((2, d)), full((d, d)), full((NA_WIDTH, d)), full((d, d)), full((1, d)),
        ],
        out_specs=pl.BlockSpec((tm, d), lambda m: (m, 0)),
        compiler_params=pltpu.CompilerParams(
            dimension_semantics=("arbitrary",), vmem_limit_bytes=VMEM_LIMIT_BYTES),
        name="merge_out",
    )(ya2, yb2, proj2, proj2, x2, mod3, b_merge, wa, wb, wo, final_gain.reshape(1, d))


def _gate_layout(n_cols):
    src = np.arange(n_cols)
    direction, gate, head = src // (2 * MLSTM_HEADS), (src // MLSTM_HEADS) % 2, src % MLSTM_HEADS
    return 8 * head + 2 * direction + gate


def _weight_prep_kernel(w_ref, perm_ref, main_ref, gate_ref):
    g_lo = 5 * MLSTM_HEADS * MLSTM_HEAD_DIM
    n_gates = perm_ref.shape[0]
    w = w_ref[...]
    main_ref[...] = jnp.concatenate([w[:, :g_lo], w[:, g_lo + n_gates:]], axis=1).astype(BF16)
    gate_ref[...] = jnp.dot(w[:, g_lo:g_lo + n_gates].astype(BF16), perm_ref[...],
                            preferred_element_type=F32).astype(BF16)


def _weight_prep(w_in):
    d, n_in = w_in.shape
    n_gates = 4 * MLSTM_HEADS
    assert n_in == N_MAIN + n_gates
    perm = np.zeros((n_gates, GATE_LANES), np.float32)
    perm[np.arange(n_gates), _gate_layout(n_gates)] = 1.0
    rows = 128
    return pl.pallas_call(
        _weight_prep_kernel,
        out_shape=(jax.ShapeDtypeStruct((d, N_MAIN), BF16), jax.ShapeDtypeStruct((d, GATE_LANES), BF16)),
        grid=(d // rows,),
        in_specs=[pl.BlockSpec((rows, n_in), lambda i: (i, 0)),
                  pl.BlockSpec((n_gates, GATE_LANES), lambda i: (0, 0))],
        out_specs=(pl.BlockSpec((rows, N_MAIN), lambda i: (i, 0)),
                   pl.BlockSpec((rows, GATE_LANES), lambda i: (i, 0))),
        compiler_params=pltpu.CompilerParams(dimension_semantics=("arbitrary",)),
        name="weight_prep",
    )(w_in, jnp.asarray(perm, BF16))


def _token_tiles(seq):
    return min(2048, seq), min(1024, seq)


def _layer(x, mod, norm_gain, w_in, conv_w, conv_b, b_igate, b_fgate, mlstm_norm_gain, rpb, w_proj_a, w_proj_b,
           b_merge, w_out, final_gain):
    bsz, seq, d = x.shape
    n_tok = bsz * seq
    wa = MLSTM_HEADS * MLSTM_HEAD_DIM
    w_main, w_gate = _weight_prep(w_in)
    gate_bias = jnp.zeros((1, GATE_LANES), F32).at[0, _gate_layout(4 * MLSTM_HEADS)].set(
        jnp.stack([b_igate, b_fgate], axis=1).reshape(-1))
    tm_in, tm_out = _token_tiles(seq)

    x2 = x.reshape(n_tok, d)
    mod3 = mod.reshape(bsz, 1, 3 * d)
    proj2, gates2 = _in_proj(x2, mod3, norm_gain, w_main, w_gate, seq, tm_in)
    proj3 = proj2.reshape(bsz, seq, N_MAIN)
    gcol, grow = _gate_prep(gates2.reshape(bsz, seq, GATE_LANES), gate_bias)
    y_a = _mlstm(proj3, gcol, grow, conv_w, conv_b, mlstm_norm_gain)
    y_b = _natten(proj3, _natten_bias_rows(rpb))
    out2 = _merge_out(y_a.reshape(n_tok, wa), y_b.reshape(n_tok, NA_WIDTH), proj2, x2, mod3, b_merge,
                      w_proj_a.astype(BF16), w_proj_b.astype(BF16), w_out.astype(BF16), final_gain, seq, tm_out)
    return out2.reshape(bsz, seq, d)


def kernel(x, c, w_ada, b_ada, norm_gain, w_in, conv_w, conv_b, b_igate, b_fgate, mlstm_norm_gain, rpb, w_proj_a,
           w_proj_b, b_merge, w_out, final_gain):
    depth = w_ada.shape[0]
    assert depth == 1, "the fused final norm assumes a single layer"
    mod = _adaln_mod(c, w_ada[0], b_ada[0])
    return _layer(x, mod, norm_gain[0], w_in[0], conv_w[0], conv_b[0], b_igate[0], b_fgate[0], mlstm_norm_gain[0],
                  rpb[0], w_proj_a[0], w_proj_b[0], b_merge[0], w_out[0], final_gain)
```

```python
import functools

import numpy as np
import jax
import jax.numpy as jnp
from jax import lax
from jax.experimental import pallas as pl
from jax.experimental.pallas import tpu as pltpu

F32 = jnp.float32
BF16 = jnp.bfloat16

D_MODEL = 1024
GRID_W = 64
MLSTM_HEADS = 4
MLSTM_HEAD_DIM = 256
NA_HEADS = 8
NA_WIDTH = 512
NA_HEAD_DIM = 64
NA_ROWS = 8
NA_COLS = 16
EPS = 1e-6
NEG = -1e30
LOG2E = 1.4426950408889634

VMEM_LIMIT_BYTES = 56 * 1024 * 1024
LANES = 128

CHUNK = 256

N_MAIN = 9 * D_MODEL
GATE_LANES = 128

Q_ROWS = 8
SLAB_ROWS = 16
Q_BLOCKS = ((0, 24), (24, 16), (40, 24))
Q_BLOCK_MAX = 24
K_STARTS = (0, 16, 32)
K_WIDTH = 32
N_KEYS = SLAB_ROWS * K_WIDTH


def _sigmoid(x):
    return 1.0 / (1.0 + jnp.exp2(x * -LOG2E))


def _silu(x):
    return x * _sigmoid(x)


def _adaln_kernel(c_ref, w_ref, b_ref, o_ref):
    cond = _silu(c_ref[...]).astype(BF16)
    o_ref[...] = jnp.dot(cond, w_ref[...].astype(BF16), preferred_element_type=F32) + b_ref[...]


def _adaln_mod(c, w_ada, b_ada):
    bsz, d = c.shape
    n = w_ada.shape[1]
    tn = 512
    return pl.pallas_call(
        _adaln_kernel,
        out_shape=jax.ShapeDtypeStruct((bsz, n), F32),
        grid=(n // tn,),
        in_specs=[
            pl.BlockSpec((bsz, d), lambda j: (0, 0)),
            pl.BlockSpec((d, tn), lambda j: (0, j)),
            pl.BlockSpec((1, tn), lambda j: (0, j)),
        ],
        out_specs=pl.BlockSpec((bsz, tn), lambda j: (0, j)),
        compiler_params=pltpu.CompilerParams(dimension_semantics=("arbitrary",)),
        name="adaln_mod",
    )(c, w_ada, b_ada.reshape(1, n))


def _in_proj_kernel(x_ref, mod_ref, gain_ref, w_ref, wg_ref, proj_ref, gates_ref, h_ref, *, rows):
    d = x_ref.shape[1]

    @pl.when(pl.program_id(1) == 0)
    def _():
        shift = mod_ref[0, :, 0:d]
        scale = mod_ref[0, :, d:2 * d]
        gain = gain_ref[...]
        for r in range(0, x_ref.shape[0], rows):
            xf = x_ref[r:r + rows, :]
            y = xf * lax.rsqrt(jnp.mean(xf * xf, axis=-1, keepdims=True) + EPS) * gain
            h_ref[r:r + rows, :] = (y * (1.0 + scale) + shift).astype(BF16)
        gates_ref[...] = jnp.dot(h_ref[...], wg_ref[...], preferred_element_type=F32)

    proj_ref[...] = jnp.dot(h_ref[...], w_ref[...], preferred_element_type=F32).astype(BF16)


def _in_proj(x2, mod3, norm_gain, w_main, w_gate, seq, tm):
    n_tok, d = x2.shape
    tn = D_MODEL
    return pl.pallas_call(
        functools.partial(_in_proj_kernel, rows=256),
        out_shape=(
            jax.ShapeDtypeStruct((n_tok, N_MAIN), BF16),
            jax.ShapeDtypeStruct((n_tok, GATE_LANES), F32),
        ),
        grid=(n_tok // tm, N_MAIN // tn),
        in_specs=[
            pl.BlockSpec((tm, d), lambda m, n: (m, 0)),
            pl.BlockSpec((1, 1, 3 * d), lambda m, n: ((m * tm) // seq, 0, 0)),
            pl.BlockSpec((1, d), lambda m, n: (0, 0)),
            pl.BlockSpec((d, tn), lambda m, n: (0, n)),
            pl.BlockSpec((d, GATE_LANES), lambda m, n: (0, 0)),
        ],
        out_specs=(
            pl.BlockSpec((tm, tn), lambda m, n: (m, n)),
            pl.BlockSpec((tm, GATE_LANES), lambda m, n: (m, 0)),
        ),
        scratch_shapes=[pltpu.VMEM((tm, d), BF16)],
        compiler_params=pltpu.CompilerParams(
            dimension_semantics=("arbitrary", "arbitrary"), vmem_limit_bytes=VMEM_LIMIT_BYTES),
        name="in_proj",
    )(x2, mod3, norm_gain.reshape(1, d), w_main, w_gate)


def _gate_prep_kernel(g_ref, bias_ref, col_ref, row_ref):
    n_rows = row_ref.shape[2]
    nc = row_ref.shape[1]
    v = (g_ref[0] + bias_ref[...]).T[0:n_rows, :]
    seq = v.shape[1]
    row = lax.broadcasted_iota(jnp.int32, v.shape, 0)
    pos = lax.broadcasted_iota(jnp.int32, v.shape, 1) % CHUNK
    is_f = (row % 2) == 1
    is_bwd = ((row // 2) % 2) == 1

    def scan(x, combine, identity):
        pre, suf = x, x
        step = 1
        while step < CHUNK:
            pre = combine(pre, jnp.where(pos >= step, pltpu.roll(pre, step, 1), identity))
            suf = combine(suf, jnp.where(pos < CHUNK - step, pltpu.roll(suf, seq - step, 1), identity))
            step *= 2
        return jnp.where(is_bwd, suf, pre)

    log_f = jnp.where(is_f, (jnp.minimum(v, 0.0) - jnp.log1p(jnp.exp(-jnp.abs(v)))) * LOG2E, 0.0)
    b = scan(log_f, jnp.add, 0.0)
    r = v * LOG2E - pltpu.roll(b, n_rows - 1, 0)
    cm = scan(r, jnp.maximum, NEG)
    cols = jnp.where(is_f, b, cm)
    col_ref[0] = jnp.concatenate([cols, jnp.zeros((GATE_LANES - n_rows, seq), F32)], axis=0).T
    for c in range(nc):
        row_ref[0, c] = r[:, c * CHUNK:(c + 1) * CHUNK]


def _gate_prep(gates3, bias_row):
    bsz, seq, _ = gates3.shape
    nc = seq // CHUNK
    rows = 8 * MLSTM_HEADS
    return pl.pallas_call(
        _gate_prep_kernel,
        out_shape=(
            jax.ShapeDtypeStruct((bsz, seq, GATE_LANES), F32),
            jax.ShapeDtypeStruct((bsz, nc, rows, CHUNK), F32),
        ),
        grid=(bsz,),
        in_specs=[
            pl.BlockSpec((1, seq, GATE_LANES), lambda b: (b, 0, 0)),
            pl.BlockSpec((1, GATE_LANES), lambda b: (0, 0)),
        ],
        out_specs=(
            pl.BlockSpec((1, seq, GATE_LANES), lambda b: (b, 0, 0)),
            pl.BlockSpec((1, nc, rows, CHUNK), lambda b: (b, 0, 0, 0)),
        ),
        compiler_params=pltpu.CompilerParams(dimension_semantics=("arbitrary",)),
        name="gate_prep",
    )(gates3, bias_row)


def _mlstm_chunk(qc, kt, vv, r_row, cm_rep, b_rep, cm_end, g, cn_ref, m_ref, causal_mask):
    d = vv.shape[1]
    cn = cn_ref[...]
    m_state = m_ref[...]
    m_col = jnp.maximum(m_state, cm_rep)
    w_inter = jnp.exp2(m_state - m_col)
    m_wide = jnp.concatenate([m_col, m_col], axis=1)
    d_mat = jnp.where(causal_mask, jnp.exp2(r_row - m_wide), 0.0)
    scores = jnp.dot(qc, kt, preferred_element_type=F32) * d_mat
    inter = jnp.dot(qc, cn.astype(BF16), preferred_element_type=F32)
    w_wide = jnp.concatenate([w_inter, w_inter], axis=1)
    num = jnp.dot(scores.astype(BF16), vv, preferred_element_type=F32) + w_wide * inter[:, 0:d]
    den = jnp.sum(scores, axis=-1, keepdims=True) + w_inter * inter[:, d:]
    inv = 1.0 / jnp.maximum(jnp.abs(den), jnp.exp2(-(b_rep + m_col)))
    h = num * jnp.concatenate([inv, inv], axis=1)

    m_end = jnp.maximum(m_state, cm_end)
    w_row = jnp.exp2(r_row - jnp.concatenate([m_end, m_end], axis=1))
    decay = jnp.exp2(m_state - m_end)
    kw = kt * w_row.astype(BF16)
    v_aug = jnp.concatenate([vv, jnp.ones((vv.shape[0], LANES), BF16)], axis=1)
    decay_wide = jnp.concatenate([decay, decay, decay], axis=1)
    cn_ref[...] = decay_wide * cn + jnp.dot(kw, v_aug, preferred_element_type=F32)
    m_ref[...] = g + m_end
    return h


def _mlstm_kernel(q_ref, k_ref, v_ref, o_ref, z_ref, gcol_ref, grow_ref, cwq_ref, cwk_ref, cbq_ref, cbk_ref,
                  gain_ref, y_ref, qc_ref, kt_ref, h_ref, gc_ref, cnf_ref, mf_ref, cnb_ref, mb_ref):
    seq, d = q_ref.shape[1], q_ref.shape[2]
    nc = seq // CHUNK
    head = pl.program_id(1)

    t_idx = lax.broadcasted_iota(jnp.int32, (CHUNK, CHUNK), 0)
    s_idx = lax.broadcasted_iota(jnp.int32, (CHUNK, CHUNK), 1)
    taps = jnp.concatenate([(s_idx == t_idx - 1).astype(BF16), (s_idx == t_idx + 1).astype(BF16)],
                           axis=0)
    edge = lax.broadcasted_iota(jnp.int32, (8, d), 0)

    def conv_silu(src_ref, w_ref, b_ref, out_scale, store):
        w0, w1, w2 = 0.5 * w_ref[0:1, :], 0.5 * w_ref[1:2, :], 0.5 * w_ref[2:3, :]
        bias = 0.5 * b_ref[...]

        def conv(c, carry):
            r = pl.multiple_of(c * CHUNK, CHUNK)
            src = src_ref[0, pl.ds(r, CHUNK), :]
            p = jnp.dot(taps, src, preferred_element_type=F32)
            t = p[0:CHUNK] * w0 + src.astype(F32) * w1 + p[CHUNK:2 * CHUNK] * w2 + bias
            above = src_ref[0, pl.ds(pl.multiple_of(jnp.maximum(r - 16, 0), 16), 16), :][15:16, :].astype(F32)
            below = src_ref[0, pl.ds(pl.multiple_of(jnp.minimum(r + CHUNK, seq - 16), 16), 16), :][0:1, :].astype(F32)
            above = above * jnp.where(c > 0, w0, 0.0)
            below = below * jnp.where(c < nc - 1, w2, 0.0)
            t = jnp.concatenate([t[0:8] + jnp.where(edge == 0, above, 0.0), t[8:CHUNK - 8],
                                 t[CHUNK - 8:CHUNK] + jnp.where(edge == 7, below, 0.0)], axis=0)
            store(c, r, (t * out_scale) * (1.0 + jnp.tanh(t)))
            return carry

        lax.fori_loop(0, nc, conv, 0, unroll=4)

    def store_q(c, r, y):
        qc_ref[pl.ds(r, CHUNK), :] = y.astype(BF16)

    def store_kt(c, r, y):
        kt_ref[c] = y.T.astype(BF16)

    conv_silu(q_ref, cwq_ref, cbq_ref, 1.0, store_q)
    conv_silu(k_ref, cwk_ref, cbk_ref, MLSTM_HEAD_DIM ** -0.5, store_kt)

    gc_ref[...] = pltpu.roll(gcol_ref[0], (GATE_LANES - 8 * head) % GATE_LANES, 1)

    for ref in (cnf_ref, mf_ref, cnb_ref, mb_ref):
        ref[...] = jnp.zeros(ref.shape, F32)

    mask_f = s_idx <= t_idx
    mask_b = s_idx >= t_idx

    def replicate(col):
        return jnp.broadcast_to(col, (col.shape[0], LANES))

    def direction(chunk, backward):
        r = pl.multiple_of(chunk * CHUNK, CHUNK)
        rows = grow_ref[0, chunk]
        cols = gc_ref[pl.ds(r, CHUNK), :]
        qc = qc_ref[pl.ds(r, CHUNK), :]
        kt = kt_ref[chunk]
        vv = v_ref[0, pl.ds(r, CHUNK), :]
        if backward:
            cm_rep, b_rep = replicate(cols[:, 2:3]), replicate(cols[:, 3:4])
            end = 0
            state, mask, r_row = (cnb_ref, mb_ref), mask_b, rows[2:3, :]
        else:
            cm_rep, b_rep = replicate(cols[:, 0:1]), replicate(cols[:, 1:2])
            end = CHUNK - 1
            state, mask, r_row = (cnf_ref, mf_ref), mask_f, rows[0:1, :]
        cm_end, g = cm_rep[end:end + 1, :], b_rep[end:end + 1, :]
        return _mlstm_chunk(qc, kt, vv, r_row, cm_rep, b_rep, cm_end, g, *state, mask), r

    def first_half(j, carry):
        hf, rf = direction(j, False)
        h_ref[pl.ds(rf, CHUNK), :] = hf
        hb, rb = direction(nc - 1 - j, True)
        h_ref[pl.ds(rb, CHUNK), :] = hb
        return carry

    gain = gain_ref[...]

    def finish(h_dir, r):
        h_sum = (h_ref[pl.ds(r, CHUNK), :] + h_dir) * _sigmoid(o_ref[0, pl.ds(r, CHUNK), :].astype(F32))
        normed = h_sum * lax.rsqrt(jnp.mean(h_sum * h_sum, axis=-1, keepdims=True) + EPS) * gain
        y_ref[0, pl.ds(r, CHUNK), :] = (normed * _silu(z_ref[0, pl.ds(r, CHUNK), :].astype(F32))).astype(BF16)

    def second_half(j, carry):
        finish(*direction(j, False))
        finish(*direction(nc - 1 - j, True))
        return carry

    lax.fori_loop(0, nc // 2, first_half, 0, unroll=2)
    lax.fori_loop(nc // 2, nc, second_half, 0)


def _mlstm(proj3, gcol, grow, conv_w, conv_b, head_gain):
    bsz, seq, _ = proj3.shape
    d = MLSTM_HEAD_DIM
    nh = MLSTM_HEADS
    nc = seq // CHUNK

    def col_block(offset):
        return pl.BlockSpec((1, seq, d), lambda b, h: (b, 0, offset + h))

    return pl.pallas_call(
        _mlstm_kernel,
        out_shape=jax.ShapeDtypeStruct((bsz, seq, nh * d), BF16),
        grid=(bsz, nh),
        in_specs=[
            col_block(0), col_block(nh), col_block(2 * nh), col_block(3 * nh), col_block(4 * nh),
            pl.BlockSpec((1, seq, GATE_LANES), lambda b, h: (b, 0, 0)),
            pl.BlockSpec((1, nc, 8, CHUNK), lambda b, h: (b, 0, h, 0)),
            pl.BlockSpec((3, d), lambda b, h: (0, h)),
            pl.BlockSpec((3, d), lambda b, h: (0, nh + h)),
            pl.BlockSpec((1, d), lambda b, h: (0, h)),
            pl.BlockSpec((1, d), lambda b, h: (0, nh + h)),
            pl.BlockSpec((1, d), lambda b, h: (0, h)),
        ],
        out_specs=pl.BlockSpec((1, seq, d), lambda b, h: (b, 0, h)),
        scratch_shapes=[
            pltpu.VMEM((seq, d), BF16),
            pltpu.VMEM((nc, d, CHUNK), BF16),
            pltpu.VMEM((seq, d), F32),
            pltpu.VMEM((seq, GATE_LANES), F32),
            pltpu.VMEM((d, d + LANES), F32), pltpu.VMEM((1, LANES), F32),
            pltpu.VMEM((d, d + LANES), F32), pltpu.VMEM((1, LANES), F32),
        ],
        compiler_params=pltpu.CompilerParams(
            dimension_semantics=("arbitrary", "arbitrary"), vmem_limit_bytes=VMEM_LIMIT_BYTES),
        name="mlstm",
    )(proj3, proj3, proj3, proj3, proj3, gcol, grow, conv_w, conv_w,
      conv_b.reshape(1, -1), conv_b.reshape(1, -1), head_gain.reshape(1, -1))


def _natten_bias_rows(rpb):
    n_dc = 2 * NA_COLS - 1
    n_dr = 2 * NA_ROWS - 1
    left = NA_COLS
    period = 64
    padded = jnp.pad(rpb.astype(F32) * LOG2E, ((0, 0), (0, 0), (left, period - left - n_dc)), constant_values=NEG)
    out = []
    for (c0, w), kb in zip(Q_BLOCKS, K_STARTS):
        base = kb - c0 + NA_COLS - 1 + left
        toe = jnp.tile(padded, (1, 1, w))[..., :(period - 1) * w].reshape(NA_HEADS, n_dr, w, period - 1)
        toe = toe[..., base:base + K_WIDTH]
        q_col = c0 + np.arange(w)[:, None]
        k_col = kb + np.arange(K_WIDTH)[None, :]
        win = np.clip(q_col - NA_COLS // 2, 0, GRID_W - NA_COLS)
        col_ok = jnp.asarray((k_col >= win) & (k_col < win + NA_COLS))
        toe = jnp.where(col_ok[None, None], toe, NEG).transpose(0, 2, 1, 3).reshape(NA_HEADS, w, n_dr * K_WIDTH)
        out.append(jnp.pad(toe, ((0, 0), (0, Q_BLOCK_MAX - w), (0, N_KEYS - n_dr * K_WIDTH)), constant_values=NEG))
    return jnp.stack(out, axis=1)


_GROUP_KINDS = (
    (0, lambda i: max(i - NA_ROWS // 2, 0)),
    (-(NA_ROWS // 2), lambda i: i),
    (Q_ROWS - SLAB_ROWS, lambda i: min(i + NA_ROWS // 2, NA_ROWS)),
)


def _build_bias(e_ref, tab_ref, slab_off, first_valid):
    for h in range(NA_HEADS):
        row_off = 0
        for blk, (_, w) in enumerate(Q_BLOCKS):
            e = e_ref[h, blk, 0:w, :]
            lane = lax.broadcasted_iota(jnp.int32, (w, N_KEYS), 1)
            for i in range(Q_ROWS):
                shift = slab_off - i + NA_ROWS - 1
                amount = (-shift * K_WIDTH) % N_KEYS
                moved = e if amount == 0 else pltpu.roll(e, amount, 1)
                lo = first_valid(i) * K_WIDTH
                ok = (lane >= lo) & (lane < lo + NA_ROWS * K_WIDTH)
                tab_ref[h, row_off + i * w:row_off + (i + 1) * w, :] = jnp.where(ok, moved, NEG)
            row_off += Q_ROWS * w


def _natten_kernel(q_ref, k0_ref, k1_ref, k2_ref, k3_ref, v0_ref, v1_ref, v2_ref, v3_ref, z_ref, e_ref,
                   y_ref, qf_ref, of_ref, tab_ref):
    k_refs = (k0_ref, k1_ref, k2_ref, k3_ref)
    v_refs = (v0_ref, v1_ref, v2_ref, v3_ref)
    piece_rows = SLAB_ROWS // 4
    group = pl.program_id(0)
    n_groups = pl.num_programs(0)

    first_batch = pl.program_id(1) == 0
    conditions = (group == 0, jnp.logical_and(group > 0, group < n_groups - 1), group == n_groups - 1)
    for cond, (slab_off, first_valid) in zip(conditions, _GROUP_KINDS):
        @pl.when(jnp.logical_and(first_batch, cond))
        def _(slab_off=slab_off, first_valid=first_valid):
            _build_bias(e_ref, tab_ref, slab_off, first_valid)

    qf_ref[...] = q_ref[0, 0].astype(F32) * (NA_HEAD_DIM ** -0.5 * LOG2E)
    row_off = 0
    for (c0, w), kb in zip(Q_BLOCKS, K_STARTS):
        nq = Q_ROWS * w
        lane = lax.broadcasted_iota(jnp.int32, (nq, LANES), 1)
        first_head = lane < NA_HEAD_DIM
        for pair in range(NA_HEADS // 2):
            lanes = pl.ds(pair * LANES, LANES)
            qp = jnp.concatenate([qf_ref[pl.ds(i * GRID_W + c0, w), lanes] for i in range(Q_ROWS)], axis=0)
            qs = jnp.concatenate([jnp.where(first_head, qp, 0.0), jnp.where(first_head, 0.0, qp)],
                                 axis=0).astype(BF16)
            kk = jnp.concatenate(
                [k_refs[a // piece_rows][0, 0, pl.ds((a % piece_rows) * GRID_W + kb, K_WIDTH), lanes]
                 for a in range(SLAB_ROWS)], axis=0)
            vv = jnp.concatenate(
                [v_refs[a // piece_rows][0, 0, pl.ds((a % piece_rows) * GRID_W + kb, K_WIDTH), lanes]
                 for a in range(SLAB_ROWS)], axis=0)
            s = lax.dot_general(qs, kk, (((1,), (1,)), ((), ())), preferred_element_type=F32)
            bias = jnp.concatenate([tab_ref[2 * pair, pl.ds(row_off, nq), :],
                                    tab_ref[2 * pair + 1, pl.ds(row_off, nq), :]], axis=0)
            s = s + bias
            p = jnp.exp2(s - jnp.max(s, axis=-1, keepdims=True))
            o2 = jnp.dot(p.astype(BF16), vv, preferred_element_type=F32)
            o2 = o2 * (1.0 / jnp.sum(p, axis=-1, keepdims=True))
            o = jnp.where(first_head, o2[0:nq], o2[nq:2 * nq])
            for i in range(Q_ROWS):
                of_ref[pl.ds(i * GRID_W + c0, w), lanes] = o[i * w:(i + 1) * w]
        row_off += nq
    y_ref[0, 0] = (of_ref[...] * _silu(z_ref[0, 0].astype(F32))).astype(BF16)


def _natten(proj3, bias_rows):
    bsz, seq, n_main = proj3.shape
    q_tok = Q_ROWS * GRID_W
    p_tok = (SLAB_ROWS // 4) * GRID_W
    groups = seq // q_tok
    pieces = seq // p_tok
    assert groups >= 2, "needs at least two 8-row query groups"
    w = NA_WIDTH
    qb, kb, vb, zb = (5 * D_MODEL) // w, (5 * D_MODEL) // w + 1, (5 * D_MODEL) // w + 2, (5 * D_MODEL) // w + 3
    proj_q = proj3.reshape(bsz, groups, q_tok, n_main)
    proj_p = proj3.reshape(bsz, pieces, p_tok, n_main)

    def slab_start(g):
        return jnp.clip(2 * g - 1, 0, pieces - 4)

    def piece_spec(col, a):
        return pl.BlockSpec((1, 1, p_tok, w), lambda g, b: (b, slab_start(g) + a, 0, col))

    return pl.pallas_call(
        _natten_kernel,
        out_shape=jax.ShapeDtypeStruct((bsz, groups, q_tok, w), BF16),
        grid=(groups, bsz),
        in_specs=[pl.BlockSpec((1, 1, q_tok, w), lambda g, b: (b, g, 0, qb))]
        + [piece_spec(kb, a) for a in range(4)] + [piece_spec(vb, a) for a in range(4)]
        + [pl.BlockSpec((1, 1, q_tok, w), lambda g, b: (b, g, 0, zb)),
           pl.BlockSpec(bias_rows.shape, lambda g, b: (0, 0, 0, 0))],
        out_specs=pl.BlockSpec((1, 1, q_tok, w), lambda g, b: (b, g, 0, 0)),
        scratch_shapes=[pltpu.VMEM((q_tok, w), F32), pltpu.VMEM((q_tok, w), F32),
                        pltpu.VMEM((NA_HEADS, q_tok, N_KEYS), F32)],
        compiler_params=pltpu.CompilerParams(
            dimension_semantics=("arbitrary", "arbitrary"), vmem_limit_bytes=VMEM_LIMIT_BYTES),
        name="natten",
    )(proj_q, *([proj_p] * 8), proj_q, bias_rows).reshape(bsz, seq, w)


def _merge_out_kernel(ya_ref, yb_ref, g0_ref, g1_ref, x_ref, mod_ref, bm_ref, wa_ref, wb_ref, wo_ref, fg_ref,
                      out_ref):
    d = x_ref.shape[1]
    pa = jnp.dot(ya_ref[...], wa_ref[...], preferred_element_type=F32)
    pb = jnp.dot(yb_ref[...], wb_ref[...], preferred_element_type=F32)
    merged = (_sigmoid(g0_ref[...].astype(F32) + bm_ref[0:1, :]) * pa
              + _sigmoid(g1_ref[...].astype(F32) + bm_ref[1:2, :]) * pb)
    mixed = jnp.dot(merged.astype(BF16), wo_ref[...], preferred_element_type=F32)
    xo = x_ref[...] + mod_ref[0, :, 2 * d:3 * d] * mixed
    out_ref[...] = xo * lax.rsqrt(jnp.mean(xo * xo, axis=-1, keepdims=True) + EPS) * fg_ref[...]


def _merge_out(ya2, yb2, proj2, x2, mod3, b_merge, wa, wb, wo, final_gain, seq, tm):
    n_tok, d = x2.shape
    g0 = (N_MAIN - 2 * d) // d

    def full(shape):
        return pl.BlockSpec(shape, lambda m: (0,) * len(shape))

    return pl.pallas_call(
        _merge_out_kernel,
        out_shape=jax.ShapeDtypeStruct((n_tok, d), F32),
        grid=(n_tok // tm,),
        in_specs=[
            pl.BlockSpec((tm, d), lambda m: (m, 0)),
            pl.BlockSpec((tm, NA_WIDTH), lambda m: (m, 0)),
            pl.BlockSpec((tm, d), lambda m: (m, g0)),
            pl.BlockSpec((tm, d), lambda m: (m, g0 + 1)),
            pl.BlockSpec((tm, d), lambda m: (m, 0)),
            pl.BlockSpec((1, 1, 3 * d), lambda m: ((m * tm) // seq, 0, 0)),
            full((2, d)), full((d, d)), full((NA_WIDTH, d)), full((d, d)), full((1, d)),
        ],
        out_specs=pl.BlockSpec((tm, d), lambda m: (m, 0)),
        compiler_params=pltpu.CompilerParams(
            dimension_semantics=("arbitrary",), vmem_limit_bytes=VMEM_LIMIT_BYTES),
        name="merge_out",
    )(ya2, yb2, proj2, proj2, x2, mod3, b_merge, wa, wb, wo, final_gain.reshape(1, d))


def _gate_layout(n_cols):
    src = np.arange(n_cols)
    direction, gate, head = src // (2 * MLSTM_HEADS), (src // MLSTM_HEADS) % 2, src % MLSTM_HEADS
    return 8 * head + 2 * direction + gate


def _weight_prep_kernel(a_ref, b_ref, perm_ref, main_ref, gate_ref, *, gate_block, n_gates):
    n = pl.program_id(0)
    rows = a_ref[0]
    past_gates = jnp.concatenate([rows[n_gates:], b_ref[0]], axis=0)
    main_ref[...] = jnp.where(n >= gate_block, past_gates, rows).T.astype(BF16)

    @pl.when(n == gate_block)
    def _():
        gate_ref[...] = jnp.dot(perm_ref[...], rows[0:n_gates], preferred_element_type=F32).T.astype(BF16)


def _weight_prep(w_in, layer):
    _, d, n_in = w_in.shape
    n_gates = 4 * MLSTM_HEADS
    tn = D_MODEL
    g_lo = 5 * MLSTM_HEADS * MLSTM_HEAD_DIM
    assert n_in == N_MAIN + n_gates and g_lo % tn == 0 and tn % n_gates == 0
    perm = np.zeros((GATE_LANES, n_gates), np.float32)
    perm[_gate_layout(n_gates), np.arange(n_gates)] = 1.0
    w_t = jnp.swapaxes(w_in, 1, 2)
    return pl.pallas_call(
        functools.partial(_weight_prep_kernel, gate_block=g_lo // tn, n_gates=n_gates),
        out_shape=(jax.ShapeDtypeStruct((d, N_MAIN), BF16), jax.ShapeDtypeStruct((d, GATE_LANES), BF16)),
        grid=(N_MAIN // tn,),
        in_specs=[pl.BlockSpec((1, tn, d), lambda n: (layer, n, 0)),
                  pl.BlockSpec((1, n_gates, d), lambda n: (layer, (n + 1) * (tn // n_gates), 0)),
                  pl.BlockSpec((GATE_LANES, n_gates), lambda n: (0, 0))],
        out_specs=(pl.BlockSpec((d, tn), lambda n: (0, n)),
                   pl.BlockSpec((d, GATE_LANES), lambda n: (0, 0))),
        compiler_params=pltpu.CompilerParams(dimension_semantics=("arbitrary",), vmem_limit_bytes=VMEM_LIMIT_BYTES),
        name="weight_prep",
    )(w_t, w_t, jnp.asarray(perm))


def _token_tiles(seq):
    return min(2048, seq), min(1024, seq)


def _layer(x, mod, norm_gain, w_in, layer, conv_w, conv_b, b_igate, b_fgate, mlstm_norm_gain, rpb, w_proj_a, w_proj_b,
           b_merge, w_out, final_gain):
    bsz, seq, d = x.shape
    n_tok = bsz * seq
    wa = MLSTM_HEADS * MLSTM_HEAD_DIM
    w_main, w_gate = _weight_prep(w_in, layer)
    gate_bias = jnp.zeros((1, GATE_LANES), F32).at[0, _gate_layout(4 * MLSTM_HEADS)].set(
        jnp.stack([b_igate, b_fgate], axis=1).reshape(-1))
    tm_in, tm_out = _token_tiles(seq)

    x2 = x.reshape(n_tok, d)
    mod3 = mod.reshape(bsz, 1, 3 * d)
    proj2, gates2 = _in_proj(x2, mod3, norm_gain, w_main, w_gate, seq, tm_in)
    proj3 = proj2.reshape(bsz, seq, N_MAIN)
    gcol, grow = _gate_prep(gates2.reshape(bsz, seq, GATE_LANES), gate_bias)
    y_a = _mlstm(proj3, gcol, grow, conv_w, conv_b, mlstm_norm_gain)
    y_b = _natten(proj3, _natten_bias_rows(rpb))
    out2 = _merge_out(y_a.reshape(n_tok, wa), y_b.reshape(n_tok, NA_WIDTH), proj2, x2, mod3, b_merge,
                      w_proj_a.astype(BF16), w_proj_b.astype(BF16), w_out.astype(BF16), final_gain, seq, tm_out)
    return out2.reshape(bsz, seq, d)


def kernel(x, c, w_ada, b_ada, norm_gain, w_in, conv_w, conv_b, b_igate, b_fgate, mlstm_norm_gain, rpb, w_proj_a,
           w_proj_b, b_merge, w_out, final_gain):
    depth = w_ada.shape[0]
    assert depth == 1, "the fused final norm assumes a single layer"
    mod = _adaln_mod(c, w_ada[0], b_ada[0])
    return _layer(x, mod, norm_gain[0], w_in, 0, conv_w[0], conv_b[0], b_igate[0], b_fgate[0], mlstm_norm_gain[0],
                  rpb[0], w_proj_a[0], w_proj_b[0], b_merge[0], w_out[0], final_gain)
```

```python
import functools

import numpy as np
import jax
import jax.numpy as jnp
from jax import lax
from jax.experimental import pallas as pl
from jax.experimental.pallas import tpu as pltpu

F32 = jnp.float32
BF16 = jnp.bfloat16

D_MODEL = 1024
GRID_W = 64
MLSTM_HEADS = 4
MLSTM_HEAD_DIM = 256
NA_HEADS = 8
NA_WIDTH = 512
NA_HEAD_DIM = 64
NA_ROWS = 8
NA_COLS = 16
EPS = 1e-6
NEG = -1e30
LOG2E = 1.4426950408889634

VMEM_LIMIT_BYTES = 56 * 1024 * 1024
LANES = 128

CHUNK = 256

MERGE_ROWS = 512

N_MAIN = 9 * D_MODEL
GATE_LANES = 128

Q_ROWS = 8
SLAB_ROWS = 16
Q_BLOCKS = ((0, 24), (24, 16), (40, 24))
Q_BLOCK_MAX = 24
K_STARTS = (0, 16, 32)
K_WIDTH = 32
N_KEYS = SLAB_ROWS * K_WIDTH


def _sigmoid(x):
    return 1.0 / (1.0 + jnp.exp2(x * -LOG2E))


def _silu(x):
    return x * _sigmoid(x)


def _adaln_kernel(c_ref, w_ref, b_ref, o_ref):
    cond = _silu(c_ref[...]).astype(BF16)
    o_ref[...] = jnp.dot(cond, w_ref[...].astype(BF16), preferred_element_type=F32) + b_ref[...]


def _adaln_mod(c, w_ada, b_ada):
    bsz, d = c.shape
    n = w_ada.shape[1]
    tn = 512
    return pl.pallas_call(
        _adaln_kernel,
        out_shape=jax.ShapeDtypeStruct((bsz, n), F32),
        grid=(n // tn,),
        in_specs=[
            pl.BlockSpec((bsz, d), lambda j: (0, 0)),
            pl.BlockSpec((d, tn), lambda j: (0, j)),
            pl.BlockSpec((1, tn), lambda j: (0, j)),
        ],
        out_specs=pl.BlockSpec((bsz, tn), lambda j: (0, j)),
        compiler_params=pltpu.CompilerParams(dimension_semantics=("arbitrary",)),
        name="adaln_mod",
    )(c, w_ada, b_ada.reshape(1, n))


def _in_proj_kernel(x_ref, mod_ref, gain_ref, w_ref, wg_ref, proj_ref, gates_ref, h_ref, *, rows):
    tm, d = x_ref.shape
    n = pl.program_id(1)

    @pl.when(n == 0)
    def _():
        shift = mod_ref[0, :, 0:d]
        scale = mod_ref[0, :, d:2 * d]
        gain = gain_ref[...]

        def normalize(r):
            xf = x_ref[r:r + rows, :]
            y = xf * lax.rsqrt(jnp.mean(xf * xf, axis=-1, keepdims=True) + EPS) * gain
            h_ref[r:r + rows, :] = (y * (1.0 + scale) + shift).astype(BF16)

        def project(r):
            hb = h_ref[r:r + rows, :]
            proj_ref[r:r + rows, :] = jnp.dot(hb, w_ref[...], preferred_element_type=F32).astype(BF16)
            gates_ref[r:r + rows, :] = jnp.dot(hb, wg_ref[...], preferred_element_type=F32)

        starts = list(range(0, tm, rows))
        normalize(starts[0])
        for i, r in enumerate(starts):
            if i + 1 < len(starts):
                normalize(starts[i + 1])
            project(r)

    @pl.when(n > 0)
    def _():
        proj_ref[...] = jnp.dot(h_ref[...], w_ref[...], preferred_element_type=F32).astype(BF16)


def _in_proj(x2, mod3, norm_gain, w_main, w_gate, seq, tm):
    n_tok, d = x2.shape
    tn = D_MODEL
    return pl.pallas_call(
        functools.partial(_in_proj_kernel, rows=min(MERGE_ROWS, tm)),
        out_shape=(
            jax.ShapeDtypeStruct((n_tok, N_MAIN), BF16),
            jax.ShapeDtypeStruct((n_tok, GATE_LANES), F32),
        ),
        grid=(n_tok // tm, N_MAIN // tn),
        in_specs=[
            pl.BlockSpec((tm, d), lambda m, n: (m, 0)),
            pl.BlockSpec((1, 1, 3 * d), lambda m, n: ((m * tm) // seq, 0, 0)),
            pl.BlockSpec((1, d), lambda m, n: (0, 0)),
            pl.BlockSpec((d, tn), lambda m, n: (0, n)),
            pl.BlockSpec((d, GATE_LANES), lambda m, n: (0, 0)),
        ],
        out_specs=(
            pl.BlockSpec((tm, tn), lambda m, n: (m, n)),
            pl.BlockSpec((tm, GATE_LANES), lambda m, n: (m, 0)),
        ),
        scratch_shapes=[pltpu.VMEM((tm, d), BF16)],
        compiler_params=pltpu.CompilerParams(
            dimension_semantics=("arbitrary", "arbitrary"), vmem_limit_bytes=VMEM_LIMIT_BYTES),
        name="in_proj",
    )(x2, mod3, norm_gain.reshape(1, d), w_main, w_gate)


def _gate_prep_kernel(g_ref, bias_ref, col_ref, row_ref):
    n_rows = row_ref.shape[2]
    nc = row_ref.shape[1]
    v = (g_ref[0] + bias_ref[...]).T[0:n_rows, :]
    seq = v.shape[1]
    row = lax.broadcasted_iota(jnp.int32, v.shape, 0)
    pos = lax.broadcasted_iota(jnp.int32, v.shape, 1) % CHUNK
    is_f = (row % 2) == 1
    is_bwd = ((row // 2) % 2) == 1

    def scan(x, combine, identity):
        pre, suf = x, x
        step = 1
        while step < CHUNK:
            pre = combine(pre, jnp.where(pos >= step, pltpu.roll(pre, step, 1), identity))
            suf = combine(suf, jnp.where(pos < CHUNK - step, pltpu.roll(suf, seq - step, 1), identity))
            step *= 2
        return jnp.where(is_bwd, suf, pre)

    log_f = jnp.where(is_f, (jnp.minimum(v, 0.0) - jnp.log1p(jnp.exp(-jnp.abs(v)))) * LOG2E, 0.0)
    b = scan(log_f, jnp.add, 0.0)
    r = v * LOG2E - pltpu.roll(b, n_rows - 1, 0)
    cm = scan(r, jnp.maximum, NEG)
    cols = jnp.where(is_f, b, cm)
    col_ref[0] = jnp.concatenate([cols, jnp.zeros((GATE_LANES - n_rows, seq), F32)], axis=0).T
    for c in range(nc):
        row_ref[0, c] = r[:, c * CHUNK:(c + 1) * CHUNK]


def _gate_prep(gates3, bias_row):
    bsz, seq, _ = gates3.shape
    nc = seq // CHUNK
    rows = 8 * MLSTM_HEADS
    return pl.pallas_call(
        _gate_prep_kernel,
        out_shape=(
            jax.ShapeDtypeStruct((bsz, seq, GATE_LANES), F32),
            jax.ShapeDtypeStruct((bsz, nc, rows, CHUNK), F32),
        ),
        grid=(bsz,),
        in_specs=[
            pl.BlockSpec((1, seq, GATE_LANES), lambda b: (b, 0, 0)),
            pl.BlockSpec((1, GATE_LANES), lambda b: (0, 0)),
        ],
        out_specs=(
            pl.BlockSpec((1, seq, GATE_LANES), lambda b: (b, 0, 0)),
            pl.BlockSpec((1, nc, rows, CHUNK), lambda b: (b, 0, 0, 0)),
        ),
        compiler_params=pltpu.CompilerParams(dimension_semantics=("arbitrary",)),
        name="gate_prep",
    )(gates3, bias_row)


def _mlstm_chunk(qc, kt, vv, r_row, cm_rep, b_rep, cm_end, g, cn_ref, m_ref, causal_mask):
    d = vv.shape[1]
    cn = cn_ref[...]
    m_state = m_ref[...]
    m_col = jnp.maximum(m_state, cm_rep)
    w_inter = jnp.exp2(m_state - m_col)
    m_wide = jnp.concatenate([m_col, m_col], axis=1)
    d_mat = jnp.where(causal_mask, jnp.exp2(r_row - m_wide), 0.0)
    scores = jnp.dot(qc, kt, preferred_element_type=F32) * d_mat
    inter = jnp.dot(qc, cn.astype(BF16), preferred_element_type=F32)
    w_wide = jnp.concatenate([w_inter, w_inter], axis=1)
    num = jnp.dot(scores.astype(BF16), vv, preferred_element_type=F32) + w_wide * inter[:, 0:d]
    den = jnp.sum(scores, axis=-1, keepdims=True) + w_inter * inter[:, d:]
    inv = 1.0 / jnp.maximum(jnp.abs(den), jnp.exp2(-(b_rep + m_col)))
    h = num * jnp.concatenate([inv, inv], axis=1)

    m_end = jnp.maximum(m_state, cm_end)
    w_row = jnp.exp2(r_row - jnp.concatenate([m_end, m_end], axis=1))
    decay = jnp.exp2(m_state - m_end)
    kw = kt * w_row.astype(BF16)
    v_aug = jnp.concatenate([vv, jnp.ones((vv.shape[0], LANES), BF16)], axis=1)
    decay_wide = jnp.concatenate([decay, decay, decay], axis=1)
    cn_ref[...] = decay_wide * cn + jnp.dot(kw, v_aug, preferred_element_type=F32)
    m_ref[...] = g + m_end
    return h


def _mlstm_kernel(q_ref, k_ref, v_ref, o_ref, z_ref, gcol_ref, grow_ref, cwq_ref, cwk_ref, cbq_ref, cbk_ref,
                  gain_ref, y_ref, qc_ref, kt_ref, h_ref, gc_ref, cnf_ref, mf_ref, cnb_ref, mb_ref):
    seq, d = q_ref.shape[1], q_ref.shape[2]
    nc = seq // CHUNK
    head = pl.program_id(1)

    t_idx = lax.broadcasted_iota(jnp.int32, (CHUNK, CHUNK), 0)
    s_idx = lax.broadcasted_iota(jnp.int32, (CHUNK, CHUNK), 1)
    taps = jnp.concatenate([(s_idx == t_idx - 1).astype(BF16), (s_idx == t_idx + 1).astype(BF16)],
                           axis=0)
    edge = lax.broadcasted_iota(jnp.int32, (8, d), 0)

    def conv_silu(src_ref, w_ref, b_ref, out_scale, c, r):
        w0, w1, w2 = 0.5 * w_ref[0:1, :], 0.5 * w_ref[1:2, :], 0.5 * w_ref[2:3, :]
        src = src_ref[0, pl.ds(r, CHUNK), :]
        p = jnp.dot(taps, src, preferred_element_type=F32)
        t = p[0:CHUNK] * w0 + src.astype(F32) * w1 + p[CHUNK:2 * CHUNK] * w2 + 0.5 * b_ref[...]
        above = src_ref[0, pl.ds(pl.multiple_of(jnp.maximum(r - 16, 0), 16), 16), :][15:16, :].astype(F32)
        below = src_ref[0, pl.ds(pl.multiple_of(jnp.minimum(r + CHUNK, seq - 16), 16), 16), :][0:1, :].astype(F32)
        above = above * jnp.where(c > 0, w0, 0.0)
        below = below * jnp.where(c < nc - 1, w2, 0.0)
        t = jnp.concatenate([t[0:8] + jnp.where(edge == 0, above, 0.0), t[8:CHUNK - 8],
                             t[CHUNK - 8:CHUNK] + jnp.where(edge == 7, below, 0.0)], axis=0)
        return (t * out_scale) * (1.0 + jnp.tanh(t))

    def conv_chunk(c, carry):
        r = pl.multiple_of(c * CHUNK, CHUNK)
        qc_ref[pl.ds(r, CHUNK), :] = conv_silu(q_ref, cwq_ref, cbq_ref, 1.0, c, r).astype(BF16)
        kt_ref[c] = conv_silu(k_ref, cwk_ref, cbk_ref, MLSTM_HEAD_DIM ** -0.5, c, r).T.astype(BF16)
        return carry

    lax.fori_loop(0, nc, conv_chunk, 0, unroll=4)

    gc_ref[...] = pltpu.roll(gcol_ref[0], (GATE_LANES - 8 * head) % GATE_LANES, 1)

    for ref in (cnf_ref, mf_ref, cnb_ref, mb_ref):
        ref[...] = jnp.zeros(ref.shape, F32)

    mask_f = s_idx <= t_idx
    mask_b = s_idx >= t_idx

    def replicate(col):
        return jnp.broadcast_to(col, (col.shape[0], LANES))

    def direction(chunk, backward):
        r = pl.multiple_of(chunk * CHUNK, CHUNK)
        rows = grow_ref[0, chunk]
        cols = gc_ref[pl.ds(r, CHUNK), :]
        qc = qc_ref[pl.ds(r, CHUNK), :]
        kt = kt_ref[chunk]
        vv = v_ref[0, pl.ds(r, CHUNK), :]
        if backward:
            cm_rep, b_rep = replicate(cols[:, 2:3]), replicate(cols[:, 3:4])
            end = 0
            state, mask, r_row = (cnb_ref, mb_ref), mask_b, rows[2:3, :]
        else:
            cm_rep, b_rep = replicate(cols[:, 0:1]), replicate(cols[:, 1:2])
            end = CHUNK - 1
            state, mask, r_row = (cnf_ref, mf_ref), mask_f, rows[0:1, :]
        cm_end, g = cm_rep[end:end + 1, :], b_rep[end:end + 1, :]
        return _mlstm_chunk(qc, kt, vv, r_row, cm_rep, b_rep, cm_end, g, *state, mask), r

    def first_half(j, carry):
        hf, rf = direction(j, False)
        h_ref[pl.ds(rf, CHUNK), :] = hf
        hb, rb = direction(nc - 1 - j, True)
        h_ref[pl.ds(rb, CHUNK), :] = hb
        return carry

    gain = gain_ref[...]

    def finish(h_dir, r):
        h_sum = (h_ref[pl.ds(r, CHUNK), :] + h_dir) * _sigmoid(o_ref[0, pl.ds(r, CHUNK), :].astype(F32))
        normed = h_sum * lax.rsqrt(jnp.mean(h_sum * h_sum, axis=-1, keepdims=True) + EPS) * gain
        y_ref[0, pl.ds(r, CHUNK), :] = (normed * _silu(z_ref[0, pl.ds(r, CHUNK), :].astype(F32))).astype(BF16)

    def second_half(j, carry):
        finish(*direction(j, False))
        finish(*direction(nc - 1 - j, True))
        return carry

    lax.fori_loop(0, nc // 2, first_half, 0, unroll=2)
    lax.fori_loop(nc // 2, nc, second_half, 0)


def _mlstm(proj3, gcol, grow, conv_w, conv_b, head_gain):
    bsz, seq, _ = proj3.shape
    d = MLSTM_HEAD_DIM
    nh = MLSTM_HEADS
    nc = seq // CHUNK

    def col_block(offset):
        return pl.BlockSpec((1, seq, d), lambda b, h: (b, 0, offset + h))

    return pl.pallas_call(
        _mlstm_kernel,
        out_shape=jax.ShapeDtypeStruct((bsz, seq, nh * d), BF16),
        grid=(bsz, nh),
        in_specs=[
            col_block(0), col_block(nh), col_block(2 * nh), col_block(3 * nh), col_block(4 * nh),
            pl.BlockSpec((1, seq, GATE_LANES), lambda b, h: (b, 0, 0)),
            pl.BlockSpec((1, nc, 8, CHUNK), lambda b, h: (b, 0, h, 0)),
            pl.BlockSpec((3, d), lambda b, h: (0, h)),
            pl.BlockSpec((3, d), lambda b, h: (0, nh + h)),
            pl.BlockSpec((1, d), lambda b, h: (0, h)),
            pl.BlockSpec((1, d), lambda b, h: (0, nh + h)),
            pl.BlockSpec((1, d), lambda b, h: (0, h)),
        ],
        out_specs=pl.BlockSpec((1, seq, d), lambda b, h: (b, 0, h)),
        scratch_shapes=[
            pltpu.VMEM((seq, d), BF16),
            pltpu.VMEM((nc, d, CHUNK), BF16),
            pltpu.VMEM((seq, d), F32),
            pltpu.VMEM((seq, GATE_LANES), F32),
            pltpu.VMEM((d, d + LANES), F32), pltpu.VMEM((1, LANES), F32),
            pltpu.VMEM((d, d + LANES), F32), pltpu.VMEM((1, LANES), F32),
        ],
        compiler_params=pltpu.CompilerParams(
            dimension_semantics=("arbitrary", "arbitrary"), vmem_limit_bytes=VMEM_LIMIT_BYTES),
        name="mlstm",
    )(proj3, proj3, proj3, proj3, proj3, gcol, grow, conv_w, conv_w,
      conv_b.reshape(1, -1), conv_b.reshape(1, -1), head_gain.reshape(1, -1))


def _natten_bias_rows(rpb):
    n_dc = 2 * NA_COLS - 1
    n_dr = 2 * NA_ROWS - 1
    left = NA_COLS
    period = 64
    padded = jnp.pad(rpb.astype(F32) * LOG2E, ((0, 0), (0, 0), (left, period - left - n_dc)), constant_values=NEG)
    out = []
    for (c0, w), kb in zip(Q_BLOCKS, K_STARTS):
        base = kb - c0 + NA_COLS - 1 + left
        toe = jnp.tile(padded, (1, 1, w))[..., :(period - 1) * w].reshape(NA_HEADS, n_dr, w, period - 1)
        toe = toe[..., base:base + K_WIDTH]
        q_col = c0 + np.arange(w)[:, None]
        k_col = kb + np.arange(K_WIDTH)[None, :]
        win = np.clip(q_col - NA_COLS // 2, 0, GRID_W - NA_COLS)
        col_ok = jnp.asarray((k_col >= win) & (k_col < win + NA_COLS))
        toe = jnp.where(col_ok[None, None], toe, NEG).transpose(0, 2, 1, 3).reshape(NA_HEADS, w, n_dr * K_WIDTH)
        out.append(jnp.pad(toe, ((0, 0), (0, Q_BLOCK_MAX - w), (0, N_KEYS - n_dr * K_WIDTH)), constant_values=NEG))
    return jnp.stack(out, axis=1)


_GROUP_KINDS = (
    (0, lambda i: max(i - NA_ROWS // 2, 0)),
    (-(NA_ROWS // 2), lambda i: i),
    (Q_ROWS - SLAB_ROWS, lambda i: min(i + NA_ROWS // 2, NA_ROWS)),
)


def _build_bias(e_ref, tab_ref, slab_off, first_valid):
    for h in range(NA_HEADS):
        row_off = 0
        for blk, (_, w) in enumerate(Q_BLOCKS):
            e = e_ref[h, blk, 0:w, :]
            lane = lax.broadcasted_iota(jnp.int32, (w, N_KEYS), 1)
            for i in range(Q_ROWS):
                shift = slab_off - i + NA_ROWS - 1
                amount = (-shift * K_WIDTH) % N_KEYS
                moved = e if amount == 0 else pltpu.roll(e, amount, 1)
                lo = first_valid(i) * K_WIDTH
                ok = (lane >= lo) & (lane < lo + NA_ROWS * K_WIDTH)
                tab_ref[h, row_off + i * w:row_off + (i + 1) * w, :] = jnp.where(ok, moved, NEG)
            row_off += Q_ROWS * w


def _natten_kernel(q_ref, k0_ref, k1_ref, k2_ref, k3_ref, v0_ref, v1_ref, v2_ref, v3_ref, z_ref, e_ref,
                   y_ref, qf_ref, of_ref, tab_ref):
    k_refs = (k0_ref, k1_ref, k2_ref, k3_ref)
    v_refs = (v0_ref, v1_ref, v2_ref, v3_ref)
    piece_rows = SLAB_ROWS // 4
    group = pl.program_id(0)
    n_groups = pl.num_programs(0)

    first_batch = pl.program_id(1) == 0
    conditions = (group == 0, jnp.logical_and(group > 0, group < n_groups - 1), group == n_groups - 1)
    for cond, (slab_off, first_valid) in zip(conditions, _GROUP_KINDS):
        @pl.when(jnp.logical_and(first_batch, cond))
        def _(slab_off=slab_off, first_valid=first_valid):
            _build_bias(e_ref, tab_ref, slab_off, first_valid)

    qf_ref[...] = q_ref[0, 0].astype(F32) * (NA_HEAD_DIM ** -0.5 * LOG2E)
    row_off = 0
    for (c0, w), kb in zip(Q_BLOCKS, K_STARTS):
        nq = Q_ROWS * w
        lane = lax.broadcasted_iota(jnp.int32, (nq, LANES), 1)
        first_head = lane < NA_HEAD_DIM
        for pair in range(NA_HEADS // 2):
            lanes = pl.ds(pair * LANES, LANES)
            qp = jnp.concatenate([qf_ref[pl.ds(i * GRID_W + c0, w), lanes] for i in range(Q_ROWS)], axis=0)
            qs = jnp.concatenate([jnp.where(first_head, qp, 0.0), jnp.where(first_head, 0.0, qp)],
                                 axis=0).astype(BF16)
            kk = jnp.concatenate(
                [k_refs[a // piece_rows][0, 0, pl.ds((a % piece_rows) * GRID_W + kb, K_WIDTH), lanes]
                 for a in range(SLAB_ROWS)], axis=0)
            vv = jnp.concatenate(
                [v_refs[a // piece_rows][0, 0, pl.ds((a % piece_rows) * GRID_W + kb, K_WIDTH), lanes]
                 for a in range(SLAB_ROWS)], axis=0)
            s = lax.dot_general(qs, kk, (((1,), (1,)), ((), ())), preferred_element_type=F32)
            bias = jnp.concatenate([tab_ref[2 * pair, pl.ds(row_off, nq), :],
                                    tab_ref[2 * pair + 1, pl.ds(row_off, nq), :]], axis=0)
            s = s + bias
            p = jnp.exp2(s - jnp.max(s, axis=-1, keepdims=True))
            o2 = jnp.dot(p.astype(BF16), vv, preferred_element_type=F32)
            o2 = o2 * (1.0 / jnp.sum(p, axis=-1, keepdims=True))
            o = jnp.where(first_head, o2[0:nq], o2[nq:2 * nq])
            for i in range(Q_ROWS):
                of_ref[pl.ds(i * GRID_W + c0, w), lanes] = o[i * w:(i + 1) * w]
        row_off += nq
    y_ref[0, 0] = (of_ref[...] * _silu(z_ref[0, 0].astype(F32))).astype(BF16)


def _natten(proj3, bias_rows):
    bsz, seq, n_main = proj3.shape
    q_tok = Q_ROWS * GRID_W
    p_tok = (SLAB_ROWS // 4) * GRID_W
    groups = seq // q_tok
    pieces = seq // p_tok
    assert groups >= 2, "needs at least two 8-row query groups"
    w = NA_WIDTH
    qb, kb, vb, zb = (5 * D_MODEL) // w, (5 * D_MODEL) // w + 1, (5 * D_MODEL) // w + 2, (5 * D_MODEL) // w + 3
    proj_q = proj3.reshape(bsz, groups, q_tok, n_main)
    proj_p = proj3.reshape(bsz, pieces, p_tok, n_main)

    def slab_start(g):
        return jnp.clip(2 * g - 1, 0, pieces - 4)

    def piece_spec(col, a):
        return pl.BlockSpec((1, 1, p_tok, w), lambda g, b: (b, slab_start(g) + a, 0, col))

    return pl.pallas_call(
        _natten_kernel,
        out_shape=jax.ShapeDtypeStruct((bsz, groups, q_tok, w), BF16),
        grid=(groups, bsz),
        in_specs=[pl.BlockSpec((1, 1, q_tok, w), lambda g, b: (b, g, 0, qb))]
        + [piece_spec(kb, a) for a in range(4)] + [piece_spec(vb, a) for a in range(4)]
        + [pl.BlockSpec((1, 1, q_tok, w), lambda g, b: (b, g, 0, zb)),
           pl.BlockSpec(bias_rows.shape, lambda g, b: (0, 0, 0, 0))],
        out_specs=pl.BlockSpec((1, 1, q_tok, w), lambda g, b: (b, g, 0, 0)),
        scratch_shapes=[pltpu.VMEM((q_tok, w), F32), pltpu.VMEM((q_tok, w), F32),
                        pltpu.VMEM((NA_HEADS, q_tok, N_KEYS), F32)],
        compiler_params=pltpu.CompilerParams(
            dimension_semantics=("arbitrary", "arbitrary"), vmem_limit_bytes=VMEM_LIMIT_BYTES),
        name="natten",
    )(proj_q, *([proj_p] * 8), proj_q, bias_rows).reshape(bsz, seq, w)


def _merge_out_kernel(ya_ref, yb_ref, g0_ref, g1_ref, x_ref, mod_ref, bm_ref, wa_ref, wb_ref, wo_ref, fg_ref,
                      out_ref, m_ref):
    tm, d = x_ref.shape
    sub = min(MERGE_ROWS, tm)
    gate = mod_ref[0, :, 2 * d:3 * d]

    def merge(r):
        rows = pl.ds(r, sub)
        pa = jnp.dot(ya_ref[rows, :], wa_ref[...], preferred_element_type=F32)
        pb = jnp.dot(yb_ref[rows, :], wb_ref[...], preferred_element_type=F32)
        m_ref[rows, :] = (_sigmoid(g0_ref[rows, :].astype(F32) + bm_ref[0:1, :]) * pa
                          + _sigmoid(g1_ref[rows, :].astype(F32) + bm_ref[1:2, :]) * pb).astype(BF16)

    def project(r):
        rows = pl.ds(r, sub)
        mixed = jnp.dot(m_ref[rows, :], wo_ref[...], preferred_element_type=F32)
        xo = x_ref[rows, :] + gate * mixed
        out_ref[rows, :] = xo * lax.rsqrt(jnp.mean(xo * xo, axis=-1, keepdims=True) + EPS) * fg_ref[...]

    starts = list(range(0, tm, sub))
    merge(starts[0])
    for i, r in enumerate(starts):
        if i + 1 < len(starts):
            merge(starts[i + 1])
        project(r)


def _merge_out(ya2, yb2, proj2, x2, mod3, b_merge, wa, wb, wo, final_gain, seq, tm):
    n_tok, d = x2.shape
    g0 = (N_MAIN - 2 * d) // d

    def full(shape):
        return pl.BlockSpec(shape, lambda m: (0,) * len(shape))

    return pl.pallas_call(
        _merge_out_kernel,
        out_shape=jax.ShapeDtypeStruct((n_tok, d), F32),
        grid=(n_tok // tm,),
        in_specs=[
            pl.BlockSpec((tm, d), lambda m: (m, 0)),
            pl.BlockSpec((tm, NA_WIDTH), lambda m: (m, 0)),
            pl.BlockSpec((tm, d), lambda m: (m, g0)),
            pl.BlockSpec((tm, d), lambda m: (m, g0 + 1)),
            pl.BlockSpec((tm, d), lambda m: (m, 0)),
            pl.BlockSpec((1, 1, 3 * d), lambda m: ((m * tm) // seq, 0, 0)),
            full((2, d)), full((d, d)), full((NA_WIDTH, d)), full((d, d)), full((1, d)),
        ],
        out_specs=pl.BlockSpec((tm, d), lambda m: (m, 0)),
        scratch_shapes=[pltpu.VMEM((tm, d), BF16)],
        compiler_params=pltpu.CompilerParams(
            dimension_semantics=("arbitrary",), vmem_limit_bytes=VMEM_LIMIT_BYTES),
        name="merge_out",
    )(ya2, yb2, proj2, proj2, x2, mod3, b_merge, wa, wb, wo, final_gain.reshape(1, d))


def _gate_layout(n_cols):
    src = np.arange(n_cols)
    direction, gate, head = src // (2 * MLSTM_HEADS), (src // MLSTM_HEADS) % 2, src % MLSTM_HEADS
    return 8 * head + 2 * direction + gate


def _weight_prep_kernel(a_ref, b_ref, perm_ref, main_ref, gate_ref, *, gate_block, n_gates):
    n = pl.program_id(0)
    rows = a_ref[0]
    past_gates = jnp.concatenate([rows[n_gates:], b_ref[0]], axis=0)
    main_ref[...] = jnp.where(n >= gate_block, past_gates, rows).T.astype(BF16)

    @pl.when(n == gate_block)
    def _():
        gate_ref[...] = jnp.dot(perm_ref[...], rows[0:n_gates], preferred_element_type=F32).T.astype(BF16)


def _weight_prep(w_in, layer):
    _, d, n_in = w_in.shape
    n_gates = 4 * MLSTM_HEADS
    tn = D_MODEL
    g_lo = 5 * MLSTM_HEADS * MLSTM_HEAD_DIM
    assert n_in == N_MAIN + n_gates and g_lo % tn == 0 and tn % n_gates == 0
    perm = np.zeros((GATE_LANES, n_gates), np.float32)
    perm[_gate_layout(n_gates), np.arange(n_gates)] = 1.0
    w_t = jnp.swapaxes(w_in, 1, 2)
    return pl.pallas_call(
        functools.partial(_weight_prep_kernel, gate_block=g_lo // tn, n_gates=n_gates),
        out_shape=(jax.ShapeDtypeStruct((d, N_MAIN), BF16), jax.ShapeDtypeStruct((d, GATE_LANES), BF16)),
        grid=(N_MAIN // tn,),
        in_specs=[pl.BlockSpec((1, tn, d), lambda n: (layer, n, 0)),
                  pl.BlockSpec((1, n_gates, d), lambda n: (layer, (n + 1) * (tn // n_gates), 0)),
                  pl.BlockSpec((GATE_LANES, n_gates), lambda n: (0, 0))],
        out_specs=(pl.BlockSpec((d, tn), lambda n: (0, n)),
                   pl.BlockSpec((d, GATE_LANES), lambda n: (0, 0))),
        compiler_params=pltpu.CompilerParams(dimension_semantics=("arbitrary",), vmem_limit_bytes=VMEM_LIMIT_BYTES),
        name="weight_prep",
    )(w_t, w_t, jnp.asarray(perm))


def _token_tiles(seq):
    return min(2048, seq), min(1024, seq)


def _layer(x, mod, norm_gain, w_in, layer, conv_w, conv_b, b_igate, b_fgate, mlstm_norm_gain, rpb, w_proj_a, w_proj_b,
           b_merge, w_out, final_gain):
    bsz, seq, d = x.shape
    n_tok = bsz * seq
    wa = MLSTM_HEADS * MLSTM_HEAD_DIM
    w_main, w_gate = _weight_prep(w_in, layer)
    gate_bias = jnp.zeros((1, GATE_LANES), F32).at[0, _gate_layout(4 * MLSTM_HEADS)].set(
        jnp.stack([b_igate, b_fgate], axis=1).reshape(-1))
    tm_in, tm_out = _token_tiles(seq)

    x2 = x.reshape(n_tok, d)
    mod3 = mod.reshape(bsz, 1, 3 * d)
    proj2, gates2 = _in_proj(x2, mod3, norm_gain, w_main, w_gate, seq, tm_in)
    proj3 = proj2.reshape(bsz, seq, N_MAIN)
    gcol, grow = _gate_prep(gates2.reshape(bsz, seq, GATE_LANES), gate_bias)
    y_a = _mlstm(proj3, gcol, grow, conv_w, conv_b, mlstm_norm_gain)
    y_b = _natten(proj3, _natten_bias_rows(rpb))
    out2 = _merge_out(y_a.reshape(n_tok, wa), y_b.reshape(n_tok, NA_WIDTH), proj2, x2, mod3, b_merge,
                      w_proj_a.astype(BF16), w_proj_b.astype(BF16), w_out.astype(BF16), final_gain, seq, tm_out)
    return out2.reshape(bsz, seq, d)


def kernel(x, c, w_ada, b_ada, norm_gain, w_in, conv_w, conv_b, b_igate, b_fgate, mlstm_norm_gain, rpb, w_proj_a,
           w_proj_b, b_merge, w_out, final_gain):
    depth = w_ada.shape[0]
    assert depth == 1, "the fused final norm assumes a single layer"
    mod = _adaln_mod(c, w_ada[0], b_ada[0])
    return _layer(x, mod, norm_gain[0], w_in, 0, conv_w[0], conv_b[0], b_igate[0], b_fgate[0], mlstm_norm_gain[0],
                  rpb[0], w_proj_a[0], w_proj_b[0], b_merge[0], w_out[0], final_gain)
```

```python
import functools

import numpy as np
import jax
import jax.numpy as jnp
from jax import lax
from jax.experimental import pallas as pl
from jax.experimental.pallas import tpu as pltpu

F32 = jnp.float32
BF16 = jnp.bfloat16

D_MODEL = 1024
GRID_W = 64
MLSTM_HEADS = 4
MLSTM_HEAD_DIM = 256
NA_HEADS = 8
NA_WIDTH = 512
NA_HEAD_DIM = 64
NA_ROWS = 8
NA_COLS = 16
EPS = 1e-6
NEG = -1e30
LOG2E = 1.4426950408889634

VMEM_LIMIT_BYTES = 56 * 1024 * 1024
LANES = 128

CHUNK = 256

MERGE_ROWS = 512

N_MAIN = 9 * D_MODEL
GATE_LANES = 128

Q_ROWS = 8
SLAB_ROWS = 16
Q_BLOCKS = ((0, 24), (24, 16), (40, 24))
Q_BLOCK_MAX = 24
K_STARTS = (0, 16, 32)
K_WIDTH = 32
N_KEYS = SLAB_ROWS * K_WIDTH


def _sigmoid(x):
    return 1.0 / (1.0 + jnp.exp2(x * -LOG2E))


def _silu(x):
    return x * _sigmoid(x)


def _adaln_kernel(c_ref, w_ref, b_ref, o_ref):
    cond = _silu(c_ref[...]).astype(BF16)
    o_ref[...] = jnp.dot(cond, w_ref[...].astype(BF16), preferred_element_type=F32) + b_ref[...]


def _adaln_mod(c, w_ada, b_ada):
    bsz, d = c.shape
    n = w_ada.shape[1]
    tn = 512
    return pl.pallas_call(
        _adaln_kernel,
        out_shape=jax.ShapeDtypeStruct((bsz, n), F32),
        grid=(n // tn,),
        in_specs=[
            pl.BlockSpec((bsz, d), lambda j: (0, 0)),
            pl.BlockSpec((d, tn), lambda j: (0, j)),
            pl.BlockSpec((1, tn), lambda j: (0, j)),
        ],
        out_specs=pl.BlockSpec((bsz, tn), lambda j: (0, j)),
        compiler_params=pltpu.CompilerParams(dimension_semantics=("arbitrary",)),
        name="adaln_mod",
    )(c, w_ada, b_ada.reshape(1, n))


def _in_proj_kernel(x_ref, mod_ref, gain_ref, w_ref, wg_ref, proj_ref, gates_ref, h_ref, *, rows):
    tm, d = x_ref.shape
    n = pl.program_id(1)

    @pl.when(n == 0)
    def _():
        shift = mod_ref[0, :, 0:d]
        scale = mod_ref[0, :, d:2 * d]
        gain = gain_ref[...]

        def normalize(r):
            xf = x_ref[r:r + rows, :]
            y = xf * lax.rsqrt(jnp.mean(xf * xf, axis=-1, keepdims=True) + EPS) * gain
            h_ref[r:r + rows, :] = (y * (1.0 + scale) + shift).astype(BF16)

        def project(r):
            hb = h_ref[r:r + rows, :]
            proj_ref[r:r + rows, :] = jnp.dot(hb, w_ref[...], preferred_element_type=F32).astype(BF16)
            gates_ref[r:r + rows, :] = jnp.dot(hb, wg_ref[...], preferred_element_type=F32)

        starts = list(range(0, tm, rows))
        normalize(starts[0])
        for i, r in enumerate(starts):
            if i + 1 < len(starts):
                normalize(starts[i + 1])
            project(r)

    @pl.when(n > 0)
    def _():
        proj_ref[...] = jnp.dot(h_ref[...], w_ref[...], preferred_element_type=F32).astype(BF16)


def _in_proj(x2, mod3, norm_gain, w_main, w_gate, seq, tm):
    n_tok, d = x2.shape
    tn = D_MODEL
    return pl.pallas_call(
        functools.partial(_in_proj_kernel, rows=min(MERGE_ROWS, tm)),
        out_shape=(
            jax.ShapeDtypeStruct((n_tok, N_MAIN), BF16),
            jax.ShapeDtypeStruct((n_tok, GATE_LANES), F32),
        ),
        grid=(n_tok // tm, N_MAIN // tn),
        in_specs=[
            pl.BlockSpec((tm, d), lambda m, n: (m, 0)),
            pl.BlockSpec((1, 1, 3 * d), lambda m, n: ((m * tm) // seq, 0, 0)),
            pl.BlockSpec((1, d), lambda m, n: (0, 0)),
            pl.BlockSpec((d, tn), lambda m, n: (0, n)),
            pl.BlockSpec((d, GATE_LANES), lambda m, n: (0, 0)),
        ],
        out_specs=(
            pl.BlockSpec((tm, tn), lambda m, n: (m, n)),
            pl.BlockSpec((tm, GATE_LANES), lambda m, n: (m, 0)),
        ),
        scratch_shapes=[pltpu.VMEM((tm, d), BF16)],
        compiler_params=pltpu.CompilerParams(
            dimension_semantics=("arbitrary", "arbitrary"), vmem_limit_bytes=VMEM_LIMIT_BYTES),
        name="in_proj",
    )(x2, mod3, norm_gain.reshape(1, d), w_main, w_gate)


def _gate_prep_kernel(g_ref, bias_ref, col_ref, row_ref):
    n_rows = row_ref.shape[2]
    nc = row_ref.shape[1]
    v = (g_ref[0] + bias_ref[...]).T[0:n_rows, :]
    seq = v.shape[1]
    row = lax.broadcasted_iota(jnp.int32, v.shape, 0)
    pos = lax.broadcasted_iota(jnp.int32, v.shape, 1) % CHUNK
    is_f = (row % 2) == 1
    is_bwd = ((row // 2) % 2) == 1

    def scan(x, combine, identity):
        pre, suf = x, x
        step = 1
        while step < CHUNK:
            pre = combine(pre, jnp.where(pos >= step, pltpu.roll(pre, step, 1), identity))
            suf = combine(suf, jnp.where(pos < CHUNK - step, pltpu.roll(suf, seq - step, 1), identity))
            step *= 2
        return jnp.where(is_bwd, suf, pre)

    log_f = jnp.where(is_f, (jnp.minimum(v, 0.0) - jnp.log1p(jnp.exp(-jnp.abs(v)))) * LOG2E, 0.0)
    b = scan(log_f, jnp.add, 0.0)
    r = v * LOG2E - pltpu.roll(b, n_rows - 1, 0)
    cm = scan(r, jnp.maximum, NEG)
    cols = jnp.where(is_f, b, cm)
    col_ref[0] = jnp.concatenate([cols, jnp.zeros((GATE_LANES - n_rows, seq), F32)], axis=0).T
    for c in range(nc):
        row_ref[0, c] = r[:, c * CHUNK:(c + 1) * CHUNK]


def _gate_prep(gates3, bias_row):
    bsz, seq, _ = gates3.shape
    nc = seq // CHUNK
    rows = 8 * MLSTM_HEADS
    return pl.pallas_call(
        _gate_prep_kernel,
        out_shape=(
            jax.ShapeDtypeStruct((bsz, seq, GATE_LANES), F32),
            jax.ShapeDtypeStruct((bsz, nc, rows, CHUNK), F32),
        ),
        grid=(bsz,),
        in_specs=[
            pl.BlockSpec((1, seq, GATE_LANES), lambda b: (b, 0, 0)),
            pl.BlockSpec((1, GATE_LANES), lambda b: (0, 0)),
        ],
        out_specs=(
            pl.BlockSpec((1, seq, GATE_LANES), lambda b: (b, 0, 0)),
            pl.BlockSpec((1, nc, rows, CHUNK), lambda b: (b, 0, 0, 0)),
        ),
        compiler_params=pltpu.CompilerParams(dimension_semantics=("arbitrary",)),
        name="gate_prep",
    )(gates3, bias_row)


def _mlstm_chunk(qc, kt, vv, r_row, cm_rep, b_rep, cm_end, g, cn_ref, m_ref, causal_mask):
    d = vv.shape[1]
    cn = cn_ref[...]
    m_state = m_ref[...]
    m_col = jnp.maximum(m_state, cm_rep)
    w_inter = jnp.exp2(m_state - m_col)
    m_wide = jnp.concatenate([m_col, m_col], axis=1)
    d_mat = jnp.where(causal_mask, jnp.exp2(r_row - m_wide), 0.0)
    scores = jnp.dot(qc, kt, preferred_element_type=F32) * d_mat
    inter = jnp.dot(qc, cn.astype(BF16), preferred_element_type=F32)
    yield None
    w_wide = jnp.concatenate([w_inter, w_inter], axis=1)
    num = jnp.dot(scores.astype(BF16), vv, preferred_element_type=F32) + w_wide * inter[:, 0:d]
    den = jnp.sum(scores, axis=-1, keepdims=True) + w_inter * inter[:, d:]
    inv = 1.0 / jnp.maximum(jnp.abs(den), jnp.exp2(-(b_rep + m_col)))
    yield num * jnp.concatenate([inv, inv], axis=1)
    m_end = jnp.maximum(m_state, cm_end)
    w_row = jnp.exp2(r_row - jnp.concatenate([m_end, m_end], axis=1))
    decay = jnp.exp2(m_state - m_end)
    kw = kt * w_row.astype(BF16)
    v_aug = jnp.concatenate([vv, jnp.ones((vv.shape[0], LANES), BF16)], axis=1)
    decay_wide = jnp.concatenate([decay, decay, decay], axis=1)
    cn_ref[...] = decay_wide * cn + jnp.dot(kw, v_aug, preferred_element_type=F32)
    m_ref[...] = g + m_end
    yield None


def _mlstm_kernel(q_ref, k_ref, v_ref, o_ref, z_ref, gcol_ref, grow_ref, cwq_ref, cwk_ref, cbq_ref, cbk_ref,
                  gain_ref, y_ref, qc_ref, kt_ref, h_ref, gc_ref, cnf_ref, mf_ref, cnb_ref, mb_ref):
    seq, d = q_ref.shape[1], q_ref.shape[2]
    nc = seq // CHUNK
    head = pl.program_id(1)

    t_idx = lax.broadcasted_iota(jnp.int32, (CHUNK, CHUNK), 0)
    s_idx = lax.broadcasted_iota(jnp.int32, (CHUNK, CHUNK), 1)
    taps = jnp.concatenate([(s_idx == t_idx - 1).astype(BF16), (s_idx == t_idx + 1).astype(BF16)],
                           axis=0)
    edge = lax.broadcasted_iota(jnp.int32, (8, d), 0)

    def conv_silu(src_ref, w_ref, b_ref, out_scale, c, r):
        w0, w1, w2 = 0.5 * w_ref[0:1, :], 0.5 * w_ref[1:2, :], 0.5 * w_ref[2:3, :]
        src = src_ref[0, pl.ds(r, CHUNK), :]
        p = jnp.dot(taps, src, preferred_element_type=F32)
        t = p[0:CHUNK] * w0 + src.astype(F32) * w1 + p[CHUNK:2 * CHUNK] * w2 + 0.5 * b_ref[...]
        above = src_ref[0, pl.ds(pl.multiple_of(jnp.maximum(r - 16, 0), 16), 16), :][15:16, :].astype(F32)
        below = src_ref[0, pl.ds(pl.multiple_of(jnp.minimum(r + CHUNK, seq - 16), 16), 16), :][0:1, :].astype(F32)
        above = above * jnp.where(c > 0, w0, 0.0)
        below = below * jnp.where(c < nc - 1, w2, 0.0)
        t = jnp.concatenate([t[0:8] + jnp.where(edge == 0, above, 0.0), t[8:CHUNK - 8],
                             t[CHUNK - 8:CHUNK] + jnp.where(edge == 7, below, 0.0)], axis=0)
        return (t * out_scale) * (1.0 + jnp.tanh(t))

    def conv_chunk(c, carry):
        r = pl.multiple_of(c * CHUNK, CHUNK)
        qc_ref[pl.ds(r, CHUNK), :] = conv_silu(q_ref, cwq_ref, cbq_ref, 1.0, c, r).astype(BF16)
        kt_ref[c] = conv_silu(k_ref, cwk_ref, cbk_ref, MLSTM_HEAD_DIM ** -0.5, c, r).T.astype(BF16)
        return carry

    lax.fori_loop(0, nc, conv_chunk, 0, unroll=4)

    gc_ref[...] = pltpu.roll(gcol_ref[0], (GATE_LANES - 8 * head) % GATE_LANES, 1)

    for ref in (cnf_ref, mf_ref, cnb_ref, mb_ref):
        ref[...] = jnp.zeros(ref.shape, F32)

    mask_f = s_idx <= t_idx
    mask_b = s_idx >= t_idx

    def replicate(col):
        return jnp.broadcast_to(col, (col.shape[0], LANES))

    def direction(chunk, backward):
        r = pl.multiple_of(chunk * CHUNK, CHUNK)
        rows = grow_ref[0, chunk]
        cols = gc_ref[pl.ds(r, CHUNK), :]
        qc = qc_ref[pl.ds(r, CHUNK), :]
        kt = kt_ref[chunk]
        vv = v_ref[0, pl.ds(r, CHUNK), :]
        if backward:
            cm_rep, b_rep = replicate(cols[:, 2:3]), replicate(cols[:, 3:4])
            end = 0
            state, mask, r_row = (cnb_ref, mb_ref), mask_b, rows[2:3, :]
        else:
            cm_rep, b_rep = replicate(cols[:, 0:1]), replicate(cols[:, 1:2])
            end = CHUNK - 1
            state, mask, r_row = (cnf_ref, mf_ref), mask_f, rows[0:1, :]
        cm_end, g = cm_rep[end:end + 1, :], b_rep[end:end + 1, :]
        return _mlstm_chunk(qc, kt, vv, r_row, cm_rep, b_rep, cm_end, g, *state, mask), r

    def chunk_steps(j, interleave):
        steps = [direction(j, False), direction(nc - 1 - j, True)]
        if interleave:
            for gen, _ in steps:
                next(gen)
            hs = [next(gen) for gen, _ in steps]
            for gen, _ in steps:
                next(gen)
        else:
            hs = []
            for gen, _ in steps:
                next(gen)
                hs.append(next(gen))
                next(gen)
        return [(h, r) for h, (_, r) in zip(hs, steps)]

    def first_half(j, carry):
        for h_dir, r in chunk_steps(j, interleave=False):
            h_ref[pl.ds(r, CHUNK), :] = h_dir
        return carry

    gain = gain_ref[...]

    def finish(h_dir, r):
        h_sum = (h_ref[pl.ds(r, CHUNK), :] + h_dir) * _sigmoid(o_ref[0, pl.ds(r, CHUNK), :].astype(F32))
        normed = h_sum * lax.rsqrt(jnp.mean(h_sum * h_sum, axis=-1, keepdims=True) + EPS) * gain
        y_ref[0, pl.ds(r, CHUNK), :] = (normed * _silu(z_ref[0, pl.ds(r, CHUNK), :].astype(F32))).astype(BF16)

    def second_half(j, carry):
        for h_dir, r in chunk_steps(j, interleave=True):
            finish(h_dir, r)
        return carry

    lax.fori_loop(0, nc // 2, first_half, 0, unroll=2)
    lax.fori_loop(nc // 2, nc, second_half, 0)


def _mlstm(proj3, gcol, grow, conv_w, conv_b, head_gain):
    bsz, seq, _ = proj3.shape
    d = MLSTM_HEAD_DIM
    nh = MLSTM_HEADS
    nc = seq // CHUNK

    def col_block(offset):
        return pl.BlockSpec((1, seq, d), lambda b, h: (b, 0, offset + h))

    return pl.pallas_call(
        _mlstm_kernel,
        out_shape=jax.ShapeDtypeStruct((bsz, seq, nh * d), BF16),
        grid=(bsz, nh),
        in_specs=[
            col_block(0), col_block(nh), col_block(2 * nh), col_block(3 * nh), col_block(4 * nh),
            pl.BlockSpec((1, seq, GATE_LANES), lambda b, h: (b, 0, 0)),
            pl.BlockSpec((1, nc, 8, CHUNK), lambda b, h: (b, 0, h, 0)),
            pl.BlockSpec((3, d), lambda b, h: (0, h)),
            pl.BlockSpec((3, d), lambda b, h: (0, nh + h)),
            pl.BlockSpec((1, d), lambda b, h: (0, h)),
            pl.BlockSpec((1, d), lambda b, h: (0, nh + h)),
            pl.BlockSpec((1, d), lambda b, h: (0, h)),
        ],
        out_specs=pl.BlockSpec((1, seq, d), lambda b, h: (b, 0, h)),
        scratch_shapes=[
            pltpu.VMEM((seq, d), BF16),
            pltpu.VMEM((nc, d, CHUNK), BF16),
            pltpu.VMEM((seq, d), F32),
            pltpu.VMEM((seq, GATE_LANES), F32),
            pltpu.VMEM((d, d + LANES), F32), pltpu.VMEM((1, LANES), F32),
            pltpu.VMEM((d, d + LANES), F32), pltpu.VMEM((1, LANES), F32),
        ],
        compiler_params=pltpu.CompilerParams(
            dimension_semantics=("arbitrary", "arbitrary"), vmem_limit_bytes=VMEM_LIMIT_BYTES),
        name="mlstm",
    )(proj3, proj3, proj3, proj3, proj3, gcol, grow, conv_w, conv_w,
      conv_b.reshape(1, -1), conv_b.reshape(1, -1), head_gain.reshape(1, -1))


def _natten_bias_rows(rpb):
    n_dc = 2 * NA_COLS - 1
    n_dr = 2 * NA_ROWS - 1
    left = NA_COLS
    period = 64
    padded = jnp.pad(rpb.astype(F32) * LOG2E, ((0, 0), (0, 0), (left, period - left - n_dc)), constant_values=NEG)
    out = []
    for (c0, w), kb in zip(Q_BLOCKS, K_STARTS):
        base = kb - c0 + NA_COLS - 1 + left
        toe = jnp.tile(padded, (1, 1, w))[..., :(period - 1) * w].reshape(NA_HEADS, n_dr, w, period - 1)
        toe = toe[..., base:base + K_WIDTH]
        q_col = c0 + np.arange(w)[:, None]
        k_col = kb + np.arange(K_WIDTH)[None, :]
        win = np.clip(q_col - NA_COLS // 2, 0, GRID_W - NA_COLS)
        col_ok = jnp.asarray((k_col >= win) & (k_col < win + NA_COLS))
        toe = jnp.where(col_ok[None, None], toe, NEG).transpose(0, 2, 1, 3).reshape(NA_HEADS, w, n_dr * K_WIDTH)
        out.append(jnp.pad(toe, ((0, 0), (0, Q_BLOCK_MAX - w), (0, N_KEYS - n_dr * K_WIDTH)), constant_values=NEG))
    return jnp.stack(out, axis=1)


_GROUP_KINDS = (
    (0, lambda i: max(i - NA_ROWS // 2, 0)),
    (-(NA_ROWS // 2), lambda i: i),
    (Q_ROWS - SLAB_ROWS, lambda i: min(i + NA_ROWS // 2, NA_ROWS)),
)


def _build_bias(e_ref, tab_ref, slab_off, first_valid):
    for h in range(NA_HEADS):
        row_off = 0
        for blk, (_, w) in enumerate(Q_BLOCKS):
            e = e_ref[h, blk, 0:w, :]
            lane = lax.broadcasted_iota(jnp.int32, (w, N_KEYS), 1)
            for i in range(Q_ROWS):
                shift = slab_off - i + NA_ROWS - 1
                amount = (-shift * K_WIDTH) % N_KEYS
                moved = e if amount == 0 else pltpu.roll(e, amount, 1)
                lo = first_valid(i) * K_WIDTH
                ok = (lane >= lo) & (lane < lo + NA_ROWS * K_WIDTH)
                tab_ref[h, row_off + i * w:row_off + (i + 1) * w, :] = jnp.where(ok, moved, NEG)
            row_off += Q_ROWS * w


def _natten_kernel(q_ref, k0_ref, k1_ref, k2_ref, k3_ref, v0_ref, v1_ref, v2_ref, v3_ref, z_ref, e_ref,
                   y_ref, qf_ref, of_ref, tab_ref, sa_ref, sb_ref):
    k_refs = (k0_ref, k1_ref, k2_ref, k3_ref)
    v_refs = (v0_ref, v1_ref, v2_ref, v3_ref)
    piece_rows = SLAB_ROWS // 4
    group = pl.program_id(0)
    n_groups = pl.num_programs(0)

    first_batch = pl.program_id(1) == 0
    conditions = (group == 0, jnp.logical_and(group > 0, group < n_groups - 1), group == n_groups - 1)
    for cond, (slab_off, first_valid) in zip(conditions, _GROUP_KINDS):
        @pl.when(jnp.logical_and(first_batch, cond))
        def _(slab_off=slab_off, first_valid=first_valid):
            _build_bias(e_ref, tab_ref, slab_off, first_valid)

    qf_ref[...] = q_ref[0, 0].astype(F32) * (NA_HEAD_DIM ** -0.5 * LOG2E)

    items = []
    row_off = 0
    for (c0, w), kb in zip(Q_BLOCKS, K_STARTS):
        for pair in range(NA_HEADS // 2):
            items.append((c0, w, kb, pair, row_off))
        row_off += Q_ROWS * w
    s_refs = (sa_ref, sb_ref)

    def slab(refs, kb, lanes):
        return jnp.concatenate(
            [refs[a // piece_rows][0, 0, pl.ds((a % piece_rows) * GRID_W + kb, K_WIDTH), lanes]
             for a in range(SLAB_ROWS)], axis=0)

    def scores(idx):
        c0, w, kb, pair, row_off = items[idx]
        nq = Q_ROWS * w
        lanes = pl.ds(pair * LANES, LANES)
        first_head = lax.broadcasted_iota(jnp.int32, (nq, LANES), 1) < NA_HEAD_DIM
        qp = jnp.concatenate([qf_ref[pl.ds(i * GRID_W + c0, w), lanes] for i in range(Q_ROWS)], axis=0)
        qs = jnp.concatenate([jnp.where(first_head, qp, 0.0), jnp.where(first_head, 0.0, qp)],
                             axis=0).astype(BF16)
        s = lax.dot_general(qs, slab(k_refs, kb, lanes), (((1,), (1,)), ((), ())), preferred_element_type=F32)
        bias = jnp.concatenate([tab_ref[2 * pair, pl.ds(row_off, nq), :],
                                tab_ref[2 * pair + 1, pl.ds(row_off, nq), :]], axis=0)
        s_refs[idx % 2][0:2 * nq, :] = s + bias

    def attend(idx):
        c0, w, kb, pair, _ = items[idx]
        nq = Q_ROWS * w
        lanes = pl.ds(pair * LANES, LANES)
        first_head = lax.broadcasted_iota(jnp.int32, (nq, LANES), 1) < NA_HEAD_DIM
        s = s_refs[idx % 2][0:2 * nq, :]
        p = jnp.exp2(s - jnp.max(s, axis=-1, keepdims=True))
        o2 = jnp.dot(p.astype(BF16), slab(v_refs, kb, lanes), preferred_element_type=F32)
        o2 = o2 * (1.0 / jnp.sum(p, axis=-1, keepdims=True))
        o = jnp.where(first_head, o2[0:nq], o2[nq:2 * nq])
        for i in range(Q_ROWS):
            of_ref[pl.ds(i * GRID_W + c0, w), lanes] = o[i * w:(i + 1) * w]

    scores(0)
    for idx in range(len(items)):
        if idx + 1 < len(items):
            scores(idx + 1)
        attend(idx)
    y_ref[0, 0] = (of_ref[...] * _silu(z_ref[0, 0].astype(F32))).astype(BF16)


def _natten(proj3, bias_rows):
    bsz, seq, n_main = proj3.shape
    q_tok = Q_ROWS * GRID_W
    p_tok = (SLAB_ROWS // 4) * GRID_W
    groups = seq // q_tok
    pieces = seq // p_tok
    assert groups >= 2, "needs at least two 8-row query groups"
    w = NA_WIDTH
    qb, kb, vb, zb = (5 * D_MODEL) // w, (5 * D_MODEL) // w + 1, (5 * D_MODEL) // w + 2, (5 * D_MODEL) // w + 3
    proj_q = proj3.reshape(bsz, groups, q_tok, n_main)
    proj_p = proj3.reshape(bsz, pieces, p_tok, n_main)

    def slab_start(g):
        return jnp.clip(2 * g - 1, 0, pieces - 4)

    def piece_spec(col, a):
        return pl.BlockSpec((1, 1, p_tok, w), lambda g, b: (b, slab_start(g) + a, 0, col))

    return pl.pallas_call(
        _natten_kernel,
        out_shape=jax.ShapeDtypeStruct((bsz, groups, q_tok, w), BF16),
        grid=(groups, bsz),
        in_specs=[pl.BlockSpec((1, 1, q_tok, w), lambda g, b: (b, g, 0, qb))]
        + [piece_spec(kb, a) for a in range(4)] + [piece_spec(vb, a) for a in range(4)]
        + [pl.BlockSpec((1, 1, q_tok, w), lambda g, b: (b, g, 0, zb)),
           pl.BlockSpec(bias_rows.shape, lambda g, b: (0, 0, 0, 0))],
        out_specs=pl.BlockSpec((1, 1, q_tok, w), lambda g, b: (b, g, 0, 0)),
        scratch_shapes=[pltpu.VMEM((q_tok, w), F32), pltpu.VMEM((q_tok, w), F32),
                        pltpu.VMEM((NA_HEADS, q_tok, N_KEYS), F32),
                        pltpu.VMEM((2 * Q_ROWS * Q_BLOCK_MAX, N_KEYS), F32),
                        pltpu.VMEM((2 * Q_ROWS * Q_BLOCK_MAX, N_KEYS), F32)],
        compiler_params=pltpu.CompilerParams(
            dimension_semantics=("arbitrary", "arbitrary"), vmem_limit_bytes=VMEM_LIMIT_BYTES),
        name="natten",
    )(proj_q, *([proj_p] * 8), proj_q, bias_rows).reshape(bsz, seq, w)


def _merge_out_kernel(ya_ref, yb_ref, g0_ref, g1_ref, x_ref, mod_ref, bm_ref, wa_ref, wb_ref, wo_ref, fg_ref,
                      out_ref, m_ref):
    tm, d = x_ref.shape
    sub = min(MERGE_ROWS, tm)
    gate = mod_ref[0, :, 2 * d:3 * d]

    def merge(r):
        rows = pl.ds(r, sub)
        pa = jnp.dot(ya_ref[rows, :], wa_ref[...], preferred_element_type=F32)
        pb = jnp.dot(yb_ref[rows, :], wb_ref[...], preferred_element_type=F32)
        m_ref[rows, :] = (_sigmoid(g0_ref[rows, :].astype(F32) + bm_ref[0:1, :]) * pa
                          + _sigmoid(g1_ref[rows, :].astype(F32) + bm_ref[1:2, :]) * pb).astype(BF16)

    def project(r):
        rows = pl.ds(r, sub)
        mixed = jnp.dot(m_ref[rows, :], wo_ref[...], preferred_element_type=F32)
        xo = x_ref[rows, :] + gate * mixed
        out_ref[rows, :] = xo * lax.rsqrt(jnp.mean(xo * xo, axis=-1, keepdims=True) + EPS) * fg_ref[...]

    starts = list(range(0, tm, sub))
    merge(starts[0])
    for i, r in enumerate(starts):
        if i + 1 < len(starts):
            merge(starts[i + 1])
        project(r)


def _merge_out(ya2, yb2, proj2, x2, mod3, b_merge, wa, wb, wo, final_gain, seq, tm):
    n_tok, d = x2.shape
    g0 = (N_MAIN - 2 * d) // d

    def full(shape):
        return pl.BlockSpec(shape, lambda m: (0,) * len(shape))

    return pl.pallas_call(
        _merge_out_kernel,
        out_shape=jax.ShapeDtypeStruct((n_tok, d), F32),
        grid=(n_tok // tm,),
        in_specs=[
            pl.BlockSpec((tm, d), lambda m: (m, 0)),
            pl.BlockSpec((tm, NA_WIDTH), lambda m: (m, 0)),
            pl.BlockSpec((tm, d), lambda m: (m, g0)),
            pl.BlockSpec((tm, d), lambda m: (m, g0 + 1)),
            pl.BlockSpec((tm, d), lambda m: (m, 0)),
            pl.BlockSpec((1, 1, 3 * d), lambda m: ((m * tm) // seq, 0, 0)),
            full((2, d)), full((d, d)), full((NA_WIDTH, d)), full((d, d)), full((1, d)),
        ],
        out_specs=pl.BlockSpec((tm, d), lambda m: (m, 0)),
        scratch_shapes=[pltpu.VMEM((tm, d), BF16)],
        compiler_params=pltpu.CompilerParams(
            dimension_semantics=("arbitrary",), vmem_limit_bytes=VMEM_LIMIT_BYTES),
        name="merge_out",
    )(ya2, yb2, proj2, proj2, x2, mod3, b_merge, wa, wb, wo, final_gain.reshape(1, d))


def _gate_layout(n_cols):
    src = np.arange(n_cols)
    direction, gate, head = src // (2 * MLSTM_HEADS), (src // MLSTM_HEADS) % 2, src % MLSTM_HEADS
    return 8 * head + 2 * direction + gate


def _weight_prep_kernel(a_ref, b_ref, perm_ref, main_ref, gate_ref, *, gate_block, n_gates):
    n = pl.program_id(0)
    rows = a_ref[0]
    past_gates = jnp.concatenate([rows[n_gates:], b_ref[0]], axis=0)
    main_ref[...] = jnp.where(n >= gate_block, past_gates, rows).T.astype(BF16)

    @pl.when(n == gate_block)
    def _():
        gate_ref[...] = jnp.dot(perm_ref[...], rows[0:n_gates], preferred_element_type=F32).T.astype(BF16)


def _weight_prep(w_in, layer):
    _, d, n_in = w_in.shape
    n_gates = 4 * MLSTM_HEADS
    tn = D_MODEL
    g_lo = 5 * MLSTM_HEADS * MLSTM_HEAD_DIM
    assert n_in == N_MAIN + n_gates and g_lo % tn == 0 and tn % n_gates == 0
    perm = np.zeros((GATE_LANES, n_gates), np.float32)
    perm[_gate_layout(n_gates), np.arange(n_gates)] = 1.0
    w_t = jnp.swapaxes(w_in, 1, 2)
    return pl.pallas_call(
        functools.partial(_weight_prep_kernel, gate_block=g_lo // tn, n_gates=n_gates),
        out_shape=(jax.ShapeDtypeStruct((d, N_MAIN), BF16), jax.ShapeDtypeStruct((d, GATE_LANES), BF16)),
        grid=(N_MAIN // tn,),
        in_specs=[pl.BlockSpec((1, tn, d), lambda n: (layer, n, 0)),
                  pl.BlockSpec((1, n_gates, d), lambda n: (layer, (n + 1) * (tn // n_gates), 0)),
                  pl.BlockSpec((GATE_LANES, n_gates), lambda n: (0, 0))],
        out_specs=(pl.BlockSpec((d, tn), lambda n: (0, n)),
                   pl.BlockSpec((d, GATE_LANES), lambda n: (0, 0))),
        compiler_params=pltpu.CompilerParams(dimension_semantics=("arbitrary",), vmem_limit_bytes=VMEM_LIMIT_BYTES),
        name="weight_prep",
    )(w_t, w_t, jnp.asarray(perm))


def _token_tiles(seq):
    return min(2048, seq), min(1024, seq)


def _layer(x, mod, norm_gain, w_in, layer, conv_w, conv_b, b_igate, b_fgate, mlstm_norm_gain, rpb, w_proj_a, w_proj_b,
           b_merge, w_out, final_gain):
    bsz, seq, d = x.shape
    n_tok = bsz * seq
    wa = MLSTM_HEADS * MLSTM_HEAD_DIM
    w_main, w_gate = _weight_prep(w_in, layer)
    gate_bias = jnp.zeros((1, GATE_LANES), F32).at[0, _gate_layout(4 * MLSTM_HEADS)].set(
        jnp.stack([b_igate, b_fgate], axis=1).reshape(-1))
    tm_in, tm_out = _token_tiles(seq)

    x2 = x.reshape(n_tok, d)
    mod3 = mod.reshape(bsz, 1, 3 * d)
    proj2, gates2 = _in_proj(x2, mod3, norm_gain, w_main, w_gate, seq, tm_in)
    proj3 = proj2.reshape(bsz, seq, N_MAIN)
    gcol, grow = _gate_prep(gates2.reshape(bsz, seq, GATE_LANES), gate_bias)
    y_a = _mlstm(proj3, gcol, grow, conv_w, conv_b, mlstm_norm_gain)
    y_b = _natten(proj3, _natten_bias_rows(rpb))
    out2 = _merge_out(y_a.reshape(n_tok, wa), y_b.reshape(n_tok, NA_WIDTH), proj2, x2, mod3, b_merge,
                      w_proj_a.astype(BF16), w_proj_b.astype(BF16), w_out.astype(BF16), final_gain, seq, tm_out)
    return out2.reshape(bsz, seq, d)


def kernel(x, c, w_ada, b_ada, norm_gain, w_in, conv_w, conv_b, b_igate, b_fgate, mlstm_norm_gain, rpb, w_proj_a,
           w_proj_b, b_merge, w_out, final_gain):
    depth = w_ada.shape[0]
    assert depth == 1, "the fused final norm assumes a single layer"
    mod = _adaln_mod(c, w_ada[0], b_ada[0])
    return _layer(x, mod, norm_gain[0], w_in, 0, conv_w[0], conv_b[0], b_igate[0], b_fgate[0], mlstm_norm_gain[0],
                  rpb[0], w_proj_a[0], w_proj_b[0], b_merge[0], w_out[0], final_gain)
```

```python
import functools

import numpy as np
import jax
import jax.numpy as jnp
from jax import lax
from jax.experimental import pallas as pl
from jax.experimental.pallas import tpu as pltpu

F32 = jnp.float32
BF16 = jnp.bfloat16

D_MODEL = 1024
GRID_W = 64
MLSTM_HEADS = 4
MLSTM_HEAD_DIM = 256
NA_HEADS = 8
NA_WIDTH = 512
NA_HEAD_DIM = 64
NA_ROWS = 8
NA_COLS = 16
EPS = 1e-6
NEG = -1e30
LOG2E = 1.4426950408889634

VMEM_LIMIT_BYTES = 56 * 1024 * 1024
LANES = 128

CHUNK = 256

MERGE_ROWS = 512

N_MAIN = 9 * D_MODEL
GATE_LANES = 128

Q_ROWS = 8
SLAB_ROWS = 16
Q_BLOCKS = ((0, 24), (24, 16), (40, 24))
Q_BLOCK_MAX = 24
K_STARTS = (0, 16, 32)
K_WIDTH = 32
N_KEYS = SLAB_ROWS * K_WIDTH


def _sigmoid(x):
    return 1.0 / (1.0 + jnp.exp2(x * -LOG2E))


def _silu(x):
    return x * _sigmoid(x)


def _adaln_kernel(c_ref, w_ref, b_ref, o_ref):
    cond = _silu(c_ref[...]).astype(BF16)
    o_ref[...] = jnp.dot(cond, w_ref[...].astype(BF16), preferred_element_type=F32) + b_ref[...]


def _adaln_mod(c, w_ada, b_ada):
    bsz, d = c.shape
    n = w_ada.shape[1]
    tn = 512
    return pl.pallas_call(
        _adaln_kernel,
        out_shape=jax.ShapeDtypeStruct((bsz, n), F32),
        grid=(n // tn,),
        in_specs=[
            pl.BlockSpec((bsz, d), lambda j: (0, 0)),
            pl.BlockSpec((d, tn), lambda j: (0, j)),
            pl.BlockSpec((1, tn), lambda j: (0, j)),
        ],
        out_specs=pl.BlockSpec((bsz, tn), lambda j: (0, j)),
        compiler_params=pltpu.CompilerParams(dimension_semantics=("arbitrary",)),
        name="adaln_mod",
    )(c, w_ada, b_ada.reshape(1, n))


def _in_proj_kernel(x_ref, mod_ref, gain_ref, w_ref, wg_ref, proj_ref, gates_ref, h_ref, *, rows):
    tm, d = x_ref.shape
    n = pl.program_id(1)

    @pl.when(n == 0)
    def _():
        shift = mod_ref[0, :, 0:d]
        scale = mod_ref[0, :, d:2 * d]
        gain = gain_ref[...]

        def normalize(r):
            xf = x_ref[r:r + rows, :]
            y = xf * lax.rsqrt(jnp.mean(xf * xf, axis=-1, keepdims=True) + EPS) * gain
            h_ref[r:r + rows, :] = (y * (1.0 + scale) + shift).astype(BF16)

        def project(r):
            hb = h_ref[r:r + rows, :]
            proj_ref[r:r + rows, :] = jnp.dot(hb, w_ref[...], preferred_element_type=F32).astype(BF16)
            gates_ref[r:r + rows, :] = jnp.dot(hb, wg_ref[...], preferred_element_type=F32)

        starts = list(range(0, tm, rows))
        normalize(starts[0])
        for i, r in enumerate(starts):
            if i + 1 < len(starts):
                normalize(starts[i + 1])
            project(r)

    @pl.when(n > 0)
    def _():
        proj_ref[...] = jnp.dot(h_ref[...], w_ref[...], preferred_element_type=F32).astype(BF16)


def _in_proj(x2, mod3, norm_gain, w_main, w_gate, seq, tm):
    n_tok, d = x2.shape
    tn = D_MODEL
    return pl.pallas_call(
        functools.partial(_in_proj_kernel, rows=min(MERGE_ROWS, tm)),
        out_shape=(
            jax.ShapeDtypeStruct((n_tok, N_MAIN), BF16),
            jax.ShapeDtypeStruct((n_tok, GATE_LANES), F32),
        ),
        grid=(n_tok // tm, N_MAIN // tn),
        in_specs=[
            pl.BlockSpec((tm, d), lambda m, n: (m, 0)),
            pl.BlockSpec((1, 1, 3 * d), lambda m, n: ((m * tm) // seq, 0, 0)),
            pl.BlockSpec((1, d), lambda m, n: (0, 0)),
            pl.BlockSpec((d, tn), lambda m, n: (0, n)),
            pl.BlockSpec((d, GATE_LANES), lambda m, n: (0, 0)),
        ],
        out_specs=(
            pl.BlockSpec((tm, tn), lambda m, n: (m, n)),
            pl.BlockSpec((tm, GATE_LANES), lambda m, n: (m, 0)),
        ),
        scratch_shapes=[pltpu.VMEM((tm, d), BF16)],
        compiler_params=pltpu.CompilerParams(
            dimension_semantics=("arbitrary", "arbitrary"), vmem_limit_bytes=VMEM_LIMIT_BYTES),
        name="in_proj",
    )(x2, mod3, norm_gain.reshape(1, d), w_main, w_gate)


def _gate_prep_kernel(g_ref, bias_ref, col_ref, row_ref):
    nc = row_ref.shape[1]
    v = (g_ref[0] + bias_ref[...]).T
    i_gate, f_gate = v[0:8, :] * LOG2E, v[8:16, :]
    seq = v.shape[1]
    pos = lax.broadcasted_iota(jnp.int32, i_gate.shape, 1) % CHUNK
    is_bwd = lax.broadcasted_iota(jnp.int32, i_gate.shape, 0) >= MLSTM_HEADS

    def scan(x, combine, identity):
        pre, suf = x, x
        step = 1
        while step < CHUNK:
            pre = combine(pre, jnp.where(pos >= step, pltpu.roll(pre, step, 1), identity))
            suf = combine(suf, jnp.where(pos < CHUNK - step, pltpu.roll(suf, seq - step, 1), identity))
            step *= 2
        return jnp.where(is_bwd, suf, pre)

    b = scan((jnp.minimum(f_gate, 0.0) - jnp.log1p(jnp.exp(-jnp.abs(f_gate)))) * LOG2E, jnp.add, 0.0)
    r = i_gate - b
    cm = scan(r, jnp.maximum, NEG)
    col_ref[0] = jnp.concatenate([cm, b, jnp.zeros((GATE_LANES - 16, seq), F32)], axis=0).T
    for c in range(nc):
        row_ref[0, c] = r[:, c * CHUNK:(c + 1) * CHUNK]


def _gate_prep(gates3, bias_row):
    bsz, seq, _ = gates3.shape
    nc = seq // CHUNK
    rows = 2 * MLSTM_HEADS
    return pl.pallas_call(
        _gate_prep_kernel,
        out_shape=(
            jax.ShapeDtypeStruct((bsz, seq, GATE_LANES), F32),
            jax.ShapeDtypeStruct((bsz, nc, rows, CHUNK), F32),
        ),
        grid=(bsz,),
        in_specs=[
            pl.BlockSpec((1, seq, GATE_LANES), lambda b: (b, 0, 0)),
            pl.BlockSpec((1, GATE_LANES), lambda b: (0, 0)),
        ],
        out_specs=(
            pl.BlockSpec((1, seq, GATE_LANES), lambda b: (b, 0, 0)),
            pl.BlockSpec((1, nc, rows, CHUNK), lambda b: (b, 0, 0, 0)),
        ),
        compiler_params=pltpu.CompilerParams(dimension_semantics=("arbitrary",)),
        name="gate_prep",
    )(gates3, bias_row)


def _mlstm_chunk(qc, kt, vv, r_row, cm_rep, b_rep, cm_end, g, cn_ref, m_ref, causal_mask):
    d = vv.shape[1]
    cn = cn_ref[...]
    m_state = m_ref[...]
    m_col = jnp.maximum(m_state, cm_rep)
    w_inter = jnp.exp2(m_state - m_col)
    m_wide = jnp.concatenate([m_col, m_col], axis=1)
    d_mat = jnp.where(causal_mask, jnp.exp2(r_row - m_wide), 0.0)
    scores = jnp.dot(qc, kt, preferred_element_type=F32) * d_mat
    inter = jnp.dot(qc, cn.astype(BF16), preferred_element_type=F32)
    yield None
    w_wide = jnp.concatenate([w_inter, w_inter], axis=1)
    num = jnp.dot(scores.astype(BF16), vv, preferred_element_type=F32) + w_wide * inter[:, 0:d]
    den = jnp.sum(scores, axis=-1, keepdims=True) + w_inter * inter[:, d:]
    inv = 1.0 / jnp.maximum(jnp.abs(den), jnp.exp2(-(b_rep + m_col)))
    yield num * jnp.concatenate([inv, inv], axis=1)
    m_end = jnp.maximum(m_state, cm_end)
    w_row = jnp.exp2(r_row - jnp.concatenate([m_end, m_end], axis=1))
    decay = jnp.exp2(m_state - m_end)
    kw = kt * w_row.astype(BF16)
    v_aug = jnp.concatenate([vv, jnp.ones((vv.shape[0], LANES), BF16)], axis=1)
    decay_wide = jnp.concatenate([decay, decay, decay], axis=1)
    cn_ref[...] = decay_wide * cn + jnp.dot(kw, v_aug, preferred_element_type=F32)
    m_ref[...] = g + m_end
    yield None


def _mlstm_kernel(q_ref, k_ref, v_ref, o_ref, z_ref, gcol_ref, grow_ref, cwq_ref, cwk_ref, cbq_ref, cbk_ref,
                  gain_ref, y_ref, qc_ref, kt_ref, h_ref, gc_ref, cnf_ref, mf_ref, cnb_ref, mb_ref):
    seq, d = q_ref.shape[1], q_ref.shape[2]
    nc = seq // CHUNK
    head = pl.program_id(1)

    t_idx = lax.broadcasted_iota(jnp.int32, (CHUNK, CHUNK), 0)
    s_idx = lax.broadcasted_iota(jnp.int32, (CHUNK, CHUNK), 1)
    taps = jnp.concatenate([(s_idx == t_idx - 1).astype(BF16), (s_idx == t_idx + 1).astype(BF16)],
                           axis=0)
    edge = lax.broadcasted_iota(jnp.int32, (8, d), 0)

    def conv_silu(src_ref, w_ref, b_ref, out_scale, c, r):
        w0, w1, w2 = 0.5 * w_ref[0:1, :], 0.5 * w_ref[1:2, :], 0.5 * w_ref[2:3, :]
        src = src_ref[0, pl.ds(r, CHUNK), :]
        p = jnp.dot(taps, src, preferred_element_type=F32)
        t = p[0:CHUNK] * w0 + src.astype(F32) * w1 + p[CHUNK:2 * CHUNK] * w2 + 0.5 * b_ref[...]
        above = src_ref[0, pl.ds(pl.multiple_of(jnp.maximum(r - 16, 0), 16), 16), :][15:16, :].astype(F32)
        below = src_ref[0, pl.ds(pl.multiple_of(jnp.minimum(r + CHUNK, seq - 16), 16), 16), :][0:1, :].astype(F32)
        above = above * jnp.where(c > 0, w0, 0.0)
        below = below * jnp.where(c < nc - 1, w2, 0.0)
        t = jnp.concatenate([t[0:8] + jnp.where(edge == 0, above, 0.0), t[8:CHUNK - 8],
                             t[CHUNK - 8:CHUNK] + jnp.where(edge == 7, below, 0.0)], axis=0)
        return (t * out_scale) * (1.0 + jnp.tanh(t))

    def conv_chunk(c, carry):
        r = pl.multiple_of(c * CHUNK, CHUNK)
        qc_ref[pl.ds(r, CHUNK), :] = conv_silu(q_ref, cwq_ref, cbq_ref, 1.0, c, r).astype(BF16)
        kt_ref[c] = conv_silu(k_ref, cwk_ref, cbk_ref, MLSTM_HEAD_DIM ** -0.5, c, r).T.astype(BF16)
        return carry

    lax.fori_loop(0, nc, conv_chunk, 0, unroll=4)

    gc_ref[...] = pltpu.roll(gcol_ref[0], (GATE_LANES - head) % GATE_LANES, 1)
    gate_row = lax.broadcasted_iota(jnp.int32, (2 * MLSTM_HEADS, CHUNK), 0)

    for ref in (cnf_ref, mf_ref, cnb_ref, mb_ref):
        ref[...] = jnp.zeros(ref.shape, F32)

    mask_f = s_idx <= t_idx
    mask_b = s_idx >= t_idx

    def replicate(col):
        return jnp.broadcast_to(col, (col.shape[0], LANES))

    def direction(chunk, backward):
        r = pl.multiple_of(chunk * CHUNK, CHUNK)
        rows = grow_ref[0, chunk]
        cols = gc_ref[pl.ds(r, CHUNK), :]
        qc = qc_ref[pl.ds(r, CHUNK), :]
        kt = kt_ref[chunk]
        vv = v_ref[0, pl.ds(r, CHUNK), :]
        if backward:
            cm_rep, b_rep = replicate(cols[:, 4:5]), replicate(cols[:, 12:13])
            end = 0
            state, mask, this_row = (cnb_ref, mb_ref), mask_b, MLSTM_HEADS + head
        else:
            cm_rep, b_rep = replicate(cols[:, 0:1]), replicate(cols[:, 8:9])
            end = CHUNK - 1
            state, mask, this_row = (cnf_ref, mf_ref), mask_f, head
        r_row = jnp.sum(jnp.where(gate_row == this_row, rows, 0.0), axis=0, keepdims=True)
        cm_end, g = cm_rep[end:end + 1, :], b_rep[end:end + 1, :]
        return _mlstm_chunk(qc, kt, vv, r_row, cm_rep, b_rep, cm_end, g, *state, mask), r

    def chunk_steps(j, interleave):
        steps = [direction(j, False), direction(nc - 1 - j, True)]
        if interleave:
            for gen, _ in steps:
                next(gen)
            hs = [next(gen) for gen, _ in steps]
            for gen, _ in steps:
                next(gen)
        else:
            hs = []
            for gen, _ in steps:
                next(gen)
                hs.append(next(gen))
                next(gen)
        return [(h, r) for h, (_, r) in zip(hs, steps)]

    def first_half(j, carry):
        for h_dir, r in chunk_steps(j, interleave=False):
            h_ref[pl.ds(r, CHUNK), :] = h_dir
        return carry

    gain = gain_ref[...]

    def finish(h_dir, r):
        h_sum = (h_ref[pl.ds(r, CHUNK), :] + h_dir) * _sigmoid(o_ref[0, pl.ds(r, CHUNK), :].astype(F32))
        normed = h_sum * lax.rsqrt(jnp.mean(h_sum * h_sum, axis=-1, keepdims=True) + EPS) * gain
        y_ref[0, pl.ds(r, CHUNK), :] = (normed * _silu(z_ref[0, pl.ds(r, CHUNK), :].astype(F32))).astype(BF16)

    def second_half(j, carry):
        for h_dir, r in chunk_steps(j, interleave=True):
            finish(h_dir, r)
        return carry

    lax.fori_loop(0, nc // 2, first_half, 0, unroll=2)
    lax.fori_loop(nc // 2, nc, second_half, 0)


def _mlstm(proj3, gcol, grow, conv_w, conv_b, head_gain):
    bsz, seq, _ = proj3.shape
    d = MLSTM_HEAD_DIM
    nh = MLSTM_HEADS
    nc = seq // CHUNK

    def col_block(offset):
        return pl.BlockSpec((1, seq, d), lambda b, h: (b, 0, offset + h))

    return pl.pallas_call(
        _mlstm_kernel,
        out_shape=jax.ShapeDtypeStruct((bsz, seq, nh * d), BF16),
        grid=(bsz, nh),
        in_specs=[
            col_block(0), col_block(nh), col_block(2 * nh), col_block(3 * nh), col_block(4 * nh),
            pl.BlockSpec((1, seq, GATE_LANES), lambda b, h: (b, 0, 0)),
            pl.BlockSpec((1, nc, 2 * MLSTM_HEADS, CHUNK), lambda b, h: (b, 0, 0, 0)),
            pl.BlockSpec((3, d), lambda b, h: (0, h)),
            pl.BlockSpec((3, d), lambda b, h: (0, nh + h)),
            pl.BlockSpec((1, d), lambda b, h: (0, h)),
            pl.BlockSpec((1, d), lambda b, h: (0, nh + h)),
            pl.BlockSpec((1, d), lambda b, h: (0, h)),
        ],
        out_specs=pl.BlockSpec((1, seq, d), lambda b, h: (b, 0, h)),
        scratch_shapes=[
            pltpu.VMEM((seq, d), BF16),
            pltpu.VMEM((nc, d, CHUNK), BF16),
            pltpu.VMEM((seq, d), F32),
            pltpu.VMEM((seq, GATE_LANES), F32),
            pltpu.VMEM((d, d + LANES), F32), pltpu.VMEM((1, LANES), F32),
            pltpu.VMEM((d, d + LANES), F32), pltpu.VMEM((1, LANES), F32),
        ],
        compiler_params=pltpu.CompilerParams(
            dimension_semantics=("arbitrary", "arbitrary"), vmem_limit_bytes=VMEM_LIMIT_BYTES),
        name="mlstm",
    )(proj3, proj3, proj3, proj3, proj3, gcol, grow, conv_w, conv_w,
      conv_b.reshape(1, -1), conv_b.reshape(1, -1), head_gain.reshape(1, -1))


def _natten_bias_rows(rpb):
    n_dc = 2 * NA_COLS - 1
    n_dr = 2 * NA_ROWS - 1
    left = NA_COLS
    period = 64
    padded = jnp.pad(rpb.astype(F32) * LOG2E, ((0, 0), (0, 0), (left, period - left - n_dc)), constant_values=NEG)
    out = []
    for (c0, w), kb in zip(Q_BLOCKS, K_STARTS):
        base = kb - c0 + NA_COLS - 1 + left
        toe = jnp.tile(padded, (1, 1, w))[..., :(period - 1) * w].reshape(NA_HEADS, n_dr, w, period - 1)
        toe = toe[..., base:base + K_WIDTH]
        q_col = c0 + np.arange(w)[:, None]
        k_col = kb + np.arange(K_WIDTH)[None, :]
        win = np.clip(q_col - NA_COLS // 2, 0, GRID_W - NA_COLS)
        col_ok = jnp.asarray((k_col >= win) & (k_col < win + NA_COLS))
        toe = jnp.where(col_ok[None, None], toe, NEG).transpose(0, 2, 1, 3).reshape(NA_HEADS, w, n_dr * K_WIDTH)
        out.append(jnp.pad(toe, ((0, 0), (0, Q_BLOCK_MAX - w), (0, N_KEYS - n_dr * K_WIDTH)), constant_values=NEG))
    return jnp.stack(out, axis=1)


_GROUP_KINDS = (
    (0, lambda i: max(i - NA_ROWS // 2, 0)),
    (-(NA_ROWS // 2), lambda i: i),
    (Q_ROWS - SLAB_ROWS, lambda i: min(i + NA_ROWS // 2, NA_ROWS)),
)


def _build_bias(e_ref, tab_ref, slab_off, first_valid):
    for h in range(NA_HEADS):
        row_off = 0
        for blk, (_, w) in enumerate(Q_BLOCKS):
            e = e_ref[h, blk, 0:w, :]
            lane = lax.broadcasted_iota(jnp.int32, (w, N_KEYS), 1)
            for i in range(Q_ROWS):
                shift = slab_off - i + NA_ROWS - 1
                amount = (-shift * K_WIDTH) % N_KEYS
                moved = e if amount == 0 else pltpu.roll(e, amount, 1)
                lo = first_valid(i) * K_WIDTH
                ok = (lane >= lo) & (lane < lo + NA_ROWS * K_WIDTH)
                tab_ref[h, row_off + i * w:row_off + (i + 1) * w, :] = jnp.where(ok, moved, NEG)
            row_off += Q_ROWS * w


def _natten_kernel(q_ref, k0_ref, k1_ref, k2_ref, k3_ref, v0_ref, v1_ref, v2_ref, v3_ref, z_ref, e_ref,
                   y_ref, qf_ref, of_ref, tab_ref, sa_ref, sb_ref):
    k_refs = (k0_ref, k1_ref, k2_ref, k3_ref)
    v_refs = (v0_ref, v1_ref, v2_ref, v3_ref)
    piece_rows = SLAB_ROWS // 4
    group = pl.program_id(0)
    n_groups = pl.num_programs(0)

    first_batch = pl.program_id(1) == 0
    conditions = (group == 0, jnp.logical_and(group > 0, group < n_groups - 1), group == n_groups - 1)
    for cond, (slab_off, first_valid) in zip(conditions, _GROUP_KINDS):
        @pl.when(jnp.logical_and(first_batch, cond))
        def _(slab_off=slab_off, first_valid=first_valid):
            _build_bias(e_ref, tab_ref, slab_off, first_valid)

    qf_ref[...] = q_ref[0, 0].astype(F32) * (NA_HEAD_DIM ** -0.5 * LOG2E)

    items = []
    row_off = 0
    for (c0, w), kb in zip(Q_BLOCKS, K_STARTS):
        for pair in range(NA_HEADS // 2):
            items.append((c0, w, kb, pair, row_off))
        row_off += Q_ROWS * w
    s_refs = (sa_ref, sb_ref)

    def slab(refs, kb, lanes):
        return jnp.concatenate(
            [refs[a // piece_rows][0, 0, pl.ds((a % piece_rows) * GRID_W + kb, K_WIDTH), lanes]
             for a in range(SLAB_ROWS)], axis=0)

    def scores(idx):
        c0, w, kb, pair, row_off = items[idx]
        nq = Q_ROWS * w
        lanes = pl.ds(pair * LANES, LANES)
        first_head = lax.broadcasted_iota(jnp.int32, (nq, LANES), 1) < NA_HEAD_DIM
        qp = jnp.concatenate([qf_ref[pl.ds(i * GRID_W + c0, w), lanes] for i in range(Q_ROWS)], axis=0)
        qs = jnp.concatenate([jnp.where(first_head, qp, 0.0), jnp.where(first_head, 0.0, qp)],
                             axis=0).astype(BF16)
        s = lax.dot_general(qs, slab(k_refs, kb, lanes), (((1,), (1,)), ((), ())), preferred_element_type=F32)
        bias = jnp.concatenate([tab_ref[2 * pair, pl.ds(row_off, nq), :],
                                tab_ref[2 * pair + 1, pl.ds(row_off, nq), :]], axis=0)
        s_refs[idx % 2][0:2 * nq, :] = s + bias

    def attend(idx):
        c0, w, kb, pair, _ = items[idx]
        nq = Q_ROWS * w
        lanes = pl.ds(pair * LANES, LANES)
        first_head = lax.broadcasted_iota(jnp.int32, (nq, LANES), 1) < NA_HEAD_DIM
        s = s_refs[idx % 2][0:2 * nq, :]
        p = jnp.exp2(s - jnp.max(s, axis=-1, keepdims=True))
        o2 = jnp.dot(p.astype(BF16), slab(v_refs, kb, lanes), preferred_element_type=F32)
        o2 = o2 * (1.0 / jnp.sum(p, axis=-1, keepdims=True))
        o = jnp.where(first_head, o2[0:nq], o2[nq:2 * nq])
        for i in range(Q_ROWS):
            of_ref[pl.ds(i * GRID_W + c0, w), lanes] = o[i * w:(i + 1) * w]

    scores(0)
    for idx in range(len(items)):
        if idx + 1 < len(items):
            scores(idx + 1)
        attend(idx)
    y_ref[0, 0] = (of_ref[...] * _silu(z_ref[0, 0].astype(F32))).astype(BF16)


def _natten(proj3, bias_rows):
    bsz, seq, n_main = proj3.shape
    q_tok = Q_ROWS * GRID_W
    p_tok = (SLAB_ROWS // 4) * GRID_W
    groups = seq // q_tok
    pieces = seq // p_tok
    assert groups >= 2, "needs at least two 8-row query groups"
    w = NA_WIDTH
    qb, kb, vb, zb = (5 * D_MODEL) // w, (5 * D_MODEL) // w + 1, (5 * D_MODEL) // w + 2, (5 * D_MODEL) // w + 3
    proj_q = proj3.reshape(bsz, groups, q_tok, n_main)
    proj_p = proj3.reshape(bsz, pieces, p_tok, n_main)

    def slab_start(g):
        return jnp.clip(2 * g - 1, 0, pieces - 4)

    def piece_spec(col, a):
        return pl.BlockSpec((1, 1, p_tok, w), lambda g, b: (b, slab_start(g) + a, 0, col))

    return pl.pallas_call(
        _natten_kernel,
        out_shape=jax.ShapeDtypeStruct((bsz, groups, q_tok, w), BF16),
        grid=(groups, bsz),
        in_specs=[pl.BlockSpec((1, 1, q_tok, w), lambda g, b: (b, g, 0, qb))]
        + [piece_spec(kb, a) for a in range(4)] + [piece_spec(vb, a) for a in range(4)]
        + [pl.BlockSpec((1, 1, q_tok, w), lambda g, b: (b, g, 0, zb)),
           pl.BlockSpec(bias_rows.shape, lambda g, b: (0, 0, 0, 0))],
        out_specs=pl.BlockSpec((1, 1, q_tok, w), lambda g, b: (b, g, 0, 0)),
        scratch_shapes=[pltpu.VMEM((q_tok, w), F32), pltpu.VMEM((q_tok, w), F32),
                        pltpu.VMEM((NA_HEADS, q_tok, N_KEYS), F32),
                        pltpu.VMEM((2 * Q_ROWS * Q_BLOCK_MAX, N_KEYS), F32),
                        pltpu.VMEM((2 * Q_ROWS * Q_BLOCK_MAX, N_KEYS), F32)],
        compiler_params=pltpu.CompilerParams(
            dimension_semantics=("arbitrary", "arbitrary"), vmem_limit_bytes=VMEM_LIMIT_BYTES),
        name="natten",
    )(proj_q, *([proj_p] * 8), proj_q, bias_rows).reshape(bsz, seq, w)


def _merge_out_kernel(ya_ref, yb_ref, g0_ref, g1_ref, x_ref, mod_ref, bm_ref, wa_ref, wb_ref, wo_ref, fg_ref,
                      out_ref, m_ref):
    tm, d = x_ref.shape
    sub = min(MERGE_ROWS, tm)
    gate = mod_ref[0, :, 2 * d:3 * d]

    def merge(r):
        rows = pl.ds(r, sub)
        pa = jnp.dot(ya_ref[rows, :], wa_ref[...], preferred_element_type=F32)
        pb = jnp.dot(yb_ref[rows, :], wb_ref[...], preferred_element_type=F32)
        m_ref[rows, :] = (_sigmoid(g0_ref[rows, :].astype(F32) + bm_ref[0:1, :]) * pa
                          + _sigmoid(g1_ref[rows, :].astype(F32) + bm_ref[1:2, :]) * pb).astype(BF16)

    def project(r):
        rows = pl.ds(r, sub)
        mixed = jnp.dot(m_ref[rows, :], wo_ref[...], preferred_element_type=F32)
        xo = x_ref[rows, :] + gate * mixed
        out_ref[rows, :] = xo * lax.rsqrt(jnp.mean(xo * xo, axis=-1, keepdims=True) + EPS) * fg_ref[...]

    starts = list(range(0, tm, sub))
    merge(starts[0])
    for i, r in enumerate(starts):
        if i + 1 < len(starts):
            merge(starts[i + 1])
        project(r)


def _merge_out(ya2, yb2, proj2, x2, mod3, b_merge, wa, wb, wo, final_gain, seq, tm):
    n_tok, d = x2.shape
    g0 = (N_MAIN - 2 * d) // d

    def full(shape):
        return pl.BlockSpec(shape, lambda m: (0,) * len(shape))

    return pl.pallas_call(
        _merge_out_kernel,
        out_shape=jax.ShapeDtypeStruct((n_tok, d), F32),
        grid=(n_tok // tm,),
        in_specs=[
            pl.BlockSpec((tm, d), lambda m: (m, 0)),
            pl.BlockSpec((tm, NA_WIDTH), lambda m: (m, 0)),
            pl.BlockSpec((tm, d), lambda m: (m, g0)),
            pl.BlockSpec((tm, d), lambda m: (m, g0 + 1)),
            pl.BlockSpec((tm, d), lambda m: (m, 0)),
            pl.BlockSpec((1, 1, 3 * d), lambda m: ((m * tm) // seq, 0, 0)),
            full((2, d)), full((d, d)), full((NA_WIDTH, d)), full((d, d)), full((1, d)),
        ],
        out_specs=pl.BlockSpec((tm, d), lambda m: (m, 0)),
        scratch_shapes=[pltpu.VMEM((tm, d), BF16)],
        compiler_params=pltpu.CompilerParams(
            dimension_semantics=("arbitrary",), vmem_limit_bytes=VMEM_LIMIT_BYTES),
        name="merge_out",
    )(ya2, yb2, proj2, proj2, x2, mod3, b_merge, wa, wb, wo, final_gain.reshape(1, d))


def _gate_layout(n_cols):
    src = np.arange(n_cols)
    direction, gate, head = src // (2 * MLSTM_HEADS), (src // MLSTM_HEADS) % 2, src % MLSTM_HEADS
    return 2 * MLSTM_HEADS * gate + MLSTM_HEADS * direction + head


def _weight_prep_kernel(a_ref, b_ref, perm_ref, main_ref, gate_ref, *, gate_block, n_gates):
    n = pl.program_id(0)
    rows = a_ref[0]
    past_gates = jnp.concatenate([rows[n_gates:], b_ref[0]], axis=0)
    main_ref[...] = jnp.where(n >= gate_block, past_gates, rows).T.astype(BF16)

    @pl.when(n == gate_block)
    def _():
        gate_ref[...] = jnp.dot(perm_ref[...], rows[0:n_gates], preferred_element_type=F32).T.astype(BF16)


def _weight_prep(w_in, layer):
    _, d, n_in = w_in.shape
    n_gates = 4 * MLSTM_HEADS
    tn = D_MODEL
    g_lo = 5 * MLSTM_HEADS * MLSTM_HEAD_DIM
    assert n_in == N_MAIN + n_gates and g_lo % tn == 0 and tn % n_gates == 0
    perm = np.zeros((GATE_LANES, n_gates), np.float32)
    perm[_gate_layout(n_gates), np.arange(n_gates)] = 1.0
    w_t = jnp.swapaxes(w_in, 1, 2)
    return pl.pallas_call(
        functools.partial(_weight_prep_kernel, gate_block=g_lo // tn, n_gates=n_gates),
        out_shape=(jax.ShapeDtypeStruct((d, N_MAIN), BF16), jax.ShapeDtypeStruct((d, GATE_LANES), BF16)),
        grid=(N_MAIN // tn,),
        in_specs=[pl.BlockSpec((1, tn, d), lambda n: (layer, n, 0)),
                  pl.BlockSpec((1, n_gates, d), lambda n: (layer, (n + 1) * (tn // n_gates), 0)),
                  pl.BlockSpec((GATE_LANES, n_gates), lambda n: (0, 0))],
        out_specs=(pl.BlockSpec((d, tn), lambda n: (0, n)),
                   pl.BlockSpec((d, GATE_LANES), lambda n: (0, 0))),
        compiler_params=pltpu.CompilerParams(dimension_semantics=("arbitrary",), vmem_limit_bytes=VMEM_LIMIT_BYTES),
        name="weight_prep",
    )(w_t, w_t, jnp.asarray(perm))


def _token_tiles(seq):
    return min(2048, seq), min(1024, seq)


def _layer(x, mod, norm_gain, w_in, layer, conv_w, conv_b, b_igate, b_fgate, mlstm_norm_gain, rpb, w_proj_a, w_proj_b,
           b_merge, w_out, final_gain):
    bsz, seq, d = x.shape
    n_tok = bsz * seq
    wa = MLSTM_HEADS * MLSTM_HEAD_DIM
    w_main, w_gate = _weight_prep(w_in, layer)
    gate_bias = jnp.zeros((1, GATE_LANES), F32).at[0, _gate_layout(4 * MLSTM_HEADS)].set(
        jnp.stack([b_igate, b_fgate], axis=1).reshape(-1))
    tm_in, tm_out = _token_tiles(seq)

    x2 = x.reshape(n_tok, d)
    mod3 = mod.reshape(bsz, 1, 3 * d)
    proj2, gates2 = _in_proj(x2, mod3, norm_gain, w_main, w_gate, seq, tm_in)
    proj3 = proj2.reshape(bsz, seq, N_MAIN)
    gcol, grow = _gate_prep(gates2.reshape(bsz, seq, GATE_LANES), gate_bias)
    y_a = _mlstm(proj3, gcol, grow, conv_w, conv_b, mlstm_norm_gain)
    y_b = _natten(proj3, _natten_bias_rows(rpb))
    out2 = _merge_out(y_a.reshape(n_tok, wa), y_b.reshape(n_tok, NA_WIDTH), proj2, x2, mod3, b_merge,
                      w_proj_a.astype(BF16), w_proj_b.astype(BF16), w_out.astype(BF16), final_gain, seq, tm_out)
    return out2.reshape(bsz, seq, d)


def kernel(x, c, w_ada, b_ada, norm_gain, w_in, conv_w, conv_b, b_igate, b_fgate, mlstm_norm_gain, rpb, w_proj_a,
           w_proj_b, b_merge, w_out, final_gain):
    depth = w_ada.shape[0]
    assert depth == 1, "the fused final norm assumes a single layer"
    mod = _adaln_mod(c, w_ada[0], b_ada[0])
    return _layer(x, mod, norm_gain[0], w_in, 0, conv_w[0], conv_b[0], b_igate[0], b_fgate[0], mlstm_norm_gain[0],
                  rpb[0], w_proj_a[0], w_proj_b[0], b_merge[0], w_out[0], final_gain)
```

```python
import functools

import numpy as np
import jax
import jax.numpy as jnp
from jax import lax
from jax.experimental import pallas as pl
from jax.experimental.pallas import tpu as pltpu

F32 = jnp.float32
BF16 = jnp.bfloat16

D_MODEL = 1024
GRID_W = 64
MLSTM_HEADS = 4
MLSTM_HEAD_DIM = 256
NA_HEADS = 8
NA_WIDTH = 512
NA_HEAD_DIM = 64
NA_ROWS = 8
NA_COLS = 16
EPS = 1e-6
NEG = -1e30
LOG2E = 1.4426950408889634

VMEM_LIMIT_BYTES = 56 * 1024 * 1024
LANES = 128

CHUNK = 256

MERGE_ROWS = 512

N_MAIN = 9 * D_MODEL
GATE_LANES = 128

Q_ROWS = 8
SLAB_ROWS = 16
Q_BLOCKS = ((0, 24), (24, 16), (40, 24))
Q_BLOCK_MAX = 24
K_STARTS = (0, 16, 32)
K_WIDTH = 32
N_KEYS = SLAB_ROWS * K_WIDTH


def _sigmoid(x):
    return 1.0 / (1.0 + jnp.exp2(x * -LOG2E))


def _silu(x):
    return x * _sigmoid(x)


def _adaln_kernel(c_ref, w_ref, b_ref, o_ref):
    cond = _silu(c_ref[...]).astype(BF16)
    o_ref[...] = jnp.dot(cond, w_ref[...].astype(BF16), preferred_element_type=F32) + b_ref[...]


def _adaln_mod(c, w_ada, b_ada):
    bsz, d = c.shape
    n = w_ada.shape[1]
    tn = 512
    return pl.pallas_call(
        _adaln_kernel,
        out_shape=jax.ShapeDtypeStruct((bsz, n), F32),
        grid=(n // tn,),
        in_specs=[
            pl.BlockSpec((bsz, d), lambda j: (0, 0)),
            pl.BlockSpec((d, tn), lambda j: (0, j)),
            pl.BlockSpec((1, tn), lambda j: (0, j)),
        ],
        out_specs=pl.BlockSpec((bsz, tn), lambda j: (0, j)),
        compiler_params=pltpu.CompilerParams(dimension_semantics=("arbitrary",)),
        name="adaln_mod",
    )(c, w_ada, b_ada.reshape(1, n))


def _in_proj_kernel(x_ref, mod_ref, gain_ref, w_ref, wg_ref, proj_ref, gates_ref, h_ref, *, rows):
    tm, d = x_ref.shape
    n = pl.program_id(1)

    @pl.when(n == 0)
    def _():
        shift = mod_ref[0, :, 0:d]
        scale = mod_ref[0, :, d:2 * d]
        gain = gain_ref[...]

        def normalize(r):
            xf = x_ref[r:r + rows, :]
            y = xf * lax.rsqrt(jnp.mean(xf * xf, axis=-1, keepdims=True) + EPS) * gain
            h_ref[r:r + rows, :] = (y * (1.0 + scale) + shift).astype(BF16)

        def project(r):
            hb = h_ref[r:r + rows, :]
            proj_ref[r:r + rows, :] = jnp.dot(hb, w_ref[...], preferred_element_type=F32).astype(BF16)
            gates_ref[r:r + rows, :] = jnp.dot(hb, wg_ref[...], preferred_element_type=F32)

        starts = list(range(0, tm, rows))
        normalize(starts[0])
        for i, r in enumerate(starts):
            if i + 1 < len(starts):
                normalize(starts[i + 1])
            project(r)

    @pl.when(n > 0)
    def _():
        proj_ref[...] = jnp.dot(h_ref[...], w_ref[...], preferred_element_type=F32).astype(BF16)


def _in_proj(x2, mod3, norm_gain, w_main, w_gate, seq, tm):
    n_tok, d = x2.shape
    tn = N_MAIN // 6
    return pl.pallas_call(
        functools.partial(_in_proj_kernel, rows=min(MERGE_ROWS, tm)),
        out_shape=(
            jax.ShapeDtypeStruct((n_tok, N_MAIN), BF16),
            jax.ShapeDtypeStruct((n_tok, GATE_LANES), F32),
        ),
        grid=(n_tok // tm, N_MAIN // tn),
        in_specs=[
            pl.BlockSpec((tm, d), lambda m, n: (m, 0)),
            pl.BlockSpec((1, 1, 3 * d), lambda m, n: ((m * tm) // seq, 0, 0)),
            pl.BlockSpec((1, d), lambda m, n: (0, 0)),
            pl.BlockSpec((d, tn), lambda m, n: (0, n)),
            pl.BlockSpec((d, GATE_LANES), lambda m, n: (0, 0)),
        ],
        out_specs=(
            pl.BlockSpec((tm, tn), lambda m, n: (m, n)),
            pl.BlockSpec((tm, GATE_LANES), lambda m, n: (m, 0)),
        ),
        scratch_shapes=[pltpu.VMEM((tm, d), BF16)],
        compiler_params=pltpu.CompilerParams(
            dimension_semantics=("arbitrary", "arbitrary"), vmem_limit_bytes=VMEM_LIMIT_BYTES),
        name="in_proj",
    )(x2, mod3, norm_gain.reshape(1, d), w_main, w_gate)


def _gate_prep_kernel(g_ref, bias_ref, col_ref, row_ref):
    nc = row_ref.shape[1]
    v = (g_ref[0] + bias_ref[...]).T
    i_gate, f_gate = v[0:8, :] * LOG2E, v[8:16, :]
    seq = v.shape[1]
    pos = lax.broadcasted_iota(jnp.int32, i_gate.shape, 1) % CHUNK
    is_bwd = lax.broadcasted_iota(jnp.int32, i_gate.shape, 0) >= MLSTM_HEADS

    def scan(x, combine, identity):
        pre, suf = x, x
        step = 1
        while step < CHUNK:
            pre = combine(pre, jnp.where(pos >= step, pltpu.roll(pre, step, 1), identity))
            suf = combine(suf, jnp.where(pos < CHUNK - step, pltpu.roll(suf, seq - step, 1), identity))
            step *= 2
        return jnp.where(is_bwd, suf, pre)

    b = scan((jnp.minimum(f_gate, 0.0) - jnp.log1p(jnp.exp(-jnp.abs(f_gate)))) * LOG2E, jnp.add, 0.0)
    r = i_gate - b
    cm = scan(r, jnp.maximum, NEG)
    col_ref[0] = jnp.concatenate([cm, b, jnp.zeros((GATE_LANES - 16, seq), F32)], axis=0).T
    for c in range(nc):
        row_ref[0, c] = r[:, c * CHUNK:(c + 1) * CHUNK]


def _gate_prep(gates3, bias_row):
    bsz, seq, _ = gates3.shape
    nc = seq // CHUNK
    rows = 2 * MLSTM_HEADS
    return pl.pallas_call(
        _gate_prep_kernel,
        out_shape=(
            jax.ShapeDtypeStruct((bsz, seq, GATE_LANES), F32),
            jax.ShapeDtypeStruct((bsz, nc, rows, CHUNK), F32),
        ),
        grid=(bsz,),
        in_specs=[
            pl.BlockSpec((1, seq, GATE_LANES), lambda b: (b, 0, 0)),
            pl.BlockSpec((1, GATE_LANES), lambda b: (0, 0)),
        ],
        out_specs=(
            pl.BlockSpec((1, seq, GATE_LANES), lambda b: (b, 0, 0)),
            pl.BlockSpec((1, nc, rows, CHUNK), lambda b: (b, 0, 0, 0)),
        ),
        compiler_params=pltpu.CompilerParams(dimension_semantics=("arbitrary",)),
        name="gate_prep",
    )(gates3, bias_row)


def _mlstm_chunk(qc, kt, vv, r_row, cm_rep, b_rep, cm_end, g, cn_ref, m_ref, causal_mask):
    d = vv.shape[1]
    cn = cn_ref[...]
    m_state = m_ref[...]
    m_col = jnp.maximum(m_state, cm_rep)
    w_inter = jnp.exp2(m_state - m_col)
    m_wide = jnp.concatenate([m_col, m_col], axis=1)
    d_mat = jnp.where(causal_mask, jnp.exp2(r_row - m_wide), 0.0)
    scores = jnp.dot(qc, kt, preferred_element_type=F32) * d_mat
    inter = jnp.dot(qc, cn.astype(BF16), preferred_element_type=F32)
    yield None
    w_wide = jnp.concatenate([w_inter, w_inter], axis=1)
    num = jnp.dot(scores.astype(BF16), vv, preferred_element_type=F32) + w_wide * inter[:, 0:d]
    den = jnp.sum(scores, axis=-1, keepdims=True) + w_inter * inter[:, d:]
    inv = 1.0 / jnp.maximum(jnp.abs(den), jnp.exp2(-(b_rep + m_col)))
    yield num * jnp.concatenate([inv, inv], axis=1)
    m_end = jnp.maximum(m_state, cm_end)
    w_row = jnp.exp2(r_row - jnp.concatenate([m_end, m_end], axis=1))
    decay = jnp.exp2(m_state - m_end)
    kw = kt * w_row.astype(BF16)
    v_aug = jnp.concatenate([vv, jnp.ones((vv.shape[0], LANES), BF16)], axis=1)
    decay_wide = jnp.concatenate([decay, decay, decay], axis=1)
    cn_ref[...] = decay_wide * cn + jnp.dot(kw, v_aug, preferred_element_type=F32)
    m_ref[...] = g + m_end
    yield None


def _mlstm_kernel(q_ref, k_ref, v_ref, o_ref, z_ref, gcol_ref, grow_ref, cwq_ref, cwk_ref, cbq_ref, cbk_ref,
                  gain_ref, y_ref, qc_ref, kt_ref, h_ref, gc_ref, cnf_ref, mf_ref, cnb_ref, mb_ref):
    seq, d = q_ref.shape[1], q_ref.shape[2]
    nc = seq // CHUNK
    head = pl.program_id(1)

    t_idx = lax.broadcasted_iota(jnp.int32, (CHUNK, CHUNK), 0)
    s_idx = lax.broadcasted_iota(jnp.int32, (CHUNK, CHUNK), 1)
    taps = jnp.concatenate([(s_idx == t_idx - 1).astype(BF16), (s_idx == t_idx + 1).astype(BF16)],
                           axis=0)
    edge = lax.broadcasted_iota(jnp.int32, (8, d), 0)

    def conv_silu(src_ref, w_ref, b_ref, out_scale, c, r):
        w0, w1, w2 = 0.5 * w_ref[0:1, :], 0.5 * w_ref[1:2, :], 0.5 * w_ref[2:3, :]
        src = src_ref[0, pl.ds(r, CHUNK), :]
        p = jnp.dot(taps, src, preferred_element_type=F32)
        t = p[0:CHUNK] * w0 + src.astype(F32) * w1 + p[CHUNK:2 * CHUNK] * w2 + 0.5 * b_ref[...]
        above = src_ref[0, pl.ds(pl.multiple_of(jnp.maximum(r - 16, 0), 16), 16), :][15:16, :].astype(F32)
        below = src_ref[0, pl.ds(pl.multiple_of(jnp.minimum(r + CHUNK, seq - 16), 16), 16), :][0:1, :].astype(F32)
        above = above * jnp.where(c > 0, w0, 0.0)
        below = below * jnp.where(c < nc - 1, w2, 0.0)
        t = jnp.concatenate([t[0:8] + jnp.where(edge == 0, above, 0.0), t[8:CHUNK - 8],
                             t[CHUNK - 8:CHUNK] + jnp.where(edge == 7, below, 0.0)], axis=0)
        return (t * out_scale) * (1.0 + jnp.tanh(t))

    def conv_chunk(c, carry):
        r = pl.multiple_of(c * CHUNK, CHUNK)
        qc_ref[pl.ds(r, CHUNK), :] = conv_silu(q_ref, cwq_ref, cbq_ref, 1.0, c, r).astype(BF16)
        kt_ref[c] = conv_silu(k_ref, cwk_ref, cbk_ref, MLSTM_HEAD_DIM ** -0.5, c, r).T.astype(BF16)
        return carry

    lax.fori_loop(0, nc, conv_chunk, 0, unroll=8)

    gc_ref[...] = pltpu.roll(gcol_ref[0], (GATE_LANES - head) % GATE_LANES, 1)
    gate_row = lax.broadcasted_iota(jnp.int32, (2 * MLSTM_HEADS, CHUNK), 0)

    for ref in (cnf_ref, mf_ref, cnb_ref, mb_ref):
        ref[...] = jnp.zeros(ref.shape, F32)

    mask_f = s_idx <= t_idx
    mask_b = s_idx >= t_idx

    def replicate(col):
        return jnp.broadcast_to(col, (col.shape[0], LANES))

    def direction(chunk, backward):
        r = pl.multiple_of(chunk * CHUNK, CHUNK)
        rows = grow_ref[0, chunk]
        cols = gc_ref[pl.ds(r, CHUNK), :]
        qc = qc_ref[pl.ds(r, CHUNK), :]
        kt = kt_ref[chunk]
        vv = v_ref[0, pl.ds(r, CHUNK), :]
        if backward:
            cm_rep, b_rep = replicate(cols[:, 4:5]), replicate(cols[:, 12:13])
            end = 0
            state, mask, this_row = (cnb_ref, mb_ref), mask_b, MLSTM_HEADS + head
        else:
            cm_rep, b_rep = replicate(cols[:, 0:1]), replicate(cols[:, 8:9])
            end = CHUNK - 1
            state, mask, this_row = (cnf_ref, mf_ref), mask_f, head
        r_row = jnp.sum(jnp.where(gate_row == this_row, rows, 0.0), axis=0, keepdims=True)
        cm_end, g = cm_rep[end:end + 1, :], b_rep[end:end + 1, :]
        return _mlstm_chunk(qc, kt, vv, r_row, cm_rep, b_rep, cm_end, g, *state, mask), r

    def chunk_steps(j, interleave):
        steps = [direction(j, False), direction(nc - 1 - j, True)]
        if interleave:
            for gen, _ in steps:
                next(gen)
            hs = [next(gen) for gen, _ in steps]
            for gen, _ in steps:
                next(gen)
        else:
            hs = []
            for gen, _ in steps:
                next(gen)
                hs.append(next(gen))
                next(gen)
        return [(h, r) for h, (_, r) in zip(hs, steps)]

    def first_half(j, carry):
        for h_dir, r in chunk_steps(j, interleave=False):
            h_ref[pl.ds(r, CHUNK), :] = h_dir
        return carry

    gain = gain_ref[...]

    def finish(h_dir, r):
        h_sum = (h_ref[pl.ds(r, CHUNK), :] + h_dir) * _sigmoid(o_ref[0, pl.ds(r, CHUNK), :].astype(F32))
        normed = h_sum * lax.rsqrt(jnp.mean(h_sum * h_sum, axis=-1, keepdims=True) + EPS) * gain
        y_ref[0, pl.ds(r, CHUNK), :] = (normed * _silu(z_ref[0, pl.ds(r, CHUNK), :].astype(F32))).astype(BF16)

    def second_half(j, carry):
        for h_dir, r in chunk_steps(j, interleave=True):
            finish(h_dir, r)
        return carry

    lax.fori_loop(0, nc // 2, first_half, 0, unroll=2)
    lax.fori_loop(nc // 2, nc, second_half, 0)


def _mlstm(proj3, gcol, grow, conv_w, conv_b, head_gain):
    bsz, seq, _ = proj3.shape
    d = MLSTM_HEAD_DIM
    nh = MLSTM_HEADS
    nc = seq // CHUNK

    def col_block(offset):
        return pl.BlockSpec((1, seq, d), lambda b, h: (b, 0, offset + h))

    return pl.pallas_call(
        _mlstm_kernel,
        out_shape=jax.ShapeDtypeStruct((bsz, seq, nh * d), BF16),
        grid=(bsz, nh),
        in_specs=[
            col_block(0), col_block(nh), col_block(2 * nh), col_block(3 * nh), col_block(4 * nh),
            pl.BlockSpec((1, seq, GATE_LANES), lambda b, h: (b, 0, 0)),
            pl.BlockSpec((1, nc, 2 * MLSTM_HEADS, CHUNK), lambda b, h: (b, 0, 0, 0)),
            pl.BlockSpec((3, d), lambda b, h: (0, h)),
            pl.BlockSpec((3, d), lambda b, h: (0, nh + h)),
            pl.BlockSpec((1, d), lambda b, h: (0, h)),
            pl.BlockSpec((1, d), lambda b, h: (0, nh + h)),
            pl.BlockSpec((1, d), lambda b, h: (0, h)),
        ],
        out_specs=pl.BlockSpec((1, seq, d), lambda b, h: (b, 0, h)),
        scratch_shapes=[
            pltpu.VMEM((seq, d), BF16),
            pltpu.VMEM((nc, d, CHUNK), BF16),
            pltpu.VMEM((seq, d), F32),
            pltpu.VMEM((seq, GATE_LANES), F32),
            pltpu.VMEM((d, d + LANES), F32), pltpu.VMEM((1, LANES), F32),
            pltpu.VMEM((d, d + LANES), F32), pltpu.VMEM((1, LANES), F32),
        ],
        compiler_params=pltpu.CompilerParams(
            dimension_semantics=("arbitrary", "arbitrary"), vmem_limit_bytes=VMEM_LIMIT_BYTES),
        name="mlstm",
    )(proj3, proj3, proj3, proj3, proj3, gcol, grow, conv_w, conv_w,
      conv_b.reshape(1, -1), conv_b.reshape(1, -1), head_gain.reshape(1, -1))


def _natten_bias_rows(rpb):
    n_dc = 2 * NA_COLS - 1
    n_dr = 2 * NA_ROWS - 1
    left = NA_COLS
    period = 64
    padded = jnp.pad(rpb.astype(F32) * LOG2E, ((0, 0), (0, 0), (left, period - left - n_dc)), constant_values=NEG)
    out = []
    for (c0, w), kb in zip(Q_BLOCKS, K_STARTS):
        base = kb - c0 + NA_COLS - 1 + left
        toe = jnp.tile(padded, (1, 1, w))[..., :(period - 1) * w].reshape(NA_HEADS, n_dr, w, period - 1)
        toe = toe[..., base:base + K_WIDTH]
        q_col = c0 + np.arange(w)[:, None]
        k_col = kb + np.arange(K_WIDTH)[None, :]
        win = np.clip(q_col - NA_COLS // 2, 0, GRID_W - NA_COLS)
        col_ok = jnp.asarray((k_col >= win) & (k_col < win + NA_COLS))
        toe = jnp.where(col_ok[None, None], toe, NEG).transpose(0, 2, 1, 3).reshape(NA_HEADS, w, n_dr * K_WIDTH)
        out.append(jnp.pad(toe, ((0, 0), (0, Q_BLOCK_MAX - w), (0, N_KEYS - n_dr * K_WIDTH)), constant_values=NEG))
    return jnp.stack(out, axis=1)


_GROUP_KINDS = (
    (0, lambda i: max(i - NA_ROWS // 2, 0)),
    (-(NA_ROWS // 2), lambda i: i),
    (Q_ROWS - SLAB_ROWS, lambda i: min(i + NA_ROWS // 2, NA_ROWS)),
)


def _build_bias(e_ref, tab_ref, slab_off, first_valid):
    for h in range(NA_HEADS):
        row_off = 0
        for blk, (_, w) in enumerate(Q_BLOCKS):
            e = e_ref[h, blk, 0:w, :]
            lane = lax.broadcasted_iota(jnp.int32, (w, N_KEYS), 1)
            for i in range(Q_ROWS):
                shift = slab_off - i + NA_ROWS - 1
                amount = (-shift * K_WIDTH) % N_KEYS
                moved = e if amount == 0 else pltpu.roll(e, amount, 1)
                lo = first_valid(i) * K_WIDTH
                ok = (lane >= lo) & (lane < lo + NA_ROWS * K_WIDTH)
                tab_ref[h, row_off + i * w:row_off + (i + 1) * w, :] = jnp.where(ok, moved, NEG)
            row_off += Q_ROWS * w


def _natten_kernel(q_ref, k0_ref, k1_ref, k2_ref, k3_ref, v0_ref, v1_ref, v2_ref, v3_ref, z_ref, e_ref,
                   y_ref, qf_ref, of_ref, tab_ref, sa_ref, sb_ref):
    k_refs = (k0_ref, k1_ref, k2_ref, k3_ref)
    v_refs = (v0_ref, v1_ref, v2_ref, v3_ref)
    piece_rows = SLAB_ROWS // 4
    group = pl.program_id(0)
    n_groups = pl.num_programs(0)

    first_batch = pl.program_id(1) == 0
    conditions = (group == 0, jnp.logical_and(group > 0, group < n_groups - 1), group == n_groups - 1)
    for cond, (slab_off, first_valid) in zip(conditions, _GROUP_KINDS):
        @pl.when(jnp.logical_and(first_batch, cond))
        def _(slab_off=slab_off, first_valid=first_valid):
            _build_bias(e_ref, tab_ref, slab_off, first_valid)

    qf_ref[...] = q_ref[0, 0].astype(F32) * (NA_HEAD_DIM ** -0.5 * LOG2E)

    items = []
    row_off = 0
    for (c0, w), kb in zip(Q_BLOCKS, K_STARTS):
        for pair in range(NA_HEADS // 2):
            items.append((c0, w, kb, pair, row_off))
        row_off += Q_ROWS * w
    s_refs = (sa_ref, sb_ref)

    def slab(refs, kb, lanes):
        return jnp.concatenate(
            [refs[a // piece_rows][0, 0, pl.ds((a % piece_rows) * GRID_W + kb, K_WIDTH), lanes]
             for a in range(SLAB_ROWS)], axis=0)

    def scores(idx):
        c0, w, kb, pair, row_off = items[idx]
        nq = Q_ROWS * w
        lanes = pl.ds(pair * LANES, LANES)
        first_head = lax.broadcasted_iota(jnp.int32, (nq, LANES), 1) < NA_HEAD_DIM
        qp = jnp.concatenate([qf_ref[pl.ds(i * GRID_W + c0, w), lanes] for i in range(Q_ROWS)], axis=0)
        qs = jnp.concatenate([jnp.where(first_head, qp, 0.0), jnp.where(first_head, 0.0, qp)],
                             axis=0).astype(BF16)
        s = lax.dot_general(qs, slab(k_refs, kb, lanes), (((1,), (1,)), ((), ())), preferred_element_type=F32)
        bias = jnp.concatenate([tab_ref[2 * pair, pl.ds(row_off, nq), :],
                                tab_ref[2 * pair + 1, pl.ds(row_off, nq), :]], axis=0)
        s_refs[idx % 2][0:2 * nq, :] = s + bias

    def attend(idx):
        c0, w, kb, pair, _ = items[idx]
        nq = Q_ROWS * w
        lanes = pl.ds(pair * LANES, LANES)
        first_head = lax.broadcasted_iota(jnp.int32, (nq, LANES), 1) < NA_HEAD_DIM
        s = s_refs[idx % 2][0:2 * nq, :]
        p = jnp.exp2(s - jnp.max(s, axis=-1, keepdims=True))
        o2 = jnp.dot(p.astype(BF16), slab(v_refs, kb, lanes), preferred_element_type=F32)
        o2 = o2 * (1.0 / jnp.sum(p, axis=-1, keepdims=True))
        o = jnp.where(first_head, o2[0:nq], o2[nq:2 * nq])
        for i in range(Q_ROWS):
            of_ref[pl.ds(i * GRID_W + c0, w), lanes] = o[i * w:(i + 1) * w]

    scores(0)
    for idx in range(len(items)):
        if idx + 1 < len(items):
            scores(idx + 1)
        attend(idx)
    y_ref[0, 0] = (of_ref[...] * _silu(z_ref[0, 0].astype(F32))).astype(BF16)


def _natten(proj3, bias_rows):
    bsz, seq, n_main = proj3.shape
    q_tok = Q_ROWS * GRID_W
    p_tok = (SLAB_ROWS // 4) * GRID_W
    groups = seq // q_tok
    pieces = seq // p_tok
    assert groups >= 2, "needs at least two 8-row query groups"
    w = NA_WIDTH
    qb, kb, vb, zb = (5 * D_MODEL) // w, (5 * D_MODEL) // w + 1, (5 * D_MODEL) // w + 2, (5 * D_MODEL) // w + 3
    proj_q = proj3.reshape(bsz, groups, q_tok, n_main)
    proj_p = proj3.reshape(bsz, pieces, p_tok, n_main)

    def slab_start(g):
        return jnp.clip(2 * g - 1, 0, pieces - 4)

    def piece_spec(col, a):
        return pl.BlockSpec((1, 1, p_tok, w), lambda g, b: (b, slab_start(g) + a, 0, col))

    return pl.pallas_call(
        _natten_kernel,
        out_shape=jax.ShapeDtypeStruct((bsz, groups, q_tok, w), BF16),
        grid=(groups, bsz),
        in_specs=[pl.BlockSpec((1, 1, q_tok, w), lambda g, b: (b, g, 0, qb))]
        + [piece_spec(kb, a) for a in range(4)] + [piece_spec(vb, a) for a in range(4)]
        + [pl.BlockSpec((1, 1, q_tok, w), lambda g, b: (b, g, 0, zb)),
           pl.BlockSpec(bias_rows.shape, lambda g, b: (0, 0, 0, 0))],
        out_specs=pl.BlockSpec((1, 1, q_tok, w), lambda g, b: (b, g, 0, 0)),
        scratch_shapes=[pltpu.VMEM((q_tok, w), F32), pltpu.VMEM((q_tok, w), F32),
                        pltpu.VMEM((NA_HEADS, q_tok, N_KEYS), F32),
                        pltpu.VMEM((2 * Q_ROWS * Q_BLOCK_MAX, N_KEYS), F32),
                        pltpu.VMEM((2 * Q_ROWS * Q_BLOCK_MAX, N_KEYS), F32)],
        compiler_params=pltpu.CompilerParams(
            dimension_semantics=("arbitrary", "arbitrary"), vmem_limit_bytes=VMEM_LIMIT_BYTES),
        name="natten",
    )(proj_q, *([proj_p] * 8), proj_q, bias_rows).reshape(bsz, seq, w)


def _merge_out_kernel(ya_ref, yb_ref, g0_ref, g1_ref, x_ref, mod_ref, bm_ref, wa_ref, wb_ref, wo_ref, fg_ref,
                      out_ref, m_ref):
    tm, d = x_ref.shape
    sub = min(MERGE_ROWS, tm)
    gate = mod_ref[0, :, 2 * d:3 * d]

    def merge(r):
        rows = pl.ds(r, sub)
        pa = jnp.dot(ya_ref[rows, :], wa_ref[...], preferred_element_type=F32)
        pb = jnp.dot(yb_ref[rows, :], wb_ref[...], preferred_element_type=F32)
        m_ref[rows, :] = (_sigmoid(g0_ref[rows, :].astype(F32) + bm_ref[0:1, :]) * pa
                          + _sigmoid(g1_ref[rows, :].astype(F32) + bm_ref[1:2, :]) * pb).astype(BF16)

    def project(r):
        rows = pl.ds(r, sub)
        mixed = jnp.dot(m_ref[rows, :], wo_ref[...], preferred_element_type=F32)
        xo = x_ref[rows, :] + gate * mixed
        out_ref[rows, :] = xo * lax.rsqrt(jnp.mean(xo * xo, axis=-1, keepdims=True) + EPS) * fg_ref[...]

    starts = list(range(0, tm, sub))
    merge(starts[0])
    for i, r in enumerate(starts):
        if i + 1 < len(starts):
            merge(starts[i + 1])
        project(r)


def _merge_out(ya2, yb2, proj2, x2, mod3, b_merge, wa, wb, wo, final_gain, seq, tm):
    n_tok, d = x2.shape
    g0 = (N_MAIN - 2 * d) // d

    def full(shape):
        return pl.BlockSpec(shape, lambda m: (0,) * len(shape))

    return pl.pallas_call(
        _merge_out_kernel,
        out_shape=jax.ShapeDtypeStruct((n_tok, d), F32),
        grid=(n_tok // tm,),
        in_specs=[
            pl.BlockSpec((tm, d), lambda m: (m, 0)),
            pl.BlockSpec((tm, NA_WIDTH), lambda m: (m, 0)),
            pl.BlockSpec((tm, d), lambda m: (m, g0)),
            pl.BlockSpec((tm, d), lambda m: (m, g0 + 1)),
            pl.BlockSpec((tm, d), lambda m: (m, 0)),
            pl.BlockSpec((1, 1, 3 * d), lambda m: ((m * tm) // seq, 0, 0)),
            full((2, d)), full((d, d)), full((NA_WIDTH, d)), full((d, d)), full((1, d)),
        ],
        out_specs=pl.BlockSpec((tm, d), lambda m: (m, 0)),
        scratch_shapes=[pltpu.VMEM((tm, d), BF16)],
        compiler_params=pltpu.CompilerParams(
            dimension_semantics=("arbitrary",), vmem_limit_bytes=VMEM_LIMIT_BYTES),
        name="merge_out",
    )(ya2, yb2, proj2, proj2, x2, mod3, b_merge, wa, wb, wo, final_gain.reshape(1, d))


def _gate_layout(n_cols):
    src = np.arange(n_cols)
    direction, gate, head = src // (2 * MLSTM_HEADS), (src // MLSTM_HEADS) % 2, src % MLSTM_HEADS
    return 2 * MLSTM_HEADS * gate + MLSTM_HEADS * direction + head


def _weight_prep_kernel(a_ref, b_ref, perm_ref, main_ref, gate_ref, *, gate_block, n_gates):
    n = pl.program_id(0)
    rows = a_ref[0]
    past_gates = jnp.concatenate([rows[n_gates:], b_ref[0]], axis=0)
    main_ref[...] = jnp.where(n >= gate_block, past_gates, rows).T.astype(BF16)

    @pl.when(n == gate_block)
    def _():
        gate_ref[...] = jnp.dot(perm_ref[...], rows[0:n_gates], preferred_element_type=F32).T.astype(BF16)


def _weight_prep(w_in, layer):
    _, d, n_in = w_in.shape
    n_gates = 4 * MLSTM_HEADS
    tn = D_MODEL
    g_lo = 5 * MLSTM_HEADS * MLSTM_HEAD_DIM
    assert n_in == N_MAIN + n_gates and g_lo % tn == 0 and tn % n_gates == 0
    perm = np.zeros((GATE_LANES, n_gates), np.float32)
    perm[_gate_layout(n_gates), np.arange(n_gates)] = 1.0
    w_t = jnp.swapaxes(w_in, 1, 2)
    return pl.pallas_call(
        functools.partial(_weight_prep_kernel, gate_block=g_lo // tn, n_gates=n_gates),
        out_shape=(jax.ShapeDtypeStruct((d, N_MAIN), BF16), jax.ShapeDtypeStruct((d, GATE_LANES), BF16)),
        grid=(N_MAIN // tn,),
        in_specs=[pl.BlockSpec((1, tn, d), lambda n: (layer, n, 0)),
                  pl.BlockSpec((1, n_gates, d), lambda n: (layer, (n + 1) * (tn // n_gates), 0)),
                  pl.BlockSpec((GATE_LANES, n_gates), lambda n: (0, 0))],
        out_specs=(pl.BlockSpec((d, tn), lambda n: (0, n)),
                   pl.BlockSpec((d, GATE_LANES), lambda n: (0, 0))),
        compiler_params=pltpu.CompilerParams(dimension_semantics=("arbitrary",), vmem_limit_bytes=VMEM_LIMIT_BYTES),
        name="weight_prep",
    )(w_t, w_t, jnp.asarray(perm))


def _token_tiles(seq):
    return min(2048, seq), min(1024, seq)


def _layer(x, mod, norm_gain, w_in, layer, conv_w, conv_b, b_igate, b_fgate, mlstm_norm_gain, rpb, w_proj_a, w_proj_b,
           b_merge, w_out, final_gain):
    bsz, seq, d = x.shape
    n_tok = bsz * seq
    wa = MLSTM_HEADS * MLSTM_HEAD_DIM
    w_main, w_gate = _weight_prep(w_in, layer)
    gate_bias = jnp.zeros((1, GATE_LANES), F32).at[0, _gate_layout(4 * MLSTM_HEADS)].set(
        jnp.stack([b_igate, b_fgate], axis=1).reshape(-1))
    tm_in, tm_out = _token_tiles(seq)

    x2 = x.reshape(n_tok, d)
    mod3 = mod.reshape(bsz, 1, 3 * d)
    proj2, gates2 = _in_proj(x2, mod3, norm_gain, w_main, w_gate, seq, tm_in)
    proj3 = proj2.reshape(bsz, seq, N_MAIN)
    gcol, grow = _gate_prep(gates2.reshape(bsz, seq, GATE_LANES), gate_bias)
    y_a = _mlstm(proj3, gcol, grow, conv_w, conv_b, mlstm_norm_gain)
    y_b = _natten(proj3, _natten_bias_rows(rpb))
    out2 = _merge_out(y_a.reshape(n_tok, wa), y_b.reshape(n_tok, NA_WIDTH), proj2, x2, mod3, b_merge,
                      w_proj_a.astype(BF16), w_proj_b.astype(BF16), w_out.astype(BF16), final_gain, seq, tm_out)
    return out2.reshape(bsz, seq, d)


def kernel(x, c, w_ada, b_ada, norm_gain, w_in, conv_w, conv_b, b_igate, b_fgate, mlstm_norm_gain, rpb, w_proj_a,
           w_proj_b, b_merge, w_out, final_gain):
    depth = w_ada.shape[0]
    assert depth == 1, "the fused final norm assumes a single layer"
    mod = _adaln_mod(c, w_ada[0], b_ada[0])
    return _layer(x, mod, norm_gain[0], w_in, 0, conv_w[0], conv_b[0], b_igate[0], b_fgate[0], mlstm_norm_gain[0],
                  rpb[0], w_proj_a[0], w_proj_b[0], b_merge[0], w_out[0], final_gain)
```

```python
import functools

import numpy as np
import jax
import jax.numpy as jnp
from jax import lax
from jax.experimental import pallas as pl
from jax.experimental.pallas import tpu as pltpu

F32 = jnp.float32
BF16 = jnp.bfloat16

D_MODEL = 1024
GRID_W = 64
MLSTM_HEADS = 4
MLSTM_HEAD_DIM = 256
NA_HEADS = 8
NA_WIDTH = 512
NA_HEAD_DIM = 64
NA_ROWS = 8
NA_COLS = 16
EPS = 1e-6
NEG = -1e30
LOG2E = 1.4426950408889634

VMEM_LIMIT_BYTES = 56 * 1024 * 1024
IN_PROJ_VMEM_LIMIT_BYTES = 60000 * 1024
LANES = 128

CHUNK = 256

MERGE_ROWS = 512

N_MAIN = 9 * D_MODEL
GATE_LANES = 128

Q_ROWS = 8
SLAB_ROWS = 16
Q_BLOCKS = ((0, 24), (24, 16), (40, 24))
Q_BLOCK_MAX = 24
K_STARTS = (0, 16, 32)
K_WIDTH = 32
N_KEYS = SLAB_ROWS * K_WIDTH


def _sigmoid(x):
    return 1.0 / (1.0 + jnp.exp2(x * -LOG2E))


def _silu(x):
    return x * _sigmoid(x)


def _adaln_kernel(c_ref, w_ref, b_ref, o_ref):
    cond = _silu(c_ref[...]).astype(BF16)
    o_ref[...] = jnp.dot(cond, w_ref[...].astype(BF16), preferred_element_type=F32) + b_ref[...]


def _adaln_mod(c, w_ada, b_ada):
    bsz, d = c.shape
    n = w_ada.shape[1]
    tn = 512
    return pl.pallas_call(
        _adaln_kernel,
        out_shape=jax.ShapeDtypeStruct((bsz, n), F32),
        grid=(n // tn,),
        in_specs=[
            pl.BlockSpec((bsz, d), lambda j: (0, 0)),
            pl.BlockSpec((d, tn), lambda j: (0, j)),
            pl.BlockSpec((1, tn), lambda j: (0, j)),
        ],
        out_specs=pl.BlockSpec((bsz, tn), lambda j: (0, j)),
        compiler_params=pltpu.CompilerParams(dimension_semantics=("arbitrary",)),
        name="adaln_mod",
    )(c, w_ada, b_ada.reshape(1, n))


def _in_proj_kernel(x_ref, mod_ref, gain_ref, w_ref, wg_ref, proj_ref, gates_ref, h_ref, *, rows):
    tm, d = x_ref.shape
    n = pl.program_id(1)

    @pl.when(n == 0)
    def _():
        shift = mod_ref[0, :, 0:d]
        scale = mod_ref[0, :, d:2 * d]
        gain = gain_ref[...]

        def normalize(r):
            xf = x_ref[r:r + rows, :]
            y = xf * lax.rsqrt(jnp.mean(xf * xf, axis=-1, keepdims=True) + EPS) * gain
            h_ref[r:r + rows, :] = (y * (1.0 + scale) + shift).astype(BF16)

        def project(r):
            hb = h_ref[r:r + rows, :]
            proj_ref[r:r + rows, :] = jnp.dot(hb, w_ref[...], preferred_element_type=F32).astype(BF16)
            gates_ref[r:r + rows, :] = jnp.dot(hb, wg_ref[...], preferred_element_type=F32)

        starts = list(range(0, tm, rows))
        normalize(starts[0])
        for i, r in enumerate(starts):
            if i + 1 < len(starts):
                normalize(starts[i + 1])
            project(r)

    @pl.when(n > 0)
    def _():
        proj_ref[...] = jnp.dot(h_ref[...], w_ref[...], preferred_element_type=F32).astype(BF16)


def _in_proj(x2, mod3, norm_gain, w_main, w_gate, seq, tm):
    n_tok, d = x2.shape
    tn = N_MAIN // 4
    return pl.pallas_call(
        functools.partial(_in_proj_kernel, rows=min(MERGE_ROWS, tm)),
        out_shape=(
            jax.ShapeDtypeStruct((n_tok, N_MAIN), BF16),
            jax.ShapeDtypeStruct((n_tok, GATE_LANES), F32),
        ),
        grid=(n_tok // tm, N_MAIN // tn),
        in_specs=[
            pl.BlockSpec((tm, d), lambda m, n: (m, 0)),
            pl.BlockSpec((1, 1, 3 * d), lambda m, n: ((m * tm) // seq, 0, 0)),
            pl.BlockSpec((1, d), lambda m, n: (0, 0)),
            pl.BlockSpec((d, tn), lambda m, n: (0, n)),
            pl.BlockSpec((d, GATE_LANES), lambda m, n: (0, 0)),
        ],
        out_specs=(
            pl.BlockSpec((tm, tn), lambda m, n: (m, n)),
            pl.BlockSpec((tm, GATE_LANES), lambda m, n: (m, 0)),
        ),
        scratch_shapes=[pltpu.VMEM((tm, d), BF16)],
        compiler_params=pltpu.CompilerParams(
            dimension_semantics=("arbitrary", "arbitrary"), vmem_limit_bytes=IN_PROJ_VMEM_LIMIT_BYTES),
        name="in_proj",
    )(x2, mod3, norm_gain.reshape(1, d), w_main, w_gate)


def _gate_prep_kernel(g_ref, bias_ref, col_ref, row_ref):
    nc = row_ref.shape[1]
    v = (g_ref[0] + bias_ref[...]).T
    i_gate, f_gate = v[0:8, :] * LOG2E, v[8:16, :]
    seq = v.shape[1]
    pos = lax.broadcasted_iota(jnp.int32, i_gate.shape, 1) % CHUNK
    is_bwd = lax.broadcasted_iota(jnp.int32, i_gate.shape, 0) >= MLSTM_HEADS

    def scan(x, combine, identity):
        pre, suf = x, x
        step = 1
        while step < CHUNK:
            pre = combine(pre, jnp.where(pos >= step, pltpu.roll(pre, step, 1), identity))
            suf = combine(suf, jnp.where(pos < CHUNK - step, pltpu.roll(suf, seq - step, 1), identity))
            step *= 2
        return jnp.where(is_bwd, suf, pre)

    b = scan((jnp.minimum(f_gate, 0.0) - jnp.log1p(jnp.exp(-jnp.abs(f_gate)))) * LOG2E, jnp.add, 0.0)
    r = i_gate - b
    cm = scan(r, jnp.maximum, NEG)
    col_ref[0] = jnp.concatenate([cm, b, jnp.zeros((GATE_LANES - 16, seq), F32)], axis=0).T
    for c in range(nc):
        row_ref[0, c] = r[:, c * CHUNK:(c + 1) * CHUNK]


def _gate_prep(gates3, bias_row):
    bsz, seq, _ = gates3.shape
    nc = seq // CHUNK
    rows = 2 * MLSTM_HEADS
    return pl.pallas_call(
        _gate_prep_kernel,
        out_shape=(
            jax.ShapeDtypeStruct((bsz, seq, GATE_LANES), F32),
            jax.ShapeDtypeStruct((bsz, nc, rows, CHUNK), F32),
        ),
        grid=(bsz,),
        in_specs=[
            pl.BlockSpec((1, seq, GATE_LANES), lambda b: (b, 0, 0)),
            pl.BlockSpec((1, GATE_LANES), lambda b: (0, 0)),
        ],
        out_specs=(
            pl.BlockSpec((1, seq, GATE_LANES), lambda b: (b, 0, 0)),
            pl.BlockSpec((1, nc, rows, CHUNK), lambda b: (b, 0, 0, 0)),
        ),
        compiler_params=pltpu.CompilerParams(dimension_semantics=("arbitrary",)),
        name="gate_prep",
    )(gates3, bias_row)


def _mlstm_chunk(qc, kt, vv, r_row, cm_rep, b_rep, cm_end, g, cn_ref, m_ref, causal_mask):
    d = vv.shape[1]
    cn = cn_ref[...]
    m_state = m_ref[...]
    m_col = jnp.maximum(m_state, cm_rep)
    w_inter = jnp.exp2(m_state - m_col)
    m_wide = jnp.concatenate([m_col, m_col], axis=1)
    d_mat = jnp.where(causal_mask, jnp.exp2(r_row - m_wide), 0.0)
    scores = jnp.dot(qc, kt, preferred_element_type=F32) * d_mat
    inter = jnp.dot(qc, cn.astype(BF16), preferred_element_type=F32)
    yield None
    w_wide = jnp.concatenate([w_inter, w_inter], axis=1)
    num = jnp.dot(scores.astype(BF16), vv, preferred_element_type=F32) + w_wide * inter[:, 0:d]
    den = jnp.sum(scores, axis=-1, keepdims=True) + w_inter * inter[:, d:]
    inv = 1.0 / jnp.maximum(jnp.abs(den), jnp.exp2(-(b_rep + m_col)))
    yield num * jnp.concatenate([inv, inv], axis=1)
    m_end = jnp.maximum(m_state, cm_end)
    w_row = jnp.exp2(r_row - jnp.concatenate([m_end, m_end], axis=1))
    decay = jnp.exp2(m_state - m_end)
    kw = kt * w_row.astype(BF16)
    v_aug = jnp.concatenate([vv, jnp.ones((vv.shape[0], LANES), BF16)], axis=1)
    decay_wide = jnp.concatenate([decay, decay, decay], axis=1)
    cn_ref[...] = decay_wide * cn + jnp.dot(kw, v_aug, preferred_element_type=F32)
    m_ref[...] = g + m_end
    yield None


def _mlstm_kernel(q_ref, k_ref, v_ref, o_ref, z_ref, gcol_ref, grow_ref, cwq_ref, cwk_ref, cbq_ref, cbk_ref,
                  gain_ref, y_ref, qc_ref, kt_ref, h_ref, gc_ref, cnf_ref, mf_ref, cnb_ref, mb_ref):
    seq, d = q_ref.shape[1], q_ref.shape[2]
    nc = seq // CHUNK
    head = pl.program_id(1)

    t_idx = lax.broadcasted_iota(jnp.int32, (CHUNK, CHUNK), 0)
    s_idx = lax.broadcasted_iota(jnp.int32, (CHUNK, CHUNK), 1)
    taps = jnp.concatenate([(s_idx == t_idx - 1).astype(BF16), (s_idx == t_idx + 1).astype(BF16)],
                           axis=0)
    edge = lax.broadcasted_iota(jnp.int32, (8, d), 0)

    def conv_silu(src_ref, w_ref, b_ref, out_scale, c, r):
        w0, w1, w2 = 0.5 * w_ref[0:1, :], 0.5 * w_ref[1:2, :], 0.5 * w_ref[2:3, :]
        src = src_ref[0, pl.ds(r, CHUNK), :]
        p = jnp.dot(taps, src, preferred_element_type=F32)
        t = p[0:CHUNK] * w0 + src.astype(F32) * w1 + p[CHUNK:2 * CHUNK] * w2 + 0.5 * b_ref[...]
        above = src_ref[0, pl.ds(pl.multiple_of(jnp.maximum(r - 16, 0), 16), 16), :][15:16, :].astype(F32)
        below = src_ref[0, pl.ds(pl.multiple_of(jnp.minimum(r + CHUNK, seq - 16), 16), 16), :][0:1, :].astype(F32)
        above = above * jnp.where(c > 0, w0, 0.0)
        below = below * jnp.where(c < nc - 1, w2, 0.0)
        t = jnp.concatenate([t[0:8] + jnp.where(edge == 0, above, 0.0), t[8:CHUNK - 8],
                             t[CHUNK - 8:CHUNK] + jnp.where(edge == 7, below, 0.0)], axis=0)
        return (t * out_scale) * (1.0 + jnp.tanh(t))

    def conv_chunk(c, carry):
        r = pl.multiple_of(c * CHUNK, CHUNK)
        qc_ref[pl.ds(r, CHUNK), :] = conv_silu(q_ref, cwq_ref, cbq_ref, 1.0, c, r).astype(BF16)
        kt_ref[c] = conv_silu(k_ref, cwk_ref, cbk_ref, MLSTM_HEAD_DIM ** -0.5, c, r).T.astype(BF16)
        return carry

    lax.fori_loop(0, nc, conv_chunk, 0, unroll=8)

    gc_ref[...] = pltpu.roll(gcol_ref[0], (GATE_LANES - head) % GATE_LANES, 1)
    gate_row = lax.broadcasted_iota(jnp.int32, (2 * MLSTM_HEADS, CHUNK), 0)

    for ref in (cnf_ref, mf_ref, cnb_ref, mb_ref):
        ref[...] = jnp.zeros(ref.shape, F32)

    mask_f = s_idx <= t_idx
    mask_b = s_idx >= t_idx

    def replicate(col):
        return jnp.broadcast_to(col, (col.shape[0], LANES))

    def direction(chunk, backward):
        r = pl.multiple_of(chunk * CHUNK, CHUNK)
        rows = grow_ref[0, chunk]
        cols = gc_ref[pl.ds(r, CHUNK), :]
        qc = qc_ref[pl.ds(r, CHUNK), :]
        kt = kt_ref[chunk]
        vv = v_ref[0, pl.ds(r, CHUNK), :]
        if backward:
            cm_rep, b_rep = replicate(cols[:, 4:5]), replicate(cols[:, 12:13])
            end = 0
            state, mask, this_row = (cnb_ref, mb_ref), mask_b, MLSTM_HEADS + head
        else:
            cm_rep, b_rep = replicate(cols[:, 0:1]), replicate(cols[:, 8:9])
            end = CHUNK - 1
            state, mask, this_row = (cnf_ref, mf_ref), mask_f, head
        r_row = jnp.sum(jnp.where(gate_row == this_row, rows, 0.0), axis=0, keepdims=True)
        cm_end, g = cm_rep[end:end + 1, :], b_rep[end:end + 1, :]
        return _mlstm_chunk(qc, kt, vv, r_row, cm_rep, b_rep, cm_end, g, *state, mask), r

    def chunk_steps(j, interleave):
        steps = [direction(j, False), direction(nc - 1 - j, True)]
        if interleave:
            for gen, _ in steps:
                next(gen)
            hs = [next(gen) for gen, _ in steps]
            for gen, _ in steps:
                next(gen)
        else:
            hs = []
            for gen, _ in steps:
                next(gen)
                hs.append(next(gen))
                next(gen)
        return [(h, r) for h, (_, r) in zip(hs, steps)]

    def first_half(j, carry):
        for h_dir, r in chunk_steps(j, interleave=False):
            h_ref[pl.ds(r, CHUNK), :] = h_dir
        return carry

    gain = gain_ref[...]

    def finish(h_dir, r):
        h_sum = (h_ref[pl.ds(r, CHUNK), :] + h_dir) * _sigmoid(o_ref[0, pl.ds(r, CHUNK), :].astype(F32))
        normed = h_sum * lax.rsqrt(jnp.mean(h_sum * h_sum, axis=-1, keepdims=True) + EPS) * gain
        y_ref[0, pl.ds(r, CHUNK), :] = (normed * _silu(z_ref[0, pl.ds(r, CHUNK), :].astype(F32))).astype(BF16)

    def second_half(j, carry):
        for h_dir, r in chunk_steps(j, interleave=True):
            finish(h_dir, r)
        return carry

    lax.fori_loop(0, nc // 2, first_half, 0, unroll=2)
    lax.fori_loop(nc // 2, nc, second_half, 0)


def _mlstm(proj3, gcol, grow, conv_w, conv_b, head_gain):
    bsz, seq, _ = proj3.shape
    d = MLSTM_HEAD_DIM
    nh = MLSTM_HEADS
    nc = seq // CHUNK

    def col_block(offset):
        return pl.BlockSpec((1, seq, d), lambda b, h: (b, 0, offset + h))

    return pl.pallas_call(
        _mlstm_kernel,
        out_shape=jax.ShapeDtypeStruct((bsz, seq, nh * d), BF16),
        grid=(bsz, nh),
        in_specs=[
            col_block(0), col_block(nh), col_block(2 * nh), col_block(3 * nh), col_block(4 * nh),
            pl.BlockSpec((1, seq, GATE_LANES), lambda b, h: (b, 0, 0)),
            pl.BlockSpec((1, nc, 2 * MLSTM_HEADS, CHUNK), lambda b, h: (b, 0, 0, 0)),
            pl.BlockSpec((3, d), lambda b, h: (0, h)),
            pl.BlockSpec((3, d), lambda b, h: (0, nh + h)),
            pl.BlockSpec((1, d), lambda b, h: (0, h)),
            pl.BlockSpec((1, d), lambda b, h: (0, nh + h)),
            pl.BlockSpec((1, d), lambda b, h: (0, h)),
        ],
        out_specs=pl.BlockSpec((1, seq, d), lambda b, h: (b, 0, h)),
        scratch_shapes=[
            pltpu.VMEM((seq, d), BF16),
            pltpu.VMEM((nc, d, CHUNK), BF16),
            pltpu.VMEM((seq, d), F32),
            pltpu.VMEM((seq, GATE_LANES), F32),
            pltpu.VMEM((d, d + LANES), F32), pltpu.VMEM((1, LANES), F32),
            pltpu.VMEM((d, d + LANES), F32), pltpu.VMEM((1, LANES), F32),
        ],
        compiler_params=pltpu.CompilerParams(
            dimension_semantics=("arbitrary", "arbitrary"), vmem_limit_bytes=VMEM_LIMIT_BYTES),
        name="mlstm",
    )(proj3, proj3, proj3, proj3, proj3, gcol, grow, conv_w, conv_w,
      conv_b.reshape(1, -1), conv_b.reshape(1, -1), head_gain.reshape(1, -1))


def _natten_bias_rows(rpb):
    n_dc = 2 * NA_COLS - 1
    n_dr = 2 * NA_ROWS - 1
    left = NA_COLS
    period = 64
    padded = jnp.pad(rpb.astype(F32) * LOG2E, ((0, 0), (0, 0), (left, period - left - n_dc)), constant_values=NEG)
    out = []
    for (c0, w), kb in zip(Q_BLOCKS, K_STARTS):
        base = kb - c0 + NA_COLS - 1 + left
        toe = jnp.tile(padded, (1, 1, w))[..., :(period - 1) * w].reshape(NA_HEADS, n_dr, w, period - 1)
        toe = toe[..., base:base + K_WIDTH]
        q_col = c0 + np.arange(w)[:, None]
        k_col = kb + np.arange(K_WIDTH)[None, :]
        win = np.clip(q_col - NA_COLS // 2, 0, GRID_W - NA_COLS)
        col_ok = jnp.asarray((k_col >= win) & (k_col < win + NA_COLS))
        toe = jnp.where(col_ok[None, None], toe, NEG).transpose(0, 2, 1, 3).reshape(NA_HEADS, w, n_dr * K_WIDTH)
        out.append(jnp.pad(toe, ((0, 0), (0, Q_BLOCK_MAX - w), (0, N_KEYS - n_dr * K_WIDTH)), constant_values=NEG))
    return jnp.stack(out, axis=1)


_GROUP_KINDS = (
    (0, lambda i: max(i - NA_ROWS // 2, 0)),
    (-(NA_ROWS // 2), lambda i: i),
    (Q_ROWS - SLAB_ROWS, lambda i: min(i + NA_ROWS // 2, NA_ROWS)),
)


def _build_bias(e_ref, tab_ref, slab_off, first_valid):
    for h in range(NA_HEADS):
        row_off = 0
        for blk, (_, w) in enumerate(Q_BLOCKS):
            e = e_ref[h, blk, 0:w, :]
            lane = lax.broadcasted_iota(jnp.int32, (w, N_KEYS), 1)
            for i in range(Q_ROWS):
                shift = slab_off - i + NA_ROWS - 1
                amount = (-shift * K_WIDTH) % N_KEYS
                moved = e if amount == 0 else pltpu.roll(e, amount, 1)
                lo = first_valid(i) * K_WIDTH
                ok = (lane >= lo) & (lane < lo + NA_ROWS * K_WIDTH)
                tab_ref[h, row_off + i * w:row_off + (i + 1) * w, :] = jnp.where(ok, moved, NEG)
            row_off += Q_ROWS * w


def _natten_kernel(q_ref, k0_ref, k1_ref, k2_ref, k3_ref, v0_ref, v1_ref, v2_ref, v3_ref, z_ref, e_ref,
                   y_ref, qf_ref, of_ref, tab_ref, sa_ref, sb_ref):
    k_refs = (k0_ref, k1_ref, k2_ref, k3_ref)
    v_refs = (v0_ref, v1_ref, v2_ref, v3_ref)
    piece_rows = SLAB_ROWS // 4
    group = pl.program_id(0)
    n_groups = pl.num_programs(0)

    first_batch = pl.program_id(1) == 0
    conditions = (group == 0, jnp.logical_and(group > 0, group < n_groups - 1), group == n_groups - 1)
    for cond, (slab_off, first_valid) in zip(conditions, _GROUP_KINDS):
        @pl.when(jnp.logical_and(first_batch, cond))
        def _(slab_off=slab_off, first_valid=first_valid):
            _build_bias(e_ref, tab_ref, slab_off, first_valid)

    n_batch, q_tok = q_ref.shape[0], q_ref.shape[2]
    for bi in range(n_batch):
        qf_ref[pl.ds(bi * q_tok, q_tok), :] = q_ref[bi, 0].astype(F32) * (NA_HEAD_DIM ** -0.5 * LOG2E)

    items = []
    for bi in range(n_batch):
        row_off = 0
        for (c0, w), kb in zip(Q_BLOCKS, K_STARTS):
            for pair in range(NA_HEADS // 2):
                items.append((bi, c0, w, kb, pair, row_off))
            row_off += Q_ROWS * w
    s_refs = (sa_ref, sb_ref)

    def slab(refs, bi, kb, lanes):
        return jnp.concatenate(
            [refs[a // piece_rows][bi, 0, pl.ds((a % piece_rows) * GRID_W + kb, K_WIDTH), lanes]
             for a in range(SLAB_ROWS)], axis=0)

    def scores(idx):
        bi, c0, w, kb, pair, row_off = items[idx]
        nq = Q_ROWS * w
        lanes = pl.ds(pair * LANES, LANES)
        first_head = lax.broadcasted_iota(jnp.int32, (nq, LANES), 1) < NA_HEAD_DIM
        qp = jnp.concatenate([qf_ref[pl.ds(bi * q_tok + i * GRID_W + c0, w), lanes] for i in range(Q_ROWS)],
                             axis=0)
        qs = jnp.concatenate([jnp.where(first_head, qp, 0.0), jnp.where(first_head, 0.0, qp)],
                             axis=0).astype(BF16)
        s = lax.dot_general(qs, slab(k_refs, bi, kb, lanes), (((1,), (1,)), ((), ())),
                            preferred_element_type=F32)
        bias = jnp.concatenate([tab_ref[2 * pair, pl.ds(row_off, nq), :],
                                tab_ref[2 * pair + 1, pl.ds(row_off, nq), :]], axis=0)
        s_refs[idx % 2][0:2 * nq, :] = s + bias

    def attend(idx):
        bi, c0, w, kb, pair, _ = items[idx]
        nq = Q_ROWS * w
        lanes = pl.ds(pair * LANES, LANES)
        first_head = lax.broadcasted_iota(jnp.int32, (nq, LANES), 1) < NA_HEAD_DIM
        s = s_refs[idx % 2][0:2 * nq, :]
        p = jnp.exp2(s - jnp.max(s, axis=-1, keepdims=True))
        o2 = jnp.dot(p.astype(BF16), slab(v_refs, bi, kb, lanes), preferred_element_type=F32)
        o2 = o2 * (1.0 / jnp.sum(p, axis=-1, keepdims=True))
        o = jnp.where(first_head, o2[0:nq], o2[nq:2 * nq])
        for i in range(Q_ROWS):
            of_ref[pl.ds(bi * q_tok + i * GRID_W + c0, w), lanes] = o[i * w:(i + 1) * w]

    scores(0)
    for idx in range(len(items)):
        if idx + 1 < len(items):
            scores(idx + 1)
        attend(idx)
    for bi in range(n_batch):
        rows = pl.ds(bi * q_tok, q_tok)
        y_ref[bi, 0] = (of_ref[rows, :] * _silu(z_ref[bi, 0].astype(F32))).astype(BF16)


def _natten(proj3, bias_rows):
    bsz, seq, n_main = proj3.shape
    q_tok = Q_ROWS * GRID_W
    p_tok = (SLAB_ROWS // 4) * GRID_W
    groups = seq // q_tok
    pieces = seq // p_tok
    assert groups >= 2, "needs at least two 8-row query groups"
    nb = 2 if bsz % 2 == 0 else 1
    w = NA_WIDTH
    qb, kb, vb, zb = (5 * D_MODEL) // w, (5 * D_MODEL) // w + 1, (5 * D_MODEL) // w + 2, (5 * D_MODEL) // w + 3
    proj_q = proj3.reshape(bsz, groups, q_tok, n_main)
    proj_p = proj3.reshape(bsz, pieces, p_tok, n_main)

    def slab_start(g):
        return jnp.clip(2 * g - 1, 0, pieces - 4)

    def piece_spec(col, a):
        return pl.BlockSpec((nb, 1, p_tok, w), lambda g, b: (b, slab_start(g) + a, 0, col))

    return pl.pallas_call(
        _natten_kernel,
        out_shape=jax.ShapeDtypeStruct((bsz, groups, q_tok, w), BF16),
        grid=(groups, bsz // nb),
        in_specs=[pl.BlockSpec((nb, 1, q_tok, w), lambda g, b: (b, g, 0, qb))]
        + [piece_spec(kb, a) for a in range(4)] + [piece_spec(vb, a) for a in range(4)]
        + [pl.BlockSpec((nb, 1, q_tok, w), lambda g, b: (b, g, 0, zb)),
           pl.BlockSpec(bias_rows.shape, lambda g, b: (0, 0, 0, 0))],
        out_specs=pl.BlockSpec((nb, 1, q_tok, w), lambda g, b: (b, g, 0, 0)),
        scratch_shapes=[pltpu.VMEM((nb * q_tok, w), F32), pltpu.VMEM((nb * q_tok, w), F32),
                        pltpu.VMEM((NA_HEADS, q_tok, N_KEYS), F32),
                        pltpu.VMEM((2 * Q_ROWS * Q_BLOCK_MAX, N_KEYS), F32),
                        pltpu.VMEM((2 * Q_ROWS * Q_BLOCK_MAX, N_KEYS), F32)],
        compiler_params=pltpu.CompilerParams(
            dimension_semantics=("arbitrary", "arbitrary"), vmem_limit_bytes=VMEM_LIMIT_BYTES),
        name="natten",
    )(proj_q, *([proj_p] * 8), proj_q, bias_rows).reshape(bsz, seq, w)


def _merge_out_kernel(ya_ref, yb_ref, g0_ref, g1_ref, x_ref, mod_ref, bm_ref, wa_ref, wb_ref, wo_ref, fg_ref,
                      out_ref, m_ref):
    tm, d = x_ref.shape
    sub = min(MERGE_ROWS, tm)
    gate = mod_ref[0, :, 2 * d:3 * d]

    def merge(r):
        rows = pl.ds(r, sub)
        pa = jnp.dot(ya_ref[rows, :], wa_ref[...], preferred_element_type=F32)
        pb = jnp.dot(yb_ref[rows, :], wb_ref[...], preferred_element_type=F32)
        m_ref[rows, :] = (_sigmoid(g0_ref[rows, :].astype(F32) + bm_ref[0:1, :]) * pa
                          + _sigmoid(g1_ref[rows, :].astype(F32) + bm_ref[1:2, :]) * pb).astype(BF16)

    def project(r):
        rows = pl.ds(r, sub)
        mixed = jnp.dot(m_ref[rows, :], wo_ref[...], preferred_element_type=F32)
        xo = x_ref[rows, :] + gate * mixed
        out_ref[rows, :] = xo * lax.rsqrt(jnp.mean(xo * xo, axis=-1, keepdims=True) + EPS) * fg_ref[...]

    starts = list(range(0, tm, sub))
    merge(starts[0])
    for i, r in enumerate(starts):
        if i + 1 < len(starts):
            merge(starts[i + 1])
        project(r)


def _merge_out(ya2, yb2, proj2, x2, mod3, b_merge, wa, wb, wo, final_gain, seq, tm):
    n_tok, d = x2.shape
    g0 = (N_MAIN - 2 * d) // d

    def full(shape):
        return pl.BlockSpec(shape, lambda m: (0,) * len(shape))

    return pl.pallas_call(
        _merge_out_kernel,
        out_shape=jax.ShapeDtypeStruct((n_tok, d), F32),
        grid=(n_tok // tm,),
        in_specs=[
            pl.BlockSpec((tm, d), lambda m: (m, 0)),
            pl.BlockSpec((tm, NA_WIDTH), lambda m: (m, 0)),
            pl.BlockSpec((tm, d), lambda m: (m, g0)),
            pl.BlockSpec((tm, d), lambda m: (m, g0 + 1)),
            pl.BlockSpec((tm, d), lambda m: (m, 0)),
            pl.BlockSpec((1, 1, 3 * d), lambda m: ((m * tm) // seq, 0, 0)),
            full((2, d)), full((d, d)), full((NA_WIDTH, d)), full((d, d)), full((1, d)),
        ],
        out_specs=pl.BlockSpec((tm, d), lambda m: (m, 0)),
        scratch_shapes=[pltpu.VMEM((tm, d), BF16)],
        compiler_params=pltpu.CompilerParams(
            dimension_semantics=("arbitrary",), vmem_limit_bytes=VMEM_LIMIT_BYTES),
        name="merge_out",
    )(ya2, yb2, proj2, proj2, x2, mod3, b_merge, wa, wb, wo, final_gain.reshape(1, d))


def _gate_layout(n_cols):
    src = np.arange(n_cols)
    direction, gate, head = src // (2 * MLSTM_HEADS), (src // MLSTM_HEADS) % 2, src % MLSTM_HEADS
    return 2 * MLSTM_HEADS * gate + MLSTM_HEADS * direction + head


def _weight_prep_kernel(a_ref, b_ref, perm_ref, main_ref, gate_ref, *, gate_block, n_gates):
    n = pl.program_id(0)
    rows = a_ref[0]
    past_gates = jnp.concatenate([rows[n_gates:], b_ref[0]], axis=0)
    main_ref[...] = jnp.where(n >= gate_block, past_gates, rows).T.astype(BF16)

    @pl.when(n == gate_block)
    def _():
        gate_ref[...] = jnp.dot(perm_ref[...], rows[0:n_gates], preferred_element_type=F32).T.astype(BF16)


def _weight_prep(w_in, layer):
    _, d, n_in = w_in.shape
    n_gates = 4 * MLSTM_HEADS
    tn = D_MODEL
    g_lo = 5 * MLSTM_HEADS * MLSTM_HEAD_DIM
    assert n_in == N_MAIN + n_gates and g_lo % tn == 0 and tn % n_gates == 0
    perm = np.zeros((GATE_LANES, n_gates), np.float32)
    perm[_gate_layout(n_gates), np.arange(n_gates)] = 1.0
    w_t = jnp.swapaxes(w_in, 1, 2)
    return pl.pallas_call(
        functools.partial(_weight_prep_kernel, gate_block=g_lo // tn, n_gates=n_gates),
        out_shape=(jax.ShapeDtypeStruct((d, N_MAIN), BF16), jax.ShapeDtypeStruct((d, GATE_LANES), BF16)),
        grid=(N_MAIN // tn,),
        in_specs=[pl.BlockSpec((1, tn, d), lambda n: (layer, n, 0)),
                  pl.BlockSpec((1, n_gates, d), lambda n: (layer, (n + 1) * (tn // n_gates), 0)),
                  pl.BlockSpec((GATE_LANES, n_gates), lambda n: (0, 0))],
        out_specs=(pl.BlockSpec((d, tn), lambda n: (0, n)),
                   pl.BlockSpec((d, GATE_LANES), lambda n: (0, 0))),
        compiler_params=pltpu.CompilerParams(dimension_semantics=("arbitrary",), vmem_limit_bytes=VMEM_LIMIT_BYTES),
        name="weight_prep",
    )(w_t, w_t, jnp.asarray(perm))


def _token_tiles(seq):
    return min(2048, seq), min(1024, seq)


def _layer(x, mod, norm_gain, w_in, layer, conv_w, conv_b, b_igate, b_fgate, mlstm_norm_gain, rpb, w_proj_a, w_proj_b,
           b_merge, w_out, final_gain):
    bsz, seq, d = x.shape
    n_tok = bsz * seq
    wa = MLSTM_HEADS * MLSTM_HEAD_DIM
    w_main, w_gate = _weight_prep(w_in, layer)
    gate_bias = jnp.zeros((1, GATE_LANES), F32).at[0, _gate_layout(4 * MLSTM_HEADS)].set(
        jnp.stack([b_igate, b_fgate], axis=1).reshape(-1))
    tm_in, tm_out = _token_tiles(seq)

    x2 = x.reshape(n_tok, d)
    mod3 = mod.reshape(bsz, 1, 3 * d)
    proj2, gates2 = _in_proj(x2, mod3, norm_gain, w_main, w_gate, seq, tm_in)
    proj3 = proj2.reshape(bsz, seq, N_MAIN)
    gcol, grow = _gate_prep(gates2.reshape(bsz, seq, GATE_LANES), gate_bias)
    y_a = _mlstm(proj3, gcol, grow, conv_w, conv_b, mlstm_norm_gain)
    y_b = _natten(proj3, _natten_bias_rows(rpb))
    out2 = _merge_out(y_a.reshape(n_tok, wa), y_b.reshape(n_tok, NA_WIDTH), proj2, x2, mod3, b_merge,
                      w_proj_a.astype(BF16), w_proj_b.astype(BF16), w_out.astype(BF16), final_gain, seq, tm_out)
    return out2.reshape(bsz, seq, d)


def kernel(x, c, w_ada, b_ada, norm_gain, w_in, conv_w, conv_b, b_igate, b_fgate, mlstm_norm_gain, rpb, w_proj_a,
           w_proj_b, b_merge, w_out, final_gain):
    depth = w_ada.shape[0]
    assert depth == 1, "the fused final norm assumes a single layer"
    mod = _adaln_mod(c, w_ada[0], b_ada[0])
    return _layer(x, mod, norm_gain[0], w_in, 0, conv_w[0], conv_b[0], b_igate[0], b_fgate[0], mlstm_norm_gain[0],
                  rpb[0], w_proj_a[0], w_proj_b[0], b_merge[0], w_out[0], final_gain)
```

```python
import functools
import math

import numpy as np
import jax
import jax.numpy as jnp
from jax import lax
from jax.experimental import pallas as pl
from jax.experimental.pallas import tpu as pltpu

F32 = jnp.float32
BF16 = jnp.bfloat16

D_MODEL = 1024
GRID_W = 64
MLSTM_HEADS = 4
MLSTM_HEAD_DIM = 256
NA_HEADS = 8
NA_WIDTH = 512
NA_HEAD_DIM = 64
NA_ROWS = 8
NA_COLS = 16
EPS = 1e-6
NEG = -1e30
LOG2E = 1.4426950408889634
K_SCALE_LOG2 = -0.5 * math.log2(MLSTM_HEAD_DIM)

VMEM_LIMIT_BYTES = 56 * 1024 * 1024
LANES = 128

CHUNK = 256

MERGE_ROWS = 512

N_MAIN = 9 * D_MODEL
GATE_LANES = 128

Q_ROWS = 8
SLAB_ROWS = 16
Q_BLOCKS = ((0, 24), (24, 16), (40, 24))
Q_BLOCK_MAX = 24
K_STARTS = (0, 16, 32)
K_WIDTH = 32
N_KEYS = SLAB_ROWS * K_WIDTH


def _sigmoid(x):
    return 1.0 / (1.0 + jnp.exp2(x * -LOG2E))


def _silu(x):
    return x * _sigmoid(x)


def _adaln_kernel(c_ref, w_ref, b_ref, o_ref):
    cond = _silu(c_ref[...]).astype(BF16)
    o_ref[...] = jnp.dot(cond, w_ref[...].astype(BF16), preferred_element_type=F32) + b_ref[...]


def _adaln_mod(c, w_ada, b_ada):
    bsz, d = c.shape
    n = w_ada.shape[1]
    tn = 512
    return pl.pallas_call(
        _adaln_kernel,
        out_shape=jax.ShapeDtypeStruct((bsz, n), F32),
        grid=(n // tn,),
        in_specs=[
            pl.BlockSpec((bsz, d), lambda j: (0, 0)),
            pl.BlockSpec((d, tn), lambda j: (0, j)),
            pl.BlockSpec((1, tn), lambda j: (0, j)),
        ],
        out_specs=pl.BlockSpec((bsz, tn), lambda j: (0, j)),
        compiler_params=pltpu.CompilerParams(dimension_semantics=("arbitrary",)),
        name="adaln_mod",
    )(c, w_ada, b_ada.reshape(1, n))


def _in_proj_kernel(x_ref, mod_ref, gain_ref, w_ref, wg_ref, proj_ref, gates_ref, h_ref, *, rows):
    tm, d = x_ref.shape
    n = pl.program_id(1)

    @pl.when(n == 0)
    def _():
        shift = mod_ref[0, :, 0:d]
        scale = mod_ref[0, :, d:2 * d]
        gain = gain_ref[...]

        def normalize(r):
            xf = x_ref[r:r + rows, :]
            y = xf * lax.rsqrt(jnp.mean(xf * xf, axis=-1, keepdims=True) + EPS) * gain
            h_ref[r:r + rows, :] = (y * (1.0 + scale) + shift).astype(BF16)

        def project(r):
            hb = h_ref[r:r + rows, :]
            proj_ref[r:r + rows, :] = jnp.dot(hb, w_ref[...], preferred_element_type=F32).astype(BF16)
            gates_ref[r:r + rows, :] = jnp.dot(hb, wg_ref[...], preferred_element_type=F32)

        starts = list(range(0, tm, rows))
        normalize(starts[0])
        for i, r in enumerate(starts):
            if i + 1 < len(starts):
                normalize(starts[i + 1])
            project(r)

    @pl.when(n > 0)
    def _():
        proj_ref[...] = jnp.dot(h_ref[...], w_ref[...], preferred_element_type=F32).astype(BF16)


def _in_proj(x2, mod3, norm_gain, w_main, w_gate, seq, tm):
    n_tok, d = x2.shape
    tn = N_MAIN // 6
    return pl.pallas_call(
        functools.partial(_in_proj_kernel, rows=min(MERGE_ROWS, tm)),
        out_shape=(
            jax.ShapeDtypeStruct((n_tok, N_MAIN), BF16),
            jax.ShapeDtypeStruct((n_tok, GATE_LANES), F32),
        ),
        grid=(n_tok // tm, N_MAIN // tn),
        in_specs=[
            pl.BlockSpec((tm, d), lambda m, n: (m, 0)),
            pl.BlockSpec((1, 1, 3 * d), lambda m, n: ((m * tm) // seq, 0, 0)),
            pl.BlockSpec((1, d), lambda m, n: (0, 0)),
            pl.BlockSpec((d, tn), lambda m, n: (0, n)),
            pl.BlockSpec((d, GATE_LANES), lambda m, n: (0, 0)),
        ],
        out_specs=(
            pl.BlockSpec((tm, tn), lambda m, n: (m, n)),
            pl.BlockSpec((tm, GATE_LANES), lambda m, n: (m, 0)),
        ),
        scratch_shapes=[pltpu.VMEM((tm, d), BF16)],
        compiler_params=pltpu.CompilerParams(
            dimension_semantics=("arbitrary", "arbitrary"), vmem_limit_bytes=VMEM_LIMIT_BYTES),
        name="in_proj",
    )(x2, mod3, norm_gain.reshape(1, d), w_main, w_gate)


def _gate_prep_kernel(g_ref, bias_ref, col_ref, row_ref):
    nc = row_ref.shape[1]
    v = (g_ref[0] + bias_ref[...]).T
    i_gate, f_gate = v[0:8, :] * LOG2E, v[8:16, :]
    seq = v.shape[1]
    pos = lax.broadcasted_iota(jnp.int32, i_gate.shape, 1) % CHUNK
    is_bwd = lax.broadcasted_iota(jnp.int32, i_gate.shape, 0) >= MLSTM_HEADS

    def scan(x, combine, identity):
        pre, suf = x, x
        step = 1
        while step < CHUNK:
            pre = combine(pre, jnp.where(pos >= step, pltpu.roll(pre, step, 1), identity))
            suf = combine(suf, jnp.where(pos < CHUNK - step, pltpu.roll(suf, seq - step, 1), identity))
            step *= 2
        return jnp.where(is_bwd, suf, pre)

    b = scan((jnp.minimum(f_gate, 0.0) - jnp.log1p(jnp.exp(-jnp.abs(f_gate)))) * LOG2E, jnp.add, 0.0)
    r = i_gate - b
    cm = scan(r, jnp.maximum, NEG)
    col_ref[0] = jnp.concatenate([cm, b, jnp.zeros((GATE_LANES - 16, seq), F32)], axis=0).T
    for c in range(nc):
        row_ref[0, c] = r[:, c * CHUNK:(c + 1) * CHUNK]


def _gate_prep(gates3, bias_row):
    bsz, seq, _ = gates3.shape
    nc = seq // CHUNK
    rows = 2 * MLSTM_HEADS
    return pl.pallas_call(
        _gate_prep_kernel,
        out_shape=(
            jax.ShapeDtypeStruct((bsz, seq, GATE_LANES), F32),
            jax.ShapeDtypeStruct((bsz, nc, rows, CHUNK), F32),
        ),
        grid=(bsz,),
        in_specs=[
            pl.BlockSpec((1, seq, GATE_LANES), lambda b: (b, 0, 0)),
            pl.BlockSpec((1, GATE_LANES), lambda b: (0, 0)),
        ],
        out_specs=(
            pl.BlockSpec((1, seq, GATE_LANES), lambda b: (b, 0, 0)),
            pl.BlockSpec((1, nc, rows, CHUNK), lambda b: (b, 0, 0, 0)),
        ),
        compiler_params=pltpu.CompilerParams(dimension_semantics=("arbitrary",)),
        name="gate_prep",
    )(gates3, bias_row)


def _mlstm_chunk(qc, kt, vv, r_row, cm_rep, b_rep, cm_end, g, cn_ref, m_ref, causal_mask):
    d = vv.shape[1]
    cn = cn_ref[...]
    m_state = m_ref[...]
    m_col = jnp.maximum(m_state, cm_rep)
    w_inter = jnp.exp2(m_state - m_col)
    m_wide = jnp.concatenate([m_col, m_col], axis=1)
    d_mat = jnp.where(causal_mask, jnp.exp2(r_row - m_wide), 0.0)
    scores = jnp.dot(qc, kt, preferred_element_type=F32) * d_mat
    inter = jnp.dot(qc, cn.astype(BF16), preferred_element_type=F32)
    yield None
    w_wide = jnp.concatenate([w_inter, w_inter], axis=1)
    num = jnp.dot(scores.astype(BF16), vv, preferred_element_type=F32) + w_wide * inter[:, 0:d]
    den = jnp.sum(scores, axis=-1, keepdims=True) + w_inter * inter[:, d:]
    inv = 1.0 / jnp.maximum(jnp.abs(den), jnp.exp2(-(b_rep + m_col)))
    yield num * jnp.concatenate([inv, inv], axis=1)
    m_end = jnp.maximum(m_state, cm_end)
    w_row = jnp.exp2(r_row - jnp.concatenate([m_end, m_end], axis=1))
    decay = jnp.exp2(m_state - m_end)
    kw = kt * w_row.astype(BF16)
    v_aug = jnp.concatenate([vv, jnp.ones((vv.shape[0], LANES), BF16)], axis=1)
    decay_wide = jnp.concatenate([decay, decay, decay], axis=1)
    cn_ref[...] = decay_wide * cn + jnp.dot(kw, v_aug, preferred_element_type=F32)
    m_ref[...] = g + m_end
    yield None


def _mlstm_kernel(q_ref, k_ref, v_ref, o_ref, z_ref, gcol_ref, grow_ref, cwq_ref, cwk_ref, cbq_ref, cbk_ref,
                  gain_ref, y_ref, qc_ref, kt_ref, h_ref, gc_ref, cnf_ref, mf_ref, cnb_ref, mb_ref):
    seq, d = q_ref.shape[1], q_ref.shape[2]
    nc = seq // CHUNK
    head = pl.program_id(1)

    t_idx = lax.broadcasted_iota(jnp.int32, (CHUNK, CHUNK), 0)
    s_idx = lax.broadcasted_iota(jnp.int32, (CHUNK, CHUNK), 1)
    taps = jnp.concatenate([(s_idx == t_idx - 1).astype(BF16), (s_idx == t_idx + 1).astype(BF16)],
                           axis=0)
    edge = lax.broadcasted_iota(jnp.int32, (8, d), 0)

    def conv_silu(src_ref, w_ref, b_ref, c, r):
        w0, w1, w2 = 0.5 * w_ref[0:1, :], 0.5 * w_ref[1:2, :], 0.5 * w_ref[2:3, :]
        src = src_ref[0, pl.ds(r, CHUNK), :]
        p = jnp.dot(taps, src, preferred_element_type=F32)
        t = p[0:CHUNK] * w0 + src.astype(F32) * w1 + p[CHUNK:2 * CHUNK] * w2 + 0.5 * b_ref[...]
        above = src_ref[0, pl.ds(pl.multiple_of(jnp.maximum(r - 16, 0), 16), 16), :][15:16, :].astype(F32)
        below = src_ref[0, pl.ds(pl.multiple_of(jnp.minimum(r + CHUNK, seq - 16), 16), 16), :][0:1, :].astype(F32)
        above = above * jnp.where(c > 0, w0, 0.0)
        below = below * jnp.where(c < nc - 1, w2, 0.0)
        t = jnp.concatenate([t[0:8] + jnp.where(edge == 0, above, 0.0), t[8:CHUNK - 8],
                             t[CHUNK - 8:CHUNK] + jnp.where(edge == 7, below, 0.0)], axis=0)
        return t * (1.0 + jnp.tanh(t))

    def conv_chunk(c, carry):
        r = pl.multiple_of(c * CHUNK, CHUNK)
        qc_ref[pl.ds(r, CHUNK), :] = conv_silu(q_ref, cwq_ref, cbq_ref, c, r).astype(BF16)
        kt_ref[c] = conv_silu(k_ref, cwk_ref, cbk_ref, c, r).T.astype(BF16)
        return carry

    lax.fori_loop(0, nc, conv_chunk, 0, unroll=8)

    gc_ref[...] = pltpu.roll(gcol_ref[0], (GATE_LANES - head) % GATE_LANES, 1)
    gate_row = lax.broadcasted_iota(jnp.int32, (2 * MLSTM_HEADS, CHUNK), 0)

    for ref in (cnf_ref, mf_ref, cnb_ref, mb_ref):
        ref[...] = jnp.zeros(ref.shape, F32)

    mask_f = s_idx <= t_idx
    mask_b = s_idx >= t_idx

    def replicate(col):
        return jnp.broadcast_to(col, (col.shape[0], LANES))

    def direction(chunk, backward):
        r = pl.multiple_of(chunk * CHUNK, CHUNK)
        rows = grow_ref[0, chunk]
        cols = gc_ref[pl.ds(r, CHUNK), :]
        qc = qc_ref[pl.ds(r, CHUNK), :]
        kt = kt_ref[chunk]
        vv = v_ref[0, pl.ds(r, CHUNK), :]
        if backward:
            cm_rep, b_rep = replicate(cols[:, 4:5]), replicate(cols[:, 12:13])
            end = 0
            state, mask, this_row = (cnb_ref, mb_ref), mask_b, MLSTM_HEADS + head
        else:
            cm_rep, b_rep = replicate(cols[:, 0:1]), replicate(cols[:, 8:9])
            end = CHUNK - 1
            state, mask, this_row = (cnf_ref, mf_ref), mask_f, head
        r_row = jnp.sum(jnp.where(gate_row == this_row, rows, 0.0), axis=0, keepdims=True) + K_SCALE_LOG2
        cm_end, g = cm_rep[end:end + 1, :], b_rep[end:end + 1, :]
        return _mlstm_chunk(qc, kt, vv, r_row, cm_rep, b_rep, cm_end, g, *state, mask), r

    def chunk_steps(j, interleave):
        steps = [direction(j, False), direction(nc - 1 - j, True)]
        if interleave:
            for gen, _ in steps:
                next(gen)
            hs = [next(gen) for gen, _ in steps]
            for gen, _ in steps:
                next(gen)
        else:
            hs = []
            for gen, _ in steps:
                next(gen)
                hs.append(next(gen))
                next(gen)
        return [(h, r) for h, (_, r) in zip(hs, steps)]

    def first_half(j, carry):
        for h_dir, r in chunk_steps(j, interleave=False):
            h_ref[pl.ds(r, CHUNK), :] = h_dir
        return carry

    gain = gain_ref[...]

    def finish(h_dir, r):
        h_sum = (h_ref[pl.ds(r, CHUNK), :] + h_dir) * _sigmoid(o_ref[0, pl.ds(r, CHUNK), :].astype(F32))
        normed = h_sum * lax.rsqrt(jnp.mean(h_sum * h_sum, axis=-1, keepdims=True) + EPS) * gain
        y_ref[0, pl.ds(r, CHUNK), :] = (normed * _silu(z_ref[0, pl.ds(r, CHUNK), :].astype(F32))).astype(BF16)

    def second_half(j, carry):
        for h_dir, r in chunk_steps(j, interleave=True):
            finish(h_dir, r)
        return carry

    lax.fori_loop(0, nc // 2, first_half, 0, unroll=2)
    lax.fori_loop(nc // 2, nc, second_half, 0)


def _mlstm(proj3, gcol, grow, conv_w, conv_b, head_gain):
    bsz, seq, _ = proj3.shape
    d = MLSTM_HEAD_DIM
    nh = MLSTM_HEADS
    nc = seq // CHUNK

    def col_block(offset):
        return pl.BlockSpec((1, seq, d), lambda b, h: (b, 0, offset + h))

    return pl.pallas_call(
        _mlstm_kernel,
        out_shape=jax.ShapeDtypeStruct((bsz, seq, nh * d), BF16),
        grid=(bsz, nh),
        in_specs=[
            col_block(0), col_block(nh), col_block(2 * nh), col_block(3 * nh), col_block(4 * nh),
            pl.BlockSpec((1, seq, GATE_LANES), lambda b, h: (b, 0, 0)),
            pl.BlockSpec((1, nc, 2 * MLSTM_HEADS, CHUNK), lambda b, h: (b, 0, 0, 0)),
            pl.BlockSpec((3, d), lambda b, h: (0, h)),
            pl.BlockSpec((3, d), lambda b, h: (0, nh + h)),
            pl.BlockSpec((1, d), lambda b, h: (0, h)),
            pl.BlockSpec((1, d), lambda b, h: (0, nh + h)),
            pl.BlockSpec((1, d), lambda b, h: (0, h)),
        ],
        out_specs=pl.BlockSpec((1, seq, d), lambda b, h: (b, 0, h)),
        scratch_shapes=[
            pltpu.VMEM((seq, d), BF16),
            pltpu.VMEM((nc, d, CHUNK), BF16),
            pltpu.VMEM((seq, d), F32),
            pltpu.VMEM((seq, GATE_LANES), F32),
            pltpu.VMEM((d, d + LANES), F32), pltpu.VMEM((1, LANES), F32),
            pltpu.VMEM((d, d + LANES), F32), pltpu.VMEM((1, LANES), F32),
        ],
        compiler_params=pltpu.CompilerParams(
            dimension_semantics=("arbitrary", "arbitrary"), vmem_limit_bytes=VMEM_LIMIT_BYTES),
        name="mlstm",
    )(proj3, proj3, proj3, proj3, proj3, gcol, grow, conv_w, conv_w,
      conv_b.reshape(1, -1), conv_b.reshape(1, -1), head_gain.reshape(1, -1))


def _natten_bias_rows(rpb):
    n_dc = 2 * NA_COLS - 1
    n_dr = 2 * NA_ROWS - 1
    left = NA_COLS
    period = 64
    padded = jnp.pad(rpb.astype(F32) * LOG2E, ((0, 0), (0, 0), (left, period - left - n_dc)), constant_values=NEG)
    out = []
    for (c0, w), kb in zip(Q_BLOCKS, K_STARTS):
        base = kb - c0 + NA_COLS - 1 + left
        toe = jnp.tile(padded, (1, 1, w))[..., :(period - 1) * w].reshape(NA_HEADS, n_dr, w, period - 1)
        toe = toe[..., base:base + K_WIDTH]
        q_col = c0 + np.arange(w)[:, None]
        k_col = kb + np.arange(K_WIDTH)[None, :]
        win = np.clip(q_col - NA_COLS // 2, 0, GRID_W - NA_COLS)
        col_ok = jnp.asarray((k_col >= win) & (k_col < win + NA_COLS))
        toe = jnp.where(col_ok[None, None], toe, NEG).transpose(0, 2, 1, 3).reshape(NA_HEADS, w, n_dr * K_WIDTH)
        out.append(jnp.pad(toe, ((0, 0), (0, Q_BLOCK_MAX - w), (0, N_KEYS - n_dr * K_WIDTH)), constant_values=NEG))
    return jnp.stack(out, axis=1)


_GROUP_KINDS = (
    (0, lambda i: max(i - NA_ROWS // 2, 0)),
    (-(NA_ROWS // 2), lambda i: i),
    (Q_ROWS - SLAB_ROWS, lambda i: min(i + NA_ROWS // 2, NA_ROWS)),
)


def _build_bias(e_ref, tab_ref, slab_off, first_valid):
    for h in range(NA_HEADS):
        row_off = 0
        for blk, (_, w) in enumerate(Q_BLOCKS):
            e = e_ref[h, blk, 0:w, :]
            lane = lax.broadcasted_iota(jnp.int32, (w, N_KEYS), 1)
            for i in range(Q_ROWS):
                shift = slab_off - i + NA_ROWS - 1
                amount = (-shift * K_WIDTH) % N_KEYS
                moved = e if amount == 0 else pltpu.roll(e, amount, 1)
                lo = first_valid(i) * K_WIDTH
                ok = (lane >= lo) & (lane < lo + NA_ROWS * K_WIDTH)
                tab_ref[h, row_off + i * w:row_off + (i + 1) * w, :] = jnp.where(ok, moved, NEG)
            row_off += Q_ROWS * w


def _natten_kernel(q_ref, k0_ref, k1_ref, k2_ref, k3_ref, v0_ref, v1_ref, v2_ref, v3_ref, z_ref, e_ref,
                   y_ref, qf_ref, of_ref, tab_ref, sa_ref, sb_ref):
    k_refs = (k0_ref, k1_ref, k2_ref, k3_ref)
    v_refs = (v0_ref, v1_ref, v2_ref, v3_ref)
    piece_rows = SLAB_ROWS // 4
    group = pl.program_id(0)
    n_groups = pl.num_programs(0)

    first_batch = pl.program_id(1) == 0
    conditions = (group == 0, jnp.logical_and(group > 0, group < n_groups - 1), group == n_groups - 1)
    for cond, (slab_off, first_valid) in zip(conditions, _GROUP_KINDS):
        @pl.when(jnp.logical_and(first_batch, cond))
        def _(slab_off=slab_off, first_valid=first_valid):
            _build_bias(e_ref, tab_ref, slab_off, first_valid)

    n_batch, q_tok = q_ref.shape[0], q_ref.shape[2]
    for bi in range(n_batch):
        qf_ref[pl.ds(bi * q_tok, q_tok), :] = q_ref[bi, 0].astype(F32) * (NA_HEAD_DIM ** -0.5 * LOG2E)

    items = []
    for bi in range(n_batch):
        row_off = 0
        for (c0, w), kb in zip(Q_BLOCKS, K_STARTS):
            for pair in range(NA_HEADS // 2):
                items.append((bi, c0, w, kb, pair, row_off))
            row_off += Q_ROWS * w
    s_refs = (sa_ref, sb_ref)

    def slab(refs, bi, kb, lanes):
        return jnp.concatenate(
            [refs[a // piece_rows][bi, 0, pl.ds((a % piece_rows) * GRID_W + kb, K_WIDTH), lanes]
             for a in range(SLAB_ROWS)], axis=0)

    def scores(idx):
        bi, c0, w, kb, pair, row_off = items[idx]
        nq = Q_ROWS * w
        lanes = pl.ds(pair * LANES, LANES)
        first_head = lax.broadcasted_iota(jnp.int32, (nq, LANES), 1) < NA_HEAD_DIM
        qp = jnp.concatenate([qf_ref[pl.ds(bi * q_tok + i * GRID_W + c0, w), lanes] for i in range(Q_ROWS)],
                             axis=0)
        qs = jnp.concatenate([jnp.where(first_head, qp, 0.0), jnp.where(first_head, 0.0, qp)],
                             axis=0).astype(BF16)
        s = lax.dot_general(qs, slab(k_refs, bi, kb, lanes), (((1,), (1,)), ((), ())),
                            preferred_element_type=F32)
        bias = jnp.concatenate([tab_ref[2 * pair, pl.ds(row_off, nq), :],
                                tab_ref[2 * pair + 1, pl.ds(row_off, nq), :]], axis=0)
        s_refs[idx % 2][0:2 * nq, :] = s + bias

    def attend(idx):
        bi, c0, w, kb, pair, _ = items[idx]
        nq = Q_ROWS * w
        lanes = pl.ds(pair * LANES, LANES)
        first_head = lax.broadcasted_iota(jnp.int32, (nq, LANES), 1) < NA_HEAD_DIM
        s = s_refs[idx % 2][0:2 * nq, :]
        p = jnp.exp2(s - jnp.max(s, axis=-1, keepdims=True))
        o2 = jnp.dot(p.astype(BF16), slab(v_refs, bi, kb, lanes), preferred_element_type=F32)
        o2 = o2 * (1.0 / jnp.sum(p, axis=-1, keepdims=True))
        o = jnp.where(first_head, o2[0:nq], o2[nq:2 * nq])
        for i in range(Q_ROWS):
            of_ref[pl.ds(bi * q_tok + i * GRID_W + c0, w), lanes] = o[i * w:(i + 1) * w]

    scores(0)
    for idx in range(len(items)):
        if idx + 1 < len(items):
            scores(idx + 1)
        attend(idx)
    for bi in range(n_batch):
        rows = pl.ds(bi * q_tok, q_tok)
        y_ref[bi, 0] = (of_ref[rows, :] * _silu(z_ref[bi, 0].astype(F32))).astype(BF16)


def _natten(proj3, bias_rows):
    bsz, seq, n_main = proj3.shape
    q_tok = Q_ROWS * GRID_W
    p_tok = (SLAB_ROWS // 4) * GRID_W
    groups = seq // q_tok
    pieces = seq // p_tok
    assert groups >= 2, "needs at least two 8-row query groups"
    nb = 2 if bsz % 2 == 0 else 1
    w = NA_WIDTH
    qb, kb, vb, zb = (5 * D_MODEL) // w, (5 * D_MODEL) // w + 1, (5 * D_MODEL) // w + 2, (5 * D_MODEL) // w + 3
    proj_q = proj3.reshape(bsz, groups, q_tok, n_main)
    proj_p = proj3.reshape(bsz, pieces, p_tok, n_main)

    def slab_start(g):
        return jnp.clip(2 * g - 1, 0, pieces - 4)

    def piece_spec(col, a):
        return pl.BlockSpec((nb, 1, p_tok, w), lambda g, b: (b, slab_start(g) + a, 0, col))

    return pl.pallas_call(
        _natten_kernel,
        out_shape=jax.ShapeDtypeStruct((bsz, groups, q_tok, w), BF16),
        grid=(groups, bsz // nb),
        in_specs=[pl.BlockSpec((nb, 1, q_tok, w), lambda g, b: (b, g, 0, qb))]
        + [piece_spec(kb, a) for a in range(4)] + [piece_spec(vb, a) for a in range(4)]
        + [pl.BlockSpec((nb, 1, q_tok, w), lambda g, b: (b, g, 0, zb)),
           pl.BlockSpec(bias_rows.shape, lambda g, b: (0, 0, 0, 0))],
        out_specs=pl.BlockSpec((nb, 1, q_tok, w), lambda g, b: (b, g, 0, 0)),
        scratch_shapes=[pltpu.VMEM((nb * q_tok, w), F32), pltpu.VMEM((nb * q_tok, w), F32),
                        pltpu.VMEM((NA_HEADS, q_tok, N_KEYS), F32),
                        pltpu.VMEM((2 * Q_ROWS * Q_BLOCK_MAX, N_KEYS), F32),
                        pltpu.VMEM((2 * Q_ROWS * Q_BLOCK_MAX, N_KEYS), F32)],
        compiler_params=pltpu.CompilerParams(
            dimension_semantics=("arbitrary", "arbitrary"), vmem_limit_bytes=VMEM_LIMIT_BYTES),
        name="natten",
    )(proj_q, *([proj_p] * 8), proj_q, bias_rows).reshape(bsz, seq, w)


def _merge_out_kernel(ya_ref, yb_ref, g0_ref, g1_ref, x_ref, mod_ref, bm_ref, wa_ref, wb_ref, wo_ref, fg_ref,
                      out_ref, m_ref):
    tm, d = x_ref.shape
    sub = min(MERGE_ROWS, tm)
    gate = mod_ref[0, :, 2 * d:3 * d]

    def merge(r):
        rows = pl.ds(r, sub)
        pa = jnp.dot(ya_ref[rows, :], wa_ref[...], preferred_element_type=F32)
        pb = jnp.dot(yb_ref[rows, :], wb_ref[...], preferred_element_type=F32)
        m_ref[rows, :] = (_sigmoid(g0_ref[rows, :].astype(F32) + bm_ref[0:1, :]) * pa
                          + _sigmoid(g1_ref[rows, :].astype(F32) + bm_ref[1:2, :]) * pb).astype(BF16)

    def project(r):
        rows = pl.ds(r, sub)
        mixed = jnp.dot(m_ref[rows, :], wo_ref[...], preferred_element_type=F32)
        xo = x_ref[rows, :] + gate * mixed
        out_ref[rows, :] = xo * lax.rsqrt(jnp.mean(xo * xo, axis=-1, keepdims=True) + EPS) * fg_ref[...]

    starts = list(range(0, tm, sub))
    merge(starts[0])
    for i, r in enumerate(starts):
        if i + 1 < len(starts):
            merge(starts[i + 1])
        project(r)


def _merge_out(ya2, yb2, proj2, x2, mod3, b_merge, wa, wb, wo, final_gain, seq, tm):
    n_tok, d = x2.shape
    g0 = (N_MAIN - 2 * d) // d

    def full(shape):
        return pl.BlockSpec(shape, lambda m: (0,) * len(shape))

    return pl.pallas_call(
        _merge_out_kernel,
        out_shape=jax.ShapeDtypeStruct((n_tok, d), F32),
        grid=(n_tok // tm,),
        in_specs=[
            pl.BlockSpec((tm, d), lambda m: (m, 0)),
            pl.BlockSpec((tm, NA_WIDTH), lambda m: (m, 0)),
            pl.BlockSpec((tm, d), lambda m: (m, g0)),
            pl.BlockSpec((tm, d), lambda m: (m, g0 + 1)),
            pl.BlockSpec((tm, d), lambda m: (m, 0)),
            pl.BlockSpec((1, 1, 3 * d), lambda m: ((m * tm) // seq, 0, 0)),
            full((2, d)), full((d, d)), full((NA_WIDTH, d)), full((d, d)), full((1, d)),
        ],
        out_specs=pl.BlockSpec((tm, d), lambda m: (m, 0)),
        scratch_shapes=[pltpu.VMEM((tm, d), BF16)],
        compiler_params=pltpu.CompilerParams(
            dimension_semantics=("arbitrary",), vmem_limit_bytes=VMEM_LIMIT_BYTES),
        name="merge_out",
    )(ya2, yb2, proj2, proj2, x2, mod3, b_merge, wa, wb, wo, final_gain.reshape(1, d))


def _gate_layout(n_cols):
    src = np.arange(n_cols)
    direction, gate, head = src // (2 * MLSTM_HEADS), (src // MLSTM_HEADS) % 2, src % MLSTM_HEADS
    return 2 * MLSTM_HEADS * gate + MLSTM_HEADS * direction + head


def _weight_prep_kernel(a_ref, b_ref, perm_ref, main_ref, gate_ref, *, gate_block, n_gates):
    n = pl.program_id(0)
    rows = a_ref[0]
    past_gates = jnp.concatenate([rows[n_gates:], b_ref[0]], axis=0)
    main_ref[...] = jnp.where(n >= gate_block, past_gates, rows).T.astype(BF16)

    @pl.when(n == gate_block)
    def _():
        gate_ref[...] = jnp.dot(perm_ref[...], rows[0:n_gates], preferred_element_type=F32).T.astype(BF16)


def _weight_prep(w_in, layer):
    _, d, n_in = w_in.shape
    n_gates = 4 * MLSTM_HEADS
    tn = D_MODEL
    g_lo = 5 * MLSTM_HEADS * MLSTM_HEAD_DIM
    assert n_in == N_MAIN + n_gates and g_lo % tn == 0 and tn % n_gates == 0
    perm = np.zeros((GATE_LANES, n_gates), np.float32)
    perm[_gate_layout(n_gates), np.arange(n_gates)] = 1.0
    w_t = jnp.swapaxes(w_in, 1, 2)
    return pl.pallas_call(
        functools.partial(_weight_prep_kernel, gate_block=g_lo // tn, n_gates=n_gates),
        out_shape=(jax.ShapeDtypeStruct((d, N_MAIN), BF16), jax.ShapeDtypeStruct((d, GATE_LANES), BF16)),
        grid=(N_MAIN // tn,),
        in_specs=[pl.BlockSpec((1, tn, d), lambda n: (layer, n, 0)),
                  pl.BlockSpec((1, n_gates, d), lambda n: (layer, (n + 1) * (tn // n_gates), 0)),
                  pl.BlockSpec((GATE_LANES, n_gates), lambda n: (0, 0))],
        out_specs=(pl.BlockSpec((d, tn), lambda n: (0, n)),
                   pl.BlockSpec((d, GATE_LANES), lambda n: (0, 0))),
        compiler_params=pltpu.CompilerParams(dimension_semantics=("arbitrary",), vmem_limit_bytes=VMEM_LIMIT_BYTES),
        name="weight_prep",
    )(w_t, w_t, jnp.asarray(perm))


def _token_tiles(seq):
    return min(2048, seq), min(1024, seq)


def _layer(x, mod, norm_gain, w_in, layer, conv_w, conv_b, b_igate, b_fgate, mlstm_norm_gain, rpb, w_proj_a, w_proj_b,
           b_merge, w_out, final_gain):
    bsz, seq, d = x.shape
    n_tok = bsz * seq
    wa = MLSTM_HEADS * MLSTM_HEAD_DIM
    w_main, w_gate = _weight_prep(w_in, layer)
    gate_bias = jnp.zeros((1, GATE_LANES), F32).at[0, _gate_layout(4 * MLSTM_HEADS)].set(
        jnp.stack([b_igate, b_fgate], axis=1).reshape(-1))
    tm_in, tm_out = _token_tiles(seq)

    x2 = x.reshape(n_tok, d)
    mod3 = mod.reshape(bsz, 1, 3 * d)
    proj2, gates2 = _in_proj(x2, mod3, norm_gain, w_main, w_gate, seq, tm_in)
    proj3 = proj2.reshape(bsz, seq, N_MAIN)
    gcol, grow = _gate_prep(gates2.reshape(bsz, seq, GATE_LANES), gate_bias)
    y_a = _mlstm(proj3, gcol, grow, conv_w, conv_b, mlstm_norm_gain)
    y_b = _natten(proj3, _natten_bias_rows(rpb))
    out2 = _merge_out(y_a.reshape(n_tok, wa), y_b.reshape(n_tok, NA_WIDTH), proj2, x2, mod3, b_merge,
                      w_proj_a.astype(BF16), w_proj_b.astype(BF16), w_out.astype(BF16), final_gain, seq, tm_out)
    return out2.reshape(bsz, seq, d)


def kernel(x, c, w_ada, b_ada, norm_gain, w_in, conv_w, conv_b, b_igate, b_fgate, mlstm_norm_gain, rpb, w_proj_a,
           w_proj_b, b_merge, w_out, final_gain):
    depth = w_ada.shape[0]
    assert depth == 1, "the fused final norm assumes a single layer"
    mod = _adaln_mod(c, w_ada[0], b_ada[0])
    return _layer(x, mod, norm_gain[0], w_in, 0, conv_w[0], conv_b[0], b_igate[0], b_fgate[0], mlstm_norm_gain[0],
                  rpb[0], w_proj_a[0], w_proj_b[0], b_merge[0], w_out[0], final_gain)
```

```python
import functools
import math

import numpy as np
import jax
import jax.numpy as jnp
from jax import lax
from jax.experimental import pallas as pl
from jax.experimental.pallas import tpu as pltpu

F32 = jnp.float32
BF16 = jnp.bfloat16

D_MODEL = 1024
GRID_W = 64
MLSTM_HEADS = 4
MLSTM_HEAD_DIM = 256
NA_HEADS = 8
NA_WIDTH = 512
NA_HEAD_DIM = 64
NA_ROWS = 8
NA_COLS = 16
EPS = 1e-6
NEG = -1e30
LOG2E = 1.4426950408889634
K_SCALE_LOG2 = -0.5 * math.log2(MLSTM_HEAD_DIM)

VMEM_LIMIT_BYTES = 56 * 1024 * 1024
LANES = 128

CHUNK = 256

MERGE_ROWS = 512

N_MAIN = 9 * D_MODEL
GATE_LANES = 128

Q_ROWS = 8
SLAB_ROWS = 16
Q_BLOCKS = ((0, 24), (24, 16), (40, 24))
Q_BLOCK_MAX = 24
K_STARTS = (0, 16, 32)
K_WIDTH = 32
N_KEYS = SLAB_ROWS * K_WIDTH


def _sigmoid(x):
    return 1.0 / (1.0 + jnp.exp2(x * -LOG2E))


def _silu(x):
    return x * _sigmoid(x)


def _adaln_kernel(c_ref, w_ref, b_ref, o_ref):
    cond = _silu(c_ref[...]).astype(BF16)
    o_ref[...] = jnp.dot(cond, w_ref[...].astype(BF16), preferred_element_type=F32) + b_ref[...]


def _adaln_mod(c, w_ada, b_ada):
    bsz, d = c.shape
    n = w_ada.shape[1]
    tn = 512
    return pl.pallas_call(
        _adaln_kernel,
        out_shape=jax.ShapeDtypeStruct((bsz, n), F32),
        grid=(n // tn,),
        in_specs=[
            pl.BlockSpec((bsz, d), lambda j: (0, 0)),
            pl.BlockSpec((d, tn), lambda j: (0, j)),
            pl.BlockSpec((1, tn), lambda j: (0, j)),
        ],
        out_specs=pl.BlockSpec((bsz, tn), lambda j: (0, j)),
        compiler_params=pltpu.CompilerParams(dimension_semantics=("arbitrary",)),
        name="adaln_mod",
    )(c, w_ada, b_ada.reshape(1, n))


def _in_proj_kernel(x_ref, mod_ref, gain_ref, w_ref, wg_ref, proj_ref, gates_ref, h_ref, *, rows):
    tm, d = x_ref.shape
    n = pl.program_id(1)

    @pl.when(n == 0)
    def _():
        shift = mod_ref[0, :, 0:d]
        scale = mod_ref[0, :, d:2 * d]
        gain = gain_ref[...]

        def normalize(r):
            xf = x_ref[r:r + rows, :]
            y = xf * lax.rsqrt(jnp.mean(xf * xf, axis=-1, keepdims=True) + EPS) * gain
            h_ref[r:r + rows, :] = (y * (1.0 + scale) + shift).astype(BF16)

        def project(r):
            hb = h_ref[r:r + rows, :]
            proj_ref[r:r + rows, :] = jnp.dot(hb, w_ref[...], preferred_element_type=F32).astype(BF16)
            gates_ref[r:r + rows, :] = jnp.dot(hb, wg_ref[...], preferred_element_type=F32)

        starts = list(range(0, tm, rows))
        normalize(starts[0])
        for i, r in enumerate(starts):
            if i + 1 < len(starts):
                normalize(starts[i + 1])
            project(r)

    @pl.when(n > 0)
    def _():
        proj_ref[...] = jnp.dot(h_ref[...], w_ref[...], preferred_element_type=F32).astype(BF16)


def _in_proj(x2, mod3, norm_gain, w_main, w_gate, seq, tm):
    n_tok, d = x2.shape
    tn = N_MAIN // 6
    return pl.pallas_call(
        functools.partial(_in_proj_kernel, rows=min(MERGE_ROWS, tm)),
        out_shape=(
            jax.ShapeDtypeStruct((n_tok, N_MAIN), BF16),
            jax.ShapeDtypeStruct((n_tok, GATE_LANES), F32),
        ),
        grid=(n_tok // tm, N_MAIN // tn),
        in_specs=[
            pl.BlockSpec((tm, d), lambda m, n: (m, 0)),
            pl.BlockSpec((1, 1, 3 * d), lambda m, n: ((m * tm) // seq, 0, 0)),
            pl.BlockSpec((1, d), lambda m, n: (0, 0)),
            pl.BlockSpec((d, tn), lambda m, n: (0, n)),
            pl.BlockSpec((d, GATE_LANES), lambda m, n: (0, 0)),
        ],
        out_specs=(
            pl.BlockSpec((tm, tn), lambda m, n: (m, n)),
            pl.BlockSpec((tm, GATE_LANES), lambda m, n: (m, 0)),
        ),
        scratch_shapes=[pltpu.VMEM((tm, d), BF16)],
        compiler_params=pltpu.CompilerParams(
            dimension_semantics=("arbitrary", "arbitrary"), vmem_limit_bytes=VMEM_LIMIT_BYTES),
        name="in_proj",
    )(x2, mod3, norm_gain.reshape(1, d), w_main, w_gate)


def _gate_prep_kernel(g_ref, bias_ref, col_ref, row_ref):
    nc = row_ref.shape[1]
    v = (g_ref[0] + bias_ref[...]).T
    i_gate, f_gate = v[0:8, :] * LOG2E, v[8:16, :]
    seq = v.shape[1]
    pos = lax.broadcasted_iota(jnp.int32, i_gate.shape, 1) % CHUNK
    is_bwd = lax.broadcasted_iota(jnp.int32, i_gate.shape, 0) >= MLSTM_HEADS

    def scan(x, combine, identity):
        pre, suf = x, x
        step = 1
        while step < CHUNK:
            pre = combine(pre, jnp.where(pos >= step, pltpu.roll(pre, step, 1), identity))
            suf = combine(suf, jnp.where(pos < CHUNK - step, pltpu.roll(suf, seq - step, 1), identity))
            step *= 2
        return jnp.where(is_bwd, suf, pre)

    b = scan((jnp.minimum(f_gate, 0.0) - jnp.log1p(jnp.exp(-jnp.abs(f_gate)))) * LOG2E, jnp.add, 0.0)
    r = i_gate - b
    cm = scan(r, jnp.maximum, NEG)
    col_ref[0] = jnp.concatenate([cm, b, jnp.zeros((GATE_LANES - 16, seq), F32)], axis=0).T
    for c in range(nc):
        row_ref[0, c] = r[:, c * CHUNK:(c + 1) * CHUNK]


def _gate_prep(gates3, bias_row):
    bsz, seq, _ = gates3.shape
    nc = seq // CHUNK
    rows = 2 * MLSTM_HEADS
    return pl.pallas_call(
        _gate_prep_kernel,
        out_shape=(
            jax.ShapeDtypeStruct((bsz, seq, GATE_LANES), F32),
            jax.ShapeDtypeStruct((bsz, nc, rows, CHUNK), F32),
        ),
        grid=(bsz,),
        in_specs=[
            pl.BlockSpec((1, seq, GATE_LANES), lambda b: (b, 0, 0)),
            pl.BlockSpec((1, GATE_LANES), lambda b: (0, 0)),
        ],
        out_specs=(
            pl.BlockSpec((1, seq, GATE_LANES), lambda b: (b, 0, 0)),
            pl.BlockSpec((1, nc, rows, CHUNK), lambda b: (b, 0, 0, 0)),
        ),
        compiler_params=pltpu.CompilerParams(dimension_semantics=("arbitrary",)),
        name="gate_prep",
    )(gates3, bias_row)


def _mlstm_chunk(qc, kt, qk_ref, reuse_qk, vv, r_row, cm_rep, b_rep, cm_end, g, cn_ref, m_ref, causal_mask):
    d = vv.shape[1]
    cn = cn_ref[...]
    m_state = m_ref[...]
    m_col = jnp.maximum(m_state, cm_rep)
    w_inter = jnp.exp2(m_state - m_col)
    m_wide = jnp.concatenate([m_col, m_col], axis=1)
    d_mat = jnp.where(causal_mask, jnp.exp2(r_row - m_wide), 0.0)
    if reuse_qk:
        qk = qk_ref[...]
    else:
        qk = jnp.dot(qc, kt, preferred_element_type=F32)
        qk_ref[...] = qk
    scores = qk * d_mat
    inter = jnp.dot(qc, cn.astype(BF16), preferred_element_type=F32)
    yield None
    w_wide = jnp.concatenate([w_inter, w_inter], axis=1)
    num = jnp.dot(scores.astype(BF16), vv, preferred_element_type=F32) + w_wide * inter[:, 0:d]
    den = jnp.sum(scores, axis=-1, keepdims=True) + w_inter * inter[:, d:]
    inv = 1.0 / jnp.maximum(jnp.abs(den), jnp.exp2(-(b_rep + m_col)))
    yield num * jnp.concatenate([inv, inv], axis=1)
    m_end = jnp.maximum(m_state, cm_end)
    w_row = jnp.exp2(r_row - jnp.concatenate([m_end, m_end], axis=1))
    decay = jnp.exp2(m_state - m_end)
    kw = kt * w_row.astype(BF16)
    v_aug = jnp.concatenate([vv, jnp.ones((vv.shape[0], LANES), BF16)], axis=1)
    decay_wide = jnp.concatenate([decay, decay, decay], axis=1)
    cn_ref[...] = decay_wide * cn + jnp.dot(kw, v_aug, preferred_element_type=F32)
    m_ref[...] = g + m_end
    yield None


def _mlstm_kernel(q_ref, k_ref, v_ref, o_ref, z_ref, gcol_ref, grow_ref, cwq_ref, cwk_ref, cbq_ref, cbk_ref,
                  gain_ref, y_ref, qc_ref, kt_ref, qk_ref, h_ref, gc_ref, cnf_ref, mf_ref, cnb_ref, mb_ref):
    seq, d = q_ref.shape[1], q_ref.shape[2]
    nc = seq // CHUNK
    head = pl.program_id(1)

    t_idx = lax.broadcasted_iota(jnp.int32, (CHUNK, CHUNK), 0)
    s_idx = lax.broadcasted_iota(jnp.int32, (CHUNK, CHUNK), 1)
    taps = jnp.concatenate([(s_idx == t_idx - 1).astype(BF16), (s_idx == t_idx + 1).astype(BF16)],
                           axis=0)
    edge = lax.broadcasted_iota(jnp.int32, (8, d), 0)

    def conv_silu(src_ref, w_ref, b_ref, c, r):
        w0, w1, w2 = 0.5 * w_ref[0:1, :], 0.5 * w_ref[1:2, :], 0.5 * w_ref[2:3, :]
        src = src_ref[0, pl.ds(r, CHUNK), :]
        p = jnp.dot(taps, src, preferred_element_type=F32)
        t = p[0:CHUNK] * w0 + src.astype(F32) * w1 + p[CHUNK:2 * CHUNK] * w2 + 0.5 * b_ref[...]
        above = src_ref[0, pl.ds(pl.multiple_of(jnp.maximum(r - 16, 0), 16), 16), :][15:16, :].astype(F32)
        below = src_ref[0, pl.ds(pl.multiple_of(jnp.minimum(r + CHUNK, seq - 16), 16), 16), :][0:1, :].astype(F32)
        above = above * jnp.where(c > 0, w0, 0.0)
        below = below * jnp.where(c < nc - 1, w2, 0.0)
        t = jnp.concatenate([t[0:8] + jnp.where(edge == 0, above, 0.0), t[8:CHUNK - 8],
                             t[CHUNK - 8:CHUNK] + jnp.where(edge == 7, below, 0.0)], axis=0)
        return t * (1.0 + jnp.tanh(t))

    def conv_chunk(c, carry):
        r = pl.multiple_of(c * CHUNK, CHUNK)
        qc_ref[pl.ds(r, CHUNK), :] = conv_silu(q_ref, cwq_ref, cbq_ref, c, r).astype(BF16)
        kt_ref[c] = conv_silu(k_ref, cwk_ref, cbk_ref, c, r).T.astype(BF16)
        return carry

    lax.fori_loop(0, nc, conv_chunk, 0, unroll=8)

    gc_ref[...] = pltpu.roll(gcol_ref[0], (GATE_LANES - head) % GATE_LANES, 1)
    gate_row = lax.broadcasted_iota(jnp.int32, (2 * MLSTM_HEADS, CHUNK), 0)

    for ref in (cnf_ref, mf_ref, cnb_ref, mb_ref):
        ref[...] = jnp.zeros(ref.shape, F32)

    mask_f = s_idx <= t_idx
    mask_b = s_idx >= t_idx

    def replicate(col):
        return jnp.broadcast_to(col, (col.shape[0], LANES))

    def direction(chunk, backward, reuse_qk):
        r = pl.multiple_of(chunk * CHUNK, CHUNK)
        rows = grow_ref[0, chunk]
        cols = gc_ref[pl.ds(r, CHUNK), :]
        qc = qc_ref[pl.ds(r, CHUNK), :]
        kt = kt_ref[chunk]
        vv = v_ref[0, pl.ds(r, CHUNK), :]
        if backward:
            cm_rep, b_rep = replicate(cols[:, 4:5]), replicate(cols[:, 12:13])
            end = 0
            state, mask, this_row = (cnb_ref, mb_ref), mask_b, MLSTM_HEADS + head
        else:
            cm_rep, b_rep = replicate(cols[:, 0:1]), replicate(cols[:, 8:9])
            end = CHUNK - 1
            state, mask, this_row = (cnf_ref, mf_ref), mask_f, head
        r_row = jnp.sum(jnp.where(gate_row == this_row, rows, 0.0), axis=0, keepdims=True) + K_SCALE_LOG2
        cm_end, g = cm_rep[end:end + 1, :], b_rep[end:end + 1, :]
        return _mlstm_chunk(qc, kt, qk_ref.at[chunk], reuse_qk, vv, r_row, cm_rep, b_rep, cm_end, g, *state, mask), r

    def chunk_steps(j, second_visit):
        steps = [direction(j, False, second_visit), direction(nc - 1 - j, True, second_visit)]
        if second_visit:
            for gen, _ in steps:
                next(gen)
            hs = [next(gen) for gen, _ in steps]
            for gen, _ in steps:
                next(gen)
        else:
            hs = []
            for gen, _ in steps:
                next(gen)
                hs.append(next(gen))
                next(gen)
        return [(h, r) for h, (_, r) in zip(hs, steps)]

    def first_half(j, carry):
        for h_dir, r in chunk_steps(j, second_visit=False):
            h_ref[pl.ds(r, CHUNK), :] = h_dir
        return carry

    gain = gain_ref[...]

    def finish(h_dir, r):
        h_sum = (h_ref[pl.ds(r, CHUNK), :] + h_dir) * _sigmoid(o_ref[0, pl.ds(r, CHUNK), :].astype(F32))
        normed = h_sum * lax.rsqrt(jnp.mean(h_sum * h_sum, axis=-1, keepdims=True) + EPS) * gain
        y_ref[0, pl.ds(r, CHUNK), :] = (normed * _silu(z_ref[0, pl.ds(r, CHUNK), :].astype(F32))).astype(BF16)

    def second_half(j, carry):
        for h_dir, r in chunk_steps(j, second_visit=True):
            finish(h_dir, r)
        return carry

    lax.fori_loop(0, nc // 2, first_half, 0, unroll=2)
    lax.fori_loop(nc // 2, nc, second_half, 0)


def _mlstm(proj3, gcol, grow, conv_w, conv_b, head_gain):
    bsz, seq, _ = proj3.shape
    d = MLSTM_HEAD_DIM
    nh = MLSTM_HEADS
    nc = seq // CHUNK

    def col_block(offset):
        return pl.BlockSpec((1, seq, d), lambda b, h: (b, 0, offset + h))

    return pl.pallas_call(
        _mlstm_kernel,
        out_shape=jax.ShapeDtypeStruct((bsz, seq, nh * d), BF16),
        grid=(bsz, nh),
        in_specs=[
            col_block(0), col_block(nh), col_block(2 * nh), col_block(3 * nh), col_block(4 * nh),
            pl.BlockSpec((1, seq, GATE_LANES), lambda b, h: (b, 0, 0)),
            pl.BlockSpec((1, nc, 2 * MLSTM_HEADS, CHUNK), lambda b, h: (b, 0, 0, 0)),
            pl.BlockSpec((3, d), lambda b, h: (0, h)),
            pl.BlockSpec((3, d), lambda b, h: (0, nh + h)),
            pl.BlockSpec((1, d), lambda b, h: (0, h)),
            pl.BlockSpec((1, d), lambda b, h: (0, nh + h)),
            pl.BlockSpec((1, d), lambda b, h: (0, h)),
        ],
        out_specs=pl.BlockSpec((1, seq, d), lambda b, h: (b, 0, h)),
        scratch_shapes=[
            pltpu.VMEM((seq, d), BF16),
            pltpu.VMEM((nc, d, CHUNK), BF16),
            pltpu.VMEM((nc, CHUNK, CHUNK), F32),
            pltpu.VMEM((seq, d), F32),
            pltpu.VMEM((seq, GATE_LANES), F32),
            pltpu.VMEM((d, d + LANES), F32), pltpu.VMEM((1, LANES), F32),
            pltpu.VMEM((d, d + LANES), F32), pltpu.VMEM((1, LANES), F32),
        ],
        compiler_params=pltpu.CompilerParams(
            dimension_semantics=("arbitrary", "arbitrary"), vmem_limit_bytes=VMEM_LIMIT_BYTES),
        name="mlstm",
    )(proj3, proj3, proj3, proj3, proj3, gcol, grow, conv_w, conv_w,
      conv_b.reshape(1, -1), conv_b.reshape(1, -1), head_gain.reshape(1, -1))


def _natten_bias_rows(rpb):
    n_dc = 2 * NA_COLS - 1
    n_dr = 2 * NA_ROWS - 1
    left = NA_COLS
    period = 64
    padded = jnp.pad(rpb.astype(F32) * LOG2E, ((0, 0), (0, 0), (left, period - left - n_dc)), constant_values=NEG)
    out = []
    for (c0, w), kb in zip(Q_BLOCKS, K_STARTS):
        base = kb - c0 + NA_COLS - 1 + left
        toe = jnp.tile(padded, (1, 1, w))[..., :(period - 1) * w].reshape(NA_HEADS, n_dr, w, period - 1)
        toe = toe[..., base:base + K_WIDTH]
        q_col = c0 + np.arange(w)[:, None]
        k_col = kb + np.arange(K_WIDTH)[None, :]
        win = np.clip(q_col - NA_COLS // 2, 0, GRID_W - NA_COLS)
        col_ok = jnp.asarray((k_col >= win) & (k_col < win + NA_COLS))
        toe = jnp.where(col_ok[None, None], toe, NEG).transpose(0, 2, 1, 3).reshape(NA_HEADS, w, n_dr * K_WIDTH)
        out.append(jnp.pad(toe, ((0, 0), (0, Q_BLOCK_MAX - w), (0, N_KEYS - n_dr * K_WIDTH)), constant_values=NEG))
    return jnp.stack(out, axis=1)


_GROUP_KINDS = (
    (0, lambda i: max(i - NA_ROWS // 2, 0)),
    (-(NA_ROWS // 2), lambda i: i),
    (Q_ROWS - SLAB_ROWS, lambda i: min(i + NA_ROWS // 2, NA_ROWS)),
)


def _build_bias(e_ref, tab_ref, slab_off, first_valid):
    for h in range(NA_HEADS):
        row_off = 0
        for blk, (_, w) in enumerate(Q_BLOCKS):
            e = e_ref[h, blk, 0:w, :]
            lane = lax.broadcasted_iota(jnp.int32, (w, N_KEYS), 1)
            for i in range(Q_ROWS):
                shift = slab_off - i + NA_ROWS - 1
                amount = (-shift * K_WIDTH) % N_KEYS
                moved = e if amount == 0 else pltpu.roll(e, amount, 1)
                lo = first_valid(i) * K_WIDTH
                ok = (lane >= lo) & (lane < lo + NA_ROWS * K_WIDTH)
                tab_ref[h, row_off + i * w:row_off + (i + 1) * w, :] = jnp.where(ok, moved, NEG)
            row_off += Q_ROWS * w


def _natten_kernel(q_ref, k0_ref, k1_ref, k2_ref, k3_ref, v0_ref, v1_ref, v2_ref, v3_ref, z_ref, e_ref,
                   y_ref, qf_ref, of_ref, tab_ref, sa_ref, sb_ref):
    k_refs = (k0_ref, k1_ref, k2_ref, k3_ref)
    v_refs = (v0_ref, v1_ref, v2_ref, v3_ref)
    piece_rows = SLAB_ROWS // 4
    group = pl.program_id(0)
    n_groups = pl.num_programs(0)

    first_batch = pl.program_id(1) == 0
    conditions = (group == 0, jnp.logical_and(group > 0, group < n_groups - 1), group == n_groups - 1)
    for cond, (slab_off, first_valid) in zip(conditions, _GROUP_KINDS):
        @pl.when(jnp.logical_and(first_batch, cond))
        def _(slab_off=slab_off, first_valid=first_valid):
            _build_bias(e_ref, tab_ref, slab_off, first_valid)

    n_batch, q_tok = q_ref.shape[0], q_ref.shape[2]
    for bi in range(n_batch):
        qf_ref[pl.ds(bi * q_tok, q_tok), :] = q_ref[bi, 0].astype(F32) * (NA_HEAD_DIM ** -0.5 * LOG2E)

    items = []
    for bi in range(n_batch):
        row_off = 0
        for (c0, w), kb in zip(Q_BLOCKS, K_STARTS):
            for pair in range(NA_HEADS // 2):
                items.append((bi, c0, w, kb, pair, row_off))
            row_off += Q_ROWS * w
    s_refs = (sa_ref, sb_ref)

    def slab(refs, bi, kb, lanes):
        return jnp.concatenate(
            [refs[a // piece_rows][bi, 0, pl.ds((a % piece_rows) * GRID_W + kb, K_WIDTH), lanes]
             for a in range(SLAB_ROWS)], axis=0)

    def scores(idx):
        bi, c0, w, kb, pair, row_off = items[idx]
        nq = Q_ROWS * w
        lanes = pl.ds(pair * LANES, LANES)
        first_head = lax.broadcasted_iota(jnp.int32, (nq, LANES), 1) < NA_HEAD_DIM
        qp = jnp.concatenate([qf_ref[pl.ds(bi * q_tok + i * GRID_W + c0, w), lanes] for i in range(Q_ROWS)],
                             axis=0)
        qs = jnp.concatenate([jnp.where(first_head, qp, 0.0), jnp.where(first_head, 0.0, qp)],
                             axis=0).astype(BF16)
        s = lax.dot_general(qs, slab(k_refs, bi, kb, lanes), (((1,), (1,)), ((), ())),
                            preferred_element_type=F32)
        bias = jnp.concatenate([tab_ref[2 * pair, pl.ds(row_off, nq), :],
                                tab_ref[2 * pair + 1, pl.ds(row_off, nq), :]], axis=0)
        s_refs[idx % 2][0:2 * nq, :] = s + bias

    def attend(idx):
        bi, c0, w, kb, pair, _ = items[idx]
        nq = Q_ROWS * w
        lanes = pl.ds(pair * LANES, LANES)
        first_head = lax.broadcasted_iota(jnp.int32, (nq, LANES), 1) < NA_HEAD_DIM
        s = s_refs[idx % 2][0:2 * nq, :]
        p = jnp.exp2(s - jnp.max(s, axis=-1, keepdims=True))
        o2 = jnp.dot(p.astype(BF16), slab(v_refs, bi, kb, lanes), preferred_element_type=F32)
        o2 = o2 * (1.0 / jnp.sum(p, axis=-1, keepdims=True))
        o = jnp.where(first_head, o2[0:nq], o2[nq:2 * nq])
        for i in range(Q_ROWS):
            of_ref[pl.ds(bi * q_tok + i * GRID_W + c0, w), lanes] = o[i * w:(i + 1) * w]

    scores(0)
    for idx in range(len(items)):
        if idx + 1 < len(items):
            scores(idx + 1)
        attend(idx)
    for bi in range(n_batch):
        rows = pl.ds(bi * q_tok, q_tok)
        y_ref[bi, 0] = (of_ref[rows, :] * _silu(z_ref[bi, 0].astype(F32))).astype(BF16)


def _natten(proj3, bias_rows):
    bsz, seq, n_main = proj3.shape
    q_tok = Q_ROWS * GRID_W
    p_tok = (SLAB_ROWS // 4) * GRID_W
    groups = seq // q_tok
    pieces = seq // p_tok
    assert groups >= 2, "needs at least two 8-row query groups"
    nb = 2 if bsz % 2 == 0 else 1
    w = NA_WIDTH
    qb, kb, vb, zb = (5 * D_MODEL) // w, (5 * D_MODEL) // w + 1, (5 * D_MODEL) // w + 2, (5 * D_MODEL) // w + 3
    proj_q = proj3.reshape(bsz, groups, q_tok, n_main)
    proj_p = proj3.reshape(bsz, pieces, p_tok, n_main)

    def slab_start(g):
        return jnp.clip(2 * g - 1, 0, pieces - 4)

    def piece_spec(col, a):
        return pl.BlockSpec((nb, 1, p_tok, w), lambda g, b: (b, slab_start(g) + a, 0, col))

    return pl.pallas_call(
        _natten_kernel,
        out_shape=jax.ShapeDtypeStruct((bsz, groups, q_tok, w), BF16),
        grid=(groups, bsz // nb),
        in_specs=[pl.BlockSpec((nb, 1, q_tok, w), lambda g, b: (b, g, 0, qb))]
        + [piece_spec(kb, a) for a in range(4)] + [piece_spec(vb, a) for a in range(4)]
        + [pl.BlockSpec((nb, 1, q_tok, w), lambda g, b: (b, g, 0, zb)),
           pl.BlockSpec(bias_rows.shape, lambda g, b: (0, 0, 0, 0))],
        out_specs=pl.BlockSpec((nb, 1, q_tok, w), lambda g, b: (b, g, 0, 0)),
        scratch_shapes=[pltpu.VMEM((nb * q_tok, w), F32), pltpu.VMEM((nb * q_tok, w), F32),
                        pltpu.VMEM((NA_HEADS, q_tok, N_KEYS), F32),
                        pltpu.VMEM((2 * Q_ROWS * Q_BLOCK_MAX, N_KEYS), F32),
                        pltpu.VMEM((2 * Q_ROWS * Q_BLOCK_MAX, N_KEYS), F32)],
        compiler_params=pltpu.CompilerParams(
            dimension_semantics=("arbitrary", "arbitrary"), vmem_limit_bytes=VMEM_LIMIT_BYTES),
        name="natten",
    )(proj_q, *([proj_p] * 8), proj_q, bias_rows).reshape(bsz, seq, w)


def _merge_out_kernel(ya_ref, yb_ref, g0_ref, g1_ref, x_ref, mod_ref, bm_ref, wa_ref, wb_ref, wo_ref, fg_ref,
                      out_ref, m_ref):
    tm, d = x_ref.shape
    sub = min(MERGE_ROWS, tm)
    gate = mod_ref[0, :, 2 * d:3 * d]

    def merge(r):
        rows = pl.ds(r, sub)
        pa = jnp.dot(ya_ref[rows, :], wa_ref[...], preferred_element_type=F32)
        pb = jnp.dot(yb_ref[rows, :], wb_ref[...], preferred_element_type=F32)
        m_ref[rows, :] = (_sigmoid(g0_ref[rows, :].astype(F32) + bm_ref[0:1, :]) * pa
                          + _sigmoid(g1_ref[rows, :].astype(F32) + bm_ref[1:2, :]) * pb).astype(BF16)

    def project(r):
        rows = pl.ds(r, sub)
        mixed = jnp.dot(m_ref[rows, :], wo_ref[...], preferred_element_type=F32)
        xo = x_ref[rows, :] + gate * mixed
        out_ref[rows, :] = xo * lax.rsqrt(jnp.mean(xo * xo, axis=-1, keepdims=True) + EPS) * fg_ref[...]

    starts = list(range(0, tm, sub))
    merge(starts[0])
    for i, r in enumerate(starts):
        if i + 1 < len(starts):
            merge(starts[i + 1])
        project(r)


def _merge_out(ya2, yb2, proj2, x2, mod3, b_merge, wa, wb, wo, final_gain, seq, tm):
    n_tok, d = x2.shape
    g0 = (N_MAIN - 2 * d) // d

    def full(shape):
        return pl.BlockSpec(shape, lambda m: (0,) * len(shape))

    return pl.pallas_call(
        _merge_out_kernel,
        out_shape=jax.ShapeDtypeStruct((n_tok, d), F32),
        grid=(n_tok // tm,),
        in_specs=[
            pl.BlockSpec((tm, d), lambda m: (m, 0)),
            pl.BlockSpec((tm, NA_WIDTH), lambda m: (m, 0)),
            pl.BlockSpec((tm, d), lambda m: (m, g0)),
            pl.BlockSpec((tm, d), lambda m: (m, g0 + 1)),
            pl.BlockSpec((tm, d), lambda m: (m, 0)),
            pl.BlockSpec((1, 1, 3 * d), lambda m: ((m * tm) // seq, 0, 0)),
            full((2, d)), full((d, d)), full((NA_WIDTH, d)), full((d, d)), full((1, d)),
        ],
        out_specs=pl.BlockSpec((tm, d), lambda m: (m, 0)),
        scratch_shapes=[pltpu.VMEM((tm, d), BF16)],
        compiler_params=pltpu.CompilerParams(
            dimension_semantics=("arbitrary",), vmem_limit_bytes=VMEM_LIMIT_BYTES),
        name="merge_out",
    )(ya2, yb2, proj2, proj2, x2, mod3, b_merge, wa, wb, wo, final_gain.reshape(1, d))


def _gate_layout(n_cols):
    src = np.arange(n_cols)
    direction, gate, head = src // (2 * MLSTM_HEADS), (src // MLSTM_HEADS) % 2, src % MLSTM_HEADS
    return 2 * MLSTM_HEADS * gate + MLSTM_HEADS * direction + head


def _weight_prep_kernel(a_ref, b_ref, perm_ref, main_ref, gate_ref, *, gate_block, n_gates):
    n = pl.program_id(0)
    rows = a_ref[0]
    past_gates = jnp.concatenate([rows[n_gates:], b_ref[0]], axis=0)
    main_ref[...] = jnp.where(n >= gate_block, past_gates, rows).T.astype(BF16)

    @pl.when(n == gate_block)
    def _():
        gate_ref[...] = jnp.dot(perm_ref[...], rows[0:n_gates], preferred_element_type=F32).T.astype(BF16)


def _weight_prep(w_in, layer):
    _, d, n_in = w_in.shape
    n_gates = 4 * MLSTM_HEADS
    tn = D_MODEL
    g_lo = 5 * MLSTM_HEADS * MLSTM_HEAD_DIM
    assert n_in == N_MAIN + n_gates and g_lo % tn == 0 and tn % n_gates == 0
    perm = np.zeros((GATE_LANES, n_gates), np.float32)
    perm[_gate_layout(n_gates), np.arange(n_gates)] = 1.0
    w_t = jnp.swapaxes(w_in, 1, 2)
    return pl.pallas_call(
        functools.partial(_weight_prep_kernel, gate_block=g_lo // tn, n_gates=n_gates),
        out_shape=(jax.ShapeDtypeStruct((d, N_MAIN), BF16), jax.ShapeDtypeStruct((d, GATE_LANES), BF16)),
        grid=(N_MAIN // tn,),
        in_specs=[pl.BlockSpec((1, tn, d), lambda n: (layer, n, 0)),
                  pl.BlockSpec((1, n_gates, d), lambda n: (layer, (n + 1) * (tn // n_gates), 0)),
                  pl.BlockSpec((GATE_LANES, n_gates), lambda n: (0, 0))],
        out_specs=(pl.BlockSpec((d, tn), lambda n: (0, n)),
                   pl.BlockSpec((d, GATE_LANES), lambda n: (0, 0))),
        compiler_params=pltpu.CompilerParams(dimension_semantics=("arbitrary",), vmem_limit_bytes=VMEM_LIMIT_BYTES),
        name="weight_prep",
    )(w_t, w_t, jnp.asarray(perm))


def _token_tiles(seq):
    return min(2048, seq), min(1024, seq)


def _layer(x, mod, norm_gain, w_in, layer, conv_w, conv_b, b_igate, b_fgate, mlstm_norm_gain, rpb, w_proj_a, w_proj_b,
           b_merge, w_out, final_gain):
    bsz, seq, d = x.shape
    n_tok = bsz * seq
    wa = MLSTM_HEADS * MLSTM_HEAD_DIM
    w_main, w_gate = _weight_prep(w_in, layer)
    gate_bias = jnp.zeros((1, GATE_LANES), F32).at[0, _gate_layout(4 * MLSTM_HEADS)].set(
        jnp.stack([b_igate, b_fgate], axis=1).reshape(-1))
    tm_in, tm_out = _token_tiles(seq)

    x2 = x.reshape(n_tok, d)
    mod3 = mod.reshape(bsz, 1, 3 * d)
    proj2, gates2 = _in_proj(x2, mod3, norm_gain, w_main, w_gate, seq, tm_in)
    proj3 = proj2.reshape(bsz, seq, N_MAIN)
    gcol, grow = _gate_prep(gates2.reshape(bsz, seq, GATE_LANES), gate_bias)
    y_a = _mlstm(proj3, gcol, grow, conv_w, conv_b, mlstm_norm_gain)
    y_b = _natten(proj3, _natten_bias_rows(rpb))
    out2 = _merge_out(y_a.reshape(n_tok, wa), y_b.reshape(n_tok, NA_WIDTH), proj2, x2, mod3, b_merge,
                      w_proj_a.astype(BF16), w_proj_b.astype(BF16), w_out.astype(BF16), final_gain, seq, tm_out)
    return out2.reshape(bsz, seq, d)


def kernel(x, c, w_ada, b_ada, norm_gain, w_in, conv_w, conv_b, b_igate, b_fgate, mlstm_norm_gain, rpb, w_proj_a,
           w_proj_b, b_merge, w_out, final_gain):
    depth = w_ada.shape[0]
    assert depth == 1, "the fused final norm assumes a single layer"
    mod = _adaln_mod(c, w_ada[0], b_ada[0])
    return _layer(x, mod, norm_gain[0], w_in, 0, conv_w[0], conv_b[0], b_igate[0], b_fgate[0], mlstm_norm_gain[0],
                  rpb[0], w_proj_a[0], w_proj_b[0], b_merge[0], w_out[0], final_gain)
```

```python
import functools
import math

import numpy as np
import jax
import jax.numpy as jnp
from jax import lax
from jax.experimental import pallas as pl
from jax.experimental.pallas import tpu as pltpu

F32 = jnp.float32
BF16 = jnp.bfloat16

D_MODEL = 1024
GRID_W = 64
MLSTM_HEADS = 4
MLSTM_HEAD_DIM = 256
NA_HEADS = 8
NA_WIDTH = 512
NA_HEAD_DIM = 64
NA_ROWS = 8
NA_COLS = 16
EPS = 1e-6
NEG = -1e30
LOG2E = 1.4426950408889634
K_SCALE_LOG2 = -0.5 * math.log2(MLSTM_HEAD_DIM)

VMEM_LIMIT_BYTES = 56 * 1024 * 1024
LANES = 128

CHUNK = 256

MERGE_ROWS = 512

N_MAIN = 9 * D_MODEL
GATE_LANES = 128

Q_ROWS = 8
SLAB_ROWS = 16
Q_BLOCKS = ((0, 24), (24, 16), (40, 24))
Q_BLOCK_MAX = 24
K_STARTS = (0, 16, 32)
K_WIDTH = 32
N_KEYS = SLAB_ROWS * K_WIDTH


def _sigmoid(x):
    return 1.0 / (1.0 + jnp.exp2(x * -LOG2E))


def _silu(x):
    return x * _sigmoid(x)


def _adaln_kernel(c_ref, w_ref, b_ref, o_ref):
    cond = _silu(c_ref[...]).astype(BF16)
    o_ref[...] = jnp.dot(cond, w_ref[...].astype(BF16), preferred_element_type=F32) + b_ref[...]


def _adaln_mod(c, w_ada, b_ada):
    bsz, d = c.shape
    n = w_ada.shape[1]
    tn = 512
    return pl.pallas_call(
        _adaln_kernel,
        out_shape=jax.ShapeDtypeStruct((bsz, n), F32),
        grid=(n // tn,),
        in_specs=[
            pl.BlockSpec((bsz, d), lambda j: (0, 0)),
            pl.BlockSpec((d, tn), lambda j: (0, j)),
            pl.BlockSpec((1, tn), lambda j: (0, j)),
        ],
        out_specs=pl.BlockSpec((bsz, tn), lambda j: (0, j)),
        compiler_params=pltpu.CompilerParams(dimension_semantics=("arbitrary",)),
        name="adaln_mod",
    )(c, w_ada, b_ada.reshape(1, n))


def _in_proj_kernel(x_ref, mod_ref, gain_ref, w_ref, wg_ref, proj_ref, gates_ref, h_ref, *, rows):
    tm, d = x_ref.shape
    n = pl.program_id(1)

    @pl.when(n == 0)
    def _():
        shift = mod_ref[0, :, 0:d]
        scale = mod_ref[0, :, d:2 * d]
        gain = gain_ref[...]

        def normalize(r):
            xf = x_ref[r:r + rows, :]
            y = xf * lax.rsqrt(jnp.mean(xf * xf, axis=-1, keepdims=True) + EPS) * gain
            h_ref[r:r + rows, :] = (y * (1.0 + scale) + shift).astype(BF16)

        def project(r):
            hb = h_ref[r:r + rows, :]
            proj_ref[r:r + rows, :] = jnp.dot(hb, w_ref[...], preferred_element_type=F32).astype(BF16)
            gates_ref[r:r + rows, :] = jnp.dot(hb, wg_ref[...], preferred_element_type=F32)

        starts = list(range(0, tm, rows))
        normalize(starts[0])
        for i, r in enumerate(starts):
            if i + 1 < len(starts):
                normalize(starts[i + 1])
            project(r)

    @pl.when(n > 0)
    def _():
        proj_ref[...] = jnp.dot(h_ref[...], w_ref[...], preferred_element_type=F32).astype(BF16)


def _in_proj(x2, mod3, norm_gain, w_main, w_gate, seq, tm):
    n_tok, d = x2.shape
    tn = N_MAIN // 6
    return pl.pallas_call(
        functools.partial(_in_proj_kernel, rows=min(MERGE_ROWS, tm)),
        out_shape=(
            jax.ShapeDtypeStruct((n_tok, N_MAIN), BF16),
            jax.ShapeDtypeStruct((n_tok, GATE_LANES), F32),
        ),
        grid=(n_tok // tm, N_MAIN // tn),
        in_specs=[
            pl.BlockSpec((tm, d), lambda m, n: (m, 0)),
            pl.BlockSpec((1, 1, 3 * d), lambda m, n: ((m * tm) // seq, 0, 0)),
            pl.BlockSpec((1, d), lambda m, n: (0, 0)),
            pl.BlockSpec((d, tn), lambda m, n: (0, n)),
            pl.BlockSpec((d, GATE_LANES), lambda m, n: (0, 0)),
        ],
        out_specs=(
            pl.BlockSpec((tm, tn), lambda m, n: (m, n)),
            pl.BlockSpec((tm, GATE_LANES), lambda m, n: (m, 0)),
        ),
        scratch_shapes=[pltpu.VMEM((tm, d), BF16)],
        compiler_params=pltpu.CompilerParams(
            dimension_semantics=("arbitrary", "arbitrary"), vmem_limit_bytes=VMEM_LIMIT_BYTES),
        name="in_proj",
    )(x2, mod3, norm_gain.reshape(1, d), w_main, w_gate)


def _gate_prep_kernel(g_ref, bias_ref, col_ref, row_ref):
    nc = row_ref.shape[1]
    v = (g_ref[0] + bias_ref[...]).T
    i_gate, f_gate = v[0:8, :] * LOG2E, v[8:16, :]
    seq = v.shape[1]
    pos = lax.broadcasted_iota(jnp.int32, i_gate.shape, 1) % CHUNK
    is_bwd = lax.broadcasted_iota(jnp.int32, i_gate.shape, 0) >= MLSTM_HEADS

    def scan(x, combine, identity):
        pre, suf = x, x
        step = 1
        while step < CHUNK:
            pre = combine(pre, jnp.where(pos >= step, pltpu.roll(pre, step, 1), identity))
            suf = combine(suf, jnp.where(pos < CHUNK - step, pltpu.roll(suf, seq - step, 1), identity))
            step *= 2
        return jnp.where(is_bwd, suf, pre)

    b = scan((jnp.minimum(f_gate, 0.0) - jnp.log1p(jnp.exp(-jnp.abs(f_gate)))) * LOG2E, jnp.add, 0.0)
    r = i_gate - b
    cm = scan(r, jnp.maximum, NEG)
    col_ref[0] = jnp.concatenate([cm, b, jnp.zeros((GATE_LANES - 16, seq), F32)], axis=0).T
    for c in range(nc):
        row_ref[0, c] = r[:, c * CHUNK:(c + 1) * CHUNK]


def _gate_prep(gates3, bias_row):
    bsz, seq, _ = gates3.shape
    nc = seq // CHUNK
    rows = 2 * MLSTM_HEADS
    return pl.pallas_call(
        _gate_prep_kernel,
        out_shape=(
            jax.ShapeDtypeStruct((bsz, seq, GATE_LANES), F32),
            jax.ShapeDtypeStruct((bsz, nc, rows, CHUNK), F32),
        ),
        grid=(bsz,),
        in_specs=[
            pl.BlockSpec((1, seq, GATE_LANES), lambda b: (b, 0, 0)),
            pl.BlockSpec((1, GATE_LANES), lambda b: (0, 0)),
        ],
        out_specs=(
            pl.BlockSpec((1, seq, GATE_LANES), lambda b: (b, 0, 0)),
            pl.BlockSpec((1, nc, rows, CHUNK), lambda b: (b, 0, 0, 0)),
        ),
        compiler_params=pltpu.CompilerParams(dimension_semantics=("arbitrary",)),
        name="gate_prep",
    )(gates3, bias_row)


def _mlstm_chunk(qc, kt, qk_ref, reuse_qk, vv, r_row, cm_rep, b_rep, cm_end, g, cn_ref, m_ref, causal_mask):
    d = vv.shape[1]
    cn = cn_ref[...]
    m_state = m_ref[...]
    m_col = jnp.maximum(m_state, cm_rep)
    w_inter = jnp.exp2(m_state - m_col)
    m_wide = jnp.concatenate([m_col, m_col], axis=1)
    d_mat = jnp.where(causal_mask, jnp.exp2(r_row - m_wide), 0.0)
    if reuse_qk:
        qk = qk_ref[...]
    else:
        qk = jnp.dot(qc, kt, preferred_element_type=F32)
        qk_ref[...] = qk
    scores = qk * d_mat
    inter = jnp.dot(qc, cn.astype(BF16), preferred_element_type=F32)
    yield None
    w_wide = jnp.concatenate([w_inter, w_inter], axis=1)
    num = jnp.dot(scores.astype(BF16), vv, preferred_element_type=F32) + w_wide * inter[:, 0:d]
    den = jnp.sum(scores, axis=-1, keepdims=True) + w_inter * inter[:, d:]
    inv = 1.0 / jnp.maximum(jnp.abs(den), jnp.exp2(-(b_rep + m_col)))
    yield num * jnp.concatenate([inv, inv], axis=1)
    m_end = jnp.maximum(m_state, cm_end)
    w_row = jnp.exp2(r_row - jnp.concatenate([m_end, m_end], axis=1))
    decay = jnp.exp2(m_state - m_end)
    kw = kt * w_row.astype(BF16)
    v_aug = jnp.concatenate([vv, jnp.ones((vv.shape[0], LANES), BF16)], axis=1)
    decay_wide = jnp.concatenate([decay, decay, decay], axis=1)
    cn_ref[...] = decay_wide * cn + jnp.dot(kw, v_aug, preferred_element_type=F32)
    m_ref[...] = g + m_end
    yield None


def _mlstm_kernel(q_ref, k_ref, v_ref, o_ref, z_ref, gcol_ref, grow_ref, cwq_ref, cwk_ref, cbq_ref, cbk_ref,
                  gain_ref, y_ref, qc_ref, kt_ref, qk_ref, h_ref, cnf_ref, mf_ref, cnb_ref, mb_ref):
    seq, d = q_ref.shape[1], q_ref.shape[2]
    nc = seq // CHUNK
    head = pl.program_id(1)

    t_idx = lax.broadcasted_iota(jnp.int32, (CHUNK, CHUNK), 0)
    s_idx = lax.broadcasted_iota(jnp.int32, (CHUNK, CHUNK), 1)
    taps = jnp.concatenate([(s_idx == t_idx - 1).astype(BF16), (s_idx == t_idx + 1).astype(BF16)],
                           axis=0)
    edge = lax.broadcasted_iota(jnp.int32, (8, d), 0)

    def conv_silu(src_ref, w_ref, b_ref, c, r):
        w0, w1, w2 = 0.5 * w_ref[0:1, :], 0.5 * w_ref[1:2, :], 0.5 * w_ref[2:3, :]
        src = src_ref[0, pl.ds(r, CHUNK), :]
        p = jnp.dot(taps, src, preferred_element_type=F32)
        t = p[0:CHUNK] * w0 + src.astype(F32) * w1 + p[CHUNK:2 * CHUNK] * w2 + 0.5 * b_ref[...]
        above = src_ref[0, pl.ds(pl.multiple_of(jnp.maximum(r - 16, 0), 16), 16), :][15:16, :].astype(F32)
        below = src_ref[0, pl.ds(pl.multiple_of(jnp.minimum(r + CHUNK, seq - 16), 16), 16), :][0:1, :].astype(F32)
        above = above * jnp.where(c > 0, w0, 0.0)
        below = below * jnp.where(c < nc - 1, w2, 0.0)
        t = jnp.concatenate([t[0:8] + jnp.where(edge == 0, above, 0.0), t[8:CHUNK - 8],
                             t[CHUNK - 8:CHUNK] + jnp.where(edge == 7, below, 0.0)], axis=0)
        return t * (1.0 + jnp.tanh(t))

    def conv_chunk(c, carry):
        r = pl.multiple_of(c * CHUNK, CHUNK)
        qc_ref[pl.ds(r, CHUNK), :] = conv_silu(q_ref, cwq_ref, cbq_ref, c, r).astype(BF16)
        kt_ref[c] = conv_silu(k_ref, cwk_ref, cbk_ref, c, r).astype(BF16).T
        return carry

    lax.fori_loop(0, nc, conv_chunk, 0, unroll=8)

    gate_row = lax.broadcasted_iota(jnp.int32, (2 * MLSTM_HEADS, CHUNK), 0)
    gate_lane = lax.broadcasted_iota(jnp.int32, (CHUNK, GATE_LANES), 1)

    for ref in (cnf_ref, mf_ref, cnb_ref, mb_ref):
        ref[...] = jnp.zeros(ref.shape, F32)

    mask_f = s_idx <= t_idx
    mask_b = s_idx >= t_idx

    def replicate(cols, lane):
        picked = jnp.sum(jnp.where(gate_lane == lane, cols, 0.0), axis=1, keepdims=True)
        return jnp.broadcast_to(picked, (cols.shape[0], LANES))

    def direction(chunk, backward, reuse_qk):
        r = pl.multiple_of(chunk * CHUNK, CHUNK)
        rows = grow_ref[0, chunk]
        cols = gcol_ref[0, pl.ds(r, CHUNK), :]
        qc = qc_ref[pl.ds(r, CHUNK), :]
        kt = kt_ref[chunk]
        vv = v_ref[0, pl.ds(r, CHUNK), :]
        if backward:
            cm_rep, b_rep = replicate(cols, MLSTM_HEADS + head), replicate(cols, 3 * MLSTM_HEADS + head)
            end = 0
            state, mask, this_row = (cnb_ref, mb_ref), mask_b, MLSTM_HEADS + head
        else:
            cm_rep, b_rep = replicate(cols, head), replicate(cols, 2 * MLSTM_HEADS + head)
            end = CHUNK - 1
            state, mask, this_row = (cnf_ref, mf_ref), mask_f, head
        r_row = jnp.sum(jnp.where(gate_row == this_row, rows, 0.0), axis=0, keepdims=True) + K_SCALE_LOG2
        cm_end, g = cm_rep[end:end + 1, :], b_rep[end:end + 1, :]
        return _mlstm_chunk(qc, kt, qk_ref.at[chunk], reuse_qk, vv, r_row, cm_rep, b_rep, cm_end, g, *state, mask), r

    def chunk_steps(j, second_visit):
        steps = [direction(j, False, second_visit), direction(nc - 1 - j, True, second_visit)]
        if second_visit:
            for gen, _ in steps:
                next(gen)
            hs = [next(gen) for gen, _ in steps]
            for gen, _ in steps:
                next(gen)
        else:
            hs = []
            for gen, _ in steps:
                next(gen)
                hs.append(next(gen))
                next(gen)
        return [(h, r) for h, (_, r) in zip(hs, steps)]

    def first_half(j, carry):
        for h_dir, r in chunk_steps(j, second_visit=False):
            h_ref[pl.ds(r, CHUNK), :] = h_dir
        return carry

    gain = gain_ref[...]

    def finish(h_dir, r):
        h_sum = (h_ref[pl.ds(r, CHUNK), :] + h_dir) * _sigmoid(o_ref[0, pl.ds(r, CHUNK), :].astype(F32))
        normed = h_sum * lax.rsqrt(jnp.mean(h_sum * h_sum, axis=-1, keepdims=True) + EPS) * gain
        y_ref[0, pl.ds(r, CHUNK), :] = (normed * _silu(z_ref[0, pl.ds(r, CHUNK), :].astype(F32))).astype(BF16)

    def second_half(j, carry):
        for h_dir, r in chunk_steps(j, second_visit=True):
            finish(h_dir, r)
        return carry

    lax.fori_loop(0, nc // 2, first_half, 0, unroll=2)
    lax.fori_loop(nc // 2, nc, second_half, 0)


def _mlstm(proj3, gcol, grow, conv_w, conv_b, head_gain):
    bsz, seq, _ = proj3.shape
    d = MLSTM_HEAD_DIM
    nh = MLSTM_HEADS
    nc = seq // CHUNK

    def col_block(offset):
        return pl.BlockSpec((1, seq, d), lambda b, h: (b, 0, offset + h))

    return pl.pallas_call(
        _mlstm_kernel,
        out_shape=jax.ShapeDtypeStruct((bsz, seq, nh * d), BF16),
        grid=(bsz, nh),
        in_specs=[
            col_block(0), col_block(nh), col_block(2 * nh), col_block(3 * nh), col_block(4 * nh),
            pl.BlockSpec((1, seq, GATE_LANES), lambda b, h: (b, 0, 0)),
            pl.BlockSpec((1, nc, 2 * MLSTM_HEADS, CHUNK), lambda b, h: (b, 0, 0, 0)),
            pl.BlockSpec((3, d), lambda b, h: (0, h)),
            pl.BlockSpec((3, d), lambda b, h: (0, nh + h)),
            pl.BlockSpec((1, d), lambda b, h: (0, h)),
            pl.BlockSpec((1, d), lambda b, h: (0, nh + h)),
            pl.BlockSpec((1, d), lambda b, h: (0, h)),
        ],
        out_specs=pl.BlockSpec((1, seq, d), lambda b, h: (b, 0, h)),
        scratch_shapes=[
            pltpu.VMEM((seq, d), BF16),
            pltpu.VMEM((nc, d, CHUNK), BF16),
            pltpu.VMEM((nc, CHUNK, CHUNK), F32),
            pltpu.VMEM((seq, d), F32),
            pltpu.VMEM((d, d + LANES), F32), pltpu.VMEM((1, LANES), F32),
            pltpu.VMEM((d, d + LANES), F32), pltpu.VMEM((1, LANES), F32),
        ],
        compiler_params=pltpu.CompilerParams(
            dimension_semantics=("arbitrary", "arbitrary"), vmem_limit_bytes=VMEM_LIMIT_BYTES),
        name="mlstm",
    )(proj3, proj3, proj3, proj3, proj3, gcol, grow, conv_w, conv_w,
      conv_b.reshape(1, -1), conv_b.reshape(1, -1), head_gain.reshape(1, -1))


def _natten_bias_rows(rpb):
    n_dc = 2 * NA_COLS - 1
    n_dr = 2 * NA_ROWS - 1
    left = NA_COLS
    period = 64
    padded = jnp.pad(rpb.astype(F32) * LOG2E, ((0, 0), (0, 0), (left, period - left - n_dc)), constant_values=NEG)
    out = []
    for (c0, w), kb in zip(Q_BLOCKS, K_STARTS):
        base = kb - c0 + NA_COLS - 1 + left
        toe = jnp.tile(padded, (1, 1, w))[..., :(period - 1) * w].reshape(NA_HEADS, n_dr, w, period - 1)
        toe = toe[..., base:base + K_WIDTH]
        q_col = c0 + np.arange(w)[:, None]
        k_col = kb + np.arange(K_WIDTH)[None, :]
        win = np.clip(q_col - NA_COLS // 2, 0, GRID_W - NA_COLS)
        col_ok = jnp.asarray((k_col >= win) & (k_col < win + NA_COLS))
        toe = jnp.where(col_ok[None, None], toe, NEG).transpose(0, 2, 1, 3).reshape(NA_HEADS, w, n_dr * K_WIDTH)
        out.append(jnp.pad(toe, ((0, 0), (0, Q_BLOCK_MAX - w), (0, N_KEYS - n_dr * K_WIDTH)), constant_values=NEG))
    return jnp.stack(out, axis=1)


_GROUP_KINDS = (
    (0, lambda i: max(i - NA_ROWS // 2, 0)),
    (-(NA_ROWS // 2), lambda i: i),
    (Q_ROWS - SLAB_ROWS, lambda i: min(i + NA_ROWS // 2, NA_ROWS)),
)


def _build_bias(e_ref, tab_ref, slab_off, first_valid):
    for h in range(NA_HEADS):
        row_off = 0
        for blk, (_, w) in enumerate(Q_BLOCKS):
            e = e_ref[h, blk, 0:w, :]
            lane = lax.broadcasted_iota(jnp.int32, (w, N_KEYS), 1)
            for i in range(Q_ROWS):
                shift = slab_off - i + NA_ROWS - 1
                amount = (-shift * K_WIDTH) % N_KEYS
                moved = e if amount == 0 else pltpu.roll(e, amount, 1)
                lo = first_valid(i) * K_WIDTH
                ok = (lane >= lo) & (lane < lo + NA_ROWS * K_WIDTH)
                tab_ref[h, row_off + i * w:row_off + (i + 1) * w, :] = jnp.where(ok, moved, NEG)
            row_off += Q_ROWS * w


def _natten_kernel(q_ref, k0_ref, k1_ref, k2_ref, k3_ref, v0_ref, v1_ref, v2_ref, v3_ref, z_ref, e_ref,
                   y_ref, qf_ref, of_ref, tab_ref, sa_ref, sb_ref):
    k_refs = (k0_ref, k1_ref, k2_ref, k3_ref)
    v_refs = (v0_ref, v1_ref, v2_ref, v3_ref)
    piece_rows = SLAB_ROWS // 4
    group = pl.program_id(0)
    n_groups = pl.num_programs(0)

    first_batch = pl.program_id(1) == 0
    conditions = (group == 0, jnp.logical_and(group > 0, group < n_groups - 1), group == n_groups - 1)
    for cond, (slab_off, first_valid) in zip(conditions, _GROUP_KINDS):
        @pl.when(jnp.logical_and(first_batch, cond))
        def _(slab_off=slab_off, first_valid=first_valid):
            _build_bias(e_ref, tab_ref, slab_off, first_valid)

    n_batch, q_tok = q_ref.shape[0], q_ref.shape[2]
    for bi in range(n_batch):
        qf_ref[pl.ds(bi * q_tok, q_tok), :] = q_ref[bi, 0].astype(F32) * (NA_HEAD_DIM ** -0.5 * LOG2E)

    items = []
    for bi in range(n_batch):
        row_off = 0
        for (c0, w), kb in zip(Q_BLOCKS, K_STARTS):
            for pair in range(NA_HEADS // 2):
                items.append((bi, c0, w, kb, pair, row_off))
            row_off += Q_ROWS * w
    s_refs = (sa_ref, sb_ref)

    def slab(refs, bi, kb, lanes):
        return jnp.concatenate(
            [refs[a // piece_rows][bi, 0, pl.ds((a % piece_rows) * GRID_W + kb, K_WIDTH), lanes]
             for a in range(SLAB_ROWS)], axis=0)

    def scores(idx):
        bi, c0, w, kb, pair, row_off = items[idx]
        nq = Q_ROWS * w
        lanes = pl.ds(pair * LANES, LANES)
        first_head = lax.broadcasted_iota(jnp.int32, (nq, LANES), 1) < NA_HEAD_DIM
        qp = jnp.concatenate([qf_ref[pl.ds(bi * q_tok + i * GRID_W + c0, w), lanes] for i in range(Q_ROWS)],
                             axis=0)
        qs = jnp.concatenate([jnp.where(first_head, qp, 0.0), jnp.where(first_head, 0.0, qp)],
                             axis=0).astype(BF16)
        s = lax.dot_general(qs, slab(k_refs, bi, kb, lanes), (((1,), (1,)), ((), ())),
                            preferred_element_type=F32)
        bias = jnp.concatenate([tab_ref[2 * pair, pl.ds(row_off, nq), :],
                                tab_ref[2 * pair + 1, pl.ds(row_off, nq), :]], axis=0)
        s_refs[idx % 2][0:2 * nq, :] = s + bias

    def attend(idx):
        bi, c0, w, kb, pair, _ = items[idx]
        nq = Q_ROWS * w
        lanes = pl.ds(pair * LANES, LANES)
        first_head = lax.broadcasted_iota(jnp.int32, (nq, LANES), 1) < NA_HEAD_DIM
        s = s_refs[idx % 2][0:2 * nq, :]
        p = jnp.exp2(s - jnp.max(s, axis=-1, keepdims=True))
        o2 = jnp.dot(p.astype(BF16), slab(v_refs, bi, kb, lanes), preferred_element_type=F32)
        o2 = o2 * (1.0 / jnp.sum(p, axis=-1, keepdims=True))
        o = jnp.where(first_head, o2[0:nq], o2[nq:2 * nq])
        for i in range(Q_ROWS):
            of_ref[pl.ds(bi * q_tok + i * GRID_W + c0, w), lanes] = o[i * w:(i + 1) * w]

    scores(0)
    for idx in range(len(items)):
        if idx + 1 < len(items):
            scores(idx + 1)
        attend(idx)
    for bi in range(n_batch):
        rows = pl.ds(bi * q_tok, q_tok)
        y_ref[bi, 0] = (of_ref[rows, :] * _silu(z_ref[bi, 0].astype(F32))).astype(BF16)


def _natten(proj3, bias_rows):
    bsz, seq, n_main = proj3.shape
    q_tok = Q_ROWS * GRID_W
    p_tok = (SLAB_ROWS // 4) * GRID_W
    groups = seq // q_tok
    pieces = seq // p_tok
    assert groups >= 2, "needs at least two 8-row query groups"
    nb = 2 if bsz % 2 == 0 else 1
    w = NA_WIDTH
    qb, kb, vb, zb = (5 * D_MODEL) // w, (5 * D_MODEL) // w + 1, (5 * D_MODEL) // w + 2, (5 * D_MODEL) // w + 3
    proj_q = proj3.reshape(bsz, groups, q_tok, n_main)
    proj_p = proj3.reshape(bsz, pieces, p_tok, n_main)

    def slab_start(g):
        return jnp.clip(2 * g - 1, 0, pieces - 4)

    def piece_spec(col, a):
        return pl.BlockSpec((nb, 1, p_tok, w), lambda g, b: (b, slab_start(g) + a, 0, col))

    return pl.pallas_call(
        _natten_kernel,
        out_shape=jax.ShapeDtypeStruct((bsz, groups, q_tok, w), BF16),
        grid=(groups, bsz // nb),
        in_specs=[pl.BlockSpec((nb, 1, q_tok, w), lambda g, b: (b, g, 0, qb))]
        + [piece_spec(kb, a) for a in range(4)] + [piece_spec(vb, a) for a in range(4)]
        + [pl.BlockSpec((nb, 1, q_tok, w), lambda g, b: (b, g, 0, zb)),
           pl.BlockSpec(bias_rows.shape, lambda g, b: (0, 0, 0, 0))],
        out_specs=pl.BlockSpec((nb, 1, q_tok, w), lambda g, b: (b, g, 0, 0)),
        scratch_shapes=[pltpu.VMEM((nb * q_tok, w), F32), pltpu.VMEM((nb * q_tok, w), F32),
                        pltpu.VMEM((NA_HEADS, q_tok, N_KEYS), F32),
                        pltpu.VMEM((2 * Q_ROWS * Q_BLOCK_MAX, N_KEYS), F32),
                        pltpu.VMEM((2 * Q_ROWS * Q_BLOCK_MAX, N_KEYS), F32)],
        compiler_params=pltpu.CompilerParams(
            dimension_semantics=("arbitrary", "arbitrary"), vmem_limit_bytes=VMEM_LIMIT_BYTES),
        name="natten",
    )(proj_q, *([proj_p] * 8), proj_q, bias_rows).reshape(bsz, seq, w)


def _merge_out_kernel(ya_ref, yb_ref, g0_ref, g1_ref, x_ref, mod_ref, bm_ref, wa_ref, wb_ref, wo_ref, fg_ref,
                      out_ref, m_ref):
    tm, d = x_ref.shape
    sub = min(MERGE_ROWS, tm)
    gate = mod_ref[0, :, 2 * d:3 * d]

    def merge(r):
        rows = pl.ds(r, sub)
        pa = jnp.dot(ya_ref[rows, :], wa_ref[...], preferred_element_type=F32)
        pb = jnp.dot(yb_ref[rows, :], wb_ref[...], preferred_element_type=F32)
        m_ref[rows, :] = (_sigmoid(g0_ref[rows, :].astype(F32) + bm_ref[0:1, :]) * pa
                          + _sigmoid(g1_ref[rows, :].astype(F32) + bm_ref[1:2, :]) * pb).astype(BF16)

    def project(r):
        rows = pl.ds(r, sub)
        mixed = jnp.dot(m_ref[rows, :], wo_ref[...], preferred_element_type=F32)
        xo = x_ref[rows, :] + gate * mixed
        out_ref[rows, :] = xo * lax.rsqrt(jnp.mean(xo * xo, axis=-1, keepdims=True) + EPS) * fg_ref[...]

    starts = list(range(0, tm, sub))
    merge(starts[0])
    for i, r in enumerate(starts):
        if i + 1 < len(starts):
            merge(starts[i + 1])
        project(r)


def _merge_out(ya2, yb2, proj2, x2, mod3, b_merge, wa, wb, wo, final_gain, seq, tm):
    n_tok, d = x2.shape
    g0 = (N_MAIN - 2 * d) // d

    def full(shape):
        return pl.BlockSpec(shape, lambda m: (0,) * len(shape))

    return pl.pallas_call(
        _merge_out_kernel,
        out_shape=jax.ShapeDtypeStruct((n_tok, d), F32),
        grid=(n_tok // tm,),
        in_specs=[
            pl.BlockSpec((tm, d), lambda m: (m, 0)),
            pl.BlockSpec((tm, NA_WIDTH), lambda m: (m, 0)),
            pl.BlockSpec((tm, d), lambda m: (m, g0)),
            pl.BlockSpec((tm, d), lambda m: (m, g0 + 1)),
            pl.BlockSpec((tm, d), lambda m: (m, 0)),
            pl.BlockSpec((1, 1, 3 * d), lambda m: ((m * tm) // seq, 0, 0)),
            full((2, d)), full((d, d)), full((NA_WIDTH, d)), full((d, d)), full((1, d)),
        ],
        out_specs=pl.BlockSpec((tm, d), lambda m: (m, 0)),
        scratch_shapes=[pltpu.VMEM((tm, d), BF16)],
        compiler_params=pltpu.CompilerParams(
            dimension_semantics=("arbitrary",), vmem_limit_bytes=VMEM_LIMIT_BYTES),
        name="merge_out",
    )(ya2, yb2, proj2, proj2, x2, mod3, b_merge, wa, wb, wo, final_gain.reshape(1, d))


def _gate_layout(n_cols):
    src = np.arange(n_cols)
    direction, gate, head = src // (2 * MLSTM_HEADS), (src // MLSTM_HEADS) % 2, src % MLSTM_HEADS
    return 2 * MLSTM_HEADS * gate + MLSTM_HEADS * direction + head


def _weight_prep_kernel(a_ref, b_ref, perm_ref, main_ref, gate_ref, *, gate_block, n_gates):
    n = pl.program_id(0)
    rows = a_ref[0]
    past_gates = jnp.concatenate([rows[n_gates:], b_ref[0]], axis=0)
    main_ref[...] = jnp.where(n >= gate_block, past_gates, rows).T.astype(BF16)

    @pl.when(n == gate_block)
    def _():
        gate_ref[...] = jnp.dot(perm_ref[...], rows[0:n_gates], preferred_element_type=F32).T.astype(BF16)


def _weight_prep(w_in, layer):
    _, d, n_in = w_in.shape
    n_gates = 4 * MLSTM_HEADS
    tn = D_MODEL
    g_lo = 5 * MLSTM_HEADS * MLSTM_HEAD_DIM
    assert n_in == N_MAIN + n_gates and g_lo % tn == 0 and tn % n_gates == 0
    perm = np.zeros((GATE_LANES, n_gates), np.float32)
    perm[_gate_layout(n_gates), np.arange(n_gates)] = 1.0
    w_t = jnp.swapaxes(w_in, 1, 2)
    return pl.pallas_call(
        functools.partial(_weight_prep_kernel, gate_block=g_lo // tn, n_gates=n_gates),
        out_shape=(jax.ShapeDtypeStruct((d, N_MAIN), BF16), jax.ShapeDtypeStruct((d, GATE_LANES), BF16)),
        grid=(N_MAIN // tn,),
        in_specs=[pl.BlockSpec((1, tn, d), lambda n: (layer, n, 0)),
                  pl.BlockSpec((1, n_gates, d), lambda n: (layer, (n + 1) * (tn // n_gates), 0)),
                  pl.BlockSpec((GATE_LANES, n_gates), lambda n: (0, 0))],
        out_specs=(pl.BlockSpec((d, tn), lambda n: (0, n)),
                   pl.BlockSpec((d, GATE_LANES), lambda n: (0, 0))),
        compiler_params=pltpu.CompilerParams(dimension_semantics=("arbitrary",), vmem_limit_bytes=VMEM_LIMIT_BYTES),
        name="weight_prep",
    )(w_t, w_t, jnp.asarray(perm))


def _token_tiles(seq):
    return min(2048, seq), min(1024, seq)


def _layer(x, mod, norm_gain, w_in, layer, conv_w, conv_b, b_igate, b_fgate, mlstm_norm_gain, rpb, w_proj_a, w_proj_b,
           b_merge, w_out, final_gain):
    bsz, seq, d = x.shape
    n_tok = bsz * seq
    wa = MLSTM_HEADS * MLSTM_HEAD_DIM
    w_main, w_gate = _weight_prep(w_in, layer)
    gate_bias = jnp.zeros((1, GATE_LANES), F32).at[0, _gate_layout(4 * MLSTM_HEADS)].set(
        jnp.stack([b_igate, b_fgate], axis=1).reshape(-1))
    tm_in, tm_out = _token_tiles(seq)

    x2 = x.reshape(n_tok, d)
    mod3 = mod.reshape(bsz, 1, 3 * d)
    proj2, gates2 = _in_proj(x2, mod3, norm_gain, w_main, w_gate, seq, tm_in)
    proj3 = proj2.reshape(bsz, seq, N_MAIN)
    gcol, grow = _gate_prep(gates2.reshape(bsz, seq, GATE_LANES), gate_bias)
    y_a = _mlstm(proj3, gcol, grow, conv_w, conv_b, mlstm_norm_gain)
    y_b = _natten(proj3, _natten_bias_rows(rpb))
    out2 = _merge_out(y_a.reshape(n_tok, wa), y_b.reshape(n_tok, NA_WIDTH), proj2, x2, mod3, b_merge,
                      w_proj_a.astype(BF16), w_proj_b.astype(BF16), w_out.astype(BF16), final_gain, seq, tm_out)
    return out2.reshape(bsz, seq, d)


def kernel(x, c, w_ada, b_ada, norm_gain, w_in, conv_w, conv_b, b_igate, b_fgate, mlstm_norm_gain, rpb, w_proj_a,
           w_proj_b, b_merge, w_out, final_gain):
    depth = w_ada.shape[0]
    assert depth == 1, "the fused final norm assumes a single layer"
    mod = _adaln_mod(c, w_ada[0], b_ada[0])
    return _layer(x, mod, norm_gain[0], w_in, 0, conv_w[0], conv_b[0], b_igate[0], b_fgate[0], mlstm_norm_gain[0],
                  rpb[0], w_proj_a[0], w_proj_b[0], b_merge[0], w_out[0], final_gain)
```

```python
import functools
import math

import numpy as np
import jax
import jax.numpy as jnp
from jax import lax
from jax.experimental import pallas as pl
from jax.experimental.pallas import tpu as pltpu

F32 = jnp.float32
BF16 = jnp.bfloat16

D_MODEL = 1024
GRID_W = 64
MLSTM_HEADS = 4
MLSTM_HEAD_DIM = 256
NA_HEADS = 8
NA_WIDTH = 512
NA_HEAD_DIM = 64
NA_ROWS = 8
NA_COLS = 16
EPS = 1e-6
NEG = -1e30
LOG2E = 1.4426950408889634
K_SCALE_LOG2 = -0.5 * math.log2(MLSTM_HEAD_DIM)

VMEM_LIMIT_BYTES = 56 * 1024 * 1024
LANES = 128

CHUNK = 256

MERGE_ROWS = 512

N_MAIN = 9 * D_MODEL
GATE_LANES = 128

Q_ROWS = 8
SLAB_ROWS = 16
Q_BLOCKS = ((0, 24), (24, 16), (40, 24))
Q_BLOCK_MAX = 24
K_STARTS = (0, 16, 32)
K_WIDTH = 32
N_KEYS = SLAB_ROWS * K_WIDTH


def _sigmoid(x):
    return 1.0 / (1.0 + jnp.exp2(x * -LOG2E))


def _silu(x):
    return x * _sigmoid(x)


def _adaln_kernel(c_ref, w_ref, b_ref, o_ref):
    cond = _silu(c_ref[...]).astype(BF16)
    o_ref[...] = jnp.dot(cond, w_ref[...].astype(BF16), preferred_element_type=F32) + b_ref[...]


def _adaln_mod(c, w_ada, b_ada):
    bsz, d = c.shape
    n = w_ada.shape[1]
    tn = 512
    return pl.pallas_call(
        _adaln_kernel,
        out_shape=jax.ShapeDtypeStruct((bsz, n), F32),
        grid=(n // tn,),
        in_specs=[
            pl.BlockSpec((bsz, d), lambda j: (0, 0)),
            pl.BlockSpec((d, tn), lambda j: (0, j)),
            pl.BlockSpec((1, tn), lambda j: (0, j)),
        ],
        out_specs=pl.BlockSpec((bsz, tn), lambda j: (0, j)),
        compiler_params=pltpu.CompilerParams(dimension_semantics=("arbitrary",)),
        name="adaln_mod",
    )(c, w_ada, b_ada.reshape(1, n))


def _gate_quantities(gates, pos_in_chunk):
    v = gates.T
    i_gate, f_gate = v[0:8, :] * LOG2E, v[8:16, :]
    n_tok = v.shape[1]
    is_bwd = lax.broadcasted_iota(jnp.int32, i_gate.shape, 0) >= MLSTM_HEADS

    def scan(x, combine, identity):
        pre, suf = x, x
        step = 1
        while step < CHUNK:
            pre = combine(pre, jnp.where(pos_in_chunk >= step, pltpu.roll(pre, step, 1), identity))
            suf = combine(suf, jnp.where(pos_in_chunk < CHUNK - step, pltpu.roll(suf, n_tok - step, 1), identity))
            step *= 2
        return jnp.where(is_bwd, suf, pre)

    b = scan((jnp.minimum(f_gate, 0.0) - jnp.log1p(jnp.exp(-jnp.abs(f_gate)))) * LOG2E, jnp.add, 0.0)
    r = i_gate - b
    cm = scan(r, jnp.maximum, NEG)
    cols = jnp.concatenate([cm, b, jnp.zeros((GATE_LANES - 16, n_tok), F32)], axis=0).T
    return cols, r


def _in_proj_kernel(x_ref, mod_ref, gain_ref, w_ref, wg_ref, gbias_ref, proj_ref, gcol_ref, grow_ref, h_ref, *, rows):
    tm, d = x_ref.shape
    n = pl.program_id(1)

    @pl.when(n == 0)
    def _():
        shift = mod_ref[0, :, 0:d]
        scale = mod_ref[0, :, d:2 * d]
        gain = gain_ref[...]
        pos = lax.broadcasted_iota(jnp.int32, (8, rows), 1) % CHUNK

        def normalize(r):
            xf = x_ref[r:r + rows, :]
            y = xf * lax.rsqrt(jnp.mean(xf * xf, axis=-1, keepdims=True) + EPS) * gain
            h_ref[r:r + rows, :] = (y * (1.0 + scale) + shift).astype(BF16)

        def project(r):
            hb = h_ref[r:r + rows, :]
            gates = jnp.dot(hb, wg_ref[...], preferred_element_type=F32) + gbias_ref[...]
            proj_ref[r:r + rows, :] = jnp.dot(hb, w_ref[...], preferred_element_type=F32).astype(BF16)
            return gates

        def gate_scans(r, gates):
            cols, r_rows = _gate_quantities(gates, pos)
            gcol_ref[r:r + rows, :] = cols
            for c in range(rows // CHUNK):
                grow_ref[0, r // CHUNK + c] = r_rows[:, c * CHUNK:(c + 1) * CHUNK]

        starts = list(range(0, tm, rows))
        normalize(starts[0])
        pending = None
        for i, r in enumerate(starts):
            if i + 1 < len(starts):
                normalize(starts[i + 1])
            gates = project(r)
            if pending is not None:
                gate_scans(*pending)
            pending = (r, gates)
        gate_scans(*pending)

    @pl.when(n > 0)
    def _():
        proj_ref[...] = jnp.dot(h_ref[...], w_ref[...], preferred_element_type=F32).astype(BF16)


def _in_proj(x2, mod3, norm_gain, w_main, w_gate, gate_bias, seq, tm):
    n_tok, d = x2.shape
    tn = N_MAIN // 6
    rows = min(MERGE_ROWS, tm)
    assert rows % CHUNK == 0 and seq % tm == 0
    tiles_per_seq = seq // tm
    return pl.pallas_call(
        functools.partial(_in_proj_kernel, rows=rows),
        out_shape=(
            jax.ShapeDtypeStruct((n_tok, N_MAIN), BF16),
            jax.ShapeDtypeStruct((n_tok, GATE_LANES), F32),
            jax.ShapeDtypeStruct((n_tok // seq, seq // CHUNK, 2 * MLSTM_HEADS, CHUNK), F32),
        ),
        grid=(n_tok // tm, N_MAIN // tn),
        in_specs=[
            pl.BlockSpec((tm, d), lambda m, n: (m, 0)),
            pl.BlockSpec((1, 1, 3 * d), lambda m, n: ((m * tm) // seq, 0, 0)),
            pl.BlockSpec((1, d), lambda m, n: (0, 0)),
            pl.BlockSpec((d, tn), lambda m, n: (0, n)),
            pl.BlockSpec((d, GATE_LANES), lambda m, n: (0, 0)),
            pl.BlockSpec((1, GATE_LANES), lambda m, n: (0, 0)),
        ],
        out_specs=(
            pl.BlockSpec((tm, tn), lambda m, n: (m, n)),
            pl.BlockSpec((tm, GATE_LANES), lambda m, n: (m, 0)),
            pl.BlockSpec((1, tm // CHUNK, 2 * MLSTM_HEADS, CHUNK),
                         lambda m, n: (m // tiles_per_seq, m % tiles_per_seq, 0, 0)),
        ),
        scratch_shapes=[pltpu.VMEM((tm, d), BF16)],
        compiler_params=pltpu.CompilerParams(
            dimension_semantics=("arbitrary", "arbitrary"), vmem_limit_bytes=VMEM_LIMIT_BYTES),
        name="in_proj",
    )(x2, mod3, norm_gain.reshape(1, d), w_main, w_gate, gate_bias)


def _mlstm_chunk(qc, kt, qk_ref, reuse_qk, vv, r_row, cm_rep, b_rep, cm_end, g, cn_ref, m_ref, causal_mask):
    d = vv.shape[1]
    cn = cn_ref[...]
    m_state = m_ref[...]
    m_col = jnp.maximum(m_state, cm_rep)
    w_inter = jnp.exp2(m_state - m_col)
    m_wide = jnp.concatenate([m_col, m_col], axis=1)
    d_mat = jnp.where(causal_mask, jnp.exp2(r_row - m_wide), 0.0)
    if reuse_qk:
        qk = qk_ref[...]
    else:
        qk = jnp.dot(qc, kt, preferred_element_type=F32)
        qk_ref[...] = qk
    scores = qk * d_mat
    inter = jnp.dot(qc, cn.astype(BF16), preferred_element_type=F32)
    yield None
    w_wide = jnp.concatenate([w_inter, w_inter], axis=1)
    num = jnp.dot(scores.astype(BF16), vv, preferred_element_type=F32) + w_wide * inter[:, 0:d]
    den = jnp.sum(scores, axis=-1, keepdims=True) + w_inter * inter[:, d:]
    inv = 1.0 / jnp.maximum(jnp.abs(den), jnp.exp2(-(b_rep + m_col)))
    yield num * jnp.concatenate([inv, inv], axis=1)
    m_end = jnp.maximum(m_state, cm_end)
    w_row = jnp.exp2(r_row - jnp.concatenate([m_end, m_end], axis=1))
    decay = jnp.exp2(m_state - m_end)
    kw = kt * w_row.astype(BF16)
    v_aug = jnp.concatenate([vv, jnp.ones((vv.shape[0], LANES), BF16)], axis=1)
    decay_wide = jnp.concatenate([decay, decay, decay], axis=1)
    cn_ref[...] = decay_wide * cn + jnp.dot(kw, v_aug, preferred_element_type=F32)
    m_ref[...] = g + m_end
    yield None


def _mlstm_kernel(q_ref, k_ref, v_ref, o_ref, z_ref, gcol_ref, grow_ref, cwq_ref, cwk_ref, cbq_ref, cbk_ref,
                  gain_ref, y_ref, qc_ref, kt_ref, qk_ref, h_ref, cnf_ref, mf_ref, cnb_ref, mb_ref):
    seq, d = q_ref.shape[1], q_ref.shape[2]
    nc = seq // CHUNK
    head = pl.program_id(1)

    t_idx = lax.broadcasted_iota(jnp.int32, (CHUNK, CHUNK), 0)
    s_idx = lax.broadcasted_iota(jnp.int32, (CHUNK, CHUNK), 1)
    taps = jnp.concatenate([(s_idx == t_idx - 1).astype(BF16), (s_idx == t_idx + 1).astype(BF16)],
                           axis=0)
    edge = lax.broadcasted_iota(jnp.int32, (8, d), 0)

    def conv_silu(src_ref, w_ref, b_ref, c, r):
        w0, w1, w2 = 0.5 * w_ref[0:1, :], 0.5 * w_ref[1:2, :], 0.5 * w_ref[2:3, :]
        src = src_ref[0, pl.ds(r, CHUNK), :]
        p = jnp.dot(taps, src, preferred_element_type=F32)
        t = p[0:CHUNK] * w0 + src.astype(F32) * w1 + p[CHUNK:2 * CHUNK] * w2 + 0.5 * b_ref[...]
        above = src_ref[0, pl.ds(pl.multiple_of(jnp.maximum(r - 16, 0), 16), 16), :][15:16, :].astype(F32)
        below = src_ref[0, pl.ds(pl.multiple_of(jnp.minimum(r + CHUNK, seq - 16), 16), 16), :][0:1, :].astype(F32)
        above = above * jnp.where(c > 0, w0, 0.0)
        below = below * jnp.where(c < nc - 1, w2, 0.0)
        t = jnp.concatenate([t[0:8] + jnp.where(edge == 0, above, 0.0), t[8:CHUNK - 8],
                             t[CHUNK - 8:CHUNK] + jnp.where(edge == 7, below, 0.0)], axis=0)
        return t * (1.0 + jnp.tanh(t))

    def conv_chunk(c, carry):
        r = pl.multiple_of(c * CHUNK, CHUNK)
        qc_ref[pl.ds(r, CHUNK), :] = conv_silu(q_ref, cwq_ref, cbq_ref, c, r).astype(BF16)
        kt_ref[c] = conv_silu(k_ref, cwk_ref, cbk_ref, c, r).astype(BF16).T
        return carry

    lax.fori_loop(0, nc, conv_chunk, 0, unroll=8)

    gate_row = lax.broadcasted_iota(jnp.int32, (2 * MLSTM_HEADS, CHUNK), 0)
    gate_lane = lax.broadcasted_iota(jnp.int32, (CHUNK, GATE_LANES), 1)

    for ref in (cnf_ref, mf_ref, cnb_ref, mb_ref):
        ref[...] = jnp.zeros(ref.shape, F32)

    mask_f = s_idx <= t_idx
    mask_b = s_idx >= t_idx

    def replicate(cols, lane):
        picked = jnp.sum(jnp.where(gate_lane == lane, cols, 0.0), axis=1, keepdims=True)
        return jnp.broadcast_to(picked, (cols.shape[0], LANES))

    def direction(chunk, backward, reuse_qk):
        r = pl.multiple_of(chunk * CHUNK, CHUNK)
        rows = grow_ref[0, chunk]
        cols = gcol_ref[0, pl.ds(r, CHUNK), :]
        qc = qc_ref[pl.ds(r, CHUNK), :]
        kt = kt_ref[chunk]
        vv = v_ref[0, pl.ds(r, CHUNK), :]
        if backward:
            cm_rep, b_rep = replicate(cols, MLSTM_HEADS + head), replicate(cols, 3 * MLSTM_HEADS + head)
            end = 0
            state, mask, this_row = (cnb_ref, mb_ref), mask_b, MLSTM_HEADS + head
        else:
            cm_rep, b_rep = replicate(cols, head), replicate(cols, 2 * MLSTM_HEADS + head)
            end = CHUNK - 1
            state, mask, this_row = (cnf_ref, mf_ref), mask_f, head
        r_row = jnp.sum(jnp.where(gate_row == this_row, rows, 0.0), axis=0, keepdims=True) + K_SCALE_LOG2
        cm_end, g = cm_rep[end:end + 1, :], b_rep[end:end + 1, :]
        return _mlstm_chunk(qc, kt, qk_ref.at[chunk], reuse_qk, vv, r_row, cm_rep, b_rep, cm_end, g, *state, mask), r

    def chunk_steps(j, second_visit):
        steps = [direction(j, False, second_visit), direction(nc - 1 - j, True, second_visit)]
        if second_visit:
            for gen, _ in steps:
                next(gen)
            hs = [next(gen) for gen, _ in steps]
            for gen, _ in steps:
                next(gen)
        else:
            hs = []
            for gen, _ in steps:
                next(gen)
                hs.append(next(gen))
                next(gen)
        return [(h, r) for h, (_, r) in zip(hs, steps)]

    def first_half(j, carry):
        for h_dir, r in chunk_steps(j, second_visit=False):
            h_ref[pl.ds(r, CHUNK), :] = h_dir
        return carry

    gain = gain_ref[...]

    def finish(h_dir, r):
        h_sum = (h_ref[pl.ds(r, CHUNK), :] + h_dir) * _sigmoid(o_ref[0, pl.ds(r, CHUNK), :].astype(F32))
        normed = h_sum * lax.rsqrt(jnp.mean(h_sum * h_sum, axis=-1, keepdims=True) + EPS) * gain
        y_ref[0, pl.ds(r, CHUNK), :] = (normed * _silu(z_ref[0, pl.ds(r, CHUNK), :].astype(F32))).astype(BF16)

    def second_half(j, carry):
        for h_dir, r in chunk_steps(j, second_visit=True):
            finish(h_dir, r)
        return carry

    lax.fori_loop(0, nc // 2, first_half, 0, unroll=2)
    lax.fori_loop(nc // 2, nc, second_half, 0)


def _mlstm(proj3, gcol, grow, conv_w, conv_b, head_gain):
    bsz, seq, _ = proj3.shape
    d = MLSTM_HEAD_DIM
    nh = MLSTM_HEADS
    nc = seq // CHUNK

    def col_block(offset):
        return pl.BlockSpec((1, seq, d), lambda b, h: (b, 0, offset + h))

    return pl.pallas_call(
        _mlstm_kernel,
        out_shape=jax.ShapeDtypeStruct((bsz, seq, nh * d), BF16),
        grid=(bsz, nh),
        in_specs=[
            col_block(0), col_block(nh), col_block(2 * nh), col_block(3 * nh), col_block(4 * nh),
            pl.BlockSpec((1, seq, GATE_LANES), lambda b, h: (b, 0, 0)),
            pl.BlockSpec((1, nc, 2 * MLSTM_HEADS, CHUNK), lambda b, h: (b, 0, 0, 0)),
            pl.BlockSpec((3, d), lambda b, h: (0, h)),
            pl.BlockSpec((3, d), lambda b, h: (0, nh + h)),
            pl.BlockSpec((1, d), lambda b, h: (0, h)),
            pl.BlockSpec((1, d), lambda b, h: (0, nh + h)),
            pl.BlockSpec((1, d), lambda b, h: (0, h)),
        ],
        out_specs=pl.BlockSpec((1, seq, d), lambda b, h: (b, 0, h)),
        scratch_shapes=[
            pltpu.VMEM((seq, d), BF16),
            pltpu.VMEM((nc, d, CHUNK), BF16),
            pltpu.VMEM((nc, CHUNK, CHUNK), F32),
            pltpu.VMEM((seq, d), F32),
            pltpu.VMEM((d, d + LANES), F32), pltpu.VMEM((1, LANES), F32),
            pltpu.VMEM((d, d + LANES), F32), pltpu.VMEM((1, LANES), F32),
        ],
        compiler_params=pltpu.CompilerParams(
            dimension_semantics=("arbitrary", "arbitrary"), vmem_limit_bytes=VMEM_LIMIT_BYTES),
        name="mlstm",
    )(proj3, proj3, proj3, proj3, proj3, gcol, grow, conv_w, conv_w,
      conv_b.reshape(1, -1), conv_b.reshape(1, -1), head_gain.reshape(1, -1))


def _natten_bias_rows(rpb):
    n_dc = 2 * NA_COLS - 1
    n_dr = 2 * NA_ROWS - 1
    left = NA_COLS
    period = 64
    padded = jnp.pad(rpb.astype(F32) * LOG2E, ((0, 0), (0, 0), (left, period - left - n_dc)), constant_values=NEG)
    out = []
    for (c0, w), kb in zip(Q_BLOCKS, K_STARTS):
        base = kb - c0 + NA_COLS - 1 + left
        toe = jnp.tile(padded, (1, 1, w))[..., :(period - 1) * w].reshape(NA_HEADS, n_dr, w, period - 1)
        toe = toe[..., base:base + K_WIDTH]
        q_col = c0 + np.arange(w)[:, None]
        k_col = kb + np.arange(K_WIDTH)[None, :]
        win = np.clip(q_col - NA_COLS // 2, 0, GRID_W - NA_COLS)
        col_ok = jnp.asarray((k_col >= win) & (k_col < win + NA_COLS))
        toe = jnp.where(col_ok[None, None], toe, NEG).transpose(0, 2, 1, 3).reshape(NA_HEADS, w, n_dr * K_WIDTH)
        out.append(jnp.pad(toe, ((0, 0), (0, Q_BLOCK_MAX - w), (0, N_KEYS - n_dr * K_WIDTH)), constant_values=NEG))
    return jnp.stack(out, axis=1)


_GROUP_KINDS = (
    (0, lambda i: max(i - NA_ROWS // 2, 0)),
    (-(NA_ROWS // 2), lambda i: i),
    (Q_ROWS - SLAB_ROWS, lambda i: min(i + NA_ROWS // 2, NA_ROWS)),
)


def _build_bias(e_ref, tab_ref, slab_off, first_valid):
    for h in range(NA_HEADS):
        row_off = 0
        for blk, (_, w) in enumerate(Q_BLOCKS):
            e = e_ref[h, blk, 0:w, :]
            lane = lax.broadcasted_iota(jnp.int32, (w, N_KEYS), 1)
            for i in range(Q_ROWS):
                shift = slab_off - i + NA_ROWS - 1
                amount = (-shift * K_WIDTH) % N_KEYS
                moved = e if amount == 0 else pltpu.roll(e, amount, 1)
                lo = first_valid(i) * K_WIDTH
                ok = (lane >= lo) & (lane < lo + NA_ROWS * K_WIDTH)
                tab_ref[h, row_off + i * w:row_off + (i + 1) * w, :] = jnp.where(ok, moved, NEG)
            row_off += Q_ROWS * w


def _natten_kernel(q_ref, k0_ref, k1_ref, k2_ref, k3_ref, v0_ref, v1_ref, v2_ref, v3_ref, z_ref, e_ref,
                   y_ref, qf_ref, of_ref, tab_ref, sa_ref, sb_ref):
    k_refs = (k0_ref, k1_ref, k2_ref, k3_ref)
    v_refs = (v0_ref, v1_ref, v2_ref, v3_ref)
    piece_rows = SLAB_ROWS // 4
    group = pl.program_id(0)
    n_groups = pl.num_programs(0)

    first_batch = pl.program_id(1) == 0
    conditions = (group == 0, jnp.logical_and(group > 0, group < n_groups - 1), group == n_groups - 1)
    for cond, (slab_off, first_valid) in zip(conditions, _GROUP_KINDS):
        @pl.when(jnp.logical_and(first_batch, cond))
        def _(slab_off=slab_off, first_valid=first_valid):
            _build_bias(e_ref, tab_ref, slab_off, first_valid)

    n_batch, q_tok = q_ref.shape[0], q_ref.shape[2]
    for bi in range(n_batch):
        qf_ref[pl.ds(bi * q_tok, q_tok), :] = q_ref[bi, 0].astype(F32) * (NA_HEAD_DIM ** -0.5 * LOG2E)

    items = []
    for bi in range(n_batch):
        row_off = 0
        for (c0, w), kb in zip(Q_BLOCKS, K_STARTS):
            for pair in range(NA_HEADS // 2):
                items.append((bi, c0, w, kb, pair, row_off))
            row_off += Q_ROWS * w
    s_refs = (sa_ref, sb_ref)

    def slab(refs, bi, kb, lanes):
        return jnp.concatenate(
            [refs[a // piece_rows][bi, 0, pl.ds((a % piece_rows) * GRID_W + kb, K_WIDTH), lanes]
             for a in range(SLAB_ROWS)], axis=0)

    def scores(idx):
        bi, c0, w, kb, pair, row_off = items[idx]
        nq = Q_ROWS * w
        lanes = pl.ds(pair * LANES, LANES)
        first_head = lax.broadcasted_iota(jnp.int32, (nq, LANES), 1) < NA_HEAD_DIM
        qp = jnp.concatenate([qf_ref[pl.ds(bi * q_tok + i * GRID_W + c0, w), lanes] for i in range(Q_ROWS)],
                             axis=0)
        qs = jnp.concatenate([jnp.where(first_head, qp, 0.0), jnp.where(first_head, 0.0, qp)],
                             axis=0).astype(BF16)
        s = lax.dot_general(qs, slab(k_refs, bi, kb, lanes), (((1,), (1,)), ((), ())),
                            preferred_element_type=F32)
        bias = jnp.concatenate([tab_ref[2 * pair, pl.ds(row_off, nq), :],
                                tab_ref[2 * pair + 1, pl.ds(row_off, nq), :]], axis=0)
        s_refs[idx % 2][0:2 * nq, :] = s + bias

    def attend(idx):
        bi, c0, w, kb, pair, _ = items[idx]
        nq = Q_ROWS * w
        lanes = pl.ds(pair * LANES, LANES)
        first_head = lax.broadcasted_iota(jnp.int32, (nq, LANES), 1) < NA_HEAD_DIM
        s = s_refs[idx % 2][0:2 * nq, :]
        p = jnp.exp2(s - jnp.max(s, axis=-1, keepdims=True))
        o2 = jnp.dot(p.astype(BF16), slab(v_refs, bi, kb, lanes), preferred_element_type=F32)
        o2 = o2 * (1.0 / jnp.sum(p, axis=-1, keepdims=True))
        o = jnp.where(first_head, o2[0:nq], o2[nq:2 * nq])
        for i in range(Q_ROWS):
            of_ref[pl.ds(bi * q_tok + i * GRID_W + c0, w), lanes] = o[i * w:(i + 1) * w]

    scores(0)
    for idx in range(len(items)):
        if idx + 1 < len(items):
            scores(idx + 1)
        attend(idx)
    for bi in range(n_batch):
        rows = pl.ds(bi * q_tok, q_tok)
        y_ref[bi, 0] = (of_ref[rows, :] * _silu(z_ref[bi, 0].astype(F32))).astype(BF16)


def _natten(proj3, bias_rows):
    bsz, seq, n_main = proj3.shape
    q_tok = Q_ROWS * GRID_W
    p_tok = (SLAB_ROWS // 4) * GRID_W
    groups = seq // q_tok
    pieces = seq // p_tok
    assert groups >= 2, "needs at least two 8-row query groups"
    nb = 2 if bsz % 2 == 0 else 1
    w = NA_WIDTH
    qb, kb, vb, zb = (5 * D_MODEL) // w, (5 * D_MODEL) // w + 1, (5 * D_MODEL) // w + 2, (5 * D_MODEL) // w + 3
    proj_q = proj3.reshape(bsz, groups, q_tok, n_main)
    proj_p = proj3.reshape(bsz, pieces, p_tok, n_main)

    def slab_start(g):
        return jnp.clip(2 * g - 1, 0, pieces - 4)

    def piece_spec(col, a):
        return pl.BlockSpec((nb, 1, p_tok, w), lambda g, b: (b, slab_start(g) + a, 0, col))

    return pl.pallas_call(
        _natten_kernel,
        out_shape=jax.ShapeDtypeStruct((bsz, groups, q_tok, w), BF16),
        grid=(groups, bsz // nb),
        in_specs=[pl.BlockSpec((nb, 1, q_tok, w), lambda g, b: (b, g, 0, qb))]
        + [piece_spec(kb, a) for a in range(4)] + [piece_spec(vb, a) for a in range(4)]
        + [pl.BlockSpec((nb, 1, q_tok, w), lambda g, b: (b, g, 0, zb)),
           pl.BlockSpec(bias_rows.shape, lambda g, b: (0, 0, 0, 0))],
        out_specs=pl.BlockSpec((nb, 1, q_tok, w), lambda g, b: (b, g, 0, 0)),
        scratch_shapes=[pltpu.VMEM((nb * q_tok, w), F32), pltpu.VMEM((nb * q_tok, w), F32),
                        pltpu.VMEM((NA_HEADS, q_tok, N_KEYS), F32),
                        pltpu.VMEM((2 * Q_ROWS * Q_BLOCK_MAX, N_KEYS), F32),
                        pltpu.VMEM((2 * Q_ROWS * Q_BLOCK_MAX, N_KEYS), F32)],
        compiler_params=pltpu.CompilerParams(
            dimension_semantics=("arbitrary", "arbitrary"), vmem_limit_bytes=VMEM_LIMIT_BYTES),
        name="natten",
    )(proj_q, *([proj_p] * 8), proj_q, bias_rows).reshape(bsz, seq, w)


def _merge_out_kernel(ya_ref, yb_ref, g0_ref, g1_ref, x_ref, mod_ref, bm_ref, wa_ref, wb_ref, wo_ref, fg_ref,
                      out_ref, m_ref):
    tm, d = x_ref.shape
    sub = min(MERGE_ROWS, tm)
    gate = mod_ref[0, :, 2 * d:3 * d]

    def merge(r):
        rows = pl.ds(r, sub)
        pa = jnp.dot(ya_ref[rows, :], wa_ref[...], preferred_element_type=F32)
        pb = jnp.dot(yb_ref[rows, :], wb_ref[...], preferred_element_type=F32)
        m_ref[rows, :] = (_sigmoid(g0_ref[rows, :].astype(F32) + bm_ref[0:1, :]) * pa
                          + _sigmoid(g1_ref[rows, :].astype(F32) + bm_ref[1:2, :]) * pb).astype(BF16)

    def project(r):
        rows = pl.ds(r, sub)
        mixed = jnp.dot(m_ref[rows, :], wo_ref[...], preferred_element_type=F32)
        xo = x_ref[rows, :] + gate * mixed
        out_ref[rows, :] = xo * lax.rsqrt(jnp.mean(xo * xo, axis=-1, keepdims=True) + EPS) * fg_ref[...]

    starts = list(range(0, tm, sub))
    merge(starts[0])
    for i, r in enumerate(starts):
        if i + 1 < len(starts):
            merge(starts[i + 1])
        project(r)


def _merge_out(ya2, yb2, proj2, x2, mod3, b_merge, wa, wb, wo, final_gain, seq, tm):
    n_tok, d = x2.shape
    g0 = (N_MAIN - 2 * d) // d

    def full(shape):
        return pl.BlockSpec(shape, lambda m: (0,) * len(shape))

    return pl.pallas_call(
        _merge_out_kernel,
        out_shape=jax.ShapeDtypeStruct((n_tok, d), F32),
        grid=(n_tok // tm,),
        in_specs=[
            pl.BlockSpec((tm, d), lambda m: (m, 0)),
            pl.BlockSpec((tm, NA_WIDTH), lambda m: (m, 0)),
            pl.BlockSpec((tm, d), lambda m: (m, g0)),
            pl.BlockSpec((tm, d), lambda m: (m, g0 + 1)),
            pl.BlockSpec((tm, d), lambda m: (m, 0)),
            pl.BlockSpec((1, 1, 3 * d), lambda m: ((m * tm) // seq, 0, 0)),
            full((2, d)), full((d, d)), full((NA_WIDTH, d)), full((d, d)), full((1, d)),
        ],
        out_specs=pl.BlockSpec((tm, d), lambda m: (m, 0)),
        scratch_shapes=[pltpu.VMEM((tm, d), BF16)],
        compiler_params=pltpu.CompilerParams(
            dimension_semantics=("arbitrary",), vmem_limit_bytes=VMEM_LIMIT_BYTES),
        name="merge_out",
    )(ya2, yb2, proj2, proj2, x2, mod3, b_merge, wa, wb, wo, final_gain.reshape(1, d))


def _gate_layout(n_cols):
    src = np.arange(n_cols)
    direction, gate, head = src // (2 * MLSTM_HEADS), (src // MLSTM_HEADS) % 2, src % MLSTM_HEADS
    return 2 * MLSTM_HEADS * gate + MLSTM_HEADS * direction + head


def _weight_prep_kernel(a_ref, b_ref, perm_ref, main_ref, gate_ref, *, gate_block, n_gates):
    n = pl.program_id(0)
    rows = a_ref[0]
    past_gates = jnp.concatenate([rows[n_gates:], b_ref[0]], axis=0)
    main_ref[...] = jnp.where(n >= gate_block, past_gates, rows).T.astype(BF16)

    @pl.when(n == gate_block)
    def _():
        gate_ref[...] = jnp.dot(perm_ref[...], rows[0:n_gates], preferred_element_type=F32).T.astype(BF16)


def _weight_prep(w_in, layer):
    _, d, n_in = w_in.shape
    n_gates = 4 * MLSTM_HEADS
    tn = D_MODEL
    g_lo = 5 * MLSTM_HEADS * MLSTM_HEAD_DIM
    assert n_in == N_MAIN + n_gates and g_lo % tn == 0 and tn % n_gates == 0
    perm = np.zeros((GATE_LANES, n_gates), np.float32)
    perm[_gate_layout(n_gates), np.arange(n_gates)] = 1.0
    w_t = jnp.swapaxes(w_in, 1, 2)
    return pl.pallas_call(
        functools.partial(_weight_prep_kernel, gate_block=g_lo // tn, n_gates=n_gates),
        out_shape=(jax.ShapeDtypeStruct((d, N_MAIN), BF16), jax.ShapeDtypeStruct((d, GATE_LANES), BF16)),
        grid=(N_MAIN // tn,),
        in_specs=[pl.BlockSpec((1, tn, d), lambda n: (layer, n, 0)),
                  pl.BlockSpec((1, n_gates, d), lambda n: (layer, (n + 1) * (tn // n_gates), 0)),
                  pl.BlockSpec((GATE_LANES, n_gates), lambda n: (0, 0))],
        out_specs=(pl.BlockSpec((d, tn), lambda n: (0, n)),
                   pl.BlockSpec((d, GATE_LANES), lambda n: (0, 0))),
        compiler_params=pltpu.CompilerParams(dimension_semantics=("arbitrary",), vmem_limit_bytes=VMEM_LIMIT_BYTES),
        name="weight_prep",
    )(w_t, w_t, jnp.asarray(perm))


def _token_tiles(seq):
    return min(2048, seq), min(1024, seq)


def _layer(x, mod, norm_gain, w_in, layer, conv_w, conv_b, b_igate, b_fgate, mlstm_norm_gain, rpb, w_proj_a, w_proj_b,
           b_merge, w_out, final_gain):
    bsz, seq, d = x.shape
    n_tok = bsz * seq
    wa = MLSTM_HEADS * MLSTM_HEAD_DIM
    w_main, w_gate = _weight_prep(w_in, layer)
    gate_bias = jnp.zeros((1, GATE_LANES), F32).at[0, _gate_layout(4 * MLSTM_HEADS)].set(
        jnp.stack([b_igate, b_fgate], axis=1).reshape(-1))
    tm_in, tm_out = _token_tiles(seq)

    x2 = x.reshape(n_tok, d)
    mod3 = mod.reshape(bsz, 1, 3 * d)
    proj2, gcol2, grow = _in_proj(x2, mod3, norm_gain, w_main, w_gate, gate_bias, seq, tm_in)
    proj3 = proj2.reshape(bsz, seq, N_MAIN)
    gcol = gcol2.reshape(bsz, seq, GATE_LANES)
    y_a = _mlstm(proj3, gcol, grow, conv_w, conv_b, mlstm_norm_gain)
    y_b = _natten(proj3, _natten_bias_rows(rpb))
    out2 = _merge_out(y_a.reshape(n_tok, wa), y_b.reshape(n_tok, NA_WIDTH), proj2, x2, mod3, b_merge,
                      w_proj_a.astype(BF16), w_proj_b.astype(BF16), w_out.astype(BF16), final_gain, seq, tm_out)
    return out2.reshape(bsz, seq, d)


def kernel(x, c, w_ada, b_ada, norm_gain, w_in, conv_w, conv_b, b_igate, b_fgate, mlstm_norm_gain, rpb, w_proj_a,
           w_proj_b, b_merge, w_out, final_gain):
    depth = w_ada.shape[0]
    assert depth == 1, "the fused final norm assumes a single layer"
    mod = _adaln_mod(c, w_ada[0], b_ada[0])
    return _layer(x, mod, norm_gain[0], w_in, 0, conv_w[0], conv_b[0], b_igate[0], b_fgate[0], mlstm_norm_gain[0],
                  rpb[0], w_proj_a[0], w_proj_b[0], b_merge[0], w_out[0], final_gain)
```

```python
import functools
import math

import numpy as np
import jax
import jax.numpy as jnp
from jax import lax
from jax.experimental import pallas as pl
from jax.experimental.pallas import tpu as pltpu

F32 = jnp.float32
BF16 = jnp.bfloat16

D_MODEL = 1024
GRID_W = 64
MLSTM_HEADS = 4
MLSTM_HEAD_DIM = 256
NA_HEADS = 8
NA_WIDTH = 512
NA_HEAD_DIM = 64
NA_ROWS = 8
NA_COLS = 16
EPS = 1e-6
NEG = -1e30
LOG2E = 1.4426950408889634
K_SCALE_LOG2 = -0.5 * math.log2(MLSTM_HEAD_DIM)

VMEM_LIMIT_BYTES = 56 * 1024 * 1024
LANES = 128

CHUNK = 256

MERGE_ROWS = 512

N_MAIN = 9 * D_MODEL
GATE_LANES = 128

Q_ROWS = 8
SLAB_ROWS = 16
Q_BLOCKS = ((0, 24), (24, 16), (40, 24))
Q_BLOCK_MAX = 24
K_STARTS = (0, 16, 32)
K_WIDTH = 32
N_KEYS = SLAB_ROWS * K_WIDTH


def _sigmoid(x):
    return 1.0 / (1.0 + jnp.exp2(x * -LOG2E))


def _silu(x):
    return x * _sigmoid(x)


def _adaln_kernel(c_ref, w_ref, b_ref, o_ref):
    cond = _silu(c_ref[...]).astype(BF16)
    o_ref[...] = jnp.dot(cond, w_ref[...].astype(BF16), preferred_element_type=F32) + b_ref[...]


def _adaln_mod(c, w_ada, b_ada):
    bsz, d = c.shape
    n = w_ada.shape[1]
    tn = 512
    return pl.pallas_call(
        _adaln_kernel,
        out_shape=jax.ShapeDtypeStruct((bsz, n), F32),
        grid=(n // tn,),
        in_specs=[
            pl.BlockSpec((bsz, d), lambda j: (0, 0)),
            pl.BlockSpec((d, tn), lambda j: (0, j)),
            pl.BlockSpec((1, tn), lambda j: (0, j)),
        ],
        out_specs=pl.BlockSpec((bsz, tn), lambda j: (0, j)),
        compiler_params=pltpu.CompilerParams(dimension_semantics=("arbitrary",)),
        name="adaln_mod",
    )(c, w_ada, b_ada.reshape(1, n))


def _gate_quantities(gates, pos_in_chunk):
    v = gates.T
    i_gate, f_gate = v[0:8, :] * LOG2E, v[8:16, :]
    n_tok = v.shape[1]
    is_bwd = lax.broadcasted_iota(jnp.int32, i_gate.shape, 0) >= MLSTM_HEADS

    def scan(x, combine, identity):
        pre, suf = x, x
        step = 1
        while step < CHUNK:
            pre = combine(pre, jnp.where(pos_in_chunk >= step, pltpu.roll(pre, step, 1), identity))
            suf = combine(suf, jnp.where(pos_in_chunk < CHUNK - step, pltpu.roll(suf, n_tok - step, 1), identity))
            step *= 2
        return jnp.where(is_bwd, suf, pre)

    b = scan((jnp.minimum(f_gate, 0.0) - jnp.log1p(jnp.exp(-jnp.abs(f_gate)))) * LOG2E, jnp.add, 0.0)
    r = i_gate - b
    cm = scan(r, jnp.maximum, NEG)
    cols = jnp.concatenate([cm, b, jnp.zeros((GATE_LANES - 16, n_tok), F32)], axis=0).T
    return cols, r


def _in_proj_kernel(x_ref, mod_ref, gain_ref, w_ref, wg_ref, gbias_ref, proj_ref, gcol_ref, grow_ref, h_ref,
                    last_gates_ref, *, rows):
    tm, d = x_ref.shape
    n = pl.program_id(1)

    def gate_scans(r, gates):
        pos = lax.broadcasted_iota(jnp.int32, (8, rows), 1) % CHUNK
        cols, r_rows = _gate_quantities(gates, pos)
        gcol_ref[r:r + rows, :] = cols
        for c in range(rows // CHUNK):
            grow_ref[0, r // CHUNK + c] = r_rows[:, c * CHUNK:(c + 1) * CHUNK]

    @pl.when(n == 0)
    def _():
        shift = mod_ref[0, :, 0:d]
        scale = mod_ref[0, :, d:2 * d]
        gain = gain_ref[...]

        def normalize(r):
            xf = x_ref[r:r + rows, :]
            y = xf * lax.rsqrt(jnp.mean(xf * xf, axis=-1, keepdims=True) + EPS) * gain
            h_ref[r:r + rows, :] = (y * (1.0 + scale) + shift).astype(BF16)

        def project(r):
            hb = h_ref[r:r + rows, :]
            gates = jnp.dot(hb, wg_ref[...], preferred_element_type=F32) + gbias_ref[...]
            proj_ref[r:r + rows, :] = jnp.dot(hb, w_ref[...], preferred_element_type=F32).astype(BF16)
            return gates

        starts = list(range(0, tm, rows))
        normalize(starts[0])
        pending = None
        for i, r in enumerate(starts):
            if i + 1 < len(starts):
                normalize(starts[i + 1])
            gates = project(r)
            if pending is not None:
                gate_scans(*pending)
            pending = (r, gates)
        last_gates_ref[...] = pending[1]

    @pl.when(n == 1)
    def _():
        proj_ref[...] = jnp.dot(h_ref[...], w_ref[...], preferred_element_type=F32).astype(BF16)
        gate_scans(tm - rows, last_gates_ref[...])

    @pl.when(n > 1)
    def _():
        proj_ref[...] = jnp.dot(h_ref[...], w_ref[...], preferred_element_type=F32).astype(BF16)


def _in_proj(x2, mod3, norm_gain, w_main, w_gate, gate_bias, seq, tm):
    n_tok, d = x2.shape
    tn = N_MAIN // 6
    rows = min(MERGE_ROWS, tm)
    assert rows % CHUNK == 0 and seq % tm == 0
    tiles_per_seq = seq // tm
    return pl.pallas_call(
        functools.partial(_in_proj_kernel, rows=rows),
        out_shape=(
            jax.ShapeDtypeStruct((n_tok, N_MAIN), BF16),
            jax.ShapeDtypeStruct((n_tok, GATE_LANES), F32),
            jax.ShapeDtypeStruct((n_tok // seq, seq // CHUNK, 2 * MLSTM_HEADS, CHUNK), F32),
        ),
        grid=(n_tok // tm, N_MAIN // tn),
        in_specs=[
            pl.BlockSpec((tm, d), lambda m, n: (m, 0)),
            pl.BlockSpec((1, 1, 3 * d), lambda m, n: ((m * tm) // seq, 0, 0)),
            pl.BlockSpec((1, d), lambda m, n: (0, 0)),
            pl.BlockSpec((d, tn), lambda m, n: (0, n)),
            pl.BlockSpec((d, GATE_LANES), lambda m, n: (0, 0)),
            pl.BlockSpec((1, GATE_LANES), lambda m, n: (0, 0)),
        ],
        out_specs=(
            pl.BlockSpec((tm, tn), lambda m, n: (m, n)),
            pl.BlockSpec((tm, GATE_LANES), lambda m, n: (m, 0)),
            pl.BlockSpec((1, tm // CHUNK, 2 * MLSTM_HEADS, CHUNK),
                         lambda m, n: (m // tiles_per_seq, m % tiles_per_seq, 0, 0)),
        ),
        scratch_shapes=[pltpu.VMEM((tm, d), BF16), pltpu.VMEM((rows, GATE_LANES), F32)],
        compiler_params=pltpu.CompilerParams(
            dimension_semantics=("arbitrary", "arbitrary"), vmem_limit_bytes=VMEM_LIMIT_BYTES),
        name="in_proj",
    )(x2, mod3, norm_gain.reshape(1, d), w_main, w_gate, gate_bias)


def _mlstm_chunk(qc, kt, qk_ref, reuse_qk, vv, r_row, cm_rep, b_rep, cm_end, g, cn_ref, m_ref, causal_mask):
    d = vv.shape[1]
    cn = cn_ref[...]
    m_state = m_ref[...]
    m_col = jnp.maximum(m_state, cm_rep)
    w_inter = jnp.exp2(m_state - m_col)
    m_wide = jnp.concatenate([m_col, m_col], axis=1)
    d_mat = jnp.where(causal_mask, jnp.exp2(r_row - m_wide), 0.0)
    if reuse_qk:
        qk = qk_ref[...]
    else:
        qk = jnp.dot(qc, kt, preferred_element_type=F32)
        qk_ref[...] = qk
    scores = qk * d_mat
    inter = jnp.dot(qc, cn.astype(BF16), preferred_element_type=F32)
    yield None
    w_wide = jnp.concatenate([w_inter, w_inter], axis=1)
    num = jnp.dot(scores.astype(BF16), vv, preferred_element_type=F32) + w_wide * inter[:, 0:d]
    den = jnp.sum(scores, axis=-1, keepdims=True) + w_inter * inter[:, d:]
    inv = 1.0 / jnp.maximum(jnp.abs(den), jnp.exp2(-(b_rep + m_col)))
    yield num * jnp.concatenate([inv, inv], axis=1)
    m_end = jnp.maximum(m_state, cm_end)
    w_row = jnp.exp2(r_row - jnp.concatenate([m_end, m_end], axis=1))
    decay = jnp.exp2(m_state - m_end)
    kw = kt * w_row.astype(BF16)
    v_aug = jnp.concatenate([vv, jnp.ones((vv.shape[0], LANES), BF16)], axis=1)
    decay_wide = jnp.concatenate([decay, decay, decay], axis=1)
    cn_ref[...] = decay_wide * cn + jnp.dot(kw, v_aug, preferred_element_type=F32)
    m_ref[...] = g + m_end
    yield None


def _mlstm_kernel(q_ref, k_ref, v_ref, o_ref, z_ref, gcol_ref, grow_ref, cwq_ref, cwk_ref, cbq_ref, cbk_ref,
                  gain_ref, y_ref, qc_ref, kt_ref, qk_ref, h_ref, cnf_ref, mf_ref, cnb_ref, mb_ref):
    seq, d = q_ref.shape[1], q_ref.shape[2]
    nc = seq // CHUNK
    head = pl.program_id(1)

    t_idx = lax.broadcasted_iota(jnp.int32, (CHUNK, CHUNK), 0)
    s_idx = lax.broadcasted_iota(jnp.int32, (CHUNK, CHUNK), 1)
    taps = jnp.concatenate([(s_idx == t_idx - 1).astype(BF16), (s_idx == t_idx + 1).astype(BF16)],
                           axis=0)
    edge = lax.broadcasted_iota(jnp.int32, (8, d), 0)

    def conv_silu(src_ref, w_ref, b_ref, c, r):
        w0, w1, w2 = 0.5 * w_ref[0:1, :], 0.5 * w_ref[1:2, :], 0.5 * w_ref[2:3, :]
        src = src_ref[0, pl.ds(r, CHUNK), :]
        p = jnp.dot(taps, src, preferred_element_type=F32)
        t = p[0:CHUNK] * w0 + src.astype(F32) * w1 + p[CHUNK:2 * CHUNK] * w2 + 0.5 * b_ref[...]
        above = src_ref[0, pl.ds(pl.multiple_of(jnp.maximum(r - 16, 0), 16), 16), :][15:16, :].astype(F32)
        below = src_ref[0, pl.ds(pl.multiple_of(jnp.minimum(r + CHUNK, seq - 16), 16), 16), :][0:1, :].astype(F32)
        above = above * jnp.where(c > 0, w0, 0.0)
        below = below * jnp.where(c < nc - 1, w2, 0.0)
        t = jnp.concatenate([t[0:8] + jnp.where(edge == 0, above, 0.0), t[8:CHUNK - 8],
                             t[CHUNK - 8:CHUNK] + jnp.where(edge == 7, below, 0.0)], axis=0)
        return t * (1.0 + jnp.tanh(t))

    def conv_chunk(c, carry):
        r = pl.multiple_of(c * CHUNK, CHUNK)
        qc_ref[pl.ds(r, CHUNK), :] = conv_silu(q_ref, cwq_ref, cbq_ref, c, r).astype(BF16)
        kt_ref[c] = conv_silu(k_ref, cwk_ref, cbk_ref, c, r).astype(BF16).T
        return carry

    lax.fori_loop(0, nc, conv_chunk, 0, unroll=8)

    gate_row = lax.broadcasted_iota(jnp.int32, (2 * MLSTM_HEADS, CHUNK), 0)
    gate_lane = lax.broadcasted_iota(jnp.int32, (CHUNK, GATE_LANES), 1)

    for ref in (cnf_ref, mf_ref, cnb_ref, mb_ref):
        ref[...] = jnp.zeros(ref.shape, F32)

    mask_f = s_idx <= t_idx
    mask_b = s_idx >= t_idx

    def replicate(cols, lane):
        picked = jnp.sum(jnp.where(gate_lane == lane, cols, 0.0), axis=1, keepdims=True)
        return jnp.broadcast_to(picked, (cols.shape[0], LANES))

    def direction(chunk, backward, reuse_qk):
        r = pl.multiple_of(chunk * CHUNK, CHUNK)
        rows = grow_ref[0, chunk]
        cols = gcol_ref[0, pl.ds(r, CHUNK), :]
        qc = qc_ref[pl.ds(r, CHUNK), :]
        kt = kt_ref[chunk]
        vv = v_ref[0, pl.ds(r, CHUNK), :]
        if backward:
            cm_rep, b_rep = replicate(cols, MLSTM_HEADS + head), replicate(cols, 3 * MLSTM_HEADS + head)
            end = 0
            state, mask, this_row = (cnb_ref, mb_ref), mask_b, MLSTM_HEADS + head
        else:
            cm_rep, b_rep = replicate(cols, head), replicate(cols, 2 * MLSTM_HEADS + head)
            end = CHUNK - 1
            state, mask, this_row = (cnf_ref, mf_ref), mask_f, head
        r_row = jnp.sum(jnp.where(gate_row == this_row, rows, 0.0), axis=0, keepdims=True) + K_SCALE_LOG2
        cm_end, g = cm_rep[end:end + 1, :], b_rep[end:end + 1, :]
        return _mlstm_chunk(qc, kt, qk_ref.at[chunk], reuse_qk, vv, r_row, cm_rep, b_rep, cm_end, g, *state, mask), r

    def chunk_steps(j, second_visit):
        steps = [direction(j, False, second_visit), direction(nc - 1 - j, True, second_visit)]
        if second_visit:
            for gen, _ in steps:
                next(gen)
            hs = [next(gen) for gen, _ in steps]
            for gen, _ in steps:
                next(gen)
        else:
            hs = []
            for gen, _ in steps:
                next(gen)
                hs.append(next(gen))
                next(gen)
        return [(h, r) for h, (_, r) in zip(hs, steps)]

    def first_half(j, carry):
        for h_dir, r in chunk_steps(j, second_visit=False):
            h_ref[pl.ds(r, CHUNK), :] = h_dir
        return carry

    gain = gain_ref[...]

    def finish(h_dir, r):
        h_sum = (h_ref[pl.ds(r, CHUNK), :] + h_dir) * _sigmoid(o_ref[0, pl.ds(r, CHUNK), :].astype(F32))
        normed = h_sum * lax.rsqrt(jnp.mean(h_sum * h_sum, axis=-1, keepdims=True) + EPS) * gain
        y_ref[0, pl.ds(r, CHUNK), :] = (normed * _silu(z_ref[0, pl.ds(r, CHUNK), :].astype(F32))).astype(BF16)

    def second_half(j, carry):
        for h_dir, r in chunk_steps(j, second_visit=True):
            finish(h_dir, r)
        return carry

    lax.fori_loop(0, nc // 2, first_half, 0, unroll=2)
    lax.fori_loop(nc // 2, nc, second_half, 0)


def _mlstm(proj3, gcol, grow, conv_w, conv_b, head_gain):
    bsz, seq, _ = proj3.shape
    d = MLSTM_HEAD_DIM
    nh = MLSTM_HEADS
    nc = seq // CHUNK

    def col_block(offset):
        return pl.BlockSpec((1, seq, d), lambda b, h: (b, 0, offset + h))

    return pl.pallas_call(
        _mlstm_kernel,
        out_shape=jax.ShapeDtypeStruct((bsz, seq, nh * d), BF16),
        grid=(bsz, nh),
        in_specs=[
            col_block(0), col_block(nh), col_block(2 * nh), col_block(3 * nh), col_block(4 * nh),
            pl.BlockSpec((1, seq, GATE_LANES), lambda b, h: (b, 0, 0)),
            pl.BlockSpec((1, nc, 2 * MLSTM_HEADS, CHUNK), lambda b, h: (b, 0, 0, 0)),
            pl.BlockSpec((3, d), lambda b, h: (0, h)),
            pl.BlockSpec((3, d), lambda b, h: (0, nh + h)),
            pl.BlockSpec((1, d), lambda b, h: (0, h)),
            pl.BlockSpec((1, d), lambda b, h: (0, nh + h)),
            pl.BlockSpec((1, d), lambda b, h: (0, h)),
        ],
        out_specs=pl.BlockSpec((1, seq, d), lambda b, h: (b, 0, h)),
        scratch_shapes=[
            pltpu.VMEM((seq, d), BF16),
            pltpu.VMEM((nc, d, CHUNK), BF16),
            pltpu.VMEM((nc, CHUNK, CHUNK), F32),
            pltpu.VMEM((seq, d), F32),
            pltpu.VMEM((d, d + LANES), F32), pltpu.VMEM((1, LANES), F32),
            pltpu.VMEM((d, d + LANES), F32), pltpu.VMEM((1, LANES), F32),
        ],
        compiler_params=pltpu.CompilerParams(
            dimension_semantics=("arbitrary", "arbitrary"), vmem_limit_bytes=VMEM_LIMIT_BYTES),
        name="mlstm",
    )(proj3, proj3, proj3, proj3, proj3, gcol, grow, conv_w, conv_w,
      conv_b.reshape(1, -1), conv_b.reshape(1, -1), head_gain.reshape(1, -1))


def _natten_bias_rows(rpb):
    n_dc = 2 * NA_COLS - 1
    n_dr = 2 * NA_ROWS - 1
    left = NA_COLS
    period = 64
    padded = jnp.pad(rpb.astype(F32) * LOG2E, ((0, 0), (0, 0), (left, period - left - n_dc)), constant_values=NEG)
    out = []
    for (c0, w), kb in zip(Q_BLOCKS, K_STARTS):
        base = kb - c0 + NA_COLS - 1 + left
        toe = jnp.tile(padded, (1, 1, w))[..., :(period - 1) * w].reshape(NA_HEADS, n_dr, w, period - 1)
        toe = toe[..., base:base + K_WIDTH]
        q_col = c0 + np.arange(w)[:, None]
        k_col = kb + np.arange(K_WIDTH)[None, :]
        win = np.clip(q_col - NA_COLS // 2, 0, GRID_W - NA_COLS)
        col_ok = jnp.asarray((k_col >= win) & (k_col < win + NA_COLS))
        toe = jnp.where(col_ok[None, None], toe, NEG).transpose(0, 2, 1, 3).reshape(NA_HEADS, w, n_dr * K_WIDTH)
        out.append(jnp.pad(toe, ((0, 0), (0, Q_BLOCK_MAX - w), (0, N_KEYS - n_dr * K_WIDTH)), constant_values=NEG))
    return jnp.stack(out, axis=1)


_GROUP_KINDS = (
    (0, lambda i: max(i - NA_ROWS // 2, 0)),
    (-(NA_ROWS // 2), lambda i: i),
    (Q_ROWS - SLAB_ROWS, lambda i: min(i + NA_ROWS // 2, NA_ROWS)),
)


def _build_bias(e_ref, tab_ref, slab_off, first_valid):
    for h in range(NA_HEADS):
        row_off = 0
        for blk, (_, w) in enumerate(Q_BLOCKS):
            e = e_ref[h, blk, 0:w, :]
            lane = lax.broadcasted_iota(jnp.int32, (w, N_KEYS), 1)
            for i in range(Q_ROWS):
                shift = slab_off - i + NA_ROWS - 1
                amount = (-shift * K_WIDTH) % N_KEYS
                moved = e if amount == 0 else pltpu.roll(e, amount, 1)
                lo = first_valid(i) * K_WIDTH
                ok = (lane >= lo) & (lane < lo + NA_ROWS * K_WIDTH)
                tab_ref[h, row_off + i * w:row_off + (i + 1) * w, :] = jnp.where(ok, moved, NEG)
            row_off += Q_ROWS * w


def _natten_kernel(q_ref, k0_ref, k1_ref, k2_ref, k3_ref, v0_ref, v1_ref, v2_ref, v3_ref, z_ref, e_ref,
                   y_ref, qf_ref, of_ref, tabs_ref, sa_ref, sb_ref):
    k_refs = (k0_ref, k1_ref, k2_ref, k3_ref)
    v_refs = (v0_ref, v1_ref, v2_ref, v3_ref)
    piece_rows = SLAB_ROWS // 4
    group = pl.program_id(0)
    n_groups = pl.num_programs(0)

    @pl.when(jnp.logical_and(group == 0, pl.program_id(1) == 0))
    def _():
        for kind, (slab_off, first_valid) in enumerate(_GROUP_KINDS):
            _build_bias(e_ref, tabs_ref.at[kind], slab_off, first_valid)

    tab_ref = tabs_ref.at[(group > 0).astype(jnp.int32) + (group == n_groups - 1).astype(jnp.int32)]

    n_batch, q_tok = q_ref.shape[0], q_ref.shape[2]
    for bi in range(n_batch):
        qf_ref[pl.ds(bi * q_tok, q_tok), :] = q_ref[bi, 0].astype(F32) * (NA_HEAD_DIM ** -0.5 * LOG2E)

    items = []
    for bi in range(n_batch):
        row_off = 0
        for (c0, w), kb in zip(Q_BLOCKS, K_STARTS):
            for pair in range(NA_HEADS // 2):
                items.append((bi, c0, w, kb, pair, row_off))
            row_off += Q_ROWS * w
    s_refs = (sa_ref, sb_ref)

    def slab(refs, bi, kb, lanes):
        return jnp.concatenate(
            [refs[a // piece_rows][bi, 0, pl.ds((a % piece_rows) * GRID_W + kb, K_WIDTH), lanes]
             for a in range(SLAB_ROWS)], axis=0)

    def scores(idx):
        bi, c0, w, kb, pair, row_off = items[idx]
        nq = Q_ROWS * w
        lanes = pl.ds(pair * LANES, LANES)
        first_head = lax.broadcasted_iota(jnp.int32, (nq, LANES), 1) < NA_HEAD_DIM
        qp = jnp.concatenate([qf_ref[pl.ds(bi * q_tok + i * GRID_W + c0, w), lanes] for i in range(Q_ROWS)],
                             axis=0)
        qs = jnp.concatenate([jnp.where(first_head, qp, 0.0), jnp.where(first_head, 0.0, qp)],
                             axis=0).astype(BF16)
        s = lax.dot_general(qs, slab(k_refs, bi, kb, lanes), (((1,), (1,)), ((), ())),
                            preferred_element_type=F32)
        bias = jnp.concatenate([tab_ref[2 * pair, pl.ds(row_off, nq), :],
                                tab_ref[2 * pair + 1, pl.ds(row_off, nq), :]], axis=0)
        s_refs[idx % 2][0:2 * nq, :] = s + bias

    def attend(idx):
        bi, c0, w, kb, pair, _ = items[idx]
        nq = Q_ROWS * w
        lanes = pl.ds(pair * LANES, LANES)
        first_head = lax.broadcasted_iota(jnp.int32, (nq, LANES), 1) < NA_HEAD_DIM
        s = s_refs[idx % 2][0:2 * nq, :]
        p = jnp.exp2(s - jnp.max(s, axis=-1, keepdims=True))
        o2 = jnp.dot(p.astype(BF16), slab(v_refs, bi, kb, lanes), preferred_element_type=F32)
        o2 = o2 * (1.0 / jnp.sum(p, axis=-1, keepdims=True))
        o = jnp.where(first_head, o2[0:nq], o2[nq:2 * nq])
        for i in range(Q_ROWS):
            of_ref[pl.ds(bi * q_tok + i * GRID_W + c0, w), lanes] = o[i * w:(i + 1) * w]

    scores(0)
    for idx in range(len(items)):
        if idx + 1 < len(items):
            scores(idx + 1)
        attend(idx)
    for bi in range(n_batch):
        rows = pl.ds(bi * q_tok, q_tok)
        y_ref[bi, 0] = (of_ref[rows, :] * _silu(z_ref[bi, 0].astype(F32))).astype(BF16)


def _natten(proj3, bias_rows):
    bsz, seq, n_main = proj3.shape
    q_tok = Q_ROWS * GRID_W
    p_tok = (SLAB_ROWS // 4) * GRID_W
    groups = seq // q_tok
    pieces = seq // p_tok
    assert groups >= 2, "needs at least two 8-row query groups"
    nb = 2 if bsz % 2 == 0 else 1
    w = NA_WIDTH
    qb, kb, vb, zb = (5 * D_MODEL) // w, (5 * D_MODEL) // w + 1, (5 * D_MODEL) // w + 2, (5 * D_MODEL) // w + 3
    proj_q = proj3.reshape(bsz, groups, q_tok, n_main)
    proj_p = proj3.reshape(bsz, pieces, p_tok, n_main)

    def slab_start(g):
        return jnp.clip(2 * g - 1, 0, pieces - 4)

    def piece_spec(col, a):
        return pl.BlockSpec((nb, 1, p_tok, w), lambda g, b: (b, slab_start(g) + a, 0, col))

    return pl.pallas_call(
        _natten_kernel,
        out_shape=jax.ShapeDtypeStruct((bsz, groups, q_tok, w), BF16),
        grid=(groups, bsz // nb),
        in_specs=[pl.BlockSpec((nb, 1, q_tok, w), lambda g, b: (b, g, 0, qb))]
        + [piece_spec(kb, a) for a in range(4)] + [piece_spec(vb, a) for a in range(4)]
        + [pl.BlockSpec((nb, 1, q_tok, w), lambda g, b: (b, g, 0, zb)),
           pl.BlockSpec(bias_rows.shape, lambda g, b: (0, 0, 0, 0))],
        out_specs=pl.BlockSpec((nb, 1, q_tok, w), lambda g, b: (b, g, 0, 0)),
        scratch_shapes=[pltpu.VMEM((nb * q_tok, w), F32), pltpu.VMEM((nb * q_tok, w), F32),
                        pltpu.VMEM((len(_GROUP_KINDS), NA_HEADS, q_tok, N_KEYS), F32),
                        pltpu.VMEM((2 * Q_ROWS * Q_BLOCK_MAX, N_KEYS), F32),
                        pltpu.VMEM((2 * Q_ROWS * Q_BLOCK_MAX, N_KEYS), F32)],
        compiler_params=pltpu.CompilerParams(
            dimension_semantics=("arbitrary", "arbitrary"), vmem_limit_bytes=VMEM_LIMIT_BYTES),
        name="natten",
    )(proj_q, *([proj_p] * 8), proj_q, bias_rows).reshape(bsz, seq, w)


def _merge_out_kernel(ya_ref, yb_ref, g0_ref, g1_ref, x_ref, mod_ref, bm_ref, wa_ref, wb_ref, wo_ref, fg_ref,
                      out_ref, m_ref):
    tm, d = x_ref.shape
    sub = min(MERGE_ROWS, tm)
    gate = mod_ref[0, :, 2 * d:3 * d]

    def merge(r):
        rows = pl.ds(r, sub)
        pa = jnp.dot(ya_ref[rows, :], wa_ref[...], preferred_element_type=F32)
        pb = jnp.dot(yb_ref[rows, :], wb_ref[...], preferred_element_type=F32)
        m_ref[rows, :] = (_sigmoid(g0_ref[rows, :].astype(F32) + bm_ref[0:1, :]) * pa
                          + _sigmoid(g1_ref[rows, :].astype(F32) + bm_ref[1:2, :]) * pb).astype(BF16)

    def project(r):
        rows = pl.ds(r, sub)
        mixed = jnp.dot(m_ref[rows, :], wo_ref[...], preferred_element_type=F32)
        xo = x_ref[rows, :] + gate * mixed
        out_ref[rows, :] = xo * lax.rsqrt(jnp.mean(xo * xo, axis=-1, keepdims=True) + EPS) * fg_ref[...]

    starts = list(range(0, tm, sub))
    merge(starts[0])
    for i, r in enumerate(starts):
        if i + 1 < len(starts):
            merge(starts[i + 1])
        project(r)


def _merge_out(ya2, yb2, proj2, x2, mod3, b_merge, wa, wb, wo, final_gain, seq, tm):
    n_tok, d = x2.shape
    g0 = (N_MAIN - 2 * d) // d

    def full(shape):
        return pl.BlockSpec(shape, lambda m: (0,) * len(shape))

    return pl.pallas_call(
        _merge_out_kernel,
        out_shape=jax.ShapeDtypeStruct((n_tok, d), F32),
        grid=(n_tok // tm,),
        in_specs=[
            pl.BlockSpec((tm, d), lambda m: (m, 0)),
            pl.BlockSpec((tm, NA_WIDTH), lambda m: (m, 0)),
            pl.BlockSpec((tm, d), lambda m: (m, g0)),
            pl.BlockSpec((tm, d), lambda m: (m, g0 + 1)),
            pl.BlockSpec((tm, d), lambda m: (m, 0)),
            pl.BlockSpec((1, 1, 3 * d), lambda m: ((m * tm) // seq, 0, 0)),
            full((2, d)), full((d, d)), full((NA_WIDTH, d)), full((d, d)), full((1, d)),
        ],
        out_specs=pl.BlockSpec((tm, d), lambda m: (m, 0)),
        scratch_shapes=[pltpu.VMEM((tm, d), BF16)],
        compiler_params=pltpu.CompilerParams(
            dimension_semantics=("arbitrary",), vmem_limit_bytes=VMEM_LIMIT_BYTES),
        name="merge_out",
    )(ya2, yb2, proj2, proj2, x2, mod3, b_merge, wa, wb, wo, final_gain.reshape(1, d))


def _gate_layout(n_cols):
    src = np.arange(n_cols)
    direction, gate, head = src // (2 * MLSTM_HEADS), (src // MLSTM_HEADS) % 2, src % MLSTM_HEADS
    return 2 * MLSTM_HEADS * gate + MLSTM_HEADS * direction + head


def _weight_prep_kernel(a_ref, b_ref, perm_ref, main_ref, gate_ref, *, gate_block, n_gates):
    n = pl.program_id(0)
    rows = a_ref[0]
    past_gates = jnp.concatenate([rows[n_gates:], b_ref[0]], axis=0)
    main_ref[...] = jnp.where(n >= gate_block, past_gates, rows).T.astype(BF16)

    @pl.when(n == gate_block)
    def _():
        gate_ref[...] = jnp.dot(perm_ref[...], rows[0:n_gates], preferred_element_type=F32).T.astype(BF16)


def _weight_prep(w_in, layer):
    _, d, n_in = w_in.shape
    n_gates = 4 * MLSTM_HEADS
    tn = D_MODEL
    g_lo = 5 * MLSTM_HEADS * MLSTM_HEAD_DIM
    assert n_in == N_MAIN + n_gates and g_lo % tn == 0 and tn % n_gates == 0
    perm = np.zeros((GATE_LANES, n_gates), np.float32)
    perm[_gate_layout(n_gates), np.arange(n_gates)] = 1.0
    w_t = jnp.swapaxes(w_in, 1, 2)
    return pl.pallas_call(
        functools.partial(_weight_prep_kernel, gate_block=g_lo // tn, n_gates=n_gates),
        out_shape=(jax.ShapeDtypeStruct((d, N_MAIN), BF16), jax.ShapeDtypeStruct((d, GATE_LANES), BF16)),
        grid=(N_MAIN // tn,),
        in_specs=[pl.BlockSpec((1, tn, d), lambda n: (layer, n, 0)),
                  pl.BlockSpec((1, n_gates, d), lambda n: (layer, (n + 1) * (tn // n_gates), 0)),
                  pl.BlockSpec((GATE_LANES, n_gates), lambda n: (0, 0))],
        out_specs=(pl.BlockSpec((d, tn), lambda n: (0, n)),
                   pl.BlockSpec((d, GATE_LANES), lambda n: (0, 0))),
        compiler_params=pltpu.CompilerParams(dimension_semantics=("arbitrary",), vmem_limit_bytes=VMEM_LIMIT_BYTES),
        name="weight_prep",
    )(w_t, w_t, jnp.asarray(perm))


def _token_tiles(seq):
    return min(2048, seq), min(1024, seq)


def _layer(x, mod, norm_gain, w_in, layer, conv_w, conv_b, b_igate, b_fgate, mlstm_norm_gain, rpb, w_proj_a, w_proj_b,
           b_merge, w_out, final_gain):
    bsz, seq, d = x.shape
    n_tok = bsz * seq
    wa = MLSTM_HEADS * MLSTM_HEAD_DIM
    w_main, w_gate = _weight_prep(w_in, layer)
    gate_bias = jnp.zeros((1, GATE_LANES), F32).at[0, _gate_layout(4 * MLSTM_HEADS)].set(
        jnp.stack([b_igate, b_fgate], axis=1).reshape(-1))
    tm_in, tm_out = _token_tiles(seq)

    x2 = x.reshape(n_tok, d)
    mod3 = mod.reshape(bsz, 1, 3 * d)
    proj2, gcol2, grow = _in_proj(x2, mod3, norm_gain, w_main, w_gate, gate_bias, seq, tm_in)
    proj3 = proj2.reshape(bsz, seq, N_MAIN)
    gcol = gcol2.reshape(bsz, seq, GATE_LANES)
    y_a = _mlstm(proj3, gcol, grow, conv_w, conv_b, mlstm_norm_gain)
    y_b = _natten(proj3, _natten_bias_rows(rpb))
    out2 = _merge_out(y_a.reshape(n_tok, wa), y_b.reshape(n_tok, NA_WIDTH), proj2, x2, mod3, b_merge,
                      w_proj_a.astype(BF16), w_proj_b.astype(BF16), w_out.astype(BF16), final_gain, seq, tm_out)
    return out2.reshape(bsz, seq, d)


def kernel(x, c, w_ada, b_ada, norm_gain, w_in, conv_w, conv_b, b_igate, b_fgate, mlstm_norm_gain, rpb, w_proj_a,
           w_proj_b, b_merge, w_out, final_gain):
    depth = w_ada.shape[0]
    assert depth == 1, "the fused final norm assumes a single layer"
    mod = _adaln_mod(c, w_ada[0], b_ada[0])
    return _layer(x, mod, norm_gain[0], w_in, 0, conv_w[0], conv_b[0], b_igate[0], b_fgate[0], mlstm_norm_gain[0],
                  rpb[0], w_proj_a[0], w_proj_b[0], b_merge[0], w_out[0], final_gain)
```

```python
import functools
import math

import numpy as np
import jax
import jax.numpy as jnp
from jax import lax
from jax.experimental import pallas as pl
from jax.experimental.pallas import tpu as pltpu

F32 = jnp.float32
BF16 = jnp.bfloat16

D_MODEL = 1024
GRID_W = 64
MLSTM_HEADS = 4
MLSTM_HEAD_DIM = 256
NA_HEADS = 8
NA_WIDTH = 512
NA_HEAD_DIM = 64
NA_ROWS = 8
NA_COLS = 16
EPS = 1e-6
NEG = -1e30
LOG2E = 1.4426950408889634
K_SCALE_LOG2 = -0.5 * math.log2(MLSTM_HEAD_DIM)

VMEM_LIMIT_BYTES = 56 * 1024 * 1024
LANES = 128

CHUNK = 256

MERGE_ROWS = 512

N_MAIN = 9 * D_MODEL
GATE_LANES = 128

Q_ROWS = 8
SLAB_ROWS = 16
Q_BLOCKS = ((0, 24), (24, 16), (40, 24))
Q_BLOCK_MAX = 24
K_STARTS = (0, 16, 32)
K_WIDTH = 32
N_KEYS = SLAB_ROWS * K_WIDTH


def _sigmoid(x):
    return 1.0 / (1.0 + jnp.exp2(x * -LOG2E))


def _silu(x):
    return x * _sigmoid(x)


def _adaln_kernel(c_ref, w_ref, b_ref, o_ref):
    cond = _silu(c_ref[...]).astype(BF16)
    o_ref[...] = jnp.dot(cond, w_ref[...].astype(BF16), preferred_element_type=F32) + b_ref[...]


def _adaln_mod(c, w_ada, b_ada):
    bsz, d = c.shape
    n = w_ada.shape[1]
    tn = 512
    return pl.pallas_call(
        _adaln_kernel,
        out_shape=jax.ShapeDtypeStruct((bsz, n), F32),
        grid=(n // tn,),
        in_specs=[
            pl.BlockSpec((bsz, d), lambda j: (0, 0)),
            pl.BlockSpec((d, tn), lambda j: (0, j)),
            pl.BlockSpec((1, tn), lambda j: (0, j)),
        ],
        out_specs=pl.BlockSpec((bsz, tn), lambda j: (0, j)),
        compiler_params=pltpu.CompilerParams(dimension_semantics=("arbitrary",)),
        name="adaln_mod",
    )(c, w_ada, b_ada.reshape(1, n))


def _gate_quantities(gates, pos_in_chunk):
    v = gates.T
    i_gate, f_gate = v[0:8, :] * LOG2E, v[8:16, :]
    n_tok = v.shape[1]
    is_bwd = lax.broadcasted_iota(jnp.int32, i_gate.shape, 0) >= MLSTM_HEADS

    def scan(x, combine, identity):
        pre, suf = x, x
        step = 1
        while step < CHUNK:
            pre = combine(pre, jnp.where(pos_in_chunk >= step, pltpu.roll(pre, step, 1), identity))
            suf = combine(suf, jnp.where(pos_in_chunk < CHUNK - step, pltpu.roll(suf, n_tok - step, 1), identity))
            step *= 2
        return jnp.where(is_bwd, suf, pre)

    b = scan((jnp.minimum(f_gate, 0.0) - jnp.log1p(jnp.exp(-jnp.abs(f_gate)))) * LOG2E, jnp.add, 0.0)
    r = i_gate - b
    cm = scan(r, jnp.maximum, NEG)
    cols = jnp.concatenate([cm, b, jnp.zeros((GATE_LANES - 16, n_tok), F32)], axis=0).T
    return cols, r


def _in_proj_kernel(x_ref, mod_ref, gain_ref, w_ref, wg_ref, gbias_ref, proj_ref, gcol_ref, grow_ref, h_ref, *, rows):
    tm, d = x_ref.shape
    n = pl.program_id(1)

    @pl.when(n == 0)
    def _():
        shift = mod_ref[0, :, 0:d]
        scale = mod_ref[0, :, d:2 * d]
        gain = gain_ref[...]
        pos = lax.broadcasted_iota(jnp.int32, (8, rows), 1) % CHUNK

        def normalize(r):
            xf = x_ref[r:r + rows, :]
            y = xf * lax.rsqrt(jnp.mean(xf * xf, axis=-1, keepdims=True) + EPS) * gain
            h_ref[r:r + rows, :] = (y * (1.0 + scale) + shift).astype(BF16)

        def project(r):
            hb = h_ref[r:r + rows, :]
            gates = jnp.dot(hb, wg_ref[...], preferred_element_type=F32) + gbias_ref[...]
            proj_ref[r:r + rows, :] = jnp.dot(hb, w_ref[...], preferred_element_type=F32).astype(BF16)
            return gates

        def gate_scans(r, gates):
            cols, r_rows = _gate_quantities(gates, pos)
            gcol_ref[r:r + rows, :] = cols
            for c in range(rows // CHUNK):
                grow_ref[0, r // CHUNK + c] = r_rows[:, c * CHUNK:(c + 1) * CHUNK]

        starts = list(range(0, tm, rows))
        normalize(starts[0])
        pending = None
        for i, r in enumerate(starts):
            if i + 1 < len(starts):
                normalize(starts[i + 1])
            gates = project(r)
            if pending is not None:
                gate_scans(*pending)
            pending = (r, gates)
        gate_scans(*pending)

    @pl.when(n > 0)
    def _():
        proj_ref[...] = jnp.dot(h_ref[...], w_ref[...], preferred_element_type=F32).astype(BF16)


def _in_proj(x2, mod3, norm_gain, w_main, w_gate, gate_bias, seq, tm):
    n_tok, d = x2.shape
    tn = N_MAIN // 6
    rows = min(MERGE_ROWS, tm)
    assert rows % CHUNK == 0 and seq % tm == 0
    tiles_per_seq = seq // tm
    return pl.pallas_call(
        functools.partial(_in_proj_kernel, rows=rows),
        out_shape=(
            jax.ShapeDtypeStruct((n_tok, N_MAIN), BF16),
            jax.ShapeDtypeStruct((n_tok, GATE_LANES), F32),
            jax.ShapeDtypeStruct((n_tok // seq, seq // CHUNK, 2 * MLSTM_HEADS, CHUNK), F32),
        ),
        grid=(n_tok // tm, N_MAIN // tn),
        in_specs=[
            pl.BlockSpec((tm, d), lambda m, n: (m, 0)),
            pl.BlockSpec((1, 1, 3 * d), lambda m, n: ((m * tm) // seq, 0, 0)),
            pl.BlockSpec((1, d), lambda m, n: (0, 0)),
            pl.BlockSpec((d, tn), lambda m, n: (0, n)),
            pl.BlockSpec((d, GATE_LANES), lambda m, n: (0, 0)),
            pl.BlockSpec((1, GATE_LANES), lambda m, n: (0, 0)),
        ],
        out_specs=(
            pl.BlockSpec((tm, tn), lambda m, n: (m, n)),
            pl.BlockSpec((tm, GATE_LANES), lambda m, n: (m, 0)),
            pl.BlockSpec((1, tm // CHUNK, 2 * MLSTM_HEADS, CHUNK),
                         lambda m, n: (m // tiles_per_seq, m % tiles_per_seq, 0, 0)),
        ),
        scratch_shapes=[pltpu.VMEM((tm, d), BF16)],
        compiler_params=pltpu.CompilerParams(
            dimension_semantics=("arbitrary", "arbitrary"), vmem_limit_bytes=VMEM_LIMIT_BYTES),
        name="in_proj",
    )(x2, mod3, norm_gain.reshape(1, d), w_main, w_gate, gate_bias)


def _mlstm_chunk(qc, kt, qk_ref, reuse_qk, vv, r_row, cm_rep, b_rep, cm_end, g, cn_ref, m_ref, causal_mask):
    d = vv.shape[1]
    cn = cn_ref[...]
    m_state = m_ref[...]
    m_col = jnp.maximum(m_state, cm_rep)
    w_inter = jnp.exp2(m_state - m_col)
    m_wide = jnp.concatenate([m_col, m_col], axis=1)
    d_mat = jnp.where(causal_mask, jnp.exp2(r_row - m_wide), 0.0)
    if reuse_qk:
        qk = qk_ref[...]
    else:
        qk = jnp.dot(qc, kt, preferred_element_type=F32)
        qk_ref[...] = qk
    scores = qk * d_mat
    inter = jnp.dot(qc, cn.astype(BF16), preferred_element_type=F32)
    yield None
    w_wide = jnp.concatenate([w_inter, w_inter], axis=1)
    num = jnp.dot(scores.astype(BF16), vv, preferred_element_type=F32) + w_wide * inter[:, 0:d]
    den = jnp.sum(scores, axis=-1, keepdims=True) + w_inter * inter[:, d:]
    inv = 1.0 / jnp.maximum(jnp.abs(den), jnp.exp2(-(b_rep + m_col)))
    yield num * jnp.concatenate([inv, inv], axis=1)
    m_end = jnp.maximum(m_state, cm_end)
    w_row = jnp.exp2(r_row - jnp.concatenate([m_end, m_end], axis=1))
    decay = jnp.exp2(m_state - m_end)
    kw = kt * w_row.astype(BF16)
    v_aug = jnp.concatenate([vv, jnp.ones((vv.shape[0], LANES), BF16)], axis=1)
    decay_wide = jnp.concatenate([decay, decay, decay], axis=1)
    cn_ref[...] = decay_wide * cn + jnp.dot(kw, v_aug, preferred_element_type=F32)
    m_ref[...] = g + m_end
    yield None


def _mlstm_kernel(q_ref, k_ref, v_ref, o_ref, z_ref, gcol_ref, grow_ref, cwq_ref, cwk_ref, cbq_ref, cbk_ref,
                  gain_ref, y_ref, qc_ref, kt_ref, qk_ref, h_ref, cnf_ref, mf_ref, cnb_ref, mb_ref):
    seq, d = q_ref.shape[1], q_ref.shape[2]
    nc = seq // CHUNK
    head = pl.program_id(1)

    t_idx = lax.broadcasted_iota(jnp.int32, (CHUNK, CHUNK), 0)
    s_idx = lax.broadcasted_iota(jnp.int32, (CHUNK, CHUNK), 1)
    taps = jnp.concatenate([(s_idx == t_idx - 1).astype(BF16), (s_idx == t_idx + 1).astype(BF16)],
                           axis=0)
    edge = lax.broadcasted_iota(jnp.int32, (8, d), 0)

    def conv_silu(src_ref, w_ref, b_ref, c, r):
        w0, w1, w2 = 0.5 * w_ref[0:1, :], 0.5 * w_ref[1:2, :], 0.5 * w_ref[2:3, :]
        src = src_ref[0, pl.ds(r, CHUNK), :]
        p = jnp.dot(taps, src, preferred_element_type=F32)
        t = p[0:CHUNK] * w0 + src.astype(F32) * w1 + p[CHUNK:2 * CHUNK] * w2 + 0.5 * b_ref[...]
        above = src_ref[0, pl.ds(pl.multiple_of(jnp.maximum(r - 16, 0), 16), 16), :][15:16, :].astype(F32)
        below = src_ref[0, pl.ds(pl.multiple_of(jnp.minimum(r + CHUNK, seq - 16), 16), 16), :][0:1, :].astype(F32)
        above = above * jnp.where(c > 0, w0, 0.0)
        below = below * jnp.where(c < nc - 1, w2, 0.0)
        t = jnp.concatenate([t[0:8] + jnp.where(edge == 0, above, 0.0), t[8:CHUNK - 8],
                             t[CHUNK - 8:CHUNK] + jnp.where(edge == 7, below, 0.0)], axis=0)
        return t * (1.0 + jnp.tanh(t))

    def conv_chunk(c, carry):
        r = pl.multiple_of(c * CHUNK, CHUNK)
        qc_ref[pl.ds(r, CHUNK), :] = conv_silu(q_ref, cwq_ref, cbq_ref, c, r).astype(BF16)
        kt_ref[c] = conv_silu(k_ref, cwk_ref, cbk_ref, c, r).astype(BF16).T
        return carry

    lax.fori_loop(0, nc, conv_chunk, 0, unroll=8)

    gate_row = lax.broadcasted_iota(jnp.int32, (2 * MLSTM_HEADS, CHUNK), 0)
    gate_lane = lax.broadcasted_iota(jnp.int32, (CHUNK, GATE_LANES), 1)

    for ref in (cnf_ref, mf_ref, cnb_ref, mb_ref):
        ref[...] = jnp.zeros(ref.shape, F32)

    mask_f = s_idx <= t_idx
    mask_b = s_idx >= t_idx

    def replicate(cols, lane):
        picked = jnp.sum(jnp.where(gate_lane == lane, cols, 0.0), axis=1, keepdims=True)
        return jnp.broadcast_to(picked, (cols.shape[0], LANES))

    def direction(chunk, backward, reuse_qk):
        r = pl.multiple_of(chunk * CHUNK, CHUNK)
        rows = grow_ref[0, chunk]
        cols = gcol_ref[0, pl.ds(r, CHUNK), :]
        qc = qc_ref[pl.ds(r, CHUNK), :]
        kt = kt_ref[chunk]
        vv = v_ref[0, pl.ds(r, CHUNK), :]
        if backward:
            cm_rep, b_rep = replicate(cols, MLSTM_HEADS + head), replicate(cols, 3 * MLSTM_HEADS + head)
            end = 0
            state, mask, this_row = (cnb_ref, mb_ref), mask_b, MLSTM_HEADS + head
        else:
            cm_rep, b_rep = replicate(cols, head), replicate(cols, 2 * MLSTM_HEADS + head)
            end = CHUNK - 1
            state, mask, this_row = (cnf_ref, mf_ref), mask_f, head
        r_row = jnp.sum(jnp.where(gate_row == this_row, rows, 0.0), axis=0, keepdims=True) + K_SCALE_LOG2
        cm_end, g = cm_rep[end:end + 1, :], b_rep[end:end + 1, :]
        return _mlstm_chunk(qc, kt, qk_ref.at[chunk], reuse_qk, vv, r_row, cm_rep, b_rep, cm_end, g, *state, mask), r

    def chunk_steps(j, second_visit):
        steps = [direction(j, False, second_visit), direction(nc - 1 - j, True, second_visit)]
        if second_visit:
            for gen, _ in steps:
                next(gen)
            hs = [next(gen) for gen, _ in steps]
            for gen, _ in steps:
                next(gen)
        else:
            hs = []
            for gen, _ in steps:
                next(gen)
                hs.append(next(gen))
                next(gen)
        return [(h, r) for h, (_, r) in zip(hs, steps)]

    def first_half(j, carry):
        for h_dir, r in chunk_steps(j, second_visit=False):
            h_ref[pl.ds(r, CHUNK), :] = h_dir
        return carry

    gain = gain_ref[...]

    def finish(h_dir, r):
        h_sum = (h_ref[pl.ds(r, CHUNK), :] + h_dir) * _sigmoid(o_ref[0, pl.ds(r, CHUNK), :].astype(F32))
        normed = h_sum * lax.rsqrt(jnp.mean(h_sum * h_sum, axis=-1, keepdims=True) + EPS) * gain
        y_ref[0, pl.ds(r, CHUNK), :] = (normed * _silu(z_ref[0, pl.ds(r, CHUNK), :].astype(F32))).astype(BF16)

    def second_half(j, carry):
        for h_dir, r in chunk_steps(j, second_visit=True):
            finish(h_dir, r)
        return carry

    lax.fori_loop(0, nc // 2, first_half, 0, unroll=2)
    lax.fori_loop(nc // 2, nc, second_half, 0)


def _mlstm(proj3, gcol, grow, conv_w, conv_b, head_gain):
    bsz, seq, _ = proj3.shape
    d = MLSTM_HEAD_DIM
    nh = MLSTM_HEADS
    nc = seq // CHUNK

    def col_block(offset):
        return pl.BlockSpec((1, seq, d), lambda b, h: (b, 0, offset + h))

    return pl.pallas_call(
        _mlstm_kernel,
        out_shape=jax.ShapeDtypeStruct((bsz, seq, nh * d), BF16),
        grid=(bsz, nh),
        in_specs=[
            col_block(0), col_block(nh), col_block(2 * nh), col_block(3 * nh), col_block(4 * nh),
            pl.BlockSpec((1, seq, GATE_LANES), lambda b, h: (b, 0, 0)),
            pl.BlockSpec((1, nc, 2 * MLSTM_HEADS, CHUNK), lambda b, h: (b, 0, 0, 0)),
            pl.BlockSpec((3, d), lambda b, h: (0, h)),
            pl.BlockSpec((3, d), lambda b, h: (0, nh + h)),
            pl.BlockSpec((1, d), lambda b, h: (0, h)),
            pl.BlockSpec((1, d), lambda b, h: (0, nh + h)),
            pl.BlockSpec((1, d), lambda b, h: (0, h)),
        ],
        out_specs=pl.BlockSpec((1, seq, d), lambda b, h: (b, 0, h)),
        scratch_shapes=[
            pltpu.VMEM((seq, d), BF16),
            pltpu.VMEM((nc, d, CHUNK), BF16),
            pltpu.VMEM((nc, CHUNK, CHUNK), F32),
            pltpu.VMEM((seq, d), F32),
            pltpu.VMEM((d, d + LANES), F32), pltpu.VMEM((1, LANES), F32),
            pltpu.VMEM((d, d + LANES), F32), pltpu.VMEM((1, LANES), F32),
        ],
        compiler_params=pltpu.CompilerParams(
            dimension_semantics=("arbitrary", "arbitrary"), vmem_limit_bytes=VMEM_LIMIT_BYTES),
        name="mlstm",
    )(proj3, proj3, proj3, proj3, proj3, gcol, grow, conv_w, conv_w,
      conv_b.reshape(1, -1), conv_b.reshape(1, -1), head_gain.reshape(1, -1))


def _natten_bias_rows(rpb):
    n_dc = 2 * NA_COLS - 1
    n_dr = 2 * NA_ROWS - 1
    left = NA_COLS
    period = 64
    padded = jnp.pad(rpb.astype(F32) * LOG2E, ((0, 0), (0, 0), (left, period - left - n_dc)), constant_values=NEG)
    out = []
    for (c0, w), kb in zip(Q_BLOCKS, K_STARTS):
        base = kb - c0 + NA_COLS - 1 + left
        toe = jnp.tile(padded, (1, 1, w))[..., :(period - 1) * w].reshape(NA_HEADS, n_dr, w, period - 1)
        toe = toe[..., base:base + K_WIDTH]
        q_col = c0 + np.arange(w)[:, None]
        k_col = kb + np.arange(K_WIDTH)[None, :]
        win = np.clip(q_col - NA_COLS // 2, 0, GRID_W - NA_COLS)
        col_ok = jnp.asarray((k_col >= win) & (k_col < win + NA_COLS))
        toe = jnp.where(col_ok[None, None], toe, NEG).transpose(0, 2, 1, 3).reshape(NA_HEADS, w, n_dr * K_WIDTH)
        out.append(jnp.pad(toe, ((0, 0), (0, Q_BLOCK_MAX - w), (0, N_KEYS - n_dr * K_WIDTH)), constant_values=NEG))
    return jnp.stack(out, axis=1)


_GROUP_KINDS = (
    (0, lambda i: max(i - NA_ROWS // 2, 0)),
    (-(NA_ROWS // 2), lambda i: i),
    (Q_ROWS - SLAB_ROWS, lambda i: min(i + NA_ROWS // 2, NA_ROWS)),
)


def _build_bias(e_ref, tab_ref, slab_off, first_valid):
    for h in range(NA_HEADS):
        row_off = 0
        for blk, (_, w) in enumerate(Q_BLOCKS):
            e = e_ref[h, blk, 0:w, :]
            lane = lax.broadcasted_iota(jnp.int32, (w, N_KEYS), 1)
            for i in range(Q_ROWS):
                shift = slab_off - i + NA_ROWS - 1
                amount = (-shift * K_WIDTH) % N_KEYS
                moved = e if amount == 0 else pltpu.roll(e, amount, 1)
                lo = first_valid(i) * K_WIDTH
                ok = (lane >= lo) & (lane < lo + NA_ROWS * K_WIDTH)
                tab_ref[h, row_off + i * w:row_off + (i + 1) * w, :] = jnp.where(ok, moved, NEG)
            row_off += Q_ROWS * w


def _natten_kernel(q_ref, k0_ref, k1_ref, k2_ref, k3_ref, v0_ref, v1_ref, v2_ref, v3_ref, z_ref, e_ref,
                   y_ref, qf_ref, of_ref, tabs_ref, sa_ref, sb_ref):
    k_refs = (k0_ref, k1_ref, k2_ref, k3_ref)
    v_refs = (v0_ref, v1_ref, v2_ref, v3_ref)
    piece_rows = SLAB_ROWS // 4
    group = pl.program_id(0)
    n_groups = pl.num_programs(0)

    @pl.when(jnp.logical_and(group == 0, pl.program_id(1) == 0))
    def _():
        for kind, (slab_off, first_valid) in enumerate(_GROUP_KINDS):
            _build_bias(e_ref, tabs_ref.at[kind], slab_off, first_valid)

    tab_ref = tabs_ref.at[(group > 0).astype(jnp.int32) + (group == n_groups - 1).astype(jnp.int32)]

    n_batch, q_tok = q_ref.shape[0], q_ref.shape[2]
    for bi in range(n_batch):
        qf_ref[pl.ds(bi * q_tok, q_tok), :] = q_ref[bi, 0].astype(F32) * (NA_HEAD_DIM ** -0.5 * LOG2E)

    items = []
    for bi in range(n_batch):
        row_off = 0
        for (c0, w), kb in zip(Q_BLOCKS, K_STARTS):
            for pair in range(NA_HEADS // 2):
                items.append((bi, c0, w, kb, pair, row_off))
            row_off += Q_ROWS * w
    s_refs = (sa_ref, sb_ref)

    def slab(refs, bi, kb, lanes):
        return jnp.concatenate(
            [refs[a // piece_rows][bi, 0, pl.ds((a % piece_rows) * GRID_W + kb, K_WIDTH), lanes]
             for a in range(SLAB_ROWS)], axis=0)

    def scores(idx):
        bi, c0, w, kb, pair, row_off = items[idx]
        nq = Q_ROWS * w
        lanes = pl.ds(pair * LANES, LANES)
        first_head = lax.broadcasted_iota(jnp.int32, (nq, LANES), 1) < NA_HEAD_DIM
        qp = jnp.concatenate([qf_ref[pl.ds(bi * q_tok + i * GRID_W + c0, w), lanes] for i in range(Q_ROWS)],
                             axis=0)
        qs = jnp.concatenate([jnp.where(first_head, qp, 0.0), jnp.where(first_head, 0.0, qp)],
                             axis=0).astype(BF16)
        s = lax.dot_general(qs, slab(k_refs, bi, kb, lanes), (((1,), (1,)), ((), ())),
                            preferred_element_type=F32)
        bias = jnp.concatenate([tab_ref[2 * pair, pl.ds(row_off, nq), :],
                                tab_ref[2 * pair + 1, pl.ds(row_off, nq), :]], axis=0)
        s_refs[idx % 2][0:2 * nq, :] = s + bias

    def attend(idx):
        bi, c0, w, kb, pair, _ = items[idx]
        nq = Q_ROWS * w
        lanes = pl.ds(pair * LANES, LANES)
        first_head = lax.broadcasted_iota(jnp.int32, (nq, LANES), 1) < NA_HEAD_DIM
        s = s_refs[idx % 2][0:2 * nq, :]
        p = jnp.exp2(s - jnp.max(s, axis=-1, keepdims=True))
        o2 = jnp.dot(p.astype(BF16), slab(v_refs, bi, kb, lanes), preferred_element_type=F32)
        o2 = o2 * (1.0 / jnp.sum(p, axis=-1, keepdims=True))
        o = jnp.where(first_head, o2[0:nq], o2[nq:2 * nq])
        for i in range(Q_ROWS):
            of_ref[pl.ds(bi * q_tok + i * GRID_W + c0, w), lanes] = o[i * w:(i + 1) * w]

    scores(0)
    for idx in range(len(items)):
        if idx + 1 < len(items):
            scores(idx + 1)
        attend(idx)
    for bi in range(n_batch):
        rows = pl.ds(bi * q_tok, q_tok)
        y_ref[bi, 0] = (of_ref[rows, :] * _silu(z_ref[bi, 0].astype(F32))).astype(BF16)


def _natten(proj3, bias_rows):
    bsz, seq, n_main = proj3.shape
    q_tok = Q_ROWS * GRID_W
    p_tok = (SLAB_ROWS // 4) * GRID_W
    groups = seq // q_tok
    pieces = seq // p_tok
    assert groups >= 2, "needs at least two 8-row query groups"
    nb = 2 if bsz % 2 == 0 else 1
    w = NA_WIDTH
    qb, kb, vb, zb = (5 * D_MODEL) // w, (5 * D_MODEL) // w + 1, (5 * D_MODEL) // w + 2, (5 * D_MODEL) // w + 3
    proj_q = proj3.reshape(bsz, groups, q_tok, n_main)
    proj_p = proj3.reshape(bsz, pieces, p_tok, n_main)

    def slab_start(g):
        return jnp.clip(2 * g - 1, 0, pieces - 4)

    def piece_spec(col, a):
        return pl.BlockSpec((nb, 1, p_tok, w), lambda g, b: (b, slab_start(g) + a, 0, col))

    return pl.pallas_call(
        _natten_kernel,
        out_shape=jax.ShapeDtypeStruct((bsz, groups, q_tok, w), BF16),
        grid=(groups, bsz // nb),
        in_specs=[pl.BlockSpec((nb, 1, q_tok, w), lambda g, b: (b, g, 0, qb))]
        + [piece_spec(kb, a) for a in range(4)] + [piece_spec(vb, a) for a in range(4)]
        + [pl.BlockSpec((nb, 1, q_tok, w), lambda g, b: (b, g, 0, zb)),
           pl.BlockSpec(bias_rows.shape, lambda g, b: (0, 0, 0, 0))],
        out_specs=pl.BlockSpec((nb, 1, q_tok, w), lambda g, b: (b, g, 0, 0)),
        scratch_shapes=[pltpu.VMEM((nb * q_tok, w), F32), pltpu.VMEM((nb * q_tok, w), F32),
                        pltpu.VMEM((len(_GROUP_KINDS), NA_HEADS, q_tok, N_KEYS), F32),
                        pltpu.VMEM((2 * Q_ROWS * Q_BLOCK_MAX, N_KEYS), F32),
                        pltpu.VMEM((2 * Q_ROWS * Q_BLOCK_MAX, N_KEYS), F32)],
        compiler_params=pltpu.CompilerParams(
            dimension_semantics=("arbitrary", "arbitrary"), vmem_limit_bytes=VMEM_LIMIT_BYTES),
        name="natten",
    )(proj_q, *([proj_p] * 8), proj_q, bias_rows).reshape(bsz, seq, w)


def _merge_out_kernel(ya_ref, yb_ref, g0_ref, g1_ref, x_ref, mod_ref, bm_ref, wa_ref, wb_ref, wo_ref, fg_ref,
                      out_ref, m_ref):
    tm, d = x_ref.shape
    sub = min(MERGE_ROWS, tm)
    gate = mod_ref[0, :, 2 * d:3 * d]

    def merge(r):
        rows = pl.ds(r, sub)
        pa = jnp.dot(ya_ref[rows, :], wa_ref[...], preferred_element_type=F32)
        pb = jnp.dot(yb_ref[rows, :], wb_ref[...], preferred_element_type=F32)
        m_ref[rows, :] = (_sigmoid(g0_ref[rows, :].astype(F32) + bm_ref[0:1, :]) * pa
                          + _sigmoid(g1_ref[rows, :].astype(F32) + bm_ref[1:2, :]) * pb).astype(BF16)

    def project(r):
        rows = pl.ds(r, sub)
        mixed = jnp.dot(m_ref[rows, :], wo_ref[...], preferred_element_type=F32)
        xo = x_ref[rows, :] + gate * mixed
        out_ref[rows, :] = xo * lax.rsqrt(jnp.mean(xo * xo, axis=-1, keepdims=True) + EPS) * fg_ref[...]

    starts = list(range(0, tm, sub))
    merge(starts[0])
    for i, r in enumerate(starts):
        if i + 1 < len(starts):
            merge(starts[i + 1])
        project(r)


def _merge_out(ya2, yb2, proj2, x2, mod3, b_merge, wa, wb, wo, final_gain, seq, tm):
    n_tok, d = x2.shape
    g0 = (N_MAIN - 2 * d) // d

    def full(shape):
        return pl.BlockSpec(shape, lambda m: (0,) * len(shape))

    return pl.pallas_call(
        _merge_out_kernel,
        out_shape=jax.ShapeDtypeStruct((n_tok, d), F32),
        grid=(n_tok // tm,),
        in_specs=[
            pl.BlockSpec((tm, d), lambda m: (m, 0)),
            pl.BlockSpec((tm, NA_WIDTH), lambda m: (m, 0)),
            pl.BlockSpec((tm, d), lambda m: (m, g0)),
            pl.BlockSpec((tm, d), lambda m: (m, g0 + 1)),
            pl.BlockSpec((tm, d), lambda m: (m, 0)),
            pl.BlockSpec((1, 1, 3 * d), lambda m: ((m * tm) // seq, 0, 0)),
            full((2, d)), full((d, d)), full((NA_WIDTH, d)), full((d, d)), full((1, d)),
        ],
        out_specs=pl.BlockSpec((tm, d), lambda m: (m, 0)),
        scratch_shapes=[pltpu.VMEM((tm, d), BF16)],
        compiler_params=pltpu.CompilerParams(
            dimension_semantics=("arbitrary",), vmem_limit_bytes=VMEM_LIMIT_BYTES),
        name="merge_out",
    )(ya2, yb2, proj2, proj2, x2, mod3, b_merge, wa, wb, wo, final_gain.reshape(1, d))


def _gate_layout(n_cols):
    src = np.arange(n_cols)
    direction, gate, head = src // (2 * MLSTM_HEADS), (src // MLSTM_HEADS) % 2, src % MLSTM_HEADS
    return 2 * MLSTM_HEADS * gate + MLSTM_HEADS * direction + head


def _weight_prep_kernel(a_ref, b_ref, perm_ref, main_ref, gate_ref, *, gate_block, n_gates):
    n = pl.program_id(0)
    rows = a_ref[0]
    past_gates = jnp.concatenate([rows[n_gates:], b_ref[0]], axis=0)
    main_ref[...] = jnp.where(n >= gate_block, past_gates, rows).T.astype(BF16)

    @pl.when(n == gate_block)
    def _():
        gate_ref[...] = jnp.dot(perm_ref[...], rows[0:n_gates], preferred_element_type=F32).T.astype(BF16)


def _weight_prep(w_in, layer):
    _, d, n_in = w_in.shape
    n_gates = 4 * MLSTM_HEADS
    tn = D_MODEL
    g_lo = 5 * MLSTM_HEADS * MLSTM_HEAD_DIM
    assert n_in == N_MAIN + n_gates and g_lo % tn == 0 and tn % n_gates == 0
    perm = np.zeros((GATE_LANES, n_gates), np.float32)
    perm[_gate_layout(n_gates), np.arange(n_gates)] = 1.0
    w_t = jnp.swapaxes(w_in, 1, 2)
    return pl.pallas_call(
        functools.partial(_weight_prep_kernel, gate_block=g_lo // tn, n_gates=n_gates),
        out_shape=(jax.ShapeDtypeStruct((d, N_MAIN), BF16), jax.ShapeDtypeStruct((d, GATE_LANES), BF16)),
        grid=(N_MAIN // tn,),
        in_specs=[pl.BlockSpec((1, tn, d), lambda n: (layer, n, 0)),
                  pl.BlockSpec((1, n_gates, d), lambda n: (layer, (n + 1) * (tn // n_gates), 0)),
                  pl.BlockSpec((GATE_LANES, n_gates), lambda n: (0, 0))],
        out_specs=(pl.BlockSpec((d, tn), lambda n: (0, n)),
                   pl.BlockSpec((d, GATE_LANES), lambda n: (0, 0))),
        compiler_params=pltpu.CompilerParams(dimension_semantics=("arbitrary",), vmem_limit_bytes=VMEM_LIMIT_BYTES),
        name="weight_prep",
    )(w_t, w_t, jnp.asarray(perm))


def _token_tiles(seq):
    return min(2048, seq), min(1024, seq)


def _layer(x, mod, norm_gain, w_in, layer, conv_w, conv_b, b_igate, b_fgate, mlstm_norm_gain, rpb, w_proj_a, w_proj_b,
           b_merge, w_out, final_gain):
    bsz, seq, d = x.shape
    n_tok = bsz * seq
    wa = MLSTM_HEADS * MLSTM_HEAD_DIM
    w_main, w_gate = _weight_prep(w_in, layer)
    gate_bias = jnp.zeros((1, GATE_LANES), F32).at[0, _gate_layout(4 * MLSTM_HEADS)].set(
        jnp.stack([b_igate, b_fgate], axis=1).reshape(-1))
    tm_in, tm_out = _token_tiles(seq)

    x2 = x.reshape(n_tok, d)
    mod3 = mod.reshape(bsz, 1, 3 * d)
    proj2, gcol2, grow = _in_proj(x2, mod3, norm_gain, w_main, w_gate, gate_bias, seq, tm_in)
    proj3 = proj2.reshape(bsz, seq, N_MAIN)
    gcol = gcol2.reshape(bsz, seq, GATE_LANES)
    y_a = _mlstm(proj3, gcol, grow, conv_w, conv_b, mlstm_norm_gain)
    y_b = _natten(proj3, _natten_bias_rows(rpb))
    out2 = _merge_out(y_a.reshape(n_tok, wa), y_b.reshape(n_tok, NA_WIDTH), proj2, x2, mod3, b_merge,
                      w_proj_a.astype(BF16), w_proj_b.astype(BF16), w_out.astype(BF16), final_gain, seq, tm_out)
    return out2.reshape(bsz, seq, d)


def kernel(x, c, w_ada, b_ada, norm_gain, w_in, conv_w, conv_b, b_igate, b_fgate, mlstm_norm_gain, rpb, w_proj_a,
           w_proj_b, b_merge, w_out, final_gain):
    depth = w_ada.shape[0]
    assert depth == 1, "the fused final norm assumes a single layer"
    mod = _adaln_mod(c, w_ada[0], b_ada[0])
    return _layer(x, mod, norm_gain[0], w_in, 0, conv_w[0], conv_b[0], b_igate[0], b_fgate[0], mlstm_norm_gain[0],
                  rpb[0], w_proj_a[0], w_proj_b[0], b_merge[0], w_out[0], final_gain)
```

```python
import functools
import math

import numpy as np
import jax
import jax.numpy as jnp
from jax import lax
from jax.experimental import pallas as pl
from jax.experimental.pallas import tpu as pltpu

F32 = jnp.float32
BF16 = jnp.bfloat16

D_MODEL = 1024
GRID_W = 64
MLSTM_HEADS = 4
MLSTM_HEAD_DIM = 256
NA_HEADS = 8
NA_WIDTH = 512
NA_HEAD_DIM = 64
NA_ROWS = 8
NA_COLS = 16
EPS = 1e-6
NEG = -1e30
LOG2E = 1.4426950408889634
K_SCALE_LOG2 = -0.5 * math.log2(MLSTM_HEAD_DIM)

VMEM_LIMIT_BYTES = 56 * 1024 * 1024
LANES = 128

CHUNK = 256

MERGE_ROWS = 512

N_MAIN = 9 * D_MODEL
GATE_LANES = 128

Q_ROWS = 8
SLAB_ROWS = 16
Q_BLOCKS = ((0, 24), (24, 16), (40, 24))
Q_BLOCK_MAX = 24
K_STARTS = (0, 16, 32)
K_WIDTH = 32
N_KEYS = SLAB_ROWS * K_WIDTH


def _sigmoid(x):
    return 1.0 / (1.0 + jnp.exp2(x * -LOG2E))


def _silu(x):
    return x * _sigmoid(x)


def _gate_quantities(gates, pos_in_chunk):
    v = gates.T
    i_gate, f_gate = v[0:8, :] * LOG2E, v[8:16, :]
    n_tok = v.shape[1]
    is_bwd = lax.broadcasted_iota(jnp.int32, i_gate.shape, 0) >= MLSTM_HEADS

    def scan(x, combine, identity):
        pre, suf = x, x
        step = 1
        while step < CHUNK:
            pre = combine(pre, jnp.where(pos_in_chunk >= step, pltpu.roll(pre, step, 1), identity))
            suf = combine(suf, jnp.where(pos_in_chunk < CHUNK - step, pltpu.roll(suf, n_tok - step, 1), identity))
            step *= 2
        return jnp.where(is_bwd, suf, pre)

    b = scan((jnp.minimum(f_gate, 0.0) - jnp.log1p(jnp.exp(-jnp.abs(f_gate)))) * LOG2E, jnp.add, 0.0)
    r = i_gate - b
    cm = scan(r, jnp.maximum, NEG)
    cols = jnp.concatenate([cm, b, jnp.zeros((GATE_LANES - 16, n_tok), F32)], axis=0).T
    return cols, r


def _in_proj_kernel(x_ref, mod_ref, gain_ref, w_ref, wg_ref, gbias_ref, proj_ref, gcol_ref, grow_ref, h_ref, *, rows):
    tm, d = x_ref.shape
    n = pl.program_id(1)

    @pl.when(n == 0)
    def _():
        shift = mod_ref[0, :, 0:d]
        scale = mod_ref[0, :, d:2 * d]
        gain = gain_ref[...]
        pos = lax.broadcasted_iota(jnp.int32, (8, rows), 1) % CHUNK

        def normalize(r):
            xf = x_ref[r:r + rows, :]
            y = xf * lax.rsqrt(jnp.mean(xf * xf, axis=-1, keepdims=True) + EPS) * gain
            h_ref[r:r + rows, :] = (y * (1.0 + scale) + shift).astype(BF16)

        def project(r):
            hb = h_ref[r:r + rows, :]
            gates = jnp.dot(hb, wg_ref[...], preferred_element_type=F32) + gbias_ref[...]
            proj_ref[r:r + rows, :] = jnp.dot(hb, w_ref[...], preferred_element_type=F32).astype(BF16)
            return gates

        def gate_scans(r, gates):
            cols, r_rows = _gate_quantities(gates, pos)
            gcol_ref[r:r + rows, :] = cols
            for c in range(rows // CHUNK):
                grow_ref[0, r // CHUNK + c] = r_rows[:, c * CHUNK:(c + 1) * CHUNK]

        starts = list(range(0, tm, rows))
        normalize(starts[0])
        pending = None
        for i, r in enumerate(starts):
            if i + 1 < len(starts):
                normalize(starts[i + 1])
            gates = project(r)
            if pending is not None:
                gate_scans(*pending)
            pending = (r, gates)
        gate_scans(*pending)

    @pl.when(n > 0)
    def _():
        proj_ref[...] = jnp.dot(h_ref[...], w_ref[...], preferred_element_type=F32).astype(BF16)


def _in_proj(x2, mod3, norm_gain, w_main, w_gate, gate_bias, seq, tm):
    n_tok, d = x2.shape
    tn = N_MAIN // 6
    rows = min(MERGE_ROWS, tm)
    assert rows % CHUNK == 0 and seq % tm == 0
    tiles_per_seq = seq // tm
    return pl.pallas_call(
        functools.partial(_in_proj_kernel, rows=rows),
        out_shape=(
            jax.ShapeDtypeStruct((n_tok, N_MAIN), BF16),
            jax.ShapeDtypeStruct((n_tok, GATE_LANES), F32),
            jax.ShapeDtypeStruct((n_tok // seq, seq // CHUNK, 2 * MLSTM_HEADS, CHUNK), F32),
        ),
        grid=(n_tok // tm, N_MAIN // tn),
        in_specs=[
            pl.BlockSpec((tm, d), lambda m, n: (m, 0)),
            pl.BlockSpec((1, 1, 3 * d), lambda m, n: ((m * tm) // seq, 0, 0)),
            pl.BlockSpec((1, d), lambda m, n: (0, 0)),
            pl.BlockSpec((d, tn), lambda m, n: (0, n)),
            pl.BlockSpec((d, GATE_LANES), lambda m, n: (0, 0)),
            pl.BlockSpec((1, GATE_LANES), lambda m, n: (0, 0)),
        ],
        out_specs=(
            pl.BlockSpec((tm, tn), lambda m, n: (m, n)),
            pl.BlockSpec((tm, GATE_LANES), lambda m, n: (m, 0)),
            pl.BlockSpec((1, tm // CHUNK, 2 * MLSTM_HEADS, CHUNK),
                         lambda m, n: (m // tiles_per_seq, m % tiles_per_seq, 0, 0)),
        ),
        scratch_shapes=[pltpu.VMEM((tm, d), BF16)],
        compiler_params=pltpu.CompilerParams(
            dimension_semantics=("arbitrary", "arbitrary"), vmem_limit_bytes=VMEM_LIMIT_BYTES),
        name="in_proj",
    )(x2, mod3, norm_gain.reshape(1, d), w_main, w_gate, gate_bias)


def _mlstm_chunk(qc, kt, qk_ref, reuse_qk, vv, r_row, cm_rep, b_rep, cm_end, g, cn_ref, m_ref, causal_mask):
    d = vv.shape[1]
    cn = cn_ref[...]
    m_state = m_ref[...]
    m_col = jnp.maximum(m_state, cm_rep)
    w_inter = jnp.exp2(m_state - m_col)
    m_wide = jnp.concatenate([m_col, m_col], axis=1)
    d_mat = jnp.where(causal_mask, jnp.exp2(r_row - m_wide), 0.0)
    if reuse_qk:
        qk = qk_ref[...]
    else:
        qk = jnp.dot(qc, kt, preferred_element_type=F32)
        qk_ref[...] = qk
    scores = qk * d_mat
    inter = jnp.dot(qc, cn.astype(BF16), preferred_element_type=F32)
    yield None
    w_wide = jnp.concatenate([w_inter, w_inter], axis=1)
    num = jnp.dot(scores.astype(BF16), vv, preferred_element_type=F32) + w_wide * inter[:, 0:d]
    den = jnp.sum(scores, axis=-1, keepdims=True) + w_inter * inter[:, d:]
    inv = 1.0 / jnp.maximum(jnp.abs(den), jnp.exp2(-(b_rep + m_col)))
    yield num * jnp.concatenate([inv, inv], axis=1)
    m_end = jnp.maximum(m_state, cm_end)
    w_row = jnp.exp2(r_row - jnp.concatenate([m_end, m_end], axis=1))
    decay = jnp.exp2(m_state - m_end)
    kw = kt * w_row.astype(BF16)
    v_aug = jnp.concatenate([vv, jnp.ones((vv.shape[0], LANES), BF16)], axis=1)
    decay_wide = jnp.concatenate([decay, decay, decay], axis=1)
    cn_ref[...] = decay_wide * cn + jnp.dot(kw, v_aug, preferred_element_type=F32)
    m_ref[...] = g + m_end
    yield None


def _mlstm_kernel(q_ref, k_ref, v_ref, o_ref, z_ref, gcol_ref, grow_ref, cwq_ref, cwk_ref, cbq_ref, cbk_ref,
                  gain_ref, y_ref, qc_ref, kt_ref, qk_ref, h_ref, cnf_ref, mf_ref, cnb_ref, mb_ref):
    seq, d = q_ref.shape[1], q_ref.shape[2]
    nc = seq // CHUNK
    head = pl.program_id(1)

    t_idx = lax.broadcasted_iota(jnp.int32, (CHUNK, CHUNK), 0)
    s_idx = lax.broadcasted_iota(jnp.int32, (CHUNK, CHUNK), 1)
    taps = jnp.concatenate([(s_idx == t_idx - 1).astype(BF16), (s_idx == t_idx + 1).astype(BF16)],
                           axis=0)
    edge = lax.broadcasted_iota(jnp.int32, (8, d), 0)

    def conv_silu(src_ref, w_ref, b_ref, c, r):
        w0, w1, w2 = 0.5 * w_ref[0:1, :], 0.5 * w_ref[1:2, :], 0.5 * w_ref[2:3, :]
        src = src_ref[0, pl.ds(r, CHUNK), :]
        p = jnp.dot(taps, src, preferred_element_type=F32)
        t = p[0:CHUNK] * w0 + src.astype(F32) * w1 + p[CHUNK:2 * CHUNK] * w2 + 0.5 * b_ref[...]
        above = src_ref[0, pl.ds(pl.multiple_of(jnp.maximum(r - 16, 0), 16), 16), :][15:16, :].astype(F32)
        below = src_ref[0, pl.ds(pl.multiple_of(jnp.minimum(r + CHUNK, seq - 16), 16), 16), :][0:1, :].astype(F32)
        above = above * jnp.where(c > 0, w0, 0.0)
        below = below * jnp.where(c < nc - 1, w2, 0.0)
        t = jnp.concatenate([t[0:8] + jnp.where(edge == 0, above, 0.0), t[8:CHUNK - 8],
                             t[CHUNK - 8:CHUNK] + jnp.where(edge == 7, below, 0.0)], axis=0)
        return t * (1.0 + jnp.tanh(t))

    def conv_chunk(c, carry):
        r = pl.multiple_of(c * CHUNK, CHUNK)
        qc_ref[pl.ds(r, CHUNK), :] = conv_silu(q_ref, cwq_ref, cbq_ref, c, r).astype(BF16)
        kt_ref[c] = conv_silu(k_ref, cwk_ref, cbk_ref, c, r).astype(BF16).T
        return carry

    lax.fori_loop(0, nc, conv_chunk, 0, unroll=8)

    gate_row = lax.broadcasted_iota(jnp.int32, (2 * MLSTM_HEADS, CHUNK), 0)
    gate_lane = lax.broadcasted_iota(jnp.int32, (CHUNK, GATE_LANES), 1)

    for ref in (cnf_ref, mf_ref, cnb_ref, mb_ref):
        ref[...] = jnp.zeros(ref.shape, F32)

    mask_f = s_idx <= t_idx
    mask_b = s_idx >= t_idx

    def replicate(cols, lane):
        picked = jnp.sum(jnp.where(gate_lane == lane, cols, 0.0), axis=1, keepdims=True)
        return jnp.broadcast_to(picked, (cols.shape[0], LANES))

    def direction(chunk, backward, reuse_qk):
        r = pl.multiple_of(chunk * CHUNK, CHUNK)
        rows = grow_ref[0, chunk]
        cols = gcol_ref[0, pl.ds(r, CHUNK), :]
        qc = qc_ref[pl.ds(r, CHUNK), :]
        kt = kt_ref[chunk]
        vv = v_ref[0, pl.ds(r, CHUNK), :]
        if backward:
            cm_rep, b_rep = replicate(cols, MLSTM_HEADS + head), replicate(cols, 3 * MLSTM_HEADS + head)
            end = 0
            state, mask, this_row = (cnb_ref, mb_ref), mask_b, MLSTM_HEADS + head
        else:
            cm_rep, b_rep = replicate(cols, head), replicate(cols, 2 * MLSTM_HEADS + head)
            end = CHUNK - 1
            state, mask, this_row = (cnf_ref, mf_ref), mask_f, head
        r_row = jnp.sum(jnp.where(gate_row == this_row, rows, 0.0), axis=0, keepdims=True) + K_SCALE_LOG2
        cm_end, g = cm_rep[end:end + 1, :], b_rep[end:end + 1, :]
        return _mlstm_chunk(qc, kt, qk_ref.at[chunk], reuse_qk, vv, r_row, cm_rep, b_rep, cm_end, g, *state, mask), r

    def chunk_steps(j, second_visit):
        steps = [direction(j, False, second_visit), direction(nc - 1 - j, True, second_visit)]
        if second_visit:
            for gen, _ in steps:
                next(gen)
            hs = [next(gen) for gen, _ in steps]
            for gen, _ in steps:
                next(gen)
        else:
            hs = []
            for gen, _ in steps:
                next(gen)
                hs.append(next(gen))
                next(gen)
        return [(h, r) for h, (_, r) in zip(hs, steps)]

    def first_half(j, carry):
        for h_dir, r in chunk_steps(j, second_visit=False):
            h_ref[pl.ds(r, CHUNK), :] = h_dir
        return carry

    gain = gain_ref[...]

    def finish(h_dir, r):
        h_sum = (h_ref[pl.ds(r, CHUNK), :] + h_dir) * _sigmoid(o_ref[0, pl.ds(r, CHUNK), :].astype(F32))
        normed = h_sum * lax.rsqrt(jnp.mean(h_sum * h_sum, axis=-1, keepdims=True) + EPS) * gain
        y_ref[0, pl.ds(r, CHUNK), :] = (normed * _silu(z_ref[0, pl.ds(r, CHUNK), :].astype(F32))).astype(BF16)

    def second_half(j, carry):
        for h_dir, r in chunk_steps(j, second_visit=True):
            finish(h_dir, r)
        return carry

    lax.fori_loop(0, nc // 2, first_half, 0, unroll=2)
    lax.fori_loop(nc // 2, nc, second_half, 0)


def _mlstm(proj3, gcol, grow, conv_w, conv_b, head_gain):
    bsz, seq, _ = proj3.shape
    d = MLSTM_HEAD_DIM
    nh = MLSTM_HEADS
    nc = seq // CHUNK

    def col_block(offset):
        return pl.BlockSpec((1, seq, d), lambda b, h: (b, 0, offset + h))

    return pl.pallas_call(
        _mlstm_kernel,
        out_shape=jax.ShapeDtypeStruct((bsz, seq, nh * d), BF16),
        grid=(bsz, nh),
        in_specs=[
            col_block(0), col_block(nh), col_block(2 * nh), col_block(3 * nh), col_block(4 * nh),
            pl.BlockSpec((1, seq, GATE_LANES), lambda b, h: (b, 0, 0)),
            pl.BlockSpec((1, nc, 2 * MLSTM_HEADS, CHUNK), lambda b, h: (b, 0, 0, 0)),
            pl.BlockSpec((3, d), lambda b, h: (0, h)),
            pl.BlockSpec((3, d), lambda b, h: (0, nh + h)),
            pl.BlockSpec((1, d), lambda b, h: (0, h)),
            pl.BlockSpec((1, d), lambda b, h: (0, nh + h)),
            pl.BlockSpec((1, d), lambda b, h: (0, h)),
        ],
        out_specs=pl.BlockSpec((1, seq, d), lambda b, h: (b, 0, h)),
        scratch_shapes=[
            pltpu.VMEM((seq, d), BF16),
            pltpu.VMEM((nc, d, CHUNK), BF16),
            pltpu.VMEM((nc, CHUNK, CHUNK), F32),
            pltpu.VMEM((seq, d), F32),
            pltpu.VMEM((d, d + LANES), F32), pltpu.VMEM((1, LANES), F32),
            pltpu.VMEM((d, d + LANES), F32), pltpu.VMEM((1, LANES), F32),
        ],
        compiler_params=pltpu.CompilerParams(
            dimension_semantics=("arbitrary", "arbitrary"), vmem_limit_bytes=VMEM_LIMIT_BYTES),
        name="mlstm",
    )(proj3, proj3, proj3, proj3, proj3, gcol, grow, conv_w, conv_w,
      conv_b.reshape(1, -1), conv_b.reshape(1, -1), head_gain.reshape(1, -1))


def _natten_bias_rows(rpb):
    n_dc = 2 * NA_COLS - 1
    n_dr = 2 * NA_ROWS - 1
    left = NA_COLS
    period = 64
    padded = jnp.pad(rpb.astype(F32) * LOG2E, ((0, 0), (0, 0), (left, period - left - n_dc)), constant_values=NEG)
    out = []
    for (c0, w), kb in zip(Q_BLOCKS, K_STARTS):
        base = kb - c0 + NA_COLS - 1 + left
        toe = jnp.tile(padded, (1, 1, w))[..., :(period - 1) * w].reshape(NA_HEADS, n_dr, w, period - 1)
        toe = toe[..., base:base + K_WIDTH]
        q_col = c0 + np.arange(w)[:, None]
        k_col = kb + np.arange(K_WIDTH)[None, :]
        win = np.clip(q_col - NA_COLS // 2, 0, GRID_W - NA_COLS)
        col_ok = jnp.asarray((k_col >= win) & (k_col < win + NA_COLS))
        toe = jnp.where(col_ok[None, None], toe, NEG).transpose(0, 2, 1, 3).reshape(NA_HEADS, w, n_dr * K_WIDTH)
        out.append(jnp.pad(toe, ((0, 0), (0, Q_BLOCK_MAX - w), (0, N_KEYS - n_dr * K_WIDTH)), constant_values=NEG))
    return jnp.stack(out, axis=1)


_GROUP_KINDS = (
    (0, lambda i: max(i - NA_ROWS // 2, 0)),
    (-(NA_ROWS // 2), lambda i: i),
    (Q_ROWS - SLAB_ROWS, lambda i: min(i + NA_ROWS // 2, NA_ROWS)),
)


def _build_bias(e_ref, tab_ref, slab_off, first_valid):
    for h in range(NA_HEADS):
        row_off = 0
        for blk, (_, w) in enumerate(Q_BLOCKS):
            e = e_ref[h, blk, 0:w, :]
            lane = lax.broadcasted_iota(jnp.int32, (w, N_KEYS), 1)
            for i in range(Q_ROWS):
                shift = slab_off - i + NA_ROWS - 1
                amount = (-shift * K_WIDTH) % N_KEYS
                moved = e if amount == 0 else pltpu.roll(e, amount, 1)
                lo = first_valid(i) * K_WIDTH
                ok = (lane >= lo) & (lane < lo + NA_ROWS * K_WIDTH)
                tab_ref[h, row_off + i * w:row_off + (i + 1) * w, :] = jnp.where(ok, moved, NEG)
            row_off += Q_ROWS * w


def _natten_kernel(q_ref, k0_ref, k1_ref, k2_ref, k3_ref, v0_ref, v1_ref, v2_ref, v3_ref, z_ref, e_ref,
                   y_ref, qf_ref, of_ref, tabs_ref, sa_ref, sb_ref):
    k_refs = (k0_ref, k1_ref, k2_ref, k3_ref)
    v_refs = (v0_ref, v1_ref, v2_ref, v3_ref)
    piece_rows = SLAB_ROWS // 4
    group = pl.program_id(0)
    n_groups = pl.num_programs(0)

    @pl.when(jnp.logical_and(group == 0, pl.program_id(1) == 0))
    def _():
        for kind, (slab_off, first_valid) in enumerate(_GROUP_KINDS):
            _build_bias(e_ref, tabs_ref.at[kind], slab_off, first_valid)

    tab_ref = tabs_ref.at[(group > 0).astype(jnp.int32) + (group == n_groups - 1).astype(jnp.int32)]

    n_batch, q_tok = q_ref.shape[0], q_ref.shape[2]
    for bi in range(n_batch):
        qf_ref[pl.ds(bi * q_tok, q_tok), :] = q_ref[bi, 0].astype(F32) * (NA_HEAD_DIM ** -0.5 * LOG2E)

    items = []
    for bi in range(n_batch):
        row_off = 0
        for (c0, w), kb in zip(Q_BLOCKS, K_STARTS):
            for pair in range(NA_HEADS // 2):
                items.append((bi, c0, w, kb, pair, row_off))
            row_off += Q_ROWS * w
    s_refs = (sa_ref, sb_ref)

    def slab(refs, bi, kb, lanes):
        return jnp.concatenate(
            [refs[a // piece_rows][bi, 0, pl.ds((a % piece_rows) * GRID_W + kb, K_WIDTH), lanes]
             for a in range(SLAB_ROWS)], axis=0)

    def scores(idx):
        bi, c0, w, kb, pair, row_off = items[idx]
        nq = Q_ROWS * w
        lanes = pl.ds(pair * LANES, LANES)
        first_head = lax.broadcasted_iota(jnp.int32, (nq, LANES), 1) < NA_HEAD_DIM
        qp = jnp.concatenate([qf_ref[pl.ds(bi * q_tok + i * GRID_W + c0, w), lanes] for i in range(Q_ROWS)],
                             axis=0)
        qs = jnp.concatenate([jnp.where(first_head, qp, 0.0), jnp.where(first_head, 0.0, qp)],
                             axis=0).astype(BF16)
        s = lax.dot_general(qs, slab(k_refs, bi, kb, lanes), (((1,), (1,)), ((), ())),
                            preferred_element_type=F32)
        bias = jnp.concatenate([tab_ref[2 * pair, pl.ds(row_off, nq), :],
                                tab_ref[2 * pair + 1, pl.ds(row_off, nq), :]], axis=0)
        s_refs[idx % 2][0:2 * nq, :] = s + bias

    def attend(idx):
        bi, c0, w, kb, pair, _ = items[idx]
        nq = Q_ROWS * w
        lanes = pl.ds(pair * LANES, LANES)
        first_head = lax.broadcasted_iota(jnp.int32, (nq, LANES), 1) < NA_HEAD_DIM
        s = s_refs[idx % 2][0:2 * nq, :]
        p = jnp.exp2(s - jnp.max(s, axis=-1, keepdims=True))
        o2 = jnp.dot(p.astype(BF16), slab(v_refs, bi, kb, lanes), preferred_element_type=F32)
        o2 = o2 * (1.0 / jnp.sum(p, axis=-1, keepdims=True))
        o = jnp.where(first_head, o2[0:nq], o2[nq:2 * nq])
        for i in range(Q_ROWS):
            of_ref[pl.ds(bi * q_tok + i * GRID_W + c0, w), lanes] = o[i * w:(i + 1) * w]

    scores(0)
    for idx in range(len(items)):
        if idx + 1 < len(items):
            scores(idx + 1)
        attend(idx)
    for bi in range(n_batch):
        rows = pl.ds(bi * q_tok, q_tok)
        y_ref[bi, 0] = (of_ref[rows, :] * _silu(z_ref[bi, 0].astype(F32))).astype(BF16)


def _natten(proj3, bias_rows):
    bsz, seq, n_main = proj3.shape
    q_tok = Q_ROWS * GRID_W
    p_tok = (SLAB_ROWS // 4) * GRID_W
    groups = seq // q_tok
    pieces = seq // p_tok
    assert groups >= 2, "needs at least two 8-row query groups"
    nb = 2 if bsz % 2 == 0 else 1
    w = NA_WIDTH
    qb, kb, vb, zb = (5 * D_MODEL) // w, (5 * D_MODEL) // w + 1, (5 * D_MODEL) // w + 2, (5 * D_MODEL) // w + 3
    proj_q = proj3.reshape(bsz, groups, q_tok, n_main)
    proj_p = proj3.reshape(bsz, pieces, p_tok, n_main)

    def slab_start(g):
        return jnp.clip(2 * g - 1, 0, pieces - 4)

    def piece_spec(col, a):
        return pl.BlockSpec((nb, 1, p_tok, w), lambda g, b: (b, slab_start(g) + a, 0, col))

    return pl.pallas_call(
        _natten_kernel,
        out_shape=jax.ShapeDtypeStruct((bsz, groups, q_tok, w), BF16),
        grid=(groups, bsz // nb),
        in_specs=[pl.BlockSpec((nb, 1, q_tok, w), lambda g, b: (b, g, 0, qb))]
        + [piece_spec(kb, a) for a in range(4)] + [piece_spec(vb, a) for a in range(4)]
        + [pl.BlockSpec((nb, 1, q_tok, w), lambda g, b: (b, g, 0, zb)),
           pl.BlockSpec(bias_rows.shape, lambda g, b: (0, 0, 0, 0))],
        out_specs=pl.BlockSpec((nb, 1, q_tok, w), lambda g, b: (b, g, 0, 0)),
        scratch_shapes=[pltpu.VMEM((nb * q_tok, w), F32), pltpu.VMEM((nb * q_tok, w), F32),
                        pltpu.VMEM((len(_GROUP_KINDS), NA_HEADS, q_tok, N_KEYS), F32),
                        pltpu.VMEM((2 * Q_ROWS * Q_BLOCK_MAX, N_KEYS), F32),
                        pltpu.VMEM((2 * Q_ROWS * Q_BLOCK_MAX, N_KEYS), F32)],
        compiler_params=pltpu.CompilerParams(
            dimension_semantics=("arbitrary", "arbitrary"), vmem_limit_bytes=VMEM_LIMIT_BYTES),
        name="natten",
    )(proj_q, *([proj_p] * 8), proj_q, bias_rows).reshape(bsz, seq, w)


def _merge_out_kernel(ya_ref, yb_ref, g0_ref, g1_ref, x_ref, mod_ref, bm_ref, wa_ref, wb_ref, wo_ref, fg_ref,
                      out_ref, m_ref):
    tm, d = x_ref.shape
    sub = min(MERGE_ROWS, tm)
    gate = mod_ref[0, :, 2 * d:3 * d]

    def merge(r):
        rows = pl.ds(r, sub)
        pa = jnp.dot(ya_ref[rows, :], wa_ref[...], preferred_element_type=F32)
        pb = jnp.dot(yb_ref[rows, :], wb_ref[...], preferred_element_type=F32)
        m_ref[rows, :] = (_sigmoid(g0_ref[rows, :].astype(F32) + bm_ref[0:1, :]) * pa
                          + _sigmoid(g1_ref[rows, :].astype(F32) + bm_ref[1:2, :]) * pb).astype(BF16)

    def project(r):
        rows = pl.ds(r, sub)
        mixed = jnp.dot(m_ref[rows, :], wo_ref[...], preferred_element_type=F32)
        xo = x_ref[rows, :] + gate * mixed
        out_ref[rows, :] = xo * lax.rsqrt(jnp.mean(xo * xo, axis=-1, keepdims=True) + EPS) * fg_ref[...]

    starts = list(range(0, tm, sub))
    merge(starts[0])
    for i, r in enumerate(starts):
        if i + 1 < len(starts):
            merge(starts[i + 1])
        project(r)


def _merge_out(ya2, yb2, proj2, x2, mod3, b_merge, wa, wb, wo, final_gain, seq, tm):
    n_tok, d = x2.shape
    g0 = (N_MAIN - 2 * d) // d

    def full(shape):
        return pl.BlockSpec(shape, lambda m: (0,) * len(shape))

    return pl.pallas_call(
        _merge_out_kernel,
        out_shape=jax.ShapeDtypeStruct((n_tok, d), F32),
        grid=(n_tok // tm,),
        in_specs=[
            pl.BlockSpec((tm, d), lambda m: (m, 0)),
            pl.BlockSpec((tm, NA_WIDTH), lambda m: (m, 0)),
            pl.BlockSpec((tm, d), lambda m: (m, g0)),
            pl.BlockSpec((tm, d), lambda m: (m, g0 + 1)),
            pl.BlockSpec((tm, d), lambda m: (m, 0)),
            pl.BlockSpec((1, 1, 3 * d), lambda m: ((m * tm) // seq, 0, 0)),
            full((2, d)), full((d, d)), full((NA_WIDTH, d)), full((d, d)), full((1, d)),
        ],
        out_specs=pl.BlockSpec((tm, d), lambda m: (m, 0)),
        scratch_shapes=[pltpu.VMEM((tm, d), BF16)],
        compiler_params=pltpu.CompilerParams(
            dimension_semantics=("arbitrary",), vmem_limit_bytes=VMEM_LIMIT_BYTES),
        name="merge_out",
    )(ya2, yb2, proj2, proj2, x2, mod3, b_merge, wa, wb, wo, final_gain.reshape(1, d))


def _gate_layout(n_cols):
    src = np.arange(n_cols)
    direction, gate, head = src // (2 * MLSTM_HEADS), (src // MLSTM_HEADS) % 2, src % MLSTM_HEADS
    return 2 * MLSTM_HEADS * gate + MLSTM_HEADS * direction + head


def _weight_prep_kernel(a_ref, b_ref, perm_ref, c_ref, wada_ref, bada_ref, pa_ref, pb_ref, po_ref,
                        main_ref, gate_ref, mod_ref, wa_ref, wb_ref, wo_ref, *, gate_block, n_gates, n_mod, n_cast):
    n = pl.program_id(0)

    @pl.when(n < n_mod)
    def _():
        cond = _silu(c_ref[...]).astype(BF16)
        mod_ref[...] = jnp.dot(cond, wada_ref[...].astype(BF16), preferred_element_type=F32) + bada_ref[...]

    @pl.when(n < n_cast)
    def _():
        wa_ref[...] = pa_ref[...].astype(BF16)
        wb_ref[...] = pb_ref[...].astype(BF16)
        wo_ref[...] = po_ref[...].astype(BF16)

    rows = a_ref[0]
    past_gates = jnp.concatenate([rows[n_gates:], b_ref[0]], axis=0)
    main_ref[...] = jnp.where(n >= gate_block, past_gates, rows).T.astype(BF16)

    @pl.when(n == gate_block)
    def _():
        gate_ref[...] = jnp.dot(perm_ref[...], rows[0:n_gates], preferred_element_type=F32).T.astype(BF16)


def _weight_prep(w_in, layer, c, w_ada, b_ada, w_proj_a, w_proj_b, w_out):
    _, d, n_in = w_in.shape
    bsz = c.shape[0]
    n_gates = 4 * MLSTM_HEADS
    tn = D_MODEL
    g_lo = 5 * MLSTM_HEADS * MLSTM_HEAD_DIM
    assert n_in == N_MAIN + n_gates and g_lo % tn == 0 and tn % n_gates == 0
    perm = np.zeros((GATE_LANES, n_gates), np.float32)
    perm[_gate_layout(n_gates), np.arange(n_gates)] = 1.0
    w_t = jnp.swapaxes(w_in, 1, 2)
    steps = N_MAIN // tn
    n_ada = w_ada.shape[1]
    t_mod = 512
    n_mod, n_cast = n_ada // t_mod, 8
    ra, rb, ro = w_proj_a.shape[0] // n_cast, w_proj_b.shape[0] // n_cast, w_out.shape[0] // n_cast
    assert n_ada % t_mod == 0 and n_mod <= steps and n_cast <= steps and min(ra, rb, ro) % 16 == 0
    mod_tile = lambda n: (0, jnp.minimum(n, n_mod - 1))
    cast_tile = lambda n: (jnp.minimum(n, n_cast - 1), 0)
    return pl.pallas_call(
        functools.partial(_weight_prep_kernel, gate_block=g_lo // tn, n_gates=n_gates, n_mod=n_mod, n_cast=n_cast),
        out_shape=(jax.ShapeDtypeStruct((d, N_MAIN), BF16), jax.ShapeDtypeStruct((d, GATE_LANES), BF16),
                   jax.ShapeDtypeStruct((bsz, n_ada), F32), jax.ShapeDtypeStruct(w_proj_a.shape, BF16),
                   jax.ShapeDtypeStruct(w_proj_b.shape, BF16), jax.ShapeDtypeStruct(w_out.shape, BF16)),
        grid=(steps,),
        in_specs=[pl.BlockSpec((1, tn, d), lambda n: (layer, n, 0)),
                  pl.BlockSpec((1, n_gates, d), lambda n: (layer, (n + 1) * (tn // n_gates), 0)),
                  pl.BlockSpec((GATE_LANES, n_gates), lambda n: (0, 0)),
                  pl.BlockSpec((bsz, d), lambda n: (0, 0)),
                  pl.BlockSpec((d, t_mod), mod_tile),
                  pl.BlockSpec((1, t_mod), mod_tile),
                  pl.BlockSpec((ra, w_proj_a.shape[1]), cast_tile),
                  pl.BlockSpec((rb, w_proj_b.shape[1]), cast_tile),
                  pl.BlockSpec((ro, w_out.shape[1]), cast_tile)],
        out_specs=(pl.BlockSpec((d, tn), lambda n: (0, n)),
                   pl.BlockSpec((d, GATE_LANES), lambda n: (0, 0)),
                   pl.BlockSpec((bsz, t_mod), mod_tile),
                   pl.BlockSpec((ra, w_proj_a.shape[1]), cast_tile),
                   pl.BlockSpec((rb, w_proj_b.shape[1]), cast_tile),
                   pl.BlockSpec((ro, w_out.shape[1]), cast_tile)),
        compiler_params=pltpu.CompilerParams(dimension_semantics=("arbitrary",), vmem_limit_bytes=VMEM_LIMIT_BYTES),
        name="weight_prep",
    )(w_t, w_t, jnp.asarray(perm), c, w_ada, b_ada.reshape(1, n_ada), w_proj_a, w_proj_b, w_out)


def _token_tiles(seq):
    return min(2048, seq), min(1024, seq)


def _layer(x, c, w_ada, b_ada, norm_gain, w_in, layer, conv_w, conv_b, b_igate, b_fgate, mlstm_norm_gain, rpb, w_proj_a,
           w_proj_b, b_merge, w_out, final_gain):
    bsz, seq, d = x.shape
    n_tok = bsz * seq
    wa = MLSTM_HEADS * MLSTM_HEAD_DIM
    w_main, w_gate, mod, wpa, wpb, wo = _weight_prep(w_in, layer, c, w_ada, b_ada, w_proj_a, w_proj_b, w_out)
    gate_bias = jnp.zeros((1, GATE_LANES), F32).at[0, _gate_layout(4 * MLSTM_HEADS)].set(
        jnp.stack([b_igate, b_fgate], axis=1).reshape(-1))
    tm_in, tm_out = _token_tiles(seq)

    x2 = x.reshape(n_tok, d)
    mod3 = mod.reshape(bsz, 1, 3 * d)
    proj2, gcol2, grow = _in_proj(x2, mod3, norm_gain, w_main, w_gate, gate_bias, seq, tm_in)
    proj3 = proj2.reshape(bsz, seq, N_MAIN)
    gcol = gcol2.reshape(bsz, seq, GATE_LANES)
    y_a = _mlstm(proj3, gcol, grow, conv_w, conv_b, mlstm_norm_gain)
    y_b = _natten(proj3, _natten_bias_rows(rpb))
    out2 = _merge_out(y_a.reshape(n_tok, wa), y_b.reshape(n_tok, NA_WIDTH), proj2, x2, mod3, b_merge,
                      wpa, wpb, wo, final_gain, seq, tm_out)
    return out2.reshape(bsz, seq, d)


def kernel(x, c, w_ada, b_ada, norm_gain, w_in, conv_w, conv_b, b_igate, b_fgate, mlstm_norm_gain, rpb, w_proj_a,
           w_proj_b, b_merge, w_out, final_gain):
    depth = w_ada.shape[0]
    assert depth == 1, "the fused final norm assumes a single layer"
    return _layer(x, c, w_ada[0], b_ada[0], norm_gain[0], w_in, 0, conv_w[0], conv_b[0], b_igate[0], b_fgate[0], mlstm_norm_gain[0],
                  rpb[0], w_proj_a[0], w_proj_b[0], b_merge[0], w_out[0], final_gain)
```

```python
import functools
import math

import numpy as np
import jax
import jax.numpy as jnp
from jax import lax
from jax.experimental import pallas as pl
from jax.experimental.pallas import tpu as pltpu

F32 = jnp.float32
BF16 = jnp.bfloat16

D_MODEL = 1024
GRID_W = 64
MLSTM_HEADS = 4
MLSTM_HEAD_DIM = 256
NA_HEADS = 8
NA_WIDTH = 512
NA_HEAD_DIM = 64
NA_ROWS = 8
NA_COLS = 16
EPS = 1e-6
NEG = -1e30
LOG2E = 1.4426950408889634
K_SCALE_LOG2 = -0.5 * math.log2(MLSTM_HEAD_DIM)

VMEM_LIMIT_BYTES = 56 * 1024 * 1024
LANES = 128

CHUNK = 256

MERGE_ROWS = 512

N_MAIN = 9 * D_MODEL
GATE_LANES = 128

Q_ROWS = 8
SLAB_ROWS = 16
Q_BLOCKS = ((0, 24), (24, 16), (40, 24))
Q_BLOCK_MAX = 24
K_STARTS = (0, 16, 32)
K_WIDTH = 32
N_KEYS = SLAB_ROWS * K_WIDTH


def _sigmoid(x):
    return 1.0 / (1.0 + jnp.exp2(x * -LOG2E))


def _silu(x):
    return x * _sigmoid(x)


def _gate_quantities(gates, pos_in_chunk):
    v = gates.T
    i_gate, f_gate = v[0:8, :] * LOG2E, v[8:16, :]
    n_tok = v.shape[1]
    is_bwd = lax.broadcasted_iota(jnp.int32, i_gate.shape, 0) >= MLSTM_HEADS

    def scan(x, combine, identity):
        pre, suf = x, x
        step = 1
        while step < CHUNK:
            pre = combine(pre, jnp.where(pos_in_chunk >= step, pltpu.roll(pre, step, 1), identity))
            suf = combine(suf, jnp.where(pos_in_chunk < CHUNK - step, pltpu.roll(suf, n_tok - step, 1), identity))
            step *= 2
        return jnp.where(is_bwd, suf, pre)

    b = scan((jnp.minimum(f_gate, 0.0) - jnp.log1p(jnp.exp(-jnp.abs(f_gate)))) * LOG2E, jnp.add, 0.0)
    r = i_gate - b
    cm = scan(r, jnp.maximum, NEG)
    cols = jnp.concatenate([cm, b, jnp.zeros((GATE_LANES - 16, n_tok), F32)], axis=0).T
    return cols, r


def _in_proj_kernel(x_ref, mod_ref, gain_ref, w_ref, wg_ref, gbias_ref, proj_ref, gcol_ref, grow_ref, h_ref, *, rows):
    tm, d = x_ref.shape
    n = pl.program_id(1)

    @pl.when(n == 0)
    def _():
        shift = mod_ref[0, :, 0:d]
        scale = mod_ref[0, :, d:2 * d]
        gain = gain_ref[...]
        pos = lax.broadcasted_iota(jnp.int32, (8, rows), 1) % CHUNK

        def normalize(r):
            xf = x_ref[r:r + rows, :]
            y = xf * lax.rsqrt(jnp.mean(xf * xf, axis=-1, keepdims=True) + EPS) * gain
            h_ref[r:r + rows, :] = (y * (1.0 + scale) + shift).astype(BF16)

        def project(r):
            hb = h_ref[r:r + rows, :]
            gates = jnp.dot(hb, wg_ref[...], preferred_element_type=F32) + gbias_ref[...]
            proj_ref[r:r + rows, :] = jnp.dot(hb, w_ref[...], preferred_element_type=F32).astype(BF16)
            return gates

        def gate_scans(r, gates):
            cols, r_rows = _gate_quantities(gates, pos)
            gcol_ref[r:r + rows, :] = cols
            for c in range(rows // CHUNK):
                grow_ref[0, r // CHUNK + c] = r_rows[:, c * CHUNK:(c + 1) * CHUNK]

        starts = list(range(0, tm, rows))
        normalize(starts[0])
        pending = None
        for i, r in enumerate(starts):
            if i + 1 < len(starts):
                normalize(starts[i + 1])
            gates = project(r)
            if pending is not None:
                gate_scans(*pending)
            pending = (r, gates)
        gate_scans(*pending)

    @pl.when(n > 0)
    def _():
        proj_ref[...] = jnp.dot(h_ref[...], w_ref[...], preferred_element_type=F32).astype(BF16)


def _in_proj(x2, mod3, norm_gain, w_main, w_gate, gate_bias, seq, tm):
    n_tok, d = x2.shape
    tn = N_MAIN // 6
    rows = min(MERGE_ROWS, tm)
    assert rows % CHUNK == 0 and seq % tm == 0
    tiles_per_seq = seq // tm
    return pl.pallas_call(
        functools.partial(_in_proj_kernel, rows=rows),
        out_shape=(
            jax.ShapeDtypeStruct((n_tok, N_MAIN), BF16),
            jax.ShapeDtypeStruct((n_tok, GATE_LANES), F32),
            jax.ShapeDtypeStruct((n_tok // seq, seq // CHUNK, 2 * MLSTM_HEADS, CHUNK), F32),
        ),
        grid=(n_tok // tm, N_MAIN // tn),
        in_specs=[
            pl.BlockSpec((tm, d), lambda m, n: (m, 0)),
            pl.BlockSpec((1, 1, 3 * d), lambda m, n: ((m * tm) // seq, 0, 0)),
            pl.BlockSpec((1, d), lambda m, n: (0, 0)),
            pl.BlockSpec((d, tn), lambda m, n: (0, n)),
            pl.BlockSpec((d, GATE_LANES), lambda m, n: (0, 0)),
            pl.BlockSpec((1, GATE_LANES), lambda m, n: (0, 0)),
        ],
        out_specs=(
            pl.BlockSpec((tm, tn), lambda m, n: (m, n)),
            pl.BlockSpec((tm, GATE_LANES), lambda m, n: (m, 0)),
            pl.BlockSpec((1, tm // CHUNK, 2 * MLSTM_HEADS, CHUNK),
                         lambda m, n: (m // tiles_per_seq, m % tiles_per_seq, 0, 0)),
        ),
        scratch_shapes=[pltpu.VMEM((tm, d), BF16)],
        compiler_params=pltpu.CompilerParams(
            dimension_semantics=("arbitrary", "arbitrary"), vmem_limit_bytes=VMEM_LIMIT_BYTES),
        name="in_proj",
    )(x2, mod3, norm_gain.reshape(1, d), w_main, w_gate, gate_bias)


def _mlstm_chunk(qc, kt, qk_ref, reuse_qk, vv, r_row, cm_rep, b_rep, cm_end, g, cn_ref, m_ref, causal_mask):
    d = vv.shape[1]
    cn = cn_ref[...]
    m_state = m_ref[...]
    m_col = jnp.maximum(m_state, cm_rep)
    w_inter = jnp.exp2(m_state - m_col)
    m_wide = jnp.concatenate([m_col, m_col], axis=1)
    d_mat = jnp.where(causal_mask, jnp.exp2(r_row - m_wide), 0.0)
    if reuse_qk:
        qk = qk_ref[...]
    else:
        qk = jnp.dot(qc, kt, preferred_element_type=F32)
        qk_ref[...] = qk
    scores = qk * d_mat
    inter = jnp.dot(qc, cn.astype(BF16), preferred_element_type=F32)
    yield None
    w_wide = jnp.concatenate([w_inter, w_inter], axis=1)
    num = jnp.dot(scores.astype(BF16), vv, preferred_element_type=F32) + w_wide * inter[:, 0:d]
    den = jnp.sum(scores, axis=-1, keepdims=True) + w_inter * inter[:, d:]
    inv = 1.0 / jnp.maximum(jnp.abs(den), jnp.exp2(-(b_rep + m_col)))
    yield num * jnp.concatenate([inv, inv], axis=1)
    m_end = jnp.maximum(m_state, cm_end)
    w_row = jnp.exp2(r_row - jnp.concatenate([m_end, m_end], axis=1))
    decay = jnp.exp2(m_state - m_end)
    kw = kt * w_row.astype(BF16)
    v_aug = jnp.concatenate([vv, jnp.ones((vv.shape[0], LANES), BF16)], axis=1)
    decay_wide = jnp.concatenate([decay, decay, decay], axis=1)
    cn_ref[...] = decay_wide * cn + jnp.dot(kw, v_aug, preferred_element_type=F32)
    m_ref[...] = g + m_end
    yield None


def _mlstm_kernel(q_ref, k_ref, v_ref, o_ref, z_ref, gcol_ref, grow_ref, cwq_ref, cwk_ref, cbq_ref, cbk_ref,
                  gain_ref, y_ref, qc_ref, kt_ref, qk_ref, h_ref, cnf_ref, mf_ref, cnb_ref, mb_ref):
    seq, d = q_ref.shape[1], q_ref.shape[2]
    nc = seq // CHUNK
    head = pl.program_id(1)

    t_idx = lax.broadcasted_iota(jnp.int32, (CHUNK, CHUNK), 0)
    s_idx = lax.broadcasted_iota(jnp.int32, (CHUNK, CHUNK), 1)
    taps = jnp.concatenate([(s_idx == t_idx - 1).astype(BF16), (s_idx == t_idx + 1).astype(BF16)],
                           axis=0)
    edge = lax.broadcasted_iota(jnp.int32, (8, d), 0)

    def conv_silu(src_ref, w_ref, b_ref, c, r):
        w0, w1, w2 = 0.5 * w_ref[0:1, :], 0.5 * w_ref[1:2, :], 0.5 * w_ref[2:3, :]
        src = src_ref[0, pl.ds(r, CHUNK), :]
        p = jnp.dot(taps, src, preferred_element_type=F32)
        t = p[0:CHUNK] * w0 + src.astype(F32) * w1 + p[CHUNK:2 * CHUNK] * w2 + 0.5 * b_ref[...]
        above = src_ref[0, pl.ds(pl.multiple_of(jnp.maximum(r - 16, 0), 16), 16), :][15:16, :].astype(F32)
        below = src_ref[0, pl.ds(pl.multiple_of(jnp.minimum(r + CHUNK, seq - 16), 16), 16), :][0:1, :].astype(F32)
        above = above * jnp.where(c > 0, w0, 0.0)
        below = below * jnp.where(c < nc - 1, w2, 0.0)
        t = jnp.concatenate([t[0:8] + jnp.where(edge == 0, above, 0.0), t[8:CHUNK - 8],
                             t[CHUNK - 8:CHUNK] + jnp.where(edge == 7, below, 0.0)], axis=0)
        return t * (1.0 + jnp.tanh(t))

    def conv_chunk(c, carry):
        r = pl.multiple_of(c * CHUNK, CHUNK)
        qc_ref[pl.ds(r, CHUNK), :] = conv_silu(q_ref, cwq_ref, cbq_ref, c, r).astype(BF16)
        kt_ref[c] = conv_silu(k_ref, cwk_ref, cbk_ref, c, r).astype(BF16).T
        return carry

    lax.fori_loop(0, nc, conv_chunk, 0, unroll=8)

    gate_row = lax.broadcasted_iota(jnp.int32, (2 * MLSTM_HEADS, CHUNK), 0)
    gate_lane = lax.broadcasted_iota(jnp.int32, (CHUNK, GATE_LANES), 1)

    for ref in (cnf_ref, mf_ref, cnb_ref, mb_ref):
        ref[...] = jnp.zeros(ref.shape, F32)

    mask_f = s_idx <= t_idx
    mask_b = s_idx >= t_idx

    def replicate(cols, lane):
        picked = jnp.sum(jnp.where(gate_lane == lane, cols, 0.0), axis=1, keepdims=True)
        return jnp.broadcast_to(picked, (cols.shape[0], LANES))

    def direction(chunk, backward, reuse_qk):
        r = pl.multiple_of(chunk * CHUNK, CHUNK)
        rows = grow_ref[0, chunk]
        cols = gcol_ref[0, pl.ds(r, CHUNK), :]
        qc = qc_ref[pl.ds(r, CHUNK), :]
        kt = kt_ref[chunk]
        vv = v_ref[0, pl.ds(r, CHUNK), :]
        if backward:
            cm_rep, b_rep = replicate(cols, MLSTM_HEADS + head), replicate(cols, 3 * MLSTM_HEADS + head)
            end = 0
            state, mask, this_row = (cnb_ref, mb_ref), mask_b, MLSTM_HEADS + head
        else:
            cm_rep, b_rep = replicate(cols, head), replicate(cols, 2 * MLSTM_HEADS + head)
            end = CHUNK - 1
            state, mask, this_row = (cnf_ref, mf_ref), mask_f, head
        r_row = jnp.sum(jnp.where(gate_row == this_row, rows, 0.0), axis=0, keepdims=True) + K_SCALE_LOG2
        cm_end, g = cm_rep[end:end + 1, :], b_rep[end:end + 1, :]
        return _mlstm_chunk(qc, kt, qk_ref.at[chunk], reuse_qk, vv, r_row, cm_rep, b_rep, cm_end, g, *state, mask), r

    def chunk_steps(j, second_visit):
        steps = [direction(j, False, second_visit), direction(nc - 1 - j, True, second_visit)]
        if second_visit:
            for gen, _ in steps:
                next(gen)
            hs = [next(gen) for gen, _ in steps]
            for gen, _ in steps:
                next(gen)
        else:
            hs = []
            for gen, _ in steps:
                next(gen)
                hs.append(next(gen))
                next(gen)
        return [(h, r) for h, (_, r) in zip(hs, steps)]

    def first_half(j, carry):
        for h_dir, r in chunk_steps(j, second_visit=False):
            h_ref[pl.ds(r, CHUNK), :] = h_dir
        return carry

    gain = gain_ref[...]

    def finish(h_dir, r):
        h_sum = (h_ref[pl.ds(r, CHUNK), :] + h_dir) * _sigmoid(o_ref[0, pl.ds(r, CHUNK), :].astype(F32))
        normed = h_sum * lax.rsqrt(jnp.mean(h_sum * h_sum, axis=-1, keepdims=True) + EPS) * gain
        y_ref[0, pl.ds(r, CHUNK), :] = (normed * _silu(z_ref[0, pl.ds(r, CHUNK), :].astype(F32))).astype(BF16)

    def second_half(j, carry):
        for h_dir, r in chunk_steps(j, second_visit=True):
            finish(h_dir, r)
        return carry

    lax.fori_loop(0, nc // 2, first_half, 0, unroll=2)
    lax.fori_loop(nc // 2, nc, second_half, 0)


def _mlstm(proj3, gcol, grow, conv_w, conv_b, head_gain):
    bsz, seq, _ = proj3.shape
    d = MLSTM_HEAD_DIM
    nh = MLSTM_HEADS
    nc = seq // CHUNK

    def col_block(offset):
        return pl.BlockSpec((1, seq, d), lambda b, h: (b, 0, offset + h))

    return pl.pallas_call(
        _mlstm_kernel,
        out_shape=jax.ShapeDtypeStruct((bsz, seq, nh * d), BF16),
        grid=(bsz, nh),
        in_specs=[
            col_block(0), col_block(nh), col_block(2 * nh), col_block(3 * nh), col_block(4 * nh),
            pl.BlockSpec((1, seq, GATE_LANES), lambda b, h: (b, 0, 0)),
            pl.BlockSpec((1, nc, 2 * MLSTM_HEADS, CHUNK), lambda b, h: (b, 0, 0, 0)),
            pl.BlockSpec((3, d), lambda b, h: (0, h)),
            pl.BlockSpec((3, d), lambda b, h: (0, nh + h)),
            pl.BlockSpec((1, d), lambda b, h: (0, h)),
            pl.BlockSpec((1, d), lambda b, h: (0, nh + h)),
            pl.BlockSpec((1, d), lambda b, h: (0, h)),
        ],
        out_specs=pl.BlockSpec((1, seq, d), lambda b, h: (b, 0, h)),
        scratch_shapes=[
            pltpu.VMEM((seq, d), BF16),
            pltpu.VMEM((nc, d, CHUNK), BF16),
            pltpu.VMEM((nc, CHUNK, CHUNK), F32),
            pltpu.VMEM((seq, d), F32),
            pltpu.VMEM((d, d + LANES), F32), pltpu.VMEM((1, LANES), F32),
            pltpu.VMEM((d, d + LANES), F32), pltpu.VMEM((1, LANES), F32),
        ],
        compiler_params=pltpu.CompilerParams(
            dimension_semantics=("arbitrary", "arbitrary"), vmem_limit_bytes=VMEM_LIMIT_BYTES),
        name="mlstm",
    )(proj3, proj3, proj3, proj3, proj3, gcol, grow, conv_w, conv_w,
      conv_b.reshape(1, -1), conv_b.reshape(1, -1), head_gain.reshape(1, -1))


def _natten_bias_rows(rpb):
    n_dc = 2 * NA_COLS - 1
    n_dr = 2 * NA_ROWS - 1
    n_blk = len(Q_BLOCKS)
    sel = np.zeros((n_dc, n_blk, Q_BLOCK_MAX, K_WIDTH), np.float32)
    col_ok = np.zeros((n_blk, Q_BLOCK_MAX, K_WIDTH), bool)
    for blk, ((c0, w), kb) in enumerate(zip(Q_BLOCKS, K_STARTS)):
        q_col = c0 + np.arange(w)[:, None]
        k_col = kb + np.arange(K_WIDTH)[None, :]
        win = np.clip(q_col - NA_COLS // 2, 0, GRID_W - NA_COLS)
        inside = (k_col >= win) & (k_col < win + NA_COLS)
        jj, cc = np.nonzero(inside)
        sel[(k_col - q_col + NA_COLS - 1)[jj, cc], blk, jj, cc] = 1.0
        col_ok[blk, :w] = inside
    vals = jnp.einsum("hrd,dn->hrn", rpb.astype(F32) * LOG2E, jnp.asarray(sel.reshape(n_dc, -1)),
                      precision=lax.Precision.HIGHEST)
    vals = vals.reshape(NA_HEADS, n_dr, n_blk, Q_BLOCK_MAX, K_WIDTH).transpose(0, 2, 3, 1, 4)
    vals = jnp.where(jnp.asarray(col_ok)[None, :, :, None, :], vals, NEG)
    vals = vals.reshape(NA_HEADS, n_blk, Q_BLOCK_MAX, n_dr * K_WIDTH)
    return jnp.pad(vals, ((0, 0), (0, 0), (0, 0), (0, N_KEYS - n_dr * K_WIDTH)), constant_values=NEG)


_GROUP_KINDS = (
    (0, lambda i: max(i - NA_ROWS // 2, 0)),
    (-(NA_ROWS // 2), lambda i: i),
    (Q_ROWS - SLAB_ROWS, lambda i: min(i + NA_ROWS // 2, NA_ROWS)),
)


def _build_bias(e_ref, tab_ref, slab_off, first_valid):
    for h in range(NA_HEADS):
        row_off = 0
        for blk, (_, w) in enumerate(Q_BLOCKS):
            e = e_ref[h, blk, 0:w, :]
            lane = lax.broadcasted_iota(jnp.int32, (w, N_KEYS), 1)
            for i in range(Q_ROWS):
                shift = slab_off - i + NA_ROWS - 1
                amount = (-shift * K_WIDTH) % N_KEYS
                moved = e if amount == 0 else pltpu.roll(e, amount, 1)
                lo = first_valid(i) * K_WIDTH
                ok = (lane >= lo) & (lane < lo + NA_ROWS * K_WIDTH)
                tab_ref[h, row_off + i * w:row_off + (i + 1) * w, :] = jnp.where(ok, moved, NEG)
            row_off += Q_ROWS * w


def _natten_kernel(q_ref, k0_ref, k1_ref, k2_ref, k3_ref, v0_ref, v1_ref, v2_ref, v3_ref, z_ref, e_ref,
                   y_ref, qf_ref, of_ref, tabs_ref, sa_ref, sb_ref):
    k_refs = (k0_ref, k1_ref, k2_ref, k3_ref)
    v_refs = (v0_ref, v1_ref, v2_ref, v3_ref)
    piece_rows = SLAB_ROWS // 4
    group = pl.program_id(0)
    n_groups = pl.num_programs(0)

    @pl.when(jnp.logical_and(group == 0, pl.program_id(1) == 0))
    def _():
        for kind, (slab_off, first_valid) in enumerate(_GROUP_KINDS):
            _build_bias(e_ref, tabs_ref.at[kind], slab_off, first_valid)

    tab_ref = tabs_ref.at[(group > 0).astype(jnp.int32) + (group == n_groups - 1).astype(jnp.int32)]

    n_batch, q_tok = q_ref.shape[0], q_ref.shape[2]
    for bi in range(n_batch):
        qf_ref[pl.ds(bi * q_tok, q_tok), :] = q_ref[bi, 0].astype(F32) * (NA_HEAD_DIM ** -0.5 * LOG2E)

    items = []
    for bi in range(n_batch):
        row_off = 0
        for (c0, w), kb in zip(Q_BLOCKS, K_STARTS):
            for pair in range(NA_HEADS // 2):
                items.append((bi, c0, w, kb, pair, row_off))
            row_off += Q_ROWS * w
    s_refs = (sa_ref, sb_ref)

    def slab(refs, bi, kb, lanes):
        return jnp.concatenate(
            [refs[a // piece_rows][bi, 0, pl.ds((a % piece_rows) * GRID_W + kb, K_WIDTH), lanes]
             for a in range(SLAB_ROWS)], axis=0)

    def scores(idx):
        bi, c0, w, kb, pair, row_off = items[idx]
        nq = Q_ROWS * w
        lanes = pl.ds(pair * LANES, LANES)
        first_head = lax.broadcasted_iota(jnp.int32, (nq, LANES), 1) < NA_HEAD_DIM
        qp = jnp.concatenate([qf_ref[pl.ds(bi * q_tok + i * GRID_W + c0, w), lanes] for i in range(Q_ROWS)],
                             axis=0)
        qs = jnp.concatenate([jnp.where(first_head, qp, 0.0), jnp.where(first_head, 0.0, qp)],
                             axis=0).astype(BF16)
        s = lax.dot_general(qs, slab(k_refs, bi, kb, lanes), (((1,), (1,)), ((), ())),
                            preferred_element_type=F32)
        bias = jnp.concatenate([tab_ref[2 * pair, pl.ds(row_off, nq), :],
                                tab_ref[2 * pair + 1, pl.ds(row_off, nq), :]], axis=0)
        s_refs[idx % 2][0:2 * nq, :] = s + bias

    def attend(idx):
        bi, c0, w, kb, pair, _ = items[idx]
        nq = Q_ROWS * w
        lanes = pl.ds(pair * LANES, LANES)
        first_head = lax.broadcasted_iota(jnp.int32, (nq, LANES), 1) < NA_HEAD_DIM
        s = s_refs[idx % 2][0:2 * nq, :]
        p = jnp.exp2(s - jnp.max(s, axis=-1, keepdims=True))
        o2 = jnp.dot(p.astype(BF16), slab(v_refs, bi, kb, lanes), preferred_element_type=F32)
        o2 = o2 * (1.0 / jnp.sum(p, axis=-1, keepdims=True))
        o = jnp.where(first_head, o2[0:nq], o2[nq:2 * nq])
        for i in range(Q_ROWS):
            of_ref[pl.ds(bi * q_tok + i * GRID_W + c0, w), lanes] = o[i * w:(i + 1) * w]

    scores(0)
    for idx in range(len(items)):
        if idx + 1 < len(items):
            scores(idx + 1)
        attend(idx)
    for bi in range(n_batch):
        rows = pl.ds(bi * q_tok, q_tok)
        y_ref[bi, 0] = (of_ref[rows, :] * _silu(z_ref[bi, 0].astype(F32))).astype(BF16)


def _natten(proj3, bias_rows):
    bsz, seq, n_main = proj3.shape
    q_tok = Q_ROWS * GRID_W
    p_tok = (SLAB_ROWS // 4) * GRID_W
    groups = seq // q_tok
    pieces = seq // p_tok
    assert groups >= 2, "needs at least two 8-row query groups"
    nb = 2 if bsz % 2 == 0 else 1
    w = NA_WIDTH
    qb, kb, vb, zb = (5 * D_MODEL) // w, (5 * D_MODEL) // w + 1, (5 * D_MODEL) // w + 2, (5 * D_MODEL) // w + 3
    proj_q = proj3.reshape(bsz, groups, q_tok, n_main)
    proj_p = proj3.reshape(bsz, pieces, p_tok, n_main)

    def slab_start(g):
        return jnp.clip(2 * g - 1, 0, pieces - 4)

    def piece_spec(col, a):
        return pl.BlockSpec((nb, 1, p_tok, w), lambda g, b: (b, slab_start(g) + a, 0, col))

    return pl.pallas_call(
        _natten_kernel,
        out_shape=jax.ShapeDtypeStruct((bsz, groups, q_tok, w), BF16),
        grid=(groups, bsz // nb),
        in_specs=[pl.BlockSpec((nb, 1, q_tok, w), lambda g, b: (b, g, 0, qb))]
        + [piece_spec(kb, a) for a in range(4)] + [piece_spec(vb, a) for a in range(4)]
        + [pl.BlockSpec((nb, 1, q_tok, w), lambda g, b: (b, g, 0, zb)),
           pl.BlockSpec(bias_rows.shape, lambda g, b: (0, 0, 0, 0))],
        out_specs=pl.BlockSpec((nb, 1, q_tok, w), lambda g, b: (b, g, 0, 0)),
        scratch_shapes=[pltpu.VMEM((nb * q_tok, w), F32), pltpu.VMEM((nb * q_tok, w), F32),
                        pltpu.VMEM((len(_GROUP_KINDS), NA_HEADS, q_tok, N_KEYS), F32),
                        pltpu.VMEM((2 * Q_ROWS * Q_BLOCK_MAX, N_KEYS), F32),
                        pltpu.VMEM((2 * Q_ROWS * Q_BLOCK_MAX, N_KEYS), F32)],
        compiler_params=pltpu.CompilerParams(
            dimension_semantics=("arbitrary", "arbitrary"), vmem_limit_bytes=VMEM_LIMIT_BYTES),
        name="natten",
    )(proj_q, *([proj_p] * 8), proj_q, bias_rows).reshape(bsz, seq, w)


def _merge_out_kernel(ya_ref, yb_ref, g0_ref, g1_ref, x_ref, mod_ref, bm_ref, wa_ref, wb_ref, wo_ref, fg_ref,
                      out_ref, m_ref):
    tm, d = x_ref.shape
    sub = min(MERGE_ROWS, tm)
    gate = mod_ref[0, :, 2 * d:3 * d]

    def merge(r):
        rows = pl.ds(r, sub)
        pa = jnp.dot(ya_ref[rows, :], wa_ref[...], preferred_element_type=F32)
        pb = jnp.dot(yb_ref[rows, :], wb_ref[...], preferred_element_type=F32)
        m_ref[rows, :] = (_sigmoid(g0_ref[rows, :].astype(F32) + bm_ref[0:1, :]) * pa
                          + _sigmoid(g1_ref[rows, :].astype(F32) + bm_ref[1:2, :]) * pb).astype(BF16)

    def project(r):
        rows = pl.ds(r, sub)
        mixed = jnp.dot(m_ref[rows, :], wo_ref[...], preferred_element_type=F32)
        xo = x_ref[rows, :] + gate * mixed
        out_ref[rows, :] = xo * lax.rsqrt(jnp.mean(xo * xo, axis=-1, keepdims=True) + EPS) * fg_ref[...]

    starts = list(range(0, tm, sub))
    merge(starts[0])
    for i, r in enumerate(starts):
        if i + 1 < len(starts):
            merge(starts[i + 1])
        project(r)


def _merge_out(ya2, yb2, proj2, x2, mod3, b_merge, wa, wb, wo, final_gain, seq, tm):
    n_tok, d = x2.shape
    g0 = (N_MAIN - 2 * d) // d

    def full(shape):
        return pl.BlockSpec(shape, lambda m: (0,) * len(shape))

    return pl.pallas_call(
        _merge_out_kernel,
        out_shape=jax.ShapeDtypeStruct((n_tok, d), F32),
        grid=(n_tok // tm,),
        in_specs=[
            pl.BlockSpec((tm, d), lambda m: (m, 0)),
            pl.BlockSpec((tm, NA_WIDTH), lambda m: (m, 0)),
            pl.BlockSpec((tm, d), lambda m: (m, g0)),
            pl.BlockSpec((tm, d), lambda m: (m, g0 + 1)),
            pl.BlockSpec((tm, d), lambda m: (m, 0)),
            pl.BlockSpec((1, 1, 3 * d), lambda m: ((m * tm) // seq, 0, 0)),
            full((2, d)), full((d, d)), full((NA_WIDTH, d)), full((d, d)), full((1, d)),
        ],
        out_specs=pl.BlockSpec((tm, d), lambda m: (m, 0)),
        scratch_shapes=[pltpu.VMEM((tm, d), BF16)],
        compiler_params=pltpu.CompilerParams(
            dimension_semantics=("arbitrary",), vmem_limit_bytes=VMEM_LIMIT_BYTES),
        name="merge_out",
    )(ya2, yb2, proj2, proj2, x2, mod3, b_merge, wa, wb, wo, final_gain.reshape(1, d))


def _gate_layout(n_cols):
    src = np.arange(n_cols)
    direction, gate, head = src // (2 * MLSTM_HEADS), (src // MLSTM_HEADS) % 2, src % MLSTM_HEADS
    return 2 * MLSTM_HEADS * gate + MLSTM_HEADS * direction + head


def _weight_prep_kernel(a_ref, b_ref, perm_ref, c_ref, wada_ref, bada_ref, pa_ref, pb_ref, po_ref,
                        main_ref, gate_ref, mod_ref, wa_ref, wb_ref, wo_ref, *, gate_block, n_gates, n_mod, n_cast):
    n = pl.program_id(0)

    @pl.when(n < n_mod)
    def _():
        cond = _silu(c_ref[...]).astype(BF16)
        mod = jnp.dot(cond, wada_ref[...].astype(BF16), preferred_element_type=F32) + bada_ref[...]
        for b in range(mod.shape[0]):
            mod_ref[b] = mod[b:b + 1, :]

    @pl.when(n < n_cast)
    def _():
        wa_ref[...] = pa_ref[...].astype(BF16)
        wb_ref[...] = pb_ref[...].astype(BF16)
        wo_ref[...] = po_ref[...].astype(BF16)

    rows = a_ref[0]
    past_gates = jnp.concatenate([rows[n_gates:], b_ref[0]], axis=0)
    main_ref[...] = jnp.where(n >= gate_block, past_gates, rows).T.astype(BF16)

    @pl.when(n == gate_block)
    def _():
        gate_ref[...] = jnp.dot(perm_ref[...], rows[0:n_gates], preferred_element_type=F32).T.astype(BF16)


def _weight_prep(w_in, layer, c, w_ada, b_ada, w_proj_a, w_proj_b, w_out):
    _, d, n_in = w_in.shape
    bsz = c.shape[0]
    n_gates = 4 * MLSTM_HEADS
    tn = D_MODEL
    g_lo = 5 * MLSTM_HEADS * MLSTM_HEAD_DIM
    assert n_in == N_MAIN + n_gates and g_lo % tn == 0 and tn % n_gates == 0
    perm = np.zeros((GATE_LANES, n_gates), np.float32)
    perm[_gate_layout(n_gates), np.arange(n_gates)] = 1.0
    w_t = jnp.swapaxes(w_in, 1, 2)
    steps = N_MAIN // tn
    n_ada = w_ada.shape[1]
    t_mod = 512
    n_mod, n_cast = n_ada // t_mod, 8
    ra, rb, ro = w_proj_a.shape[0] // n_cast, w_proj_b.shape[0] // n_cast, w_out.shape[0] // n_cast
    assert n_ada % t_mod == 0 and n_mod <= steps and n_cast <= steps and min(ra, rb, ro) % 16 == 0
    mod_tile = lambda n: (0, jnp.minimum(n, n_mod - 1))
    cast_tile = lambda n: (jnp.minimum(n, n_cast - 1), 0)
    return pl.pallas_call(
        functools.partial(_weight_prep_kernel, gate_block=g_lo // tn, n_gates=n_gates, n_mod=n_mod, n_cast=n_cast),
        out_shape=(jax.ShapeDtypeStruct((d, N_MAIN), BF16), jax.ShapeDtypeStruct((d, GATE_LANES), BF16),
                   jax.ShapeDtypeStruct((bsz, 1, n_ada), F32), jax.ShapeDtypeStruct(w_proj_a.shape, BF16),
                   jax.ShapeDtypeStruct(w_proj_b.shape, BF16), jax.ShapeDtypeStruct(w_out.shape, BF16)),
        grid=(steps,),
        in_specs=[pl.BlockSpec((1, tn, d), lambda n: (layer, n, 0)),
                  pl.BlockSpec((1, n_gates, d), lambda n: (layer, (n + 1) * (tn // n_gates), 0)),
                  pl.BlockSpec((GATE_LANES, n_gates), lambda n: (0, 0)),
                  pl.BlockSpec((bsz, d), lambda n: (0, 0)),
                  pl.BlockSpec((d, t_mod), mod_tile),
                  pl.BlockSpec((1, t_mod), mod_tile),
                  pl.BlockSpec((ra, w_proj_a.shape[1]), cast_tile),
                  pl.BlockSpec((rb, w_proj_b.shape[1]), cast_tile),
                  pl.BlockSpec((ro, w_out.shape[1]), cast_tile)],
        out_specs=(pl.BlockSpec((d, tn), lambda n: (0, n)),
                   pl.BlockSpec((d, GATE_LANES), lambda n: (0, 0)),
                   pl.BlockSpec((bsz, 1, t_mod), lambda n: (0, 0, jnp.minimum(n, n_mod - 1))),
                   pl.BlockSpec((ra, w_proj_a.shape[1]), cast_tile),
                   pl.BlockSpec((rb, w_proj_b.shape[1]), cast_tile),
                   pl.BlockSpec((ro, w_out.shape[1]), cast_tile)),
        compiler_params=pltpu.CompilerParams(dimension_semantics=("arbitrary",), vmem_limit_bytes=VMEM_LIMIT_BYTES),
        name="weight_prep",
    )(w_t, w_t, jnp.asarray(perm), c, w_ada, b_ada.reshape(1, n_ada), w_proj_a, w_proj_b, w_out)


def _token_tiles(seq):
    return min(2048, seq), min(1024, seq)


def _layer(x, c, w_ada, b_ada, norm_gain, w_in, layer, conv_w, conv_b, b_igate, b_fgate, mlstm_norm_gain, rpb, w_proj_a,
           w_proj_b, b_merge, w_out, final_gain):
    bsz, seq, d = x.shape
    n_tok = bsz * seq
    wa = MLSTM_HEADS * MLSTM_HEAD_DIM
    w_main, w_gate, mod, wpa, wpb, wo = _weight_prep(w_in, layer, c, w_ada, b_ada, w_proj_a, w_proj_b, w_out)
    n_gates = 4 * MLSTM_HEADS
    assert np.array_equal(_gate_layout(n_gates).reshape(2, 2, MLSTM_HEADS).transpose(1, 0, 2).reshape(-1),
                          np.arange(n_gates))
    gate_bias = jnp.pad(jnp.concatenate([b_igate.reshape(-1), b_fgate.reshape(-1)]).astype(F32),
                        (0, GATE_LANES - n_gates)).reshape(1, GATE_LANES)
    tm_in, tm_out = _token_tiles(seq)

    x2 = x.reshape(n_tok, d)
    mod3 = mod
    proj2, gcol2, grow = _in_proj(x2, mod3, norm_gain, w_main, w_gate, gate_bias, seq, tm_in)
    proj3 = proj2.reshape(bsz, seq, N_MAIN)
    gcol = gcol2.reshape(bsz, seq, GATE_LANES)
    y_a = _mlstm(proj3, gcol, grow, conv_w, conv_b, mlstm_norm_gain)
    y_b = _natten(proj3, _natten_bias_rows(rpb))
    out2 = _merge_out(y_a.reshape(n_tok, wa), y_b.reshape(n_tok, NA_WIDTH), proj2, x2, mod3, b_merge,
                      wpa, wpb, wo, final_gain, seq, tm_out)
    return out2.reshape(bsz, seq, d)


def kernel(x, c, w_ada, b_ada, norm_gain, w_in, conv_w, conv_b, b_igate, b_fgate, mlstm_norm_gain, rpb, w_proj_a,
           w_proj_b, b_merge, w_out, final_gain):
    depth = w_ada.shape[0]
    assert depth == 1, "the fused final norm assumes a single layer"
    return _layer(x, c, w_ada[0], b_ada[0], norm_gain[0], w_in, 0, conv_w[0], conv_b[0], b_igate[0], b_fgate[0], mlstm_norm_gain[0],
                  rpb[0], w_proj_a[0], w_proj_b[0], b_merge[0], w_out[0], final_gain)
```

```python
import functools
import math

import numpy as np
import jax
import jax.numpy as jnp
from jax import lax
from jax.experimental import pallas as pl
from jax.experimental.pallas import tpu as pltpu

F32 = jnp.float32
BF16 = jnp.bfloat16

D_MODEL = 1024
GRID_W = 64
MLSTM_HEADS = 4
MLSTM_HEAD_DIM = 256
NA_HEADS = 8
NA_WIDTH = 512
NA_HEAD_DIM = 64
NA_ROWS = 8
NA_COLS = 16
EPS = 1e-6
NEG = -1e30
LOG2E = 1.4426950408889634
K_SCALE_LOG2 = -0.5 * math.log2(MLSTM_HEAD_DIM)

VMEM_LIMIT_BYTES = 56 * 1024 * 1024
LANES = 128

CHUNK = 256

MERGE_ROWS = 512

N_MAIN = 9 * D_MODEL
GATE_LANES = 128

Q_ROWS = 8
SLAB_ROWS = 16
Q_BLOCKS = ((0, 24), (24, 16), (40, 24))
Q_BLOCK_MAX = 24
K_STARTS = (0, 16, 32)
K_WIDTH = 32
N_KEYS = SLAB_ROWS * K_WIDTH


def _sigmoid(x):
    return 1.0 / (1.0 + jnp.exp2(x * -LOG2E))


def _silu(x):
    return x * _sigmoid(x)


def _gate_quantities(gates, pos_in_chunk):
    v = gates.T
    i_gate, f_gate = v[0:8, :] * LOG2E, v[8:16, :]
    n_tok = v.shape[1]
    is_bwd = lax.broadcasted_iota(jnp.int32, i_gate.shape, 0) >= MLSTM_HEADS

    def scan(x, combine, identity):
        pre, suf = x, x
        step = 1
        while step < CHUNK:
            pre = combine(pre, jnp.where(pos_in_chunk >= step, pltpu.roll(pre, step, 1), identity))
            suf = combine(suf, jnp.where(pos_in_chunk < CHUNK - step, pltpu.roll(suf, n_tok - step, 1), identity))
            step *= 2
        return jnp.where(is_bwd, suf, pre)

    b = scan((jnp.minimum(f_gate, 0.0) - jnp.log1p(jnp.exp(-jnp.abs(f_gate)))) * LOG2E, jnp.add, 0.0)
    r = i_gate - b
    cm = scan(r, jnp.maximum, NEG)
    cols = jnp.concatenate([cm, b, jnp.zeros((GATE_LANES - 16, n_tok), F32)], axis=0).T
    return cols, r


def _in_proj_kernel(x_ref, mod_ref, gain_ref, w_ref, wg_ref, gbias_ref, proj_ref, gcol_ref, grow_ref, h_ref, *, rows):
    tm, d = x_ref.shape
    n = pl.program_id(1)

    @pl.when(n == 0)
    def _():
        shift = mod_ref[0, :, 0:d]
        scale = mod_ref[0, :, d:2 * d]
        gain = gain_ref[...]
        pos = lax.broadcasted_iota(jnp.int32, (8, rows), 1) % CHUNK

        def normalize(r):
            xf = x_ref[r:r + rows, :]
            y = xf * lax.rsqrt(jnp.mean(xf * xf, axis=-1, keepdims=True) + EPS) * gain
            h_ref[r:r + rows, :] = (y * (1.0 + scale) + shift).astype(BF16)

        def project(r):
            hb = h_ref[r:r + rows, :]
            gates = jnp.dot(hb, wg_ref[...], preferred_element_type=F32) + gbias_ref[...]
            proj_ref[r:r + rows, :] = jnp.dot(hb, w_ref[...], preferred_element_type=F32).astype(BF16)
            return gates

        def gate_scans(r, gates):
            cols, r_rows = _gate_quantities(gates, pos)
            gcol_ref[r:r + rows, :] = cols
            for c in range(rows // CHUNK):
                grow_ref[0, r // CHUNK + c] = r_rows[:, c * CHUNK:(c + 1) * CHUNK]

        starts = list(range(0, tm, rows))
        normalize(starts[0])
        pending = None
        for i, r in enumerate(starts):
            if i + 1 < len(starts):
                normalize(starts[i + 1])
            gates = project(r)
            if pending is not None:
                gate_scans(*pending)
            pending = (r, gates)
        gate_scans(*pending)

    @pl.when(n > 0)
    def _():
        proj_ref[...] = jnp.dot(h_ref[...], w_ref[...], preferred_element_type=F32).astype(BF16)


def _in_proj(x2, mod3, norm_gain, w_main, w_gate, gate_bias, seq, tm):
    n_tok, d = x2.shape
    tn = N_MAIN // 6
    rows = min(MERGE_ROWS, tm)
    assert rows % CHUNK == 0 and seq % tm == 0
    tiles_per_seq = seq // tm
    return pl.pallas_call(
        functools.partial(_in_proj_kernel, rows=rows),
        out_shape=(
            jax.ShapeDtypeStruct((n_tok, N_MAIN), BF16),
            jax.ShapeDtypeStruct((n_tok, GATE_LANES), F32),
            jax.ShapeDtypeStruct((n_tok // seq, seq // CHUNK, 2 * MLSTM_HEADS, CHUNK), F32),
        ),
        grid=(n_tok // tm, N_MAIN // tn),
        in_specs=[
            pl.BlockSpec((tm, d), lambda m, n: (m, 0)),
            pl.BlockSpec((1, 1, 3 * d), lambda m, n: ((m * tm) // seq, 0, 0)),
            pl.BlockSpec((1, d), lambda m, n: (0, 0)),
            pl.BlockSpec((d, tn), lambda m, n: (0, n)),
            pl.BlockSpec((d, GATE_LANES), lambda m, n: (0, 0)),
            pl.BlockSpec((1, GATE_LANES), lambda m, n: (0, 0)),
        ],
        out_specs=(
            pl.BlockSpec((tm, tn), lambda m, n: (m, n)),
            pl.BlockSpec((tm, GATE_LANES), lambda m, n: (m, 0)),
            pl.BlockSpec((1, tm // CHUNK, 2 * MLSTM_HEADS, CHUNK),
                         lambda m, n: (m // tiles_per_seq, m % tiles_per_seq, 0, 0)),
        ),
        scratch_shapes=[pltpu.VMEM((tm, d), BF16)],
        compiler_params=pltpu.CompilerParams(
            dimension_semantics=("arbitrary", "arbitrary"), vmem_limit_bytes=VMEM_LIMIT_BYTES),
        name="in_proj",
    )(x2, mod3, norm_gain.reshape(1, d), w_main, w_gate, gate_bias)


def _mlstm_chunk(qc, kt, qk_ref, reuse_qk, vv, r_row, cm_rep, b_rep, cm_end, g, cn_ref, m_ref, causal_mask):
    d = vv.shape[1]
    cn = cn_ref[...]
    m_state = m_ref[...]
    m_col = jnp.maximum(m_state, cm_rep)
    w_inter = jnp.exp2(m_state - m_col)
    m_wide = jnp.concatenate([m_col, m_col], axis=1)
    d_mat = jnp.where(causal_mask, jnp.exp2(r_row - m_wide), 0.0)
    if reuse_qk:
        qk = qk_ref[...]
    else:
        qk = jnp.dot(qc, kt, preferred_element_type=F32)
        qk_ref[...] = qk
    scores = qk * d_mat
    inter = jnp.dot(qc, cn.astype(BF16), preferred_element_type=F32)
    yield None
    w_wide = jnp.concatenate([w_inter, w_inter], axis=1)
    num = jnp.dot(scores.astype(BF16), vv, preferred_element_type=F32) + w_wide * inter[:, 0:d]
    den = jnp.sum(scores, axis=-1, keepdims=True) + w_inter * inter[:, d:]
    inv = 1.0 / jnp.maximum(jnp.abs(den), jnp.exp2(-(b_rep + m_col)))
    yield num * jnp.concatenate([inv, inv], axis=1)
    m_end = jnp.maximum(m_state, cm_end)
    w_row = jnp.exp2(r_row - jnp.concatenate([m_end, m_end], axis=1))
    decay = jnp.exp2(m_state - m_end)
    kw = kt * w_row.astype(BF16)
    v_aug = jnp.concatenate([vv, jnp.ones((vv.shape[0], LANES), BF16)], axis=1)
    decay_wide = jnp.concatenate([decay, decay, decay], axis=1)
    cn_ref[...] = decay_wide * cn + jnp.dot(kw, v_aug, preferred_element_type=F32)
    m_ref[...] = g + m_end
    yield None


def _mlstm_kernel(q_ref, k_ref, v_ref, o_ref, z_ref, gcol_ref, grow_ref, cwq_ref, cwk_ref, cbq_ref, cbk_ref,
                  gain_ref, y_ref, qc_ref, kt_ref, qk_ref, h_ref, cnf_ref, mf_ref, cnb_ref, mb_ref):
    seq, d = q_ref.shape[1], q_ref.shape[2]
    nc = seq // CHUNK
    head = pl.program_id(1)

    t_idx = lax.broadcasted_iota(jnp.int32, (CHUNK, CHUNK), 0)
    s_idx = lax.broadcasted_iota(jnp.int32, (CHUNK, CHUNK), 1)
    taps = jnp.concatenate([(s_idx == t_idx - 1).astype(BF16), (s_idx == t_idx + 1).astype(BF16)],
                           axis=0)
    edge = lax.broadcasted_iota(jnp.int32, (8, d), 0)

    def conv_silu(src_ref, w_ref, b_ref, c, r):
        w0, w1, w2 = 0.5 * w_ref[0:1, :], 0.5 * w_ref[1:2, :], 0.5 * w_ref[2:3, :]
        src = src_ref[0, pl.ds(r, CHUNK), :]
        p = jnp.dot(taps, src, preferred_element_type=F32)
        t = p[0:CHUNK] * w0 + src.astype(F32) * w1 + p[CHUNK:2 * CHUNK] * w2 + 0.5 * b_ref[...]
        above = src_ref[0, pl.ds(pl.multiple_of(jnp.maximum(r - 16, 0), 16), 16), :][15:16, :].astype(F32)
        below = src_ref[0, pl.ds(pl.multiple_of(jnp.minimum(r + CHUNK, seq - 16), 16), 16), :][0:1, :].astype(F32)
        above = above * jnp.where(c > 0, w0, 0.0)
        below = below * jnp.where(c < nc - 1, w2, 0.0)
        t = jnp.concatenate([t[0:8] + jnp.where(edge == 0, above, 0.0), t[8:CHUNK - 8],
                             t[CHUNK - 8:CHUNK] + jnp.where(edge == 7, below, 0.0)], axis=0)
        return t * (1.0 + jnp.tanh(t))

    def conv_chunk(c, carry):
        r = pl.multiple_of(c * CHUNK, CHUNK)
        qc_ref[pl.ds(r, CHUNK), :] = conv_silu(q_ref, cwq_ref, cbq_ref, c, r).astype(BF16)
        kt_ref[c] = conv_silu(k_ref, cwk_ref, cbk_ref, c, r).astype(BF16).T
        return carry

    lax.fori_loop(0, nc, conv_chunk, 0, unroll=8)

    gate_row = lax.broadcasted_iota(jnp.int32, (2 * MLSTM_HEADS, CHUNK), 0)
    gate_lane = lax.broadcasted_iota(jnp.int32, (CHUNK, GATE_LANES), 1)

    for ref in (cnf_ref, mf_ref, cnb_ref, mb_ref):
        ref[...] = jnp.zeros(ref.shape, F32)

    mask_f = s_idx <= t_idx
    mask_b = s_idx >= t_idx

    def replicate(cols, lane):
        picked = jnp.sum(jnp.where(gate_lane == lane, cols, 0.0), axis=1, keepdims=True)
        return jnp.broadcast_to(picked, (cols.shape[0], LANES))

    def direction(chunk, backward, reuse_qk):
        r = pl.multiple_of(chunk * CHUNK, CHUNK)
        rows = grow_ref[0, chunk]
        cols = gcol_ref[0, pl.ds(r, CHUNK), :]
        qc = qc_ref[pl.ds(r, CHUNK), :]
        kt = kt_ref[chunk]
        vv = v_ref[0, pl.ds(r, CHUNK), :]
        if backward:
            cm_rep, b_rep = replicate(cols, MLSTM_HEADS + head), replicate(cols, 3 * MLSTM_HEADS + head)
            end = 0
            state, mask, this_row = (cnb_ref, mb_ref), mask_b, MLSTM_HEADS + head
        else:
            cm_rep, b_rep = replicate(cols, head), replicate(cols, 2 * MLSTM_HEADS + head)
            end = CHUNK - 1
            state, mask, this_row = (cnf_ref, mf_ref), mask_f, head
        r_row = jnp.sum(jnp.where(gate_row == this_row, rows, 0.0), axis=0, keepdims=True) + K_SCALE_LOG2
        cm_end, g = cm_rep[end:end + 1, :], b_rep[end:end + 1, :]
        return _mlstm_chunk(qc, kt, qk_ref.at[chunk], reuse_qk, vv, r_row, cm_rep, b_rep, cm_end, g, *state, mask), r

    def chunk_steps(j, second_visit):
        steps = [direction(j, False, second_visit), direction(nc - 1 - j, True, second_visit)]
        if second_visit:
            for gen, _ in steps:
                next(gen)
            hs = [next(gen) for gen, _ in steps]
            for gen, _ in steps:
                next(gen)
        else:
            hs = []
            for gen, _ in steps:
                next(gen)
                hs.append(next(gen))
                next(gen)
        return [(h, r) for h, (_, r) in zip(hs, steps)]

    def first_half(j, carry):
        for h_dir, r in chunk_steps(j, second_visit=False):
            h_ref[pl.ds(r, CHUNK), :] = h_dir
        return carry

    gain = gain_ref[...]

    def finish(h_dir, r):
        h_sum = (h_ref[pl.ds(r, CHUNK), :] + h_dir) * _sigmoid(o_ref[0, pl.ds(r, CHUNK), :].astype(F32))
        normed = h_sum * lax.rsqrt(jnp.mean(h_sum * h_sum, axis=-1, keepdims=True) + EPS) * gain
        y_ref[0, pl.ds(r, CHUNK), :] = (normed * _silu(z_ref[0, pl.ds(r, CHUNK), :].astype(F32))).astype(BF16)

    def second_half(j, carry):
        for h_dir, r in chunk_steps(j, second_visit=True):
            finish(h_dir, r)
        return carry

    lax.fori_loop(0, nc // 2, first_half, 0, unroll=2)
    lax.fori_loop(nc // 2, nc, second_half, 0)


def _mlstm(proj3, gcol, grow, conv_w, conv_b, head_gain):
    bsz, seq, _ = proj3.shape
    d = MLSTM_HEAD_DIM
    nh = MLSTM_HEADS
    nc = seq // CHUNK

    def col_block(offset):
        return pl.BlockSpec((1, seq, d), lambda b, h: (b, 0, offset + h))

    return pl.pallas_call(
        _mlstm_kernel,
        out_shape=jax.ShapeDtypeStruct((bsz, seq, nh * d), BF16),
        grid=(bsz, nh),
        in_specs=[
            col_block(0), col_block(nh), col_block(2 * nh), col_block(3 * nh), col_block(4 * nh),
            pl.BlockSpec((1, seq, GATE_LANES), lambda b, h: (b, 0, 0)),
            pl.BlockSpec((1, nc, 2 * MLSTM_HEADS, CHUNK), lambda b, h: (b, 0, 0, 0)),
            pl.BlockSpec((3, d), lambda b, h: (0, h)),
            pl.BlockSpec((3, d), lambda b, h: (0, nh + h)),
            pl.BlockSpec((1, d), lambda b, h: (0, h)),
            pl.BlockSpec((1, d), lambda b, h: (0, nh + h)),
            pl.BlockSpec((1, d), lambda b, h: (0, h)),
        ],
        out_specs=pl.BlockSpec((1, seq, d), lambda b, h: (b, 0, h)),
        scratch_shapes=[
            pltpu.VMEM((seq, d), BF16),
            pltpu.VMEM((nc, d, CHUNK), BF16),
            pltpu.VMEM((nc, CHUNK, CHUNK), F32),
            pltpu.VMEM((seq, d), F32),
            pltpu.VMEM((d, d + LANES), F32), pltpu.VMEM((1, LANES), F32),
            pltpu.VMEM((d, d + LANES), F32), pltpu.VMEM((1, LANES), F32),
        ],
        compiler_params=pltpu.CompilerParams(
            dimension_semantics=("arbitrary", "arbitrary"), vmem_limit_bytes=VMEM_LIMIT_BYTES),
        name="mlstm",
    )(proj3, proj3, proj3, proj3, proj3, gcol, grow, conv_w, conv_w,
      conv_b.reshape(1, -1), conv_b.reshape(1, -1), head_gain.reshape(1, -1))


def _natten_bias_rows(rpb):
    n_dc = 2 * NA_COLS - 1
    n_dr = 2 * NA_ROWS - 1
    left = NA_COLS
    period = 64
    padded = jnp.pad(rpb.astype(F32) * LOG2E, ((0, 0), (0, 0), (left, period - left - n_dc)), constant_values=NEG)
    out = []
    for (c0, w), kb in zip(Q_BLOCKS, K_STARTS):
        base = kb - c0 + NA_COLS - 1 + left
        toe = jnp.tile(padded, (1, 1, w))[..., :(period - 1) * w].reshape(NA_HEADS, n_dr, w, period - 1)
        toe = toe[..., base:base + K_WIDTH]
        q_col = c0 + np.arange(w)[:, None]
        k_col = kb + np.arange(K_WIDTH)[None, :]
        win = np.clip(q_col - NA_COLS // 2, 0, GRID_W - NA_COLS)
        col_ok = jnp.asarray((k_col >= win) & (k_col < win + NA_COLS))
        toe = jnp.where(col_ok[None, None], toe, NEG).transpose(0, 2, 1, 3).reshape(NA_HEADS, w, n_dr * K_WIDTH)
        out.append(jnp.pad(toe, ((0, 0), (0, Q_BLOCK_MAX - w), (0, N_KEYS - n_dr * K_WIDTH)), constant_values=NEG))
    return jnp.stack(out, axis=1)


_GROUP_KINDS = (
    (0, lambda i: max(i - NA_ROWS // 2, 0)),
    (-(NA_ROWS // 2), lambda i: i),
    (Q_ROWS - SLAB_ROWS, lambda i: min(i + NA_ROWS // 2, NA_ROWS)),
)


def _build_bias(e_ref, tab_ref, slab_off, first_valid):
    for h in range(NA_HEADS):
        row_off = 0
        for blk, (_, w) in enumerate(Q_BLOCKS):
            e = e_ref[h, blk, 0:w, :]
            lane = lax.broadcasted_iota(jnp.int32, (w, N_KEYS), 1)
            for i in range(Q_ROWS):
                shift = slab_off - i + NA_ROWS - 1
                amount = (-shift * K_WIDTH) % N_KEYS
                moved = e if amount == 0 else pltpu.roll(e, amount, 1)
                lo = first_valid(i) * K_WIDTH
                ok = (lane >= lo) & (lane < lo + NA_ROWS * K_WIDTH)
                tab_ref[h, row_off + i * w:row_off + (i + 1) * w, :] = jnp.where(ok, moved, NEG)
            row_off += Q_ROWS * w


def _natten_kernel(q_ref, k0_ref, k1_ref, k2_ref, k3_ref, v0_ref, v1_ref, v2_ref, v3_ref, z_ref, e_ref,
                   y_ref, qf_ref, of_ref, tabs_ref, sa_ref, sb_ref):
    k_refs = (k0_ref, k1_ref, k2_ref, k3_ref)
    v_refs = (v0_ref, v1_ref, v2_ref, v3_ref)
    piece_rows = SLAB_ROWS // 4
    group = pl.program_id(0)
    n_groups = pl.num_programs(0)

    @pl.when(jnp.logical_and(group == 0, pl.program_id(1) == 0))
    def _():
        for kind, (slab_off, first_valid) in enumerate(_GROUP_KINDS):
            _build_bias(e_ref, tabs_ref.at[kind], slab_off, first_valid)

    tab_ref = tabs_ref.at[(group > 0).astype(jnp.int32) + (group == n_groups - 1).astype(jnp.int32)]

    n_batch, q_tok = q_ref.shape[0], q_ref.shape[2]
    for bi in range(n_batch):
        qf_ref[pl.ds(bi * q_tok, q_tok), :] = q_ref[bi, 0].astype(F32) * (NA_HEAD_DIM ** -0.5 * LOG2E)

    items = []
    for bi in range(n_batch):
        row_off = 0
        for (c0, w), kb in zip(Q_BLOCKS, K_STARTS):
            for pair in range(NA_HEADS // 2):
                items.append((bi, c0, w, kb, pair, row_off))
            row_off += Q_ROWS * w
    s_refs = (sa_ref, sb_ref)

    def slab(refs, bi, kb, lanes):
        return jnp.concatenate(
            [refs[a // piece_rows][bi, 0, pl.ds((a % piece_rows) * GRID_W + kb, K_WIDTH), lanes]
             for a in range(SLAB_ROWS)], axis=0)

    def scores(idx):
        bi, c0, w, kb, pair, row_off = items[idx]
        nq = Q_ROWS * w
        lanes = pl.ds(pair * LANES, LANES)
        first_head = lax.broadcasted_iota(jnp.int32, (nq, LANES), 1) < NA_HEAD_DIM
        qp = jnp.concatenate([qf_ref[pl.ds(bi * q_tok + i * GRID_W + c0, w), lanes] for i in range(Q_ROWS)],
                             axis=0)
        qs = jnp.concatenate([jnp.where(first_head, qp, 0.0), jnp.where(first_head, 0.0, qp)],
                             axis=0).astype(BF16)
        s = lax.dot_general(qs, slab(k_refs, bi, kb, lanes), (((1,), (1,)), ((), ())),
                            preferred_element_type=F32)
        bias = jnp.concatenate([tab_ref[2 * pair, pl.ds(row_off, nq), :],
                                tab_ref[2 * pair + 1, pl.ds(row_off, nq), :]], axis=0)
        s_refs[idx % 2][0:2 * nq, :] = s + bias

    def attend(idx):
        bi, c0, w, kb, pair, _ = items[idx]
        nq = Q_ROWS * w
        lanes = pl.ds(pair * LANES, LANES)
        first_head = lax.broadcasted_iota(jnp.int32, (nq, LANES), 1) < NA_HEAD_DIM
        s = s_refs[idx % 2][0:2 * nq, :]
        p = jnp.exp2(s - jnp.max(s, axis=-1, keepdims=True))
        o2 = jnp.dot(p.astype(BF16), slab(v_refs, bi, kb, lanes), preferred_element_type=F32)
        o2 = o2 * (1.0 / jnp.sum(p, axis=-1, keepdims=True))
        o = jnp.where(first_head, o2[0:nq], o2[nq:2 * nq])
        for i in range(Q_ROWS):
            of_ref[pl.ds(bi * q_tok + i * GRID_W + c0, w), lanes] = o[i * w:(i + 1) * w]

    scores(0)
    for idx in range(len(items)):
        if idx + 1 < len(items):
            scores(idx + 1)
        attend(idx)
    for bi in range(n_batch):
        rows = pl.ds(bi * q_tok, q_tok)
        y_ref[bi, 0] = (of_ref[rows, :] * _silu(z_ref[bi, 0].astype(F32))).astype(BF16)


def _natten(proj3, bias_rows):
    bsz, seq, n_main = proj3.shape
    q_tok = Q_ROWS * GRID_W
    p_tok = (SLAB_ROWS // 4) * GRID_W
    groups = seq // q_tok
    pieces = seq // p_tok
    assert groups >= 2, "needs at least two 8-row query groups"
    nb = 2 if bsz % 2 == 0 else 1
    w = NA_WIDTH
    qb, kb, vb, zb = (5 * D_MODEL) // w, (5 * D_MODEL) // w + 1, (5 * D_MODEL) // w + 2, (5 * D_MODEL) // w + 3
    proj_q = proj3.reshape(bsz, groups, q_tok, n_main)
    proj_p = proj3.reshape(bsz, pieces, p_tok, n_main)

    def slab_start(g):
        return jnp.clip(2 * g - 1, 0, pieces - 4)

    def piece_spec(col, a):
        return pl.BlockSpec((nb, 1, p_tok, w), lambda g, b: (b, slab_start(g) + a, 0, col))

    return pl.pallas_call(
        _natten_kernel,
        out_shape=jax.ShapeDtypeStruct((bsz, groups, q_tok, w), BF16),
        grid=(groups, bsz // nb),
        in_specs=[pl.BlockSpec((nb, 1, q_tok, w), lambda g, b: (b, g, 0, qb))]
        + [piece_spec(kb, a) for a in range(4)] + [piece_spec(vb, a) for a in range(4)]
        + [pl.BlockSpec((nb, 1, q_tok, w), lambda g, b: (b, g, 0, zb)),
           pl.BlockSpec(bias_rows.shape, lambda g, b: (0, 0, 0, 0))],
        out_specs=pl.BlockSpec((nb, 1, q_tok, w), lambda g, b: (b, g, 0, 0)),
        scratch_shapes=[pltpu.VMEM((nb * q_tok, w), F32), pltpu.VMEM((nb * q_tok, w), F32),
                        pltpu.VMEM((len(_GROUP_KINDS), NA_HEADS, q_tok, N_KEYS), F32),
                        pltpu.VMEM((2 * Q_ROWS * Q_BLOCK_MAX, N_KEYS), F32),
                        pltpu.VMEM((2 * Q_ROWS * Q_BLOCK_MAX, N_KEYS), F32)],
        compiler_params=pltpu.CompilerParams(
            dimension_semantics=("arbitrary", "arbitrary"), vmem_limit_bytes=VMEM_LIMIT_BYTES),
        name="natten",
    )(proj_q, *([proj_p] * 8), proj_q, bias_rows).reshape(bsz, seq, w)


def _merge_out_kernel(ya_ref, yb_ref, g0_ref, g1_ref, x_ref, mod_ref, bm_ref, wa_ref, wb_ref, wo_ref, fg_ref,
                      out_ref, m_ref):
    tm, d = x_ref.shape
    sub = min(MERGE_ROWS, tm)
    gate = mod_ref[0, :, 2 * d:3 * d]

    def merge(r):
        rows = pl.ds(r, sub)
        pa = jnp.dot(ya_ref[rows, :], wa_ref[...], preferred_element_type=F32)
        pb = jnp.dot(yb_ref[rows, :], wb_ref[...], preferred_element_type=F32)
        m_ref[rows, :] = (_sigmoid(g0_ref[rows, :].astype(F32) + bm_ref[0:1, :]) * pa
                          + _sigmoid(g1_ref[rows, :].astype(F32) + bm_ref[1:2, :]) * pb).astype(BF16)

    def project(r):
        rows = pl.ds(r, sub)
        mixed = jnp.dot(m_ref[rows, :], wo_ref[...], preferred_element_type=F32)
        xo = x_ref[rows, :] + gate * mixed
        out_ref[rows, :] = xo * lax.rsqrt(jnp.mean(xo * xo, axis=-1, keepdims=True) + EPS) * fg_ref[...]

    starts = list(range(0, tm, sub))
    merge(starts[0])
    for i, r in enumerate(starts):
        if i + 1 < len(starts):
            merge(starts[i + 1])
        project(r)


def _merge_out(ya2, yb2, proj2, x2, mod3, b_merge, wa, wb, wo, final_gain, seq, tm):
    n_tok, d = x2.shape
    g0 = (N_MAIN - 2 * d) // d

    def full(shape):
        return pl.BlockSpec(shape, lambda m: (0,) * len(shape))

    return pl.pallas_call(
        _merge_out_kernel,
        out_shape=jax.ShapeDtypeStruct((n_tok, d), F32),
        grid=(n_tok // tm,),
        in_specs=[
            pl.BlockSpec((tm, d), lambda m: (m, 0)),
            pl.BlockSpec((tm, NA_WIDTH), lambda m: (m, 0)),
            pl.BlockSpec((tm, d), lambda m: (m, g0)),
            pl.BlockSpec((tm, d), lambda m: (m, g0 + 1)),
            pl.BlockSpec((tm, d), lambda m: (m, 0)),
            pl.BlockSpec((1, 1, 3 * d), lambda m: ((m * tm) // seq, 0, 0)),
            full((2, d)), full((d, d)), full((NA_WIDTH, d)), full((d, d)), full((1, d)),
        ],
        out_specs=pl.BlockSpec((tm, d), lambda m: (m, 0)),
        scratch_shapes=[pltpu.VMEM((tm, d), BF16)],
        compiler_params=pltpu.CompilerParams(
            dimension_semantics=("arbitrary",), vmem_limit_bytes=VMEM_LIMIT_BYTES),
        name="merge_out",
    )(ya2, yb2, proj2, proj2, x2, mod3, b_merge, wa, wb, wo, final_gain.reshape(1, d))


def _gate_layout(n_cols):
    src = np.arange(n_cols)
    direction, gate, head = src // (2 * MLSTM_HEADS), (src // MLSTM_HEADS) % 2, src % MLSTM_HEADS
    return 2 * MLSTM_HEADS * gate + MLSTM_HEADS * direction + head


def _weight_prep_kernel(a_ref, b_ref, perm_ref, c_ref, wada_ref, bada_ref, pa_ref, pb_ref, po_ref,
                        main_ref, gate_ref, mod_ref, wa_ref, wb_ref, wo_ref, *, gate_block, n_gates, n_mod, n_cast):
    n = pl.program_id(0)

    @pl.when(n < n_mod)
    def _():
        cond = _silu(c_ref[...]).astype(BF16)
        mod = jnp.dot(cond, wada_ref[...].astype(BF16), preferred_element_type=F32) + bada_ref[...]
        for b in range(mod.shape[0]):
            mod_ref[b] = mod[b:b + 1, :]

    @pl.when(n < n_cast)
    def _():
        wa_ref[...] = pa_ref[...].astype(BF16)
        wb_ref[...] = pb_ref[...].astype(BF16)
        wo_ref[...] = po_ref[...].astype(BF16)

    rows = a_ref[0]
    past_gates = jnp.concatenate([rows[n_gates:], b_ref[0]], axis=0)
    main_ref[...] = jnp.where(n >= gate_block, past_gates, rows).T.astype(BF16)

    @pl.when(n == gate_block)
    def _():
        gate_ref[...] = jnp.dot(perm_ref[...], rows[0:n_gates], preferred_element_type=F32).T.astype(BF16)


def _weight_prep(w_in, layer, c, w_ada, b_ada, w_proj_a, w_proj_b, w_out):
    _, d, n_in = w_in.shape
    bsz = c.shape[0]
    n_gates = 4 * MLSTM_HEADS
    tn = D_MODEL
    g_lo = 5 * MLSTM_HEADS * MLSTM_HEAD_DIM
    assert n_in == N_MAIN + n_gates and g_lo % tn == 0 and tn % n_gates == 0
    perm = np.zeros((GATE_LANES, n_gates), np.float32)
    perm[_gate_layout(n_gates), np.arange(n_gates)] = 1.0
    w_t = jnp.swapaxes(w_in, 1, 2)
    steps = N_MAIN // tn
    n_ada = w_ada.shape[1]
    t_mod = 512
    n_mod, n_cast = n_ada // t_mod, 8
    ra, rb, ro = w_proj_a.shape[0] // n_cast, w_proj_b.shape[0] // n_cast, w_out.shape[0] // n_cast
    assert n_ada % t_mod == 0 and n_mod <= steps and n_cast <= steps and min(ra, rb, ro) % 16 == 0
    mod_tile = lambda n: (0, jnp.minimum(n, n_mod - 1))
    cast_tile = lambda n: (jnp.minimum(n, n_cast - 1), 0)
    return pl.pallas_call(
        functools.partial(_weight_prep_kernel, gate_block=g_lo // tn, n_gates=n_gates, n_mod=n_mod, n_cast=n_cast),
        out_shape=(jax.ShapeDtypeStruct((d, N_MAIN), BF16), jax.ShapeDtypeStruct((d, GATE_LANES), BF16),
                   jax.ShapeDtypeStruct((bsz, 1, n_ada), F32), jax.ShapeDtypeStruct(w_proj_a.shape, BF16),
                   jax.ShapeDtypeStruct(w_proj_b.shape, BF16), jax.ShapeDtypeStruct(w_out.shape, BF16)),
        grid=(steps,),
        in_specs=[pl.BlockSpec((1, tn, d), lambda n: (layer, n, 0)),
                  pl.BlockSpec((1, n_gates, d), lambda n: (layer, (n + 1) * (tn // n_gates), 0)),
                  pl.BlockSpec((GATE_LANES, n_gates), lambda n: (0, 0)),
                  pl.BlockSpec((bsz, d), lambda n: (0, 0)),
                  pl.BlockSpec((d, t_mod), mod_tile),
                  pl.BlockSpec((1, t_mod), mod_tile),
                  pl.BlockSpec((ra, w_proj_a.shape[1]), cast_tile),
                  pl.BlockSpec((rb, w_proj_b.shape[1]), cast_tile),
                  pl.BlockSpec((ro, w_out.shape[1]), cast_tile)],
        out_specs=(pl.BlockSpec((d, tn), lambda n: (0, n)),
                   pl.BlockSpec((d, GATE_LANES), lambda n: (0, 0)),
                   pl.BlockSpec((bsz, 1, t_mod), lambda n: (0, 0, jnp.minimum(n, n_mod - 1))),
                   pl.BlockSpec((ra, w_proj_a.shape[1]), cast_tile),
                   pl.BlockSpec((rb, w_proj_b.shape[1]), cast_tile),
                   pl.BlockSpec((ro, w_out.shape[1]), cast_tile)),
        compiler_params=pltpu.CompilerParams(dimension_semantics=("arbitrary",), vmem_limit_bytes=VMEM_LIMIT_BYTES),
        name="weight_prep",
    )(w_t, w_t, jnp.asarray(perm), c, w_ada, b_ada.reshape(1, n_ada), w_proj_a, w_proj_b, w_out)


def _token_tiles(seq):
    return min(2048, seq), min(1024, seq)


def _layer(x, c, w_ada, b_ada, norm_gain, w_in, layer, conv_w, conv_b, b_igate, b_fgate, mlstm_norm_gain, rpb, w_proj_a,
           w_proj_b, b_merge, w_out, final_gain):
    bsz, seq, d = x.shape
    n_tok = bsz * seq
    wa = MLSTM_HEADS * MLSTM_HEAD_DIM
    w_main, w_gate, mod, wpa, wpb, wo = _weight_prep(w_in, layer, c, w_ada, b_ada, w_proj_a, w_proj_b, w_out)
    n_gates = 4 * MLSTM_HEADS
    assert np.array_equal(_gate_layout(n_gates).reshape(2, 2, MLSTM_HEADS).transpose(1, 0, 2).reshape(-1),
                          np.arange(n_gates))
    gate_bias = jnp.pad(jnp.concatenate([b_igate.reshape(-1), b_fgate.reshape(-1)]).astype(F32),
                        (0, GATE_LANES - n_gates)).reshape(1, GATE_LANES)
    tm_in, tm_out = _token_tiles(seq)

    x2 = x.reshape(n_tok, d)
    mod3 = mod
    proj2, gcol2, grow = _in_proj(x2, mod3, norm_gain, w_main, w_gate, gate_bias, seq, tm_in)
    proj3 = proj2.reshape(bsz, seq, N_MAIN)
    gcol = gcol2.reshape(bsz, seq, GATE_LANES)
    y_a = _mlstm(proj3, gcol, grow, conv_w, conv_b, mlstm_norm_gain)
    y_b = _natten(proj3, _natten_bias_rows(rpb))
    out2 = _merge_out(y_a.reshape(n_tok, wa), y_b.reshape(n_tok, NA_WIDTH), proj2, x2, mod3, b_merge,
                      wpa, wpb, wo, final_gain, seq, tm_out)
    return out2.reshape(bsz, seq, d)


def kernel(x, c, w_ada, b_ada, norm_gain, w_in, conv_w, conv_b, b_igate, b_fgate, mlstm_norm_gain, rpb, w_proj_a,
           w_proj_b, b_merge, w_out, final_gain):
    depth = w_ada.shape[0]
    assert depth == 1, "the fused final norm assumes a single layer"
    return _layer(x, c, w_ada[0], b_ada[0], norm_gain[0], w_in, 0, conv_w[0], conv_b[0], b_igate[0], b_fgate[0], mlstm_norm_gain[0],
                  rpb[0], w_proj_a[0], w_proj_b[0], b_merge[0], w_out[0], final_gain)
```

```python
import functools
import math

import numpy as np
import jax
import jax.numpy as jnp
from jax import lax
from jax.experimental import pallas as pl
from jax.experimental.pallas import tpu as pltpu

F32 = jnp.float32
BF16 = jnp.bfloat16

D_MODEL = 1024
GRID_W = 64
MLSTM_HEADS = 4
MLSTM_HEAD_DIM = 256
NA_HEADS = 8
NA_WIDTH = 512
NA_HEAD_DIM = 64
NA_ROWS = 8
NA_COLS = 16
EPS = 1e-6
NEG = -1e30
LOG2E = 1.4426950408889634
K_SCALE_LOG2 = -0.5 * math.log2(MLSTM_HEAD_DIM)

VMEM_LIMIT_BYTES = 56 * 1024 * 1024
LANES = 128

CHUNK = 256

MERGE_ROWS = 512

N_MAIN = 9 * D_MODEL
GATE_LANES = 128

Q_ROWS = 8
SLAB_ROWS = 16
Q_BLOCKS = ((0, 24), (24, 16), (40, 24))
Q_BLOCK_MAX = 24
K_STARTS = (0, 16, 32)
K_WIDTH = 32
N_KEYS = SLAB_ROWS * K_WIDTH


def _sigmoid(x):
    return 1.0 / (1.0 + jnp.exp2(x * -LOG2E))


def _silu(x):
    return x * _sigmoid(x)


def _gate_quantities(gates, pos_in_chunk):
    v = gates.T
    i_gate, f_gate = v[0:8, :] * LOG2E, v[8:16, :]
    n_tok = v.shape[1]
    is_bwd = lax.broadcasted_iota(jnp.int32, i_gate.shape, 0) >= MLSTM_HEADS

    def scan(x, combine, identity):
        pre, suf = x, x
        step = 1
        while step < CHUNK:
            pre = combine(pre, jnp.where(pos_in_chunk >= step, pltpu.roll(pre, step, 1), identity))
            suf = combine(suf, jnp.where(pos_in_chunk < CHUNK - step, pltpu.roll(suf, n_tok - step, 1), identity))
            step *= 2
        return jnp.where(is_bwd, suf, pre)

    b = scan((jnp.minimum(f_gate, 0.0) - jnp.log1p(jnp.exp(-jnp.abs(f_gate)))) * LOG2E, jnp.add, 0.0)
    r = i_gate - b
    cm = scan(r, jnp.maximum, NEG)
    cols = jnp.concatenate([cm, b, jnp.zeros((GATE_LANES - 16, n_tok), F32)], axis=0).T
    return cols, r


def _in_proj_kernel(x_ref, mod_ref, gain_ref, w_ref, wg_ref, gbias_ref, proj_ref, gcol_ref, grow_ref, h_ref, *, rows):
    tm, d = x_ref.shape
    n = pl.program_id(1)

    @pl.when(n == 0)
    def _():
        shift = mod_ref[0, :, 0:d]
        scale = mod_ref[0, :, d:2 * d]
        gain = gain_ref[...]
        pos = lax.broadcasted_iota(jnp.int32, (8, rows), 1) % CHUNK

        def normalize(r):
            xf = x_ref[r:r + rows, :]
            y = xf * lax.rsqrt(jnp.mean(xf * xf, axis=-1, keepdims=True) + EPS) * gain
            h_ref[r:r + rows, :] = (y * (1.0 + scale) + shift).astype(BF16)

        def project(r):
            hb = h_ref[r:r + rows, :]
            gates = jnp.dot(hb, wg_ref[...], preferred_element_type=F32) + gbias_ref[...]
            proj_ref[r:r + rows, :] = jnp.dot(hb, w_ref[...], preferred_element_type=F32).astype(BF16)
            return gates

        def gate_scans(r, gates):
            cols, r_rows = _gate_quantities(gates, pos)
            gcol_ref[r:r + rows, :] = cols
            for c in range(rows // CHUNK):
                grow_ref[0, r // CHUNK + c] = r_rows[:, c * CHUNK:(c + 1) * CHUNK]

        starts = list(range(0, tm, rows))
        normalize(starts[0])
        pending = None
        for i, r in enumerate(starts):
            if i + 1 < len(starts):
                normalize(starts[i + 1])
            gates = project(r)
            if pending is not None:
                gate_scans(*pending)
            pending = (r, gates)
        gate_scans(*pending)

    @pl.when(n > 0)
    def _():
        proj_ref[...] = jnp.dot(h_ref[...], w_ref[...], preferred_element_type=F32).astype(BF16)


def _in_proj(x2, mod3, norm_gain, w_main, w_gate, gate_bias, seq, tm):
    n_tok, d = x2.shape
    tn = N_MAIN // 6
    rows = min(MERGE_ROWS, tm)
    assert rows % CHUNK == 0 and seq % tm == 0
    tiles_per_seq = seq // tm
    return pl.pallas_call(
        functools.partial(_in_proj_kernel, rows=rows),
        out_shape=(
            jax.ShapeDtypeStruct((n_tok, N_MAIN), BF16),
            jax.ShapeDtypeStruct((n_tok, GATE_LANES), F32),
            jax.ShapeDtypeStruct((n_tok // seq, seq // CHUNK, 2 * MLSTM_HEADS, CHUNK), F32),
        ),
        grid=(n_tok // tm, N_MAIN // tn),
        in_specs=[
            pl.BlockSpec((tm, d), lambda m, n: (m, 0)),
            pl.BlockSpec((1, 1, 3 * d), lambda m, n: ((m * tm) // seq, 0, 0)),
            pl.BlockSpec((1, d), lambda m, n: (0, 0)),
            pl.BlockSpec((d, tn), lambda m, n: (0, n)),
            pl.BlockSpec((d, GATE_LANES), lambda m, n: (0, 0)),
            pl.BlockSpec((1, GATE_LANES), lambda m, n: (0, 0)),
        ],
        out_specs=(
            pl.BlockSpec((tm, tn), lambda m, n: (m, n)),
            pl.BlockSpec((tm, GATE_LANES), lambda m, n: (m, 0)),
            pl.BlockSpec((1, tm // CHUNK, 2 * MLSTM_HEADS, CHUNK),
                         lambda m, n: (m // tiles_per_seq, m % tiles_per_seq, 0, 0)),
        ),
        scratch_shapes=[pltpu.VMEM((tm, d), BF16)],
        compiler_params=pltpu.CompilerParams(
            dimension_semantics=("arbitrary", "arbitrary"), vmem_limit_bytes=VMEM_LIMIT_BYTES),
        name="in_proj",
    )(x2, mod3, norm_gain.reshape(1, d), w_main, w_gate, gate_bias)


def _mlstm_chunk(qc, kt, qk_ref, reuse_qk, vv, r_row, cm_rep, b_rep, cm_end, g, cn_ref, m_ref, causal_mask):
    d = vv.shape[1]
    cn = cn_ref[...]
    m_state = m_ref[...]
    m_col = jnp.maximum(m_state, cm_rep)
    w_inter = jnp.exp2(m_state - m_col)
    m_wide = jnp.concatenate([m_col, m_col], axis=1)
    d_mat = jnp.where(causal_mask, jnp.exp2(r_row - m_wide), 0.0)
    if reuse_qk:
        qk = qk_ref[...]
    else:
        qk = jnp.dot(qc, kt, preferred_element_type=F32)
        qk_ref[...] = qk
    scores = qk * d_mat
    inter = jnp.dot(qc, cn.astype(BF16), preferred_element_type=F32)
    yield None
    w_wide = jnp.concatenate([w_inter, w_inter], axis=1)
    num = jnp.dot(scores.astype(BF16), vv, preferred_element_type=F32) + w_wide * inter[:, 0:d]
    den = jnp.sum(scores, axis=-1, keepdims=True) + w_inter * inter[:, d:]
    inv = 1.0 / jnp.maximum(jnp.abs(den), jnp.exp2(-(b_rep + m_col)))
    yield num * jnp.concatenate([inv, inv], axis=1)
    m_end = jnp.maximum(m_state, cm_end)
    w_row = jnp.exp2(r_row - jnp.concatenate([m_end, m_end], axis=1))
    decay = jnp.exp2(m_state - m_end)
    kw = kt * w_row.astype(BF16)
    v_aug = jnp.concatenate([vv, jnp.ones((vv.shape[0], LANES), BF16)], axis=1)
    decay_wide = jnp.concatenate([decay, decay, decay], axis=1)
    cn_ref[...] = decay_wide * cn + jnp.dot(kw, v_aug, preferred_element_type=F32)
    m_ref[...] = g + m_end
    yield None


def _mlstm_kernel(q_ref, k_ref, v_ref, o_ref, z_ref, gcol_ref, grow_ref, cwq_ref, cwk_ref, cbq_ref, cbk_ref,
                  gain_ref, y_ref, qc_ref, kt_ref, qk_ref, h_ref, cnf_ref, cnb_ref, mf_ref, mb_ref):
    seq, d = q_ref.shape[1], q_ref.shape[2]
    nc = seq // CHUNK
    head = pl.program_id(1)

    t_idx = lax.broadcasted_iota(jnp.int32, (CHUNK, CHUNK), 0)
    s_idx = lax.broadcasted_iota(jnp.int32, (CHUNK, CHUNK), 1)
    taps = jnp.concatenate([(s_idx == t_idx - 1).astype(BF16), (s_idx == t_idx + 1).astype(BF16)],
                           axis=0)
    edge = lax.broadcasted_iota(jnp.int32, (8, d), 0)

    def conv_silu(src_ref, w_ref, b_ref, c, r):
        w0, w1, w2 = 0.5 * w_ref[0:1, :], 0.5 * w_ref[1:2, :], 0.5 * w_ref[2:3, :]
        src = src_ref[0, pl.ds(r, CHUNK), :]
        p = jnp.dot(taps, src, preferred_element_type=F32)
        t = p[0:CHUNK] * w0 + src.astype(F32) * w1 + p[CHUNK:2 * CHUNK] * w2 + 0.5 * b_ref[...]
        above = src_ref[0, pl.ds(pl.multiple_of(jnp.maximum(r - 16, 0), 16), 16), :][15:16, :].astype(F32)
        below = src_ref[0, pl.ds(pl.multiple_of(jnp.minimum(r + CHUNK, seq - 16), 16), 16), :][0:1, :].astype(F32)
        above = above * jnp.where(c > 0, w0, 0.0)
        below = below * jnp.where(c < nc - 1, w2, 0.0)
        t = jnp.concatenate([t[0:8] + jnp.where(edge == 0, above, 0.0), t[8:CHUNK - 8],
                             t[CHUNK - 8:CHUNK] + jnp.where(edge == 7, below, 0.0)], axis=0)
        return t * (1.0 + jnp.tanh(t))

    def conv_chunk(c, carry):
        r = pl.multiple_of(c * CHUNK, CHUNK)
        qc_ref[pl.ds(r, CHUNK), :] = conv_silu(q_ref, cwq_ref, cbq_ref, c, r).astype(BF16)
        kt_ref[c] = conv_silu(k_ref, cwk_ref, cbk_ref, c, r).astype(BF16).T
        return carry

    lax.fori_loop(0, nc, conv_chunk, 0, unroll=8)

    gate_row = lax.broadcasted_iota(jnp.int32, (2 * MLSTM_HEADS, CHUNK), 0)
    gate_lane = lax.broadcasted_iota(jnp.int32, (CHUNK, GATE_LANES), 1)

    for ref in (cnf_ref, mf_ref, cnb_ref, mb_ref):
        ref[...] = jnp.zeros(ref.shape, F32)

    mask_f = s_idx <= t_idx
    mask_b = s_idx >= t_idx

    def replicate(cols, lane):
        picked = jnp.sum(jnp.where(gate_lane == lane, cols, 0.0), axis=1, keepdims=True)
        return jnp.broadcast_to(picked, (cols.shape[0], LANES))

    def direction(chunk, backward, reuse_qk):
        r = pl.multiple_of(chunk * CHUNK, CHUNK)
        rows = grow_ref[0, chunk]
        cols = gcol_ref[0, pl.ds(r, CHUNK), :]
        qc = qc_ref[pl.ds(r, CHUNK), :]
        kt = kt_ref[chunk]
        vv = v_ref[0, pl.ds(r, CHUNK), :]
        if backward:
            cm_rep, b_rep = replicate(cols, MLSTM_HEADS + head), replicate(cols, 3 * MLSTM_HEADS + head)
            end = 0
            state, mask, this_row = (cnb_ref, mb_ref), mask_b, MLSTM_HEADS + head
        else:
            cm_rep, b_rep = replicate(cols, head), replicate(cols, 2 * MLSTM_HEADS + head)
            end = CHUNK - 1
            state, mask, this_row = (cnf_ref, mf_ref), mask_f, head
        r_row = jnp.sum(jnp.where(gate_row == this_row, rows, 0.0), axis=0, keepdims=True) + K_SCALE_LOG2
        cm_end, g = cm_rep[end:end + 1, :], b_rep[end:end + 1, :]
        return _mlstm_chunk(qc, kt, qk_ref.at[chunk], reuse_qk, vv, r_row, cm_rep, b_rep, cm_end, g, *state, mask), r

    def chunk_steps(j, second_visit):
        steps = [direction(j, False, second_visit), direction(nc - 1 - j, True, second_visit)]
        if second_visit:
            for gen, _ in steps:
                next(gen)
            hs = [next(gen) for gen, _ in steps]
            for gen, _ in steps:
                next(gen)
        else:
            hs = []
            for gen, _ in steps:
                next(gen)
                hs.append(next(gen))
                next(gen)
        return [(h, r) for h, (_, r) in zip(hs, steps)]

    def first_half(j, carry):
        for h_dir, r in chunk_steps(j, second_visit=False):
            h_ref[pl.ds(r, CHUNK), :] = h_dir
        return carry

    gain = gain_ref[...]

    def finish(h_dir, r):
        h_sum = (h_ref[pl.ds(r, CHUNK), :] + h_dir) * _sigmoid(o_ref[0, pl.ds(r, CHUNK), :].astype(F32))
        normed = h_sum * lax.rsqrt(jnp.mean(h_sum * h_sum, axis=-1, keepdims=True) + EPS) * gain
        y_ref[0, pl.ds(r, CHUNK), :] = (normed * _silu(z_ref[0, pl.ds(r, CHUNK), :].astype(F32))).astype(BF16)

    def second_half(j, carry):
        for h_dir, r in chunk_steps(j, second_visit=True):
            finish(h_dir, r)
        return carry

    lax.fori_loop(0, nc // 2, first_half, 0, unroll=2)
    lax.fori_loop(nc // 2, nc, second_half, 0)


def _mlstm(proj3, gcol, grow, conv_w, conv_b, head_gain):
    bsz, seq, _ = proj3.shape
    d = MLSTM_HEAD_DIM
    nh = MLSTM_HEADS
    nc = seq // CHUNK

    def col_block(offset):
        return pl.BlockSpec((1, seq, d), lambda b, h: (b, 0, offset + h))

    return pl.pallas_call(
        _mlstm_kernel,
        out_shape=jax.ShapeDtypeStruct((bsz, seq, nh * d), BF16),
        grid=(bsz, nh),
        in_specs=[
            col_block(0), col_block(nh), col_block(2 * nh), col_block(3 * nh), col_block(4 * nh),
            pl.BlockSpec((1, seq, GATE_LANES), lambda b, h: (b, 0, 0)),
            pl.BlockSpec((1, nc, 2 * MLSTM_HEADS, CHUNK), lambda b, h: (b, 0, 0, 0)),
            pl.BlockSpec((3, d), lambda b, h: (0, h)),
            pl.BlockSpec((3, d), lambda b, h: (0, nh + h)),
            pl.BlockSpec((1, d), lambda b, h: (0, h)),
            pl.BlockSpec((1, d), lambda b, h: (0, nh + h)),
            pl.BlockSpec((1, d), lambda b, h: (0, h)),
        ],
        out_specs=pl.BlockSpec((1, seq, d), lambda b, h: (b, 0, h)),
        scratch_shapes=[
            pltpu.VMEM((seq, d), BF16),
            pltpu.VMEM((nc, d, CHUNK), BF16),
            pltpu.VMEM((nc, CHUNK, CHUNK), F32),
            pltpu.VMEM((seq, d), F32),
            pltpu.VMEM((d, d + LANES), F32), pltpu.VMEM((d, d + LANES), F32),
            pltpu.VMEM((1, LANES), F32), pltpu.VMEM((1, LANES), F32),
        ],
        compiler_params=pltpu.CompilerParams(
            dimension_semantics=("arbitrary", "arbitrary"), vmem_limit_bytes=VMEM_LIMIT_BYTES),
        name="mlstm",
    )(proj3, proj3, proj3, proj3, proj3, gcol, grow, conv_w, conv_w,
      conv_b.reshape(1, -1), conv_b.reshape(1, -1), head_gain.reshape(1, -1))


def _natten_bias_rows(rpb):
    n_dc = 2 * NA_COLS - 1
    n_dr = 2 * NA_ROWS - 1
    left = NA_COLS
    period = 64
    padded = jnp.pad(rpb.astype(F32) * LOG2E, ((0, 0), (0, 0), (left, period - left - n_dc)), constant_values=NEG)
    out = []
    for (c0, w), kb in zip(Q_BLOCKS, K_STARTS):
        base = kb - c0 + NA_COLS - 1 + left
        toe = jnp.tile(padded, (1, 1, w))[..., :(period - 1) * w].reshape(NA_HEADS, n_dr, w, period - 1)
        toe = toe[..., base:base + K_WIDTH]
        q_col = c0 + np.arange(w)[:, None]
        k_col = kb + np.arange(K_WIDTH)[None, :]
        win = np.clip(q_col - NA_COLS // 2, 0, GRID_W - NA_COLS)
        col_ok = jnp.asarray((k_col >= win) & (k_col < win + NA_COLS))
        toe = jnp.where(col_ok[None, None], toe, NEG).transpose(0, 2, 1, 3).reshape(NA_HEADS, w, n_dr * K_WIDTH)
        out.append(jnp.pad(toe, ((0, 0), (0, Q_BLOCK_MAX - w), (0, N_KEYS - n_dr * K_WIDTH)), constant_values=NEG))
    return jnp.stack(out, axis=1)


_GROUP_KINDS = (
    (0, lambda i: max(i - NA_ROWS // 2, 0)),
    (-(NA_ROWS // 2), lambda i: i),
    (Q_ROWS - SLAB_ROWS, lambda i: min(i + NA_ROWS // 2, NA_ROWS)),
)


def _build_bias(e_ref, tab_ref, slab_off, first_valid):
    for h in range(NA_HEADS):
        row_off = 0
        for blk, (_, w) in enumerate(Q_BLOCKS):
            e = e_ref[h, blk, 0:w, :]
            lane = lax.broadcasted_iota(jnp.int32, (w, N_KEYS), 1)
            for i in range(Q_ROWS):
                shift = slab_off - i + NA_ROWS - 1
                amount = (-shift * K_WIDTH) % N_KEYS
                moved = e if amount == 0 else pltpu.roll(e, amount, 1)
                lo = first_valid(i) * K_WIDTH
                ok = (lane >= lo) & (lane < lo + NA_ROWS * K_WIDTH)
                tab_ref[h, row_off + i * w:row_off + (i + 1) * w, :] = jnp.where(ok, moved, NEG)
            row_off += Q_ROWS * w


def _natten_kernel(q_ref, k0_ref, k1_ref, k2_ref, k3_ref, v0_ref, v1_ref, v2_ref, v3_ref, z_ref, e_ref,
                   y_ref, qf_ref, of_ref, tabs_ref, sa_ref, sb_ref):
    k_refs = (k0_ref, k1_ref, k2_ref, k3_ref)
    v_refs = (v0_ref, v1_ref, v2_ref, v3_ref)
    piece_rows = SLAB_ROWS // 4
    group = pl.program_id(0)
    n_groups = pl.num_programs(0)

    @pl.when(jnp.logical_and(group == 0, pl.program_id(1) == 0))
    def _():
        for kind, (slab_off, first_valid) in enumerate(_GROUP_KINDS):
            _build_bias(e_ref, tabs_ref.at[kind], slab_off, first_valid)

    tab_ref = tabs_ref.at[(group > 0).astype(jnp.int32) + (group == n_groups - 1).astype(jnp.int32)]

    n_batch, q_tok = q_ref.shape[0], q_ref.shape[2]
    for bi in range(n_batch):
        qf_ref[pl.ds(bi * q_tok, q_tok), :] = q_ref[bi, 0].astype(F32) * (NA_HEAD_DIM ** -0.5 * LOG2E)

    items = []
    for bi in range(n_batch):
        row_off = 0
        for (c0, w), kb in zip(Q_BLOCKS, K_STARTS):
            for pair in range(NA_HEADS // 2):
                items.append((bi, c0, w, kb, pair, row_off))
            row_off += Q_ROWS * w
    s_refs = (sa_ref, sb_ref)

    def slab(refs, bi, kb, lanes):
        return jnp.concatenate(
            [refs[a // piece_rows][bi, 0, pl.ds((a % piece_rows) * GRID_W + kb, K_WIDTH), lanes]
             for a in range(SLAB_ROWS)], axis=0)

    def scores(idx):
        bi, c0, w, kb, pair, row_off = items[idx]
        nq = Q_ROWS * w
        lanes = pl.ds(pair * LANES, LANES)
        first_head = lax.broadcasted_iota(jnp.int32, (nq, LANES), 1) < NA_HEAD_DIM
        qp = jnp.concatenate([qf_ref[pl.ds(bi * q_tok + i * GRID_W + c0, w), lanes] for i in range(Q_ROWS)],
                             axis=0)
        qs = jnp.concatenate([jnp.where(first_head, qp, 0.0), jnp.where(first_head, 0.0, qp)],
                             axis=0).astype(BF16)
        s = lax.dot_general(qs, slab(k_refs, bi, kb, lanes), (((1,), (1,)), ((), ())),
                            preferred_element_type=F32)
        bias = jnp.concatenate([tab_ref[2 * pair, pl.ds(row_off, nq), :],
                                tab_ref[2 * pair + 1, pl.ds(row_off, nq), :]], axis=0)
        s_refs[idx % 2][0:2 * nq, :] = s + bias

    def attend(idx):
        bi, c0, w, kb, pair, _ = items[idx]
        nq = Q_ROWS * w
        lanes = pl.ds(pair * LANES, LANES)
        first_head = lax.broadcasted_iota(jnp.int32, (nq, LANES), 1) < NA_HEAD_DIM
        s = s_refs[idx % 2][0:2 * nq, :]
        p = jnp.exp2(s - jnp.max(s, axis=-1, keepdims=True))
        o2 = jnp.dot(p.astype(BF16), slab(v_refs, bi, kb, lanes), preferred_element_type=F32)
        o2 = o2 * (1.0 / jnp.sum(p, axis=-1, keepdims=True))
        o = jnp.where(first_head, o2[0:nq], o2[nq:2 * nq])
        for i in range(Q_ROWS):
            of_ref[pl.ds(bi * q_tok + i * GRID_W + c0, w), lanes] = o[i * w:(i + 1) * w]

    scores(0)
    for idx in range(len(items)):
        if idx + 1 < len(items):
            scores(idx + 1)
        attend(idx)
    for bi in range(n_batch):
        rows = pl.ds(bi * q_tok, q_tok)
        y_ref[bi, 0] = (of_ref[rows, :] * _silu(z_ref[bi, 0].astype(F32))).astype(BF16)


def _natten(proj3, bias_rows):
    bsz, seq, n_main = proj3.shape
    q_tok = Q_ROWS * GRID_W
    p_tok = (SLAB_ROWS // 4) * GRID_W
    groups = seq // q_tok
    pieces = seq // p_tok
    assert groups >= 2, "needs at least two 8-row query groups"
    nb = 2 if bsz % 2 == 0 else 1
    w = NA_WIDTH
    qb, kb, vb, zb = (5 * D_MODEL) // w, (5 * D_MODEL) // w + 1, (5 * D_MODEL) // w + 2, (5 * D_MODEL) // w + 3
    proj_q = proj3.reshape(bsz, groups, q_tok, n_main)
    proj_p = proj3.reshape(bsz, pieces, p_tok, n_main)

    def slab_start(g):
        return jnp.clip(2 * g - 1, 0, pieces - 4)

    def piece_spec(col, a):
        return pl.BlockSpec((nb, 1, p_tok, w), lambda g, b: (b, slab_start(g) + a, 0, col))

    return pl.pallas_call(
        _natten_kernel,
        out_shape=jax.ShapeDtypeStruct((bsz, groups, q_tok, w), BF16),
        grid=(groups, bsz // nb),
        in_specs=[pl.BlockSpec((nb, 1, q_tok, w), lambda g, b: (b, g, 0, qb))]
        + [piece_spec(kb, a) for a in range(4)] + [piece_spec(vb, a) for a in range(4)]
        + [pl.BlockSpec((nb, 1, q_tok, w), lambda g, b: (b, g, 0, zb)),
           pl.BlockSpec(bias_rows.shape, lambda g, b: (0, 0, 0, 0))],
        out_specs=pl.BlockSpec((nb, 1, q_tok, w), lambda g, b: (b, g, 0, 0)),
        scratch_shapes=[pltpu.VMEM((nb * q_tok, w), F32), pltpu.VMEM((nb * q_tok, w), F32),
                        pltpu.VMEM((len(_GROUP_KINDS), NA_HEADS, q_tok, N_KEYS), F32),
                        pltpu.VMEM((2 * Q_ROWS * Q_BLOCK_MAX, N_KEYS), F32),
                        pltpu.VMEM((2 * Q_ROWS * Q_BLOCK_MAX, N_KEYS), F32)],
        compiler_params=pltpu.CompilerParams(
            dimension_semantics=("arbitrary", "arbitrary"), vmem_limit_bytes=VMEM_LIMIT_BYTES),
        name="natten",
    )(proj_q, *([proj_p] * 8), proj_q, bias_rows).reshape(bsz, seq, w)


def _merge_out_kernel(ya_ref, yb_ref, g0_ref, g1_ref, x_ref, mod_ref, bm_ref, wa_ref, wb_ref, wo_ref, fg_ref,
                      out_ref, m_ref):
    tm, d = x_ref.shape
    sub = min(MERGE_ROWS, tm)
    gate = mod_ref[0, :, 2 * d:3 * d]

    def merge(r):
        rows = pl.ds(r, sub)
        pa = jnp.dot(ya_ref[rows, :], wa_ref[...], preferred_element_type=F32)
        pb = jnp.dot(yb_ref[rows, :], wb_ref[...], preferred_element_type=F32)
        m_ref[rows, :] = (_sigmoid(g0_ref[rows, :].astype(F32) + bm_ref[0:1, :]) * pa
                          + _sigmoid(g1_ref[rows, :].astype(F32) + bm_ref[1:2, :]) * pb).astype(BF16)

    def project(r):
        rows = pl.ds(r, sub)
        mixed = jnp.dot(m_ref[rows, :], wo_ref[...], preferred_element_type=F32)
        xo = x_ref[rows, :] + gate * mixed
        out_ref[rows, :] = xo * lax.rsqrt(jnp.mean(xo * xo, axis=-1, keepdims=True) + EPS) * fg_ref[...]

    starts = list(range(0, tm, sub))
    merge(starts[0])
    for i, r in enumerate(starts):
        if i + 1 < len(starts):
            merge(starts[i + 1])
        project(r)


def _merge_out(ya2, yb2, proj2, x2, mod3, b_merge, wa, wb, wo, final_gain, seq, tm):
    n_tok, d = x2.shape
    g0 = (N_MAIN - 2 * d) // d

    def full(shape):
        return pl.BlockSpec(shape, lambda m: (0,) * len(shape))

    return pl.pallas_call(
        _merge_out_kernel,
        out_shape=jax.ShapeDtypeStruct((n_tok, d), F32),
        grid=(n_tok // tm,),
        in_specs=[
            pl.BlockSpec((tm, d), lambda m: (m, 0)),
            pl.BlockSpec((tm, NA_WIDTH), lambda m: (m, 0)),
            pl.BlockSpec((tm, d), lambda m: (m, g0)),
            pl.BlockSpec((tm, d), lambda m: (m, g0 + 1)),
            pl.BlockSpec((tm, d), lambda m: (m, 0)),
            pl.BlockSpec((1, 1, 3 * d), lambda m: ((m * tm) // seq, 0, 0)),
            full((2, d)), full((d, d)), full((NA_WIDTH, d)), full((d, d)), full((1, d)),
        ],
        out_specs=pl.BlockSpec((tm, d), lambda m: (m, 0)),
        scratch_shapes=[pltpu.VMEM((tm, d), BF16)],
        compiler_params=pltpu.CompilerParams(
            dimension_semantics=("arbitrary",), vmem_limit_bytes=VMEM_LIMIT_BYTES),
        name="merge_out",
    )(ya2, yb2, proj2, proj2, x2, mod3, b_merge, wa, wb, wo, final_gain.reshape(1, d))


def _gate_layout(n_cols):
    src = np.arange(n_cols)
    direction, gate, head = src // (2 * MLSTM_HEADS), (src // MLSTM_HEADS) % 2, src % MLSTM_HEADS
    return 2 * MLSTM_HEADS * gate + MLSTM_HEADS * direction + head


def _weight_prep_kernel(a_ref, b_ref, perm_ref, c_ref, wada_ref, bada_ref, pa_ref, pb_ref, po_ref,
                        main_ref, gate_ref, mod_ref, wa_ref, wb_ref, wo_ref, *, gate_block, n_gates, n_mod, n_cast):
    n = pl.program_id(0)

    @pl.when(n < n_mod)
    def _():
        cond = _silu(c_ref[...]).astype(BF16)
        mod = jnp.dot(cond, wada_ref[...].astype(BF16), preferred_element_type=F32) + bada_ref[...]
        for b in range(mod.shape[0]):
            mod_ref[b] = mod[b:b + 1, :]

    @pl.when(n < n_cast)
    def _():
        wa_ref[...] = pa_ref[...].astype(BF16)
        wb_ref[...] = pb_ref[...].astype(BF16)
        wo_ref[...] = po_ref[...].astype(BF16)

    rows = a_ref[0]
    past_gates = jnp.concatenate([rows[n_gates:], b_ref[0]], axis=0)
    main_ref[...] = jnp.where(n >= gate_block, past_gates, rows).T.astype(BF16)

    @pl.when(n == gate_block)
    def _():
        gate_ref[...] = jnp.dot(perm_ref[...], rows[0:n_gates], preferred_element_type=F32).T.astype(BF16)


def _weight_prep(w_in, layer, c, w_ada, b_ada, w_proj_a, w_proj_b, w_out):
    _, d, n_in = w_in.shape
    bsz = c.shape[0]
    n_gates = 4 * MLSTM_HEADS
    tn = D_MODEL
    g_lo = 5 * MLSTM_HEADS * MLSTM_HEAD_DIM
    assert n_in == N_MAIN + n_gates and g_lo % tn == 0 and tn % n_gates == 0
    perm = np.zeros((GATE_LANES, n_gates), np.float32)
    perm[_gate_layout(n_gates), np.arange(n_gates)] = 1.0
    w_t = jnp.swapaxes(w_in, 1, 2)
    steps = N_MAIN // tn
    n_ada = w_ada.shape[1]
    t_mod = 512
    n_mod, n_cast = n_ada // t_mod, 8
    ra, rb, ro = w_proj_a.shape[0] // n_cast, w_proj_b.shape[0] // n_cast, w_out.shape[0] // n_cast
    assert n_ada % t_mod == 0 and n_mod <= steps and n_cast <= steps and min(ra, rb, ro) % 16 == 0
    mod_tile = lambda n: (0, jnp.minimum(n, n_mod - 1))
    cast_tile = lambda n: (jnp.minimum(n, n_cast - 1), 0)
    return pl.pallas_call(
        functools.partial(_weight_prep_kernel, gate_block=g_lo // tn, n_gates=n_gates, n_mod=n_mod, n_cast=n_cast),
        out_shape=(jax.ShapeDtypeStruct((d, N_MAIN), BF16), jax.ShapeDtypeStruct((d, GATE_LANES), BF16),
                   jax.ShapeDtypeStruct((bsz, 1, n_ada), F32), jax.ShapeDtypeStruct(w_proj_a.shape, BF16),
                   jax.ShapeDtypeStruct(w_proj_b.shape, BF16), jax.ShapeDtypeStruct(w_out.shape, BF16)),
        grid=(steps,),
        in_specs=[pl.BlockSpec((1, tn, d), lambda n: (layer, n, 0)),
                  pl.BlockSpec((1, n_gates, d), lambda n: (layer, (n + 1) * (tn // n_gates), 0)),
                  pl.BlockSpec((GATE_LANES, n_gates), lambda n: (0, 0)),
                  pl.BlockSpec((bsz, d), lambda n: (0, 0)),
                  pl.BlockSpec((d, t_mod), mod_tile),
                  pl.BlockSpec((1, t_mod), mod_tile),
                  pl.BlockSpec((ra, w_proj_a.shape[1]), cast_tile),
                  pl.BlockSpec((rb, w_proj_b.shape[1]), cast_tile),
                  pl.BlockSpec((ro, w_out.shape[1]), cast_tile)],
        out_specs=(pl.BlockSpec((d, tn), lambda n: (0, n)),
                   pl.BlockSpec((d, GATE_LANES), lambda n: (0, 0)),
                   pl.BlockSpec((bsz, 1, t_mod), lambda n: (0, 0, jnp.minimum(n, n_mod - 1))),
                   pl.BlockSpec((ra, w_proj_a.shape[1]), cast_tile),
                   pl.BlockSpec((rb, w_proj_b.shape[1]), cast_tile),
                   pl.BlockSpec((ro, w_out.shape[1]), cast_tile)),
        compiler_params=pltpu.CompilerParams(dimension_semantics=("arbitrary",), vmem_limit_bytes=VMEM_LIMIT_BYTES),
        name="weight_prep",
    )(w_t, w_t, jnp.asarray(perm), c, w_ada, b_ada.reshape(1, n_ada), w_proj_a, w_proj_b, w_out)


def _token_tiles(seq):
    return min(2048, seq), min(1024, seq)


def _layer(x, c, w_ada, b_ada, norm_gain, w_in, layer, conv_w, conv_b, b_igate, b_fgate, mlstm_norm_gain, rpb, w_proj_a,
           w_proj_b, b_merge, w_out, final_gain):
    bsz, seq, d = x.shape
    n_tok = bsz * seq
    wa = MLSTM_HEADS * MLSTM_HEAD_DIM
    w_main, w_gate, mod, wpa, wpb, wo = _weight_prep(w_in, layer, c, w_ada, b_ada, w_proj_a, w_proj_b, w_out)
    n_gates = 4 * MLSTM_HEADS
    assert np.array_equal(_gate_layout(n_gates).reshape(2, 2, MLSTM_HEADS).transpose(1, 0, 2).reshape(-1),
                          np.arange(n_gates))
    gate_bias = jnp.pad(jnp.concatenate([b_igate.reshape(-1), b_fgate.reshape(-1)]).astype(F32),
                        (0, GATE_LANES - n_gates)).reshape(1, GATE_LANES)
    tm_in, tm_out = _token_tiles(seq)

    x2 = x.reshape(n_tok, d)
    mod3 = mod
    proj2, gcol2, grow = _in_proj(x2, mod3, norm_gain, w_main, w_gate, gate_bias, seq, tm_in)
    proj3 = proj2.reshape(bsz, seq, N_MAIN)
    gcol = gcol2.reshape(bsz, seq, GATE_LANES)
    y_a = _mlstm(proj3, gcol, grow, conv_w, conv_b, mlstm_norm_gain)
    y_b = _natten(proj3, _natten_bias_rows(rpb))
    out2 = _merge_out(y_a.reshape(n_tok, wa), y_b.reshape(n_tok, NA_WIDTH), proj2, x2, mod3, b_merge,
                      wpa, wpb, wo, final_gain, seq, tm_out)
    return out2.reshape(bsz, seq, d)


def kernel(x, c, w_ada, b_ada, norm_gain, w_in, conv_w, conv_b, b_igate, b_fgate, mlstm_norm_gain, rpb, w_proj_a,
           w_proj_b, b_merge, w_out, final_gain):
    depth = w_ada.shape[0]
    assert depth == 1, "the fused final norm assumes a single layer"
    return _layer(x, c, w_ada[0], b_ada[0], norm_gain[0], w_in, 0, conv_w[0], conv_b[0], b_igate[0], b_fgate[0], mlstm_norm_gain[0],
                  rpb[0], w_proj_a[0], w_proj_b[0], b_merge[0], w_out[0], final_gain)
```
